```python
import math
import jax, jax.numpy as jnp
from jax import lax
import numpy as np

D_MODEL = 1024
BATCH = 32
SEQ = 2048
DEPTH = 4
DEC_BATCH = 16
DEC_SEQ = 32
PAST_LEN = 2048

CHUNK = 64
PLE_DIM = 256
N_MIXERS = 2
N_SSD = (DEPTH + 1) // 2
N_MLA = DEPTH // 2
EPS = 1e-6

SSD_D_INNER = 2 * D_MODEL
SSD_HEAD_DIM = 64
SSD_HEADS = SSD_D_INNER // SSD_HEAD_DIM
SSD_GROUPS = 8
SSD_HPG = SSD_HEADS // SSD_GROUPS
SSD_STATE = 128
SSD_CONV_W = 4
SSD_CONV_DIM = SSD_D_INNER + 2 * SSD_GROUPS * SSD_STATE
SSD_IN = SSD_D_INNER + SSD_CONV_DIM + SSD_HEADS
SSD_CHUNK = CHUNK

MLA_HEADS = 16
MLA_NOPE = 64
MLA_ROPE = 32
MLA_V = 64
MLA_Q_LORA = 384
MLA_KV_LORA = 256
MLA_QK = MLA_NOPE + MLA_ROPE
MLA_WIDTH = MLA_HEADS * MLA_V
MLA_IN = MLA_Q_LORA + MLA_KV_LORA + MLA_ROPE + MLA_WIDTH
ROPE_BASE = 10000.0
Q_BLOCK = 128

kernel_name = "hybrid_ssd_mla_streaming_step"

F32 = jnp.float32


def rmsnorm(x, w):
    xf = x.astype(F32)
    y = xf * lax.rsqrt(jnp.mean(xf * xf, axis=-1, keepdims=True) + EPS)
    return (y * w.astype(F32)).astype(x.dtype)


def causal_conv(xbc, prev, w, b):
    T = xbc.shape[1]
    xp = jnp.concatenate([prev.astype(xbc.dtype), xbc], axis=1)
    out = b
    for k in range(SSD_CONV_W):
        out = out + xp[:, k:k + T] * w[k]
    return out, xp[:, xp.shape[1] - (SSD_CONV_W - 1):]


def ssd_scan(x, dt, A, Bm, Cm, h0, chunk):
    b, T = x.shape[:2]
    c = T // chunk
    x = x.reshape(b, c, chunk, SSD_GROUPS, SSD_HPG, SSD_HEAD_DIM)
    dt = dt.reshape(b, c, chunk, SSD_GROUPS, SSD_HPG)
    Bm = Bm.reshape(b, c, chunk, SSD_GROUPS, SSD_STATE)
    Cm = Cm.reshape(b, c, chunk, SSD_GROUPS, SSD_STATE)
    Acs = jnp.cumsum(dt * A, axis=2)
    tri = jnp.tril(jnp.ones((chunk, chunk), bool))[:, :, None, None]
    seg = Acs[:, :, :, None] - Acs[:, :, None, :]
    Lmat = jnp.exp(jnp.where(tri, seg, -jnp.inf))
    CB = jnp.einsum('bclgn,bcsgn->bclsg', Cm, Bm)
    M = CB[..., None] * Lmat * dt[:, :, None]
    y_diag = jnp.einsum('bclsgr,bcsgrp->bclgrp', M, x)
    decay = jnp.exp(Acs[:, :, -1:] - Acs)
    states = jnp.einsum('bclgn,bclgr,bclgrp->bcgrpn', Bm, decay * dt, x).astype(F32)
    chunk_decay = jnp.exp(Acs[:, :, -1])

    def step(h, inp):
        s, d = inp
        return h * d[..., None, None] + s, h

    hT, h_prev = lax.scan(step, h0.astype(F32),
                          (jnp.moveaxis(states, 1, 0), jnp.moveaxis(chunk_decay, 1, 0)))
    h_prev = jnp.moveaxis(h_prev, 0, 1)
    y_off = jnp.einsum('bclgn,bcgrpn,bclgr->bclgrp', Cm, h_prev, jnp.exp(Acs))
    y = (y_diag + y_off).reshape(b, T, SSD_GROUPS, SSD_HPG, SSD_HEAD_DIM)
    return y, hT


def ssd_mixer(hn, conv_prev, h0, chunk, in_w, conv_w, conv_b, dt_bias, A_log, Dp, norm_w, out_w):
    b, T, _ = hn.shape
    proj = hn @ in_w
    z, xbc, dt = jnp.split(proj, [SSD_D_INNER, SSD_D_INNER + SSD_CONV_DIM], axis=-1)
    xbc, conv_new = causal_conv(xbc, conv_prev, conv_w, conv_b)
    xbc = jax.nn.silu(xbc)
    xs, Bm, Cm = jnp.split(xbc, [SSD_D_INNER, SSD_D_INNER + SSD_GROUPS * SSD_STATE], axis=-1)
    xs = xs.reshape(b, T, SSD_GROUPS, SSD_HPG, SSD_HEAD_DIM)
    Bm = Bm.reshape(b, T, SSD_GROUPS, SSD_STATE)
    Cm = Cm.reshape(b, T, SSD_GROUPS, SSD_STATE)
    dt = jax.nn.softplus((dt + dt_bias).astype(F32)).reshape(b, T, SSD_GROUPS, SSD_HPG)
    A = -jnp.exp(A_log.astype(F32)).reshape(SSD_GROUPS, SSD_HPG)
    h0 = h0.reshape(b, SSD_GROUPS, SSD_HPG, SSD_HEAD_DIM, SSD_STATE)
    y, hT = ssd_scan(xs, dt, A, Bm, Cm, h0, chunk)
    y = y + Dp.reshape(SSD_GROUPS, SSD_HPG)[..., None] * xs
    y = y.reshape(b, T, SSD_D_INNER) * jax.nn.silu(z)
    y = rmsnorm(y.reshape(b, T, SSD_GROUPS, SSD_D_INNER // SSD_GROUPS),
                norm_w.reshape(SSD_GROUPS, SSD_D_INNER // SSD_GROUPS)).reshape(b, T, SSD_D_INNER)
    out = (y @ out_w).astype(hn.dtype)
    return out, conv_new, hT.reshape(b, SSD_HEADS, SSD_HEAD_DIM, SSD_STATE)


def rope_cs(pos):
    inv = 1.0 / (ROPE_BASE ** (jnp.arange(0, MLA_ROPE, 2, dtype=F32) / MLA_ROPE))
    ang = pos.astype(F32)[:, None] * inv[None, :]
    return jnp.cos(ang), jnp.sin(ang)


def apply_rope(x, cos, sin):
    x1, x2 = jnp.split(x, 2, axis=-1)
    return jnp.concatenate([x1 * cos - x2 * sin, x1 * sin + x2 * cos], axis=-1).astype(x.dtype)


def chunk_causal_attention(q_nope, q_rope, k_nope, k_rope, v, q_pos, k_pos):
    scale = MLA_QK ** -0.5
    k_chunk = k_pos // CHUNK

    def block(args):
        qn, qr, qp = args
        s = (jnp.einsum('bqhd,bkhd->bhqk', qn, k_nope)
             + jnp.einsum('bqhd,bkd->bhqk', qr, k_rope)).astype(F32) * scale
        mask = k_chunk[None, :] <= (qp // CHUNK)[:, None]
        p = jax.nn.softmax(jnp.where(mask, s, -jnp.inf), axis=-1).astype(v.dtype)
        return jnp.einsum('bhqk,bkhd->bqhd', p, v)

    b, Tq = q_nope.shape[:2]
    if Tq > Q_BLOCK and Tq % Q_BLOCK == 0:
        nb = Tq // Q_BLOCK
        split = lambda t: jnp.moveaxis(t.reshape(b, nb, Q_BLOCK, *t.shape[2:]), 1, 0)
        o = lax.map(block, (split(q_nope), split(q_rope), q_pos.reshape(nb, Q_BLOCK)))
        return jnp.moveaxis(o, 0, 1).reshape(b, Tq, MLA_HEADS, MLA_V)
    return block((q_nope, q_rope, q_pos))


def mla_mixer(hn, lat_past, kr_past, pos0, in_w, q_a_norm, q_b_w, kv_a_norm, kv_b_w,
              qn_norm, qr_norm, kn_norm, kr_norm, out_w):
    b, T, _ = hn.shape
    proj = hn @ in_w
    q_a, kv_a, k_r, gate = jnp.split(
        proj, [MLA_Q_LORA, MLA_Q_LORA + MLA_KV_LORA, MLA_Q_LORA + MLA_KV_LORA + MLA_ROPE], axis=-1)
    q = (rmsnorm(q_a, q_a_norm) @ q_b_w).reshape(b, T, MLA_HEADS, MLA_QK)
    q_pos = pos0 + jnp.arange(T)
    cos, sin = rope_cs(q_pos)
    q_nope = rmsnorm(q[..., :MLA_NOPE], qn_norm)
    q_rope = apply_rope(rmsnorm(q[..., MLA_NOPE:], qr_norm), cos[:, None], sin[:, None])
    lat_new = rmsnorm(kv_a, kv_a_norm)
    kr_new = apply_rope(rmsnorm(k_r, kr_norm), cos, sin)
    if lat_past is None:
        lat_all, kr_all = lat_new, kr_new
    else:
        lat_all = jnp.concatenate([lat_past.astype(lat_new.dtype), lat_new], axis=1)
        kr_all = jnp.concatenate([kr_past.astype(kr_new.dtype), kr_new], axis=1)
    Tk = lat_all.shape[1]
    kv = (lat_all @ kv_b_w).reshape(b, Tk, MLA_HEADS, MLA_NOPE + MLA_V)
    k_nope = rmsnorm(kv[..., :MLA_NOPE], kn_norm)
    v = kv[..., MLA_NOPE:]
    o = chunk_causal_attention(q_nope, q_rope, k_nope, kr_all, v, q_pos, jnp.arange(Tk))
    o = o.reshape(b, T, MLA_WIDTH) * jax.nn.silu(gate)
    return (o @ out_w).astype(hn.dtype), lat_new, kr_new


def ple(h, p, up_w, norm_w, gate_w):
    g = jax.nn.sigmoid((rmsnorm(h, norm_w) @ gate_w).astype(F32)).astype(h.dtype)
    return (p @ up_w).astype(h.dtype) * g


def setup_inputs(seed: int = 0) -> dict:
    key = jax.random.key(seed)
    ks = iter(jax.random.split(key, 48))
    nrm = lambda shape, scale: jax.random.normal(next(ks), shape, F32) * scale
    gain = lambda shape: 1.0 + 0.05 * jax.random.normal(next(ks), shape, F32)
    u = jax.random.uniform(next(ks), (N_SSD, SSD_HEADS), F32)
    dt0 = jnp.exp(u * (math.log(0.1) - math.log(0.001)) + math.log(0.001))
    dt_bias = dt0 + jnp.log(-jnp.expm1(-dt0))
    A_log = jnp.log(jax.random.uniform(next(ks), (N_SSD, SSD_HEADS), F32, 1.0, 16.0))
    return {
        "x_prompt": nrm((BATCH, SEQ, D_MODEL), 1.0),
        "x_sample": nrm((DEC_BATCH, DEC_SEQ, D_MODEL), 1.0),
        "cache_conv": nrm((N_SSD, DEC_BATCH, SSD_CONV_W - 1, SSD_CONV_DIM), 1.0),
        "state_ssm": nrm((N_SSD, DEC_BATCH, SSD_HEADS, SSD_HEAD_DIM, SSD_STATE), 0.1),
        "cache_kv_latent": nrm((N_MLA, DEC_BATCH, PAST_LEN, MLA_KV_LORA), 1.0),
        "cache_k_rope": nrm((N_MLA, DEC_BATCH, PAST_LEN, MLA_ROPE), 1.0),
        "p_prompt": nrm((DEPTH, BATCH, SEQ, PLE_DIM), 1.0),
        "p_sample": nrm((DEPTH, DEC_BATCH, DEC_SEQ, PLE_DIM), 1.0),
        "ln_w": gain((DEPTH, D_MODEL)),
        "ssd_in_w": nrm((N_SSD, D_MODEL, SSD_IN), D_MODEL ** -0.5),
        "ssd_conv_w": nrm((N_SSD, SSD_CONV_W, SSD_CONV_DIM), SSD_CONV_W ** -0.5),
        "ssd_conv_b": nrm((N_SSD, SSD_CONV_DIM), 0.02),
        "ssd_dt_bias": dt_bias,
        "ssd_A_log": A_log,
        "ssd_D": gain((N_SSD, SSD_HEADS)),
        "ssd_norm_w": gain((N_SSD, SSD_D_INNER)),
        "ssd_out_w": nrm((N_SSD, SSD_D_INNER, D_MODEL), SSD_D_INNER ** -0.5),
        "mla_in_w": nrm((N_MLA, D_MODEL, MLA_IN), D_MODEL ** -0.5),
        "mla_q_a_norm": gain((N_MLA, MLA_Q_LORA)),
        "mla_q_b_w": nrm((N_MLA, MLA_Q_LORA, MLA_HEADS * MLA_QK), MLA_Q_LORA ** -0.5),
        "mla_kv_a_norm": gain((N_MLA, MLA_KV_LORA)),
        "mla_kv_b_w": nrm((N_MLA, MLA_KV_LORA, MLA_HEADS * (MLA_NOPE + MLA_V)), MLA_KV_LORA ** -0.5),
        "mla_q_nope_norm": gain((N_MLA, MLA_NOPE)),
        "mla_q_rope_norm": gain((N_MLA, MLA_ROPE)),
        "mla_k_nope_norm": gain((N_MLA, MLA_NOPE)),
        "mla_k_rope_norm": gain((N_MLA, MLA_ROPE)),
        "mla_out_w": nrm((N_MLA, MLA_WIDTH, D_MODEL), MLA_WIDTH ** -0.5),
        "ple_up_w": nrm((DEPTH, PLE_DIM, D_MODEL), PLE_DIM ** -0.5),
        "ple_norm_w": gain((DEPTH, D_MODEL)),
        "ple_gate_w": nrm((DEPTH, D_MODEL, D_MODEL), D_MODEL ** -0.5),
    }


def reference(x_prompt, x_sample, cache_conv, state_ssm, cache_kv_latent, cache_k_rope,
              p_prompt, p_sample, ln_w, ssd_in_w, ssd_conv_w, ssd_conv_b, ssd_dt_bias, ssd_A_log,
              ssd_D, ssd_norm_w, ssd_out_w, mla_in_w, mla_q_a_norm, mla_q_b_w, mla_kv_a_norm,
              mla_kv_b_w, mla_q_nope_norm, mla_q_rope_norm, mla_k_nope_norm, mla_k_rope_norm,
              mla_out_w, ple_up_w, ple_norm_w, ple_gate_w):
    hp, hs = x_prompt, x_sample
    conv_p, ssm_p, lat_p, kr_p = [], [], [], []
    conv_s, ssm_s, lat_s, kr_s = [], [], [], []
    for i in range(DEPTH):
        j = i // N_MIXERS
        hn_p = rmsnorm(hp, ln_w[i])
        hn_s = rmsnorm(hs, ln_w[i])
        if i % N_MIXERS == 0:
            w = (ssd_in_w[j], ssd_conv_w[j], ssd_conv_b[j], ssd_dt_bias[j], ssd_A_log[j],
                 ssd_D[j], ssd_norm_w[j], ssd_out_w[j])
            b = hp.shape[0]
            conv0 = jnp.zeros((b, SSD_CONV_W - 1, SSD_CONV_DIM), hp.dtype)
            h0 = jnp.zeros((b, SSD_HEADS, SSD_HEAD_DIM, SSD_STATE), F32)
            yp, cp, sp = ssd_mixer(hn_p, conv0, h0, SSD_CHUNK, *w)
            ys, cs, ss = ssd_mixer(hn_s, cache_conv[j], state_ssm[j], hs.shape[1], *w)
            conv_p.append(cp); ssm_p.append(sp); conv_s.append(cs); ssm_s.append(ss)
        else:
            w = (mla_in_w[j], mla_q_a_norm[j], mla_q_b_w[j], mla_kv_a_norm[j], mla_kv_b_w[j],
                 mla_q_nope_norm[j], mla_q_rope_norm[j], mla_k_nope_norm[j], mla_k_rope_norm[j],
                 mla_out_w[j])
            yp, lp, rp = mla_mixer(hn_p, None, None, 0, *w)
            ys, ls, rs = mla_mixer(hn_s, cache_kv_latent[j], cache_k_rope[j], PAST_LEN, *w)
            lat_p.append(lp); kr_p.append(rp); lat_s.append(ls); kr_s.append(rs)
        hp = hp + yp
        hs = hs + ys
        hp = hp + ple(hp, p_prompt[i], ple_up_w[i], ple_norm_w[i], ple_gate_w[i])
        hs = hs + ple(hs, p_sample[i], ple_up_w[i], ple_norm_w[i], ple_gate_w[i])
    return (hp, hs, jnp.stack(conv_p), jnp.stack(ssm_p), jnp.stack(lat_p), jnp.stack(kr_p),
            jnp.stack(conv_s), jnp.stack(ssm_s), jnp.stack(lat_s), jnp.stack(kr_s))
```

```python
import functools
import math

import jax
import jax.numpy as jnp
from jax import lax
from jax.experimental import pallas as pl
from jax.experimental.pallas import tpu as pltpu

F32 = jnp.float32
BF16 = jnp.bfloat16

D_MODEL = 1024
DEPTH = 4
CHUNK = 64
PLE_DIM = 256
EPS = 1e-6

SSD_D_INNER = 2 * D_MODEL
SSD_HEAD_DIM = 64
SSD_HEADS = SSD_D_INNER // SSD_HEAD_DIM
SSD_GROUPS = 8
SSD_HPG = SSD_HEADS // SSD_GROUPS
SSD_STATE = 128
SSD_CONV_W = 4
SSD_BC = SSD_GROUPS * SSD_STATE
SSD_CONV_DIM = SSD_D_INNER + 2 * SSD_BC
SSD_GROUP_W = SSD_HPG * SSD_HEAD_DIM

MLA_HEADS = 16
MLA_NOPE = 64
MLA_ROPE = 32
MLA_V = 64
MLA_Q_LORA = 384
MLA_KV_LORA = 256
MLA_QK = MLA_NOPE + MLA_ROPE
MLA_WIDTH = MLA_HEADS * MLA_V
ROPE_BASE = 10000.0

LANES = 128
SUBLANES = 8
HEAD_PAD = LANES
VMEM_LIMIT_BYTES = 52 * 1024 * 1024
NEG_BIG = -1e30
COL_CHUNK = 512


def _cparams(*sem):
    return pltpu.CompilerParams(dimension_semantics=sem, vmem_limit_bytes=VMEM_LIMIT_BYTES)


def _const_spec(shape):
    nd = len(shape)
    return pl.BlockSpec(shape, lambda *_: (0,) * nd, pipeline_mode=pl.Buffered(1))


def _rms(x, w):
    return x * lax.rsqrt(jnp.mean(x * x, axis=-1, keepdims=True) + EPS) * w


def _dot(a, b):
    return jnp.dot(a, b, preferred_element_type=F32)


def _dot_nt(a, b):
    return lax.dot_general(a, b, (((1,), (1,)), ((), ())), preferred_element_type=F32)


def _dot_tn(a, b):
    return lax.dot_general(a, b, (((0,), (0,)), ((), ())), preferred_element_type=F32)


def _store_dot(o_ref, a, w_ref, col0=0):
    width = o_ref.shape[1]
    for c in range(0, width, COL_CHUNK):
        cw = min(COL_CHUNK, width - c)
        o_ref[:, c:c + cw] = _dot(a, w_ref[:, col0 + c:col0 + c + cw]).astype(o_ref.dtype)


def _norm_proj_kernel(x_ref, lnw_ref, w_ref, *o_refs):
    xn = _rms(x_ref[...], lnw_ref[...]).astype(BF16)
    col = 0
    for o_ref in o_refs:
        _store_dot(o_ref, xn, w_ref, col)
        col += o_ref.shape[1]


def _norm_proj(x, lnw, w, widths, tm):
    m, d = x.shape
    n = w.shape[1]
    assert sum(widths) == n and m % tm == 0
    return pl.pallas_call(
        _norm_proj_kernel,
        grid=(m // tm,),
        in_specs=[pl.BlockSpec((tm, d), lambda i: (i, 0)), _const_spec((1, d)), _const_spec((d, n))],
        out_specs=[pl.BlockSpec((tm, wd), lambda i: (i, 0)) for wd in widths],
        out_shape=[jax.ShapeDtypeStruct((m, wd), F32) for wd in widths],
        compiler_params=_cparams("arbitrary"),
        name="norm_proj",
    )(x, lnw, w)


def _dot_exact_lhs(a, x):
    x1 = x.astype(BF16)
    r1 = x - x1.astype(F32)
    x2 = r1.astype(BF16)
    x3 = (r1 - x2.astype(F32)).astype(BF16)
    return _dot(a, x1) + _dot(a, x2) + _dot(a, x3)


def _expand_heads(cols, g, lane_head):
    shape = (cols.shape[0], SSD_GROUP_W)
    h0 = SSD_HPG * g
    out = jnp.broadcast_to(cols[:, h0 + SSD_HPG - 1:h0 + SSD_HPG], shape)
    for r in range(SSD_HPG - 2, -1, -1):
        out = jnp.where(lane_head[:shape[0]] == r, jnp.broadcast_to(cols[:, h0 + r:h0 + r + 1], shape), out)
    return out


def _ssd_kernel(*refs, L, has_init):
    if has_init:
        (z_ref, x_ref, bc_ref, dt_ref, cprev_ref, h0_ref, convw_ref, convb_ref, dtb_ref, alog_ref,
         dexp_ref, normw_ref, yn_ref, convnew_ref, ht_ref, ext, xc, st) = refs
    else:
        (z_ref, x_ref, bc_ref, dt_ref, convw_ref, convb_ref, dtb_ref, alog_ref,
         dexp_ref, normw_ref, yn_ref, convnew_ref, ht_ref, ext, xc, st) = refs
    c = pl.program_id(1)
    last = pl.num_programs(1) - 1
    tail = SSD_CONV_W - 1

    @pl.when(c == 0)
    def _init():
        ext[0:SUBLANES, :] = jnp.zeros((SUBLANES, SSD_CONV_DIM), F32)
        if has_init:
            ext[SUBLANES - tail:SUBLANES, :] = cprev_ref[0]
            for g in range(SSD_GROUPS):
                st[g] = h0_ref[0, g * SSD_GROUP_W:(g + 1) * SSD_GROUP_W, :].T
        else:
            st[...] = jnp.zeros(st.shape, F32)

    ext[SUBLANES:SUBLANES + L, 0:SSD_D_INNER] = x_ref[...]
    ext[SUBLANES:SUBLANES + L, SSD_D_INNER:SSD_CONV_DIM] = bc_ref[...]
    for cc in range(0, SSD_CONV_DIM, COL_CHUNK):
        a = convb_ref[:, cc:cc + COL_CHUNK]
        for k in range(SSD_CONV_W):
            a = a + ext[pl.ds(SUBLANES - tail + k, L), cc:cc + COL_CHUNK] * convw_ref[k:k + 1, cc:cc + COL_CHUNK]
        xc[:, cc:cc + COL_CHUNK] = a * jax.nn.sigmoid(a)

    @pl.when(c == last)
    def _conv_out():
        convnew_ref[0] = ext[pl.ds(SUBLANES + L - tail, tail), :]

    ext[0:SUBLANES, :] = ext[L:L + SUBLANES, :]

    dt = jax.nn.softplus(dt_ref[...] + dtb_ref[...])
    dta = dt * (-jnp.exp(alog_ref[...]))
    row = lax.broadcasted_iota(jnp.int32, (L, L), 0)
    col = lax.broadcasted_iota(jnp.int32, (L, L), 1)
    tri = row >= col
    acs = _dot_exact_lhs(tri.astype(BF16), dta)
    acs_t = acs.T
    dt_t = dt.T
    a_last = acs[L - 1:L, :]
    e_acs = jnp.exp(acs)
    wdt = jnp.exp(a_last - acs) * dt
    cdec = jnp.exp(a_last)
    lane_head = lax.broadcasted_iota(jnp.int32, (L, SSD_GROUP_W), 1) // SSD_HEAD_DIM

    for g in range(SSD_GROUPS):
        gs = slice(g * SSD_GROUP_W, (g + 1) * SSD_GROUP_W)
        xg = xc[:, gs]
        bg = xc[:, SSD_D_INNER + g * SSD_STATE:SSD_D_INNER + (g + 1) * SSD_STATE].astype(BF16)
        cg = xc[:, SSD_D_INNER + SSD_BC + g * SSD_STATE:SSD_D_INNER + SSD_BC + (g + 1) * SSD_STATE].astype(BF16)
        cb = _dot_nt(cg, bg)
        sg = st[g]
        y = _dot(cg, sg.astype(BF16)) * _expand_heads(e_acs, g, lane_head)
        for r in range(SSD_HPG):
            h = SSD_HPG * g + r
            seg = acs[:, h:h + 1] - acs_t[h:h + 1, :]
            m = cb * jnp.exp(jnp.where(tri, seg, -jnp.inf)) * dt_t[h:h + 1, :]
            xh = jnp.where(lane_head == r, xg, 0.0).astype(BF16)
            y = y + _dot(m.astype(BF16), xh)
        y = y + dexp_ref[:, gs] * xg
        zg = z_ref[:, gs]
        y = y * (zg * jax.nn.sigmoid(zg))
        yn_ref[:, gs] = _rms(y, normw_ref[:, gs]).astype(yn_ref.dtype)
        xw = (xg * _expand_heads(wdt, g, lane_head)).astype(BF16)
        st[g] = sg * _expand_heads(cdec, g, lane_head) + _dot_tn(bg, xw)

    @pl.when(c == last)
    def _state_out():
        for g in range(SSD_GROUPS):
            ht_ref[0, g * SSD_GROUP_W:(g + 1) * SSD_GROUP_W, :] = st[g].T


def _ssd_scan(z, x, bc, dt, conv_prev, h0, w, batch, seq, L):
    nc = seq // L
    assert nc * L == seq and L % SUBLANES == 0
    has_init = conv_prev is not None
    tail = SSD_CONV_W - 1
    row_spec = lambda wd: pl.BlockSpec((L, wd), lambda b, c: (b * nc + c, 0))
    in_specs = [row_spec(SSD_D_INNER), row_spec(SSD_D_INNER), row_spec(2 * SSD_BC), row_spec(LANES)]
    args = [z, x, bc, dt]
    if has_init:
        in_specs += [pl.BlockSpec((1, tail, SSD_CONV_DIM), lambda b, c: (b, 0, 0)),
                     pl.BlockSpec((1, SSD_D_INNER, SSD_STATE), lambda b, c: (b, 0, 0))]
        args += [conv_prev, h0.reshape(batch, SSD_D_INNER, SSD_STATE)]
    consts = [w["conv_w"], w["conv_b"], w["dt_bias"], w["a_log"], w["d_exp"], w["norm_w"]]
    in_specs += [_const_spec(a.shape) for a in consts]
    args += consts
    return pl.pallas_call(
        functools.partial(_ssd_kernel, L=L, has_init=has_init),
        grid=(batch, nc),
        in_specs=in_specs,
        out_specs=[row_spec(SSD_D_INNER),
                   pl.BlockSpec((1, tail, SSD_CONV_DIM), lambda b, c: (b, 0, 0)),
                   pl.BlockSpec((1, SSD_D_INNER, SSD_STATE), lambda b, c: (b, 0, 0))],
        out_shape=[jax.ShapeDtypeStruct((batch * seq, SSD_D_INNER), BF16),
                   jax.ShapeDtypeStruct((batch, tail, SSD_CONV_DIM), F32),
                   jax.ShapeDtypeStruct((batch, SSD_D_INNER, SSD_STATE), F32)],
        scratch_shapes=[pltpu.VMEM((L + SUBLANES, SSD_CONV_DIM), F32),
                        pltpu.VMEM((L, SSD_CONV_DIM), F32),
                        pltpu.VMEM((SSD_GROUPS, SSD_STATE, SSD_GROUP_W), F32)],
        compiler_params=_cparams("arbitrary", "arbitrary"),
        name="ssd_scan",
    )(*args)


def _out_ple_kernel(y_ref, h_ref, p_ref, wout_ref, plenorm_ref, wgate_ref, wup_ref, o_ref):
    h1 = h_ref[...] + _dot(y_ref[...], wout_ref[...])
    hn = _rms(h1, plenorm_ref[...]).astype(BF16)
    pb = p_ref[...].astype(BF16)
    for c in range(0, D_MODEL, COL_CHUNK):
        cs = slice(c, c + COL_CHUNK)
        gate = jax.nn.sigmoid(_dot(hn, wgate_ref[:, cs]))
        o_ref[:, cs] = h1[:, cs] + _dot(pb, wup_ref[:, cs]) * gate


def _out_ple(y, h, p, wout, plenorm, wgate, wup, tm):
    m, kd = y.shape
    assert m % tm == 0
    return pl.pallas_call(
        _out_ple_kernel,
        grid=(m // tm,),
        in_specs=[pl.BlockSpec((tm, kd), lambda i: (i, 0)),
                  pl.BlockSpec((tm, D_MODEL), lambda i: (i, 0)),
                  pl.BlockSpec((tm, PLE_DIM), lambda i: (i, 0)),
                  _const_spec(wout.shape), _const_spec(plenorm.shape),
                  _const_spec(wgate.shape), _const_spec(wup.shape)],
        out_specs=pl.BlockSpec((tm, D_MODEL), lambda i: (i, 0)),
        out_shape=jax.ShapeDtypeStruct((m, D_MODEL), F32),
        compiler_params=_cparams("arbitrary"),
        name="out_ple",
    )(y, h, p, wout, plenorm, wgate, wup)


def _head_norm_rope(x, gain, cos_t, sin_lo, sin_hi):
    lane = lax.broadcasted_iota(jnp.int32, x.shape, 1)
    is_nope = lane < MLA_NOPE
    x2 = x * x
    ms_n = jnp.sum(jnp.where(is_nope, x2, 0.0), axis=-1, keepdims=True) * (1.0 / MLA_NOPE)
    ms_r = jnp.sum(jnp.where(is_nope, 0.0, x2), axis=-1, keepdims=True) * (1.0 / MLA_ROPE)
    xn = x * jnp.where(is_nope, lax.rsqrt(ms_n + EPS), lax.rsqrt(ms_r + EPS)) * gain
    half = MLA_ROPE // 2
    return xn * cos_t + pltpu.roll(xn, LANES - half, 1) * sin_lo + pltpu.roll(xn, half, 1) * sin_hi


def _mla_in_kernel(h_ref, lnw_ref, win_ref, qan_ref, wqb_ref, kvan_ref, gq_ref, gkr_ref,
                   cos_ref, slo_ref, shi_ref, q_ref, lat_ref, kr_ref, gate_ref):
    xn = _rms(h_ref[...], lnw_ref[...]).astype(BF16)
    c_q, c_kv, c_kr = 0, MLA_Q_LORA, MLA_Q_LORA + MLA_KV_LORA
    c_gate = c_kr + HEAD_PAD
    _store_dot(gate_ref, xn, win_ref, c_gate)
    lat_ref[...] = _rms(_dot(xn, win_ref[:, c_kv:c_kr]), kvan_ref[...])
    cos_t, sin_lo, sin_hi = cos_ref[...], slo_ref[...], shi_ref[...]
    kr = _head_norm_rope(_dot(xn, win_ref[:, c_kr:c_gate]), gkr_ref[...], cos_t, sin_lo, sin_hi)
    kr_ref[...] = kr[:, MLA_NOPE:MLA_QK]
    qa = _rms(_dot(xn, win_ref[:, c_q:c_kv]), qan_ref[...]).astype(BF16)
    for h in range(MLA_HEADS):
        hs = slice(h * HEAD_PAD, (h + 1) * HEAD_PAD)
        qh = _head_norm_rope(_dot(qa, wqb_ref[:, hs]), gq_ref[...], cos_t, sin_lo, sin_hi)
        q_ref[:, hs] = qh.astype(q_ref.dtype)


def _mla_in(h, lnw, w, rope, tm):
    m = h.shape[0]
    assert m % tm == 0
    cos_t, sin_lo, sin_hi = rope
    period = cos_t.shape[0] // tm
    assert period * tm == cos_t.shape[0]
    rope_spec = pl.BlockSpec((tm, HEAD_PAD), lambda i: (i % period, 0))
    consts = [lnw, w["in_w"], w["q_a_norm"], w["q_b_w"], w["kv_a_norm"], w["gain_q"], w["gain_kr"]]
    row = lambda wd: pl.BlockSpec((tm, wd), lambda i: (i, 0))
    return pl.pallas_call(
        _mla_in_kernel,
        grid=(m // tm,),
        in_specs=[row(D_MODEL)] + [_const_spec(a.shape) for a in consts] + [rope_spec] * 3,
        out_specs=[row(MLA_HEADS * HEAD_PAD), row(MLA_KV_LORA), row(MLA_ROPE), row(MLA_WIDTH)],
        out_shape=[jax.ShapeDtypeStruct((m, MLA_HEADS * HEAD_PAD), BF16),
                   jax.ShapeDtypeStruct((m, MLA_KV_LORA), F32),
                   jax.ShapeDtypeStruct((m, MLA_ROPE), F32),
                   jax.ShapeDtypeStruct((m, MLA_WIDTH), F32)],
        compiler_params=_cparams("arbitrary"),
        name="mla_in",
    )(h, *consts, cos_t, sin_lo, sin_hi)


def _mla_kv_kernel(lat_ref, kr_ref, wk_ref, wv_ref, gkn_ref, k_ref, v_ref):
    latb = lat_ref[...].astype(BF16)
    _store_dot(v_ref, latb, wv_ref)
    rows = latb.shape[0]
    kr = jnp.concatenate([jnp.zeros((rows, MLA_NOPE), F32), kr_ref[...],
                          jnp.zeros((rows, HEAD_PAD - MLA_QK), F32)], axis=1)
    for h in range(MLA_HEADS):
        hs = slice(h * HEAD_PAD, (h + 1) * HEAD_PAD)
        kh = _dot(latb, wk_ref[:, hs])
        ms = jnp.sum(kh * kh, axis=-1, keepdims=True) * (1.0 / MLA_NOPE)
        k_ref[:, hs] = (kh * lax.rsqrt(ms + EPS) * gkn_ref[...] + kr).astype(k_ref.dtype)


def _mla_kv(lat, kr, w, tm):
    m = lat.shape[0]
    assert m % tm == 0
    consts = [w["kv_k_w"], w["kv_v_w"], w["gain_kn"]]
    row = lambda wd: pl.BlockSpec((tm, wd), lambda i: (i, 0))
    return pl.pallas_call(
        _mla_kv_kernel,
        grid=(m // tm,),
        in_specs=[row(MLA_KV_LORA), row(MLA_ROPE)] + [_const_spec(a.shape) for a in consts],
        out_specs=[row(MLA_HEADS * HEAD_PAD), row(MLA_WIDTH)],
        out_shape=[jax.ShapeDtypeStruct((m, MLA_HEADS * HEAD_PAD), BF16),
                   jax.ShapeDtypeStruct((m, MLA_WIDTH), BF16)],
        compiler_params=_cparams("arbitrary"),
        name="mla_kv",
    )(lat, kr, *consts)


def _softmax_step(q, k, v, carry, mask):
    m_prev, l_prev, acc = carry
    s = _dot_nt(q, k) * (MLA_QK ** -0.5)
    if mask is not None:
        s = jnp.where(mask, s, -jnp.inf)
    m_new = jnp.maximum(m_prev, jnp.max(s, axis=-1, keepdims=True))
    alpha = jnp.exp(m_prev - m_new)
    p = jnp.exp(s - m_new)
    l_new = alpha * l_prev + jnp.sum(p, axis=-1, keepdims=True)
    return m_new, l_new, alpha * acc + _dot(p.astype(BF16), v)


def _gated_pair_out(o_ref, gate_ref, outs, pair):
    lane = lax.broadcasted_iota(jnp.int32, outs[0].shape, 1)
    ps = slice(pair * LANES, (pair + 1) * LANES)
    g = gate_ref[:, ps]
    o_ref[:, ps] = (jnp.where(lane < MLA_V, outs[0], outs[1]) * (g * jax.nn.sigmoid(g))).astype(o_ref.dtype)


def _attn_kernel(q_ref, k_ref, v_ref, gate_ref, o_ref, *, tq):
    i = pl.program_id(1)
    row = lax.broadcasted_iota(jnp.int32, (tq, tq), 0) // CHUNK
    col = lax.broadcasted_iota(jnp.int32, (tq, tq), 1) // CHUNK
    diag_mask = col <= row
    init = (jnp.full((tq, 1), NEG_BIG, F32), jnp.zeros((tq, 1), F32), jnp.zeros((tq, LANES), F32))
    for pair in range(MLA_HEADS // 2):
        vs = slice(pair * LANES, (pair + 1) * LANES)
        outs = []
        for sub in range(2):
            hs = slice((2 * pair + sub) * HEAD_PAD, (2 * pair + sub + 1) * HEAD_PAD)
            q = q_ref[:, hs]

            def body(j, carry):
                rows = pl.ds(pl.multiple_of(j * tq, tq), tq)
                return _softmax_step(q, k_ref[rows, hs], v_ref[rows, vs], carry, None)

            carry = lax.fori_loop(0, i, body, init)
            rows = pl.ds(pl.multiple_of(i * tq, tq), tq)
            _, l, acc = _softmax_step(q, k_ref[rows, hs], v_ref[rows, vs], carry, diag_mask)
            outs.append(acc / l)
        _gated_pair_out(o_ref, gate_ref, outs, pair)


def _attn_prompt(q, k, v, gate, batch, seq, tq):
    nq = seq // tq
    assert nq * tq == seq and tq % CHUNK == 0
    return pl.pallas_call(
        functools.partial(_attn_kernel, tq=tq),
        grid=(batch, nq),
        in_specs=[pl.BlockSpec((tq, MLA_HEADS * HEAD_PAD), lambda b, i: (b * nq + i, 0)),
                  pl.BlockSpec((seq, MLA_HEADS * HEAD_PAD), lambda b, i: (b, 0)),
                  pl.BlockSpec((seq, MLA_WIDTH), lambda b, i: (b, 0)),
                  pl.BlockSpec((tq, MLA_WIDTH), lambda b, i: (b * nq + i, 0))],
        out_specs=pl.BlockSpec((tq, MLA_WIDTH), lambda b, i: (b * nq + i, 0)),
        out_shape=jax.ShapeDtypeStruct((batch * seq, MLA_WIDTH), BF16),
        compiler_params=_cparams("arbitrary", "arbitrary"),
        name="attn_prompt",
    )(q, k, v, gate)


def _attn_step_kernel(q_ref, k_ref, v_ref, gate_ref, o_ref, *, q_pos0, n_keys):
    tq = q_ref.shape[0]
    tk = k_ref.shape[1]
    k_pos = lax.broadcasted_iota(jnp.int32, (tq, tk), 1)
    q_pos = lax.broadcasted_iota(jnp.int32, (tq, tk), 0) + q_pos0
    mask = jnp.logical_and(k_pos < n_keys, k_pos // CHUNK <= q_pos // CHUNK)
    init = (jnp.full((tq, 1), NEG_BIG, F32), jnp.zeros((tq, 1), F32), jnp.zeros((tq, LANES), F32))
    for pair in range(MLA_HEADS // 2):
        vs = slice(pair * LANES, (pair + 1) * LANES)
        outs = []
        for sub in range(2):
            hs = slice((2 * pair + sub) * HEAD_PAD, (2 * pair + sub + 1) * HEAD_PAD)
            _, l, acc = _softmax_step(q_ref[:, hs], k_ref[0, :, hs], v_ref[0, :, vs], init, mask)
            outs.append(acc / l)
        _gated_pair_out(o_ref, gate_ref, outs, pair)


def _attn_step(q, k, v, gate, batch, tq, n_keys):
    tk = k.shape[1]
    return pl.pallas_call(
        functools.partial(_attn_step_kernel, q_pos0=n_keys - tq, n_keys=n_keys),
        grid=(batch,),
        in_specs=[pl.BlockSpec((tq, MLA_HEADS * HEAD_PAD), lambda b: (b, 0)),
                  pl.BlockSpec((1, tk, MLA_HEADS * HEAD_PAD), lambda b: (b, 0, 0)),
                  pl.BlockSpec((1, tk, MLA_WIDTH), lambda b: (b, 0, 0)),
                  pl.BlockSpec((tq, MLA_WIDTH), lambda b: (b, 0))],
        out_specs=pl.BlockSpec((tq, MLA_WIDTH), lambda b: (b, 0)),
        out_shape=jax.ShapeDtypeStruct((batch * tq, MLA_WIDTH), BF16),
        compiler_params=_cparams("arbitrary"),
        name="attn_step",
    )(q, k, v, gate)


def _pad_cols(a, width):
    return jnp.pad(a, ((0, 0), (0, width - a.shape[1])))


def _ssd_weights(j, in_w, conv_w, conv_b, dt_bias, a_log, d, norm_w, out_w):
    zx = SSD_D_INNER + SSD_CONV_DIM
    return {
        "in_w": jnp.concatenate([in_w[j][:, :zx], _pad_cols(in_w[j][:, zx:], LANES)], axis=1).astype(BF16),
        "conv_w": conv_w[j],
        "conv_b": conv_b[j][None, :],
        "dt_bias": _pad_cols(dt_bias[j][None, :], LANES),
        "a_log": _pad_cols(a_log[j][None, :], LANES),
        "d_exp": jnp.repeat(d[j], SSD_HEAD_DIM)[None, :],
        "norm_w": norm_w[j][None, :],
        "out_w": out_w[j].astype(BF16),
    }


def _head_pad_cols(w, per_head, take, offset=0):
    k = w.shape[0]
    w = w.reshape(k, MLA_HEADS, per_head)[:, :, offset:offset + take]
    return jnp.pad(w, ((0, 0), (0, 0), (0, HEAD_PAD - take))).reshape(k, MLA_HEADS * HEAD_PAD)


def _mla_weights(j, in_w, q_a_norm, q_b_w, kv_a_norm, kv_b_w, qn, qr, kn, kr, out_w):
    c1, c2, c3 = MLA_Q_LORA, MLA_Q_LORA + MLA_KV_LORA, MLA_Q_LORA + MLA_KV_LORA + MLA_ROPE
    w = in_w[j]
    kr_cols = jnp.pad(w[:, c2:c3], ((0, 0), (MLA_NOPE, HEAD_PAD - MLA_QK)))
    zeros = lambda n: jnp.zeros((n,), F32)
    return {
        "in_w": jnp.concatenate([w[:, :c2], kr_cols, w[:, c3:]], axis=1).astype(BF16),
        "q_a_norm": q_a_norm[j][None, :],
        "q_b_w": _head_pad_cols(q_b_w[j], MLA_QK, MLA_QK).astype(BF16),
        "kv_a_norm": kv_a_norm[j][None, :],
        "kv_k_w": _head_pad_cols(kv_b_w[j], MLA_NOPE + MLA_V, MLA_NOPE).astype(BF16),
        "kv_v_w": kv_b_w[j].reshape(MLA_KV_LORA, MLA_HEADS, MLA_NOPE + MLA_V)[:, :, MLA_NOPE:]
                  .reshape(MLA_KV_LORA, MLA_WIDTH).astype(BF16),
        "gain_q": jnp.concatenate([qn[j], qr[j], zeros(HEAD_PAD - MLA_QK)])[None, :],
        "gain_kr": jnp.concatenate([zeros(MLA_NOPE), kr[j], zeros(HEAD_PAD - MLA_QK)])[None, :],
        "gain_kn": jnp.concatenate([kn[j], zeros(HEAD_PAD - MLA_NOPE)])[None, :],
        "out_w": out_w[j].astype(BF16),
    }


def _rope_tables(pos0, seq, rows):
    half = MLA_ROPE // 2
    inv = 1.0 / (ROPE_BASE ** (jnp.arange(0, MLA_ROPE, 2, dtype=F32) / MLA_ROPE))
    ang = (pos0 + jnp.arange(seq)).astype(F32)[:, None] * inv[None, :]
    cos, sin = jnp.cos(ang), jnp.sin(ang)
    z = lambda n: jnp.zeros((seq, n), F32)
    cos_t = jnp.concatenate([jnp.ones((seq, MLA_NOPE), F32), cos, cos, z(HEAD_PAD - MLA_QK)], axis=1)
    sin_lo = jnp.concatenate([z(MLA_NOPE), -sin, z(HEAD_PAD - MLA_NOPE - half)], axis=1)
    sin_hi = jnp.concatenate([z(MLA_NOPE + half), sin, z(HEAD_PAD - MLA_QK)], axis=1)
    reps = max(1, rows // seq)
    return tuple(jnp.tile(t, (reps, 1)) for t in (cos_t, sin_lo, sin_hi))


def _row_tile(m, want):
    tm = min(m, want)
    assert m % tm == 0
    return tm


def _ssd_layer(h, p, lnw, ple, w, batch, seq, chunk, conv_prev, h0):
    tm = _row_tile(h.shape[0], 256)
    z, x, bc, dt = _norm_proj(h, lnw, w["in_w"], (SSD_D_INNER, SSD_D_INNER, 2 * SSD_BC, LANES), tm)
    yn, conv_new, ht = _ssd_scan(z, x, bc, dt, conv_prev, h0, w, batch, seq, chunk)
    h = _out_ple(yn, h, p, w["out_w"], *ple, tm)
    return h, conv_new, ht.reshape(batch, SSD_HEADS, SSD_HEAD_DIM, SSD_STATE)


def _mla_layer(h, p, lnw, ple, w, batch, seq, lat_past, kr_past):
    tm = _row_tile(h.shape[0], 256)
    pos0 = 0 if lat_past is None else lat_past.shape[1]
    rope = _rope_tables(pos0, seq, tm)
    q, lat, kr, gate = _mla_in(h, lnw, w, rope, tm)
    if lat_past is None:
        k, v = _mla_kv(lat, kr, w, tm)
        og = _attn_prompt(q, k, v, gate, batch, seq, 256)
    else:
        n_keys = pos0 + seq
        tk = -(-n_keys // LANES) * LANES
        pad = ((0, 0), (0, tk - n_keys), (0, 0))
        lat_all = jnp.pad(jnp.concatenate([lat_past, lat.reshape(batch, seq, -1)], axis=1), pad)
        kr_all = jnp.pad(jnp.concatenate([kr_past, kr.reshape(batch, seq, -1)], axis=1), pad)
        k, v = _mla_kv(lat_all.reshape(batch * tk, -1), kr_all.reshape(batch * tk, -1), w,
                       math.gcd(batch * tk, 512))
        og = _attn_step(q, k.reshape(batch, tk, -1), v.reshape(batch, tk, -1), gate, batch, seq, n_keys)
    h = _out_ple(og, h, p, w["out_w"], *ple, tm)
    return h, lat.reshape(batch, seq, -1), kr.reshape(batch, seq, -1)


def kernel(x_prompt, x_sample, cache_conv, state_ssm, cache_kv_latent, cache_k_rope, p_prompt, p_sample, ln_w, ssd_in_w, ssd_conv_w, ssd_conv_b, ssd_dt_bias, ssd_A_log, ssd_D, ssd_norm_w, ssd_out_w, mla_in_w, mla_q_a_norm, mla_q_b_w, mla_kv_a_norm, mla_kv_b_w, mla_q_nope_norm, mla_q_rope_norm, mla_k_nope_norm, mla_k_rope_norm, mla_out_w, ple_up_w, ple_norm_w, ple_gate_w):
    bp, tp, d = x_prompt.shape
    bs, ts, _ = x_sample.shape
    hp = x_prompt.reshape(bp * tp, d)
    hs = x_sample.reshape(bs * ts, d)
    conv_p, ssm_p, lat_p, kr_p = [], [], [], []
    conv_s, ssm_s, lat_s, kr_s = [], [], [], []
    for i in range(DEPTH):
        j = i // 2
        lnw = ln_w[i][None, :]
        ple = (ple_norm_w[i][None, :], ple_gate_w[i].astype(BF16), ple_up_w[i].astype(BF16))
        pp = p_prompt[i].reshape(bp * tp, PLE_DIM)
        ps = p_sample[i].reshape(bs * ts, PLE_DIM)
        if i % 2 == 0:
            w = _ssd_weights(j, ssd_in_w, ssd_conv_w, ssd_conv_b, ssd_dt_bias, ssd_A_log, ssd_D,
                             ssd_norm_w, ssd_out_w)
            hp, cp, sp = _ssd_layer(hp, pp, lnw, ple, w, bp, tp, CHUNK, None, None)
            hs, cs, ss = _ssd_layer(hs, ps, lnw, ple, w, bs, ts, ts, cache_conv[j], state_ssm[j])
            conv_p.append(cp); ssm_p.append(sp); conv_s.append(cs); ssm_s.append(ss)
        else:
            w = _mla_weights(j, mla_in_w, mla_q_a_norm, mla_q_b_w, mla_kv_a_norm, mla_kv_b_w,
                             mla_q_nope_norm, mla_q_rope_norm, mla_k_nope_norm, mla_k_rope_norm, mla_out_w)
            hp, lp, rp = _mla_layer(hp, pp, lnw, ple, w, bp, tp, None, None)
            hs, ls, rs = _mla_layer(hs, ps, lnw, ple, w, bs, ts, cache_kv_latent[j], cache_k_rope[j])
            lat_p.append(lp); kr_p.append(rp); lat_s.append(ls); kr_s.append(rs)
    return (hp.reshape(bp, tp, d), hs.reshape(bs, ts, d),
            jnp.stack(conv_p), jnp.stack(ssm_p), jnp.stack(lat_p), jnp.stack(kr_p),
            jnp.stack(conv_s), jnp.stack(ssm_s), jnp.stack(lat_s), jnp.stack(kr_s))
```

```python
import functools
import math

import jax
import jax.numpy as jnp
from jax import lax
from jax.experimental import pallas as pl
from jax.experimental.pallas import tpu as pltpu

F32 = jnp.float32
BF16 = jnp.bfloat16

D_MODEL = 1024
DEPTH = 4
CHUNK = 64
PLE_DIM = 256
EPS = 1e-6

SSD_D_INNER = 2 * D_MODEL
SSD_HEAD_DIM = 64
SSD_HEADS = SSD_D_INNER // SSD_HEAD_DIM
SSD_GROUPS = 8
SSD_HPG = SSD_HEADS // SSD_GROUPS
SSD_STATE = 128
SSD_CONV_W = 4
SSD_BC = SSD_GROUPS * SSD_STATE
SSD_CONV_DIM = SSD_D_INNER + 2 * SSD_BC
SSD_GROUP_W = SSD_HPG * SSD_HEAD_DIM

MLA_HEADS = 16
MLA_NOPE = 64
MLA_ROPE = 32
MLA_V = 64
MLA_Q_LORA = 384
MLA_KV_LORA = 256
MLA_QK = MLA_NOPE + MLA_ROPE
MLA_WIDTH = MLA_HEADS * MLA_V
ROPE_BASE = 10000.0

LANES = 128
SUBLANES = 8
HEAD_PAD = LANES
VMEM_LIMIT_BYTES = 52 * 1024 * 1024
NEG_BIG = -1e30
SOFTMAX_LOG2_SCALE = (MLA_QK ** -0.5) * math.log2(math.e)
COL_CHUNK = 512


def _cparams(*sem):
    return pltpu.CompilerParams(dimension_semantics=sem, vmem_limit_bytes=VMEM_LIMIT_BYTES)


def _const_spec(shape):
    nd = len(shape)
    return pl.BlockSpec(shape, lambda *_: (0,) * nd, pipeline_mode=pl.Buffered(1))


def _rms(x, w):
    return x * lax.rsqrt(jnp.mean(x * x, axis=-1, keepdims=True) + EPS) * w


def _dot(a, b):
    return jnp.dot(a, b, preferred_element_type=F32)


def _dot_nt(a, b):
    return lax.dot_general(a, b, (((1,), (1,)), ((), ())), preferred_element_type=F32)


def _dot_tn(a, b):
    return lax.dot_general(a, b, (((0,), (0,)), ((), ())), preferred_element_type=F32)


def _store_dot(o_ref, a, w_ref, col0=0):
    width = o_ref.shape[1]
    for c in range(0, width, COL_CHUNK):
        cw = min(COL_CHUNK, width - c)
        o_ref[:, c:c + cw] = _dot(a, w_ref[:, col0 + c:col0 + c + cw]).astype(o_ref.dtype)


def _norm_proj_kernel(x_ref, lnw_ref, w_ref, *o_refs):
    xn = _rms(x_ref[...], lnw_ref[...]).astype(BF16)
    col = 0
    for o_ref in o_refs:
        _store_dot(o_ref, xn, w_ref, col)
        col += o_ref.shape[1]


def _norm_proj(x, lnw, w, widths, tm):
    m, d = x.shape
    n = w.shape[1]
    assert sum(widths) == n and m % tm == 0
    return pl.pallas_call(
        _norm_proj_kernel,
        grid=(m // tm,),
        in_specs=[pl.BlockSpec((tm, d), lambda i: (i, 0)), _const_spec((1, d)), _const_spec((d, n))],
        out_specs=[pl.BlockSpec((tm, wd), lambda i: (i, 0)) for wd in widths],
        out_shape=[jax.ShapeDtypeStruct((m, wd), F32) for wd in widths],
        compiler_params=_cparams("arbitrary"),
        name="norm_proj",
    )(x, lnw, w)


def _dot_exact_lhs(a, x):
    x1 = x.astype(BF16)
    r1 = x - x1.astype(F32)
    x2 = r1.astype(BF16)
    x3 = (r1 - x2.astype(F32)).astype(BF16)
    return _dot(a, x1) + _dot(a, x2) + _dot(a, x3)


def _expand_heads(cols, g, lane_head):
    shape = (cols.shape[0], SSD_GROUP_W)
    h0 = SSD_HPG * g
    out = jnp.broadcast_to(cols[:, h0 + SSD_HPG - 1:h0 + SSD_HPG], shape)
    for r in range(SSD_HPG - 2, -1, -1):
        out = jnp.where(lane_head[:shape[0]] == r, jnp.broadcast_to(cols[:, h0 + r:h0 + r + 1], shape), out)
    return out


def _ssd_kernel(*refs, L, has_init):
    if has_init:
        (z_ref, x_ref, bc_ref, dt_ref, cprev_ref, h0_ref, convw_ref, convb_ref, dtb_ref, alog_ref,
         dexp_ref, normw_ref, yn_ref, convnew_ref, ht_ref, ext, xc, st) = refs
    else:
        (z_ref, x_ref, bc_ref, dt_ref, convw_ref, convb_ref, dtb_ref, alog_ref,
         dexp_ref, normw_ref, yn_ref, convnew_ref, ht_ref, ext, xc, st) = refs
    c = pl.program_id(1)
    last = pl.num_programs(1) - 1
    tail = SSD_CONV_W - 1

    @pl.when(c == 0)
    def _init():
        ext[0:SUBLANES, :] = jnp.zeros((SUBLANES, SSD_CONV_DIM), F32)
        if has_init:
            ext[SUBLANES - tail:SUBLANES, :] = cprev_ref[0]
            for g in range(SSD_GROUPS):
                st[g] = h0_ref[0, g * SSD_GROUP_W:(g + 1) * SSD_GROUP_W, :].T
        else:
            st[...] = jnp.zeros(st.shape, F32)

    ext[SUBLANES:SUBLANES + L, 0:SSD_D_INNER] = x_ref[...]
    ext[SUBLANES:SUBLANES + L, SSD_D_INNER:SSD_CONV_DIM] = bc_ref[...]
    for cc in range(0, SSD_CONV_DIM, COL_CHUNK):
        a = convb_ref[:, cc:cc + COL_CHUNK]
        for k in range(SSD_CONV_W):
            a = a + ext[pl.ds(SUBLANES - tail + k, L), cc:cc + COL_CHUNK] * convw_ref[k:k + 1, cc:cc + COL_CHUNK]
        xc[:, cc:cc + COL_CHUNK] = a * jax.nn.sigmoid(a)

    @pl.when(c == last)
    def _conv_out():
        convnew_ref[0] = ext[pl.ds(SUBLANES + L - tail, tail), :]

    ext[0:SUBLANES, :] = ext[L:L + SUBLANES, :]

    dt = jax.nn.softplus(dt_ref[...] + dtb_ref[...])
    dta = dt * (-jnp.exp(alog_ref[...]))
    row = lax.broadcasted_iota(jnp.int32, (L, L), 0)
    col = lax.broadcasted_iota(jnp.int32, (L, L), 1)
    tri = row >= col
    acs = _dot_exact_lhs(tri.astype(BF16), dta)
    acs_t = acs.T
    dt_t = dt.T
    a_last = acs[L - 1:L, :]
    e_acs = jnp.exp(acs)
    wdt = jnp.exp(a_last - acs) * dt
    cdec = jnp.exp(a_last)
    lane_head = lax.broadcasted_iota(jnp.int32, (L, SSD_GROUP_W), 1) // SSD_HEAD_DIM

    for g in range(SSD_GROUPS):
        gs = slice(g * SSD_GROUP_W, (g + 1) * SSD_GROUP_W)
        xg = xc[:, gs]
        bg = xc[:, SSD_D_INNER + g * SSD_STATE:SSD_D_INNER + (g + 1) * SSD_STATE].astype(BF16)
        cg = xc[:, SSD_D_INNER + SSD_BC + g * SSD_STATE:SSD_D_INNER + SSD_BC + (g + 1) * SSD_STATE].astype(BF16)
        cb = _dot_nt(cg, bg)
        sg = st[g]
        y = _dot(cg, sg.astype(BF16)) * _expand_heads(e_acs, g, lane_head)
        for r in range(SSD_HPG):
            h = SSD_HPG * g + r
            seg = acs[:, h:h + 1] - acs_t[h:h + 1, :]
            m = cb * jnp.exp(jnp.where(tri, seg, -jnp.inf)) * dt_t[h:h + 1, :]
            xh = jnp.where(lane_head == r, xg, 0.0).astype(BF16)
            y = y + _dot(m.astype(BF16), xh)
        y = y + dexp_ref[:, gs] * xg
        zg = z_ref[:, gs]
        y = y * (zg * jax.nn.sigmoid(zg))
        yn_ref[:, gs] = _rms(y, normw_ref[:, gs]).astype(yn_ref.dtype)
        xw = (xg * _expand_heads(wdt, g, lane_head)).astype(BF16)
        st[g] = sg * _expand_heads(cdec, g, lane_head) + _dot_tn(bg, xw)

    @pl.when(c == last)
    def _state_out():
        for g in range(SSD_GROUPS):
            ht_ref[0, g * SSD_GROUP_W:(g + 1) * SSD_GROUP_W, :] = st[g].T


def _ssd_scan(z, x, bc, dt, conv_prev, h0, w, batch, seq, L):
    nc = seq // L
    assert nc * L == seq and L % SUBLANES == 0
    has_init = conv_prev is not None
    tail = SSD_CONV_W - 1
    row_spec = lambda wd: pl.BlockSpec((L, wd), lambda b, c: (b * nc + c, 0))
    in_specs = [row_spec(SSD_D_INNER), row_spec(SSD_D_INNER), row_spec(2 * SSD_BC), row_spec(LANES)]
    args = [z, x, bc, dt]
    if has_init:
        in_specs += [pl.BlockSpec((1, tail, SSD_CONV_DIM), lambda b, c: (b, 0, 0)),
                     pl.BlockSpec((1, SSD_D_INNER, SSD_STATE), lambda b, c: (b, 0, 0))]
        args += [conv_prev, h0.reshape(batch, SSD_D_INNER, SSD_STATE)]
    consts = [w["conv_w"], w["conv_b"], w["dt_bias"], w["a_log"], w["d_exp"], w["norm_w"]]
    in_specs += [_const_spec(a.shape) for a in consts]
    args += consts
    return pl.pallas_call(
        functools.partial(_ssd_kernel, L=L, has_init=has_init),
        grid=(batch, nc),
        in_specs=in_specs,
        out_specs=[row_spec(SSD_D_INNER),
                   pl.BlockSpec((1, tail, SSD_CONV_DIM), lambda b, c: (b, 0, 0)),
                   pl.BlockSpec((1, SSD_D_INNER, SSD_STATE), lambda b, c: (b, 0, 0))],
        out_shape=[jax.ShapeDtypeStruct((batch * seq, SSD_D_INNER), BF16),
                   jax.ShapeDtypeStruct((batch, tail, SSD_CONV_DIM), F32),
                   jax.ShapeDtypeStruct((batch, SSD_D_INNER, SSD_STATE), F32)],
        scratch_shapes=[pltpu.VMEM((L + SUBLANES, SSD_CONV_DIM), F32),
                        pltpu.VMEM((L, SSD_CONV_DIM), F32),
                        pltpu.VMEM((SSD_GROUPS, SSD_STATE, SSD_GROUP_W), F32)],
        compiler_params=_cparams("arbitrary", "arbitrary"),
        name="ssd_scan",
    )(*args)


def _out_ple_kernel(y_ref, h_ref, p_ref, wout_ref, plenorm_ref, wgate_ref, wup_ref, o_ref):
    h1 = h_ref[...] + _dot(y_ref[...], wout_ref[...])
    hn = _rms(h1, plenorm_ref[...]).astype(BF16)
    pb = p_ref[...].astype(BF16)
    for c in range(0, D_MODEL, COL_CHUNK):
        cs = slice(c, c + COL_CHUNK)
        gate = jax.nn.sigmoid(_dot(hn, wgate_ref[:, cs]))
        o_ref[:, cs] = h1[:, cs] + _dot(pb, wup_ref[:, cs]) * gate


def _out_ple(y, h, p, wout, plenorm, wgate, wup, tm):
    m, kd = y.shape
    assert m % tm == 0
    return pl.pallas_call(
        _out_ple_kernel,
        grid=(m // tm,),
        in_specs=[pl.BlockSpec((tm, kd), lambda i: (i, 0)),
                  pl.BlockSpec((tm, D_MODEL), lambda i: (i, 0)),
                  pl.BlockSpec((tm, PLE_DIM), lambda i: (i, 0)),
                  _const_spec(wout.shape), _const_spec(plenorm.shape),
                  _const_spec(wgate.shape), _const_spec(wup.shape)],
        out_specs=pl.BlockSpec((tm, D_MODEL), lambda i: (i, 0)),
        out_shape=jax.ShapeDtypeStruct((m, D_MODEL), F32),
        compiler_params=_cparams("arbitrary"),
        name="out_ple",
    )(y, h, p, wout, plenorm, wgate, wup)


def _head_norm_rope(x, gain, cos_t, sin_lo, sin_hi):
    lane = lax.broadcasted_iota(jnp.int32, x.shape, 1)
    is_nope = lane < MLA_NOPE
    x2 = x * x
    ms_n = jnp.sum(jnp.where(is_nope, x2, 0.0), axis=-1, keepdims=True) * (1.0 / MLA_NOPE)
    ms_r = jnp.sum(jnp.where(is_nope, 0.0, x2), axis=-1, keepdims=True) * (1.0 / MLA_ROPE)
    xn = x * jnp.where(is_nope, lax.rsqrt(ms_n + EPS), lax.rsqrt(ms_r + EPS)) * gain
    half = MLA_ROPE // 2
    return xn * cos_t + pltpu.roll(xn, LANES - half, 1) * sin_lo + pltpu.roll(xn, half, 1) * sin_hi


def _mla_in_kernel(h_ref, lnw_ref, win_ref, qan_ref, wqb_ref, kvan_ref, gq_ref, gkr_ref,
                   cos_ref, slo_ref, shi_ref, q_ref, lat_ref, kr_ref, gate_ref):
    xn = _rms(h_ref[...], lnw_ref[...]).astype(BF16)
    c_q, c_kv, c_kr = 0, MLA_Q_LORA, MLA_Q_LORA + MLA_KV_LORA
    c_gate = c_kr + HEAD_PAD
    _store_dot(gate_ref, xn, win_ref, c_gate)
    lat_ref[...] = _rms(_dot(xn, win_ref[:, c_kv:c_kr]), kvan_ref[...])
    cos_t, sin_lo, sin_hi = cos_ref[...], slo_ref[...], shi_ref[...]
    kr = _head_norm_rope(_dot(xn, win_ref[:, c_kr:c_gate]), gkr_ref[...], cos_t, sin_lo, sin_hi)
    kr_ref[...] = kr[:, MLA_NOPE:MLA_QK]
    qa = _rms(_dot(xn, win_ref[:, c_q:c_kv]), qan_ref[...]).astype(BF16)
    for h in range(MLA_HEADS):
        hs = slice(h * HEAD_PAD, (h + 1) * HEAD_PAD)
        qh = _head_norm_rope(_dot(qa, wqb_ref[:, hs]), gq_ref[...], cos_t, sin_lo, sin_hi)
        q_ref[:, hs] = qh.astype(q_ref.dtype)


def _mla_in(h, lnw, w, rope, tm):
    m = h.shape[0]
    assert m % tm == 0
    cos_t, sin_lo, sin_hi = rope
    period = cos_t.shape[0] // tm
    assert period * tm == cos_t.shape[0]
    rope_spec = pl.BlockSpec((tm, HEAD_PAD), lambda i: (i % period, 0))
    consts = [lnw, w["in_w"], w["q_a_norm"], w["q_b_w"], w["kv_a_norm"], w["gain_q"], w["gain_kr"]]
    row = lambda wd: pl.BlockSpec((tm, wd), lambda i: (i, 0))
    return pl.pallas_call(
        _mla_in_kernel,
        grid=(m // tm,),
        in_specs=[row(D_MODEL)] + [_const_spec(a.shape) for a in consts] + [rope_spec] * 3,
        out_specs=[row(MLA_HEADS * HEAD_PAD), row(MLA_KV_LORA), row(MLA_ROPE), row(MLA_WIDTH)],
        out_shape=[jax.ShapeDtypeStruct((m, MLA_HEADS * HEAD_PAD), BF16),
                   jax.ShapeDtypeStruct((m, MLA_KV_LORA), F32),
                   jax.ShapeDtypeStruct((m, MLA_ROPE), F32),
                   jax.ShapeDtypeStruct((m, MLA_WIDTH), F32)],
        compiler_params=_cparams("arbitrary"),
        name="mla_in",
    )(h, *consts, cos_t, sin_lo, sin_hi)


def _mla_kv_kernel(lat_ref, kr_ref, wk_ref, wv_ref, gkn_ref, k_ref, v_ref):
    latb = lat_ref[...].astype(BF16)
    _store_dot(v_ref, latb, wv_ref)
    rows = latb.shape[0]
    kr = jnp.concatenate([jnp.zeros((rows, MLA_NOPE), F32), kr_ref[...],
                          jnp.zeros((rows, HEAD_PAD - MLA_QK), F32)], axis=1)
    for h in range(MLA_HEADS):
        hs = slice(h * HEAD_PAD, (h + 1) * HEAD_PAD)
        kh = _dot(latb, wk_ref[:, hs])
        ms = jnp.sum(kh * kh, axis=-1, keepdims=True) * (1.0 / MLA_NOPE)
        k_ref[:, hs] = (kh * lax.rsqrt(ms + EPS) * gkn_ref[...] + kr).astype(k_ref.dtype)


def _mla_kv(lat, kr, w, tm):
    m = lat.shape[0]
    assert m % tm == 0
    consts = [w["kv_k_w"], w["kv_v_w"], w["gain_kn"]]
    row = lambda wd: pl.BlockSpec((tm, wd), lambda i: (i, 0))
    return pl.pallas_call(
        _mla_kv_kernel,
        grid=(m // tm,),
        in_specs=[row(MLA_KV_LORA), row(MLA_ROPE)] + [_const_spec(a.shape) for a in consts],
        out_specs=[row(MLA_HEADS * HEAD_PAD), row(MLA_WIDTH)],
        out_shape=[jax.ShapeDtypeStruct((m, MLA_HEADS * HEAD_PAD), BF16),
                   jax.ShapeDtypeStruct((m, MLA_WIDTH), BF16)],
        compiler_params=_cparams("arbitrary"),
        name="mla_kv",
    )(lat, kr, *consts)


def _softmax_step(q, k, v, carry, mask):
    m_prev, l_prev, acc = carry
    s = _dot_nt(q, k) * (MLA_QK ** -0.5)
    if mask is not None:
        s = jnp.where(mask, s, -jnp.inf)
    m_new = jnp.maximum(m_prev, jnp.max(s, axis=-1, keepdims=True))
    alpha = jnp.exp(m_prev - m_new)
    p = jnp.exp(s - m_new)
    l_new = alpha * l_prev + jnp.sum(p, axis=-1, keepdims=True)
    return m_new, l_new, alpha * acc + _dot(p.astype(BF16), v)


def _gated_pair_out(o_ref, gate_ref, outs, pair):
    lane = lax.broadcasted_iota(jnp.int32, outs[0].shape, 1)
    ps = slice(pair * LANES, (pair + 1) * LANES)
    g = gate_ref[:, ps]
    o_ref[:, ps] = (jnp.where(lane < MLA_V, outs[0], outs[1]) * (g * jax.nn.sigmoid(g))).astype(o_ref.dtype)


def _attn_kernel(q_ref, k_ref, v_ref, gate_ref, o_ref, m_s, l_s, acc_s, *, tq):
    i = pl.program_id(1)
    row = lax.broadcasted_iota(jnp.int32, (tq, tq), 0) // CHUNK
    col = lax.broadcasted_iota(jnp.int32, (tq, tq), 1) // CHUNK
    diag_mask = col <= row
    m_s[...] = jnp.full(m_s.shape, NEG_BIG, F32)
    l_s[...] = jnp.zeros(l_s.shape, F32)
    acc_s[...] = jnp.zeros(acc_s.shape, F32)

    def tile_step(j, mask):
        rows = pl.ds(pl.multiple_of(j * tq, tq), tq)
        for h in range(MLA_HEADS):
            hs = slice(h * HEAD_PAD, (h + 1) * HEAD_PAD)
            vs = slice((h // 2) * LANES, (h // 2 + 1) * LANES)
            s = _dot_nt(q_ref[:, hs], k_ref[rows, hs])
            if mask is not None:
                s = jnp.where(mask, s, -jnp.inf)
            m_prev = m_s[h]
            m_new = jnp.maximum(m_prev, jnp.max(s, axis=-1, keepdims=True))
            alpha = jnp.exp2((m_prev - m_new) * SOFTMAX_LOG2_SCALE)
            p = jnp.exp2((s - jnp.concatenate([m_new] * (tq // LANES), axis=1)) * SOFTMAX_LOG2_SCALE)
            l_s[h] = alpha * l_s[h] + jnp.sum(p, axis=-1, keepdims=True)
            acc_s[h] = alpha * acc_s[h] + _dot(p.astype(BF16), v_ref[rows, vs])
            m_s[h] = m_new

    def body(j, carry):
        tile_step(j, None)
        return carry

    lax.fori_loop(0, i, body, 0)
    tile_step(i, diag_mask)
    for pair in range(MLA_HEADS // 2):
        outs = [acc_s[2 * pair + sub] / l_s[2 * pair + sub] for sub in range(2)]
        _gated_pair_out(o_ref, gate_ref, outs, pair)


def _attn_prompt(q, k, v, gate, batch, seq, tq):
    nq = seq // tq
    assert nq * tq == seq and tq % CHUNK == 0
    return pl.pallas_call(
        functools.partial(_attn_kernel, tq=tq),
        grid=(batch, nq),
        in_specs=[pl.BlockSpec((tq, MLA_HEADS * HEAD_PAD), lambda b, i: (b * nq + i, 0)),
                  pl.BlockSpec((seq, MLA_HEADS * HEAD_PAD), lambda b, i: (b, 0)),
                  pl.BlockSpec((seq, MLA_WIDTH), lambda b, i: (b, 0)),
                  pl.BlockSpec((tq, MLA_WIDTH), lambda b, i: (b * nq + i, 0))],
        out_specs=pl.BlockSpec((tq, MLA_WIDTH), lambda b, i: (b * nq + i, 0)),
        out_shape=jax.ShapeDtypeStruct((batch * seq, MLA_WIDTH), BF16),
        scratch_shapes=[pltpu.VMEM((MLA_HEADS, tq, LANES), F32)] * 3,
        compiler_params=_cparams("arbitrary", "arbitrary"),
        name="attn_prompt",
    )(q, k, v, gate)


def _attn_step_kernel(q_ref, k_ref, v_ref, gate_ref, o_ref, *, q_pos0, n_keys):
    tq = q_ref.shape[0]
    tk = k_ref.shape[1]
    k_pos = lax.broadcasted_iota(jnp.int32, (tq, tk), 1)
    q_pos = lax.broadcasted_iota(jnp.int32, (tq, tk), 0) + q_pos0
    mask = jnp.logical_and(k_pos < n_keys, k_pos // CHUNK <= q_pos // CHUNK)
    init = (jnp.full((tq, 1), NEG_BIG, F32), jnp.zeros((tq, 1), F32), jnp.zeros((tq, LANES), F32))
    for pair in range(MLA_HEADS // 2):
        vs = slice(pair * LANES, (pair + 1) * LANES)
        outs = []
        for sub in range(2):
            hs = slice((2 * pair + sub) * HEAD_PAD, (2 * pair + sub + 1) * HEAD_PAD)
            _, l, acc = _softmax_step(q_ref[:, hs], k_ref[0, :, hs], v_ref[0, :, vs], init, mask)
            outs.append(acc / l)
        _gated_pair_out(o_ref, gate_ref, outs, pair)


def _attn_step(q, k, v, gate, batch, tq, n_keys):
    tk = k.shape[1]
    return pl.pallas_call(
        functools.partial(_attn_step_kernel, q_pos0=n_keys - tq, n_keys=n_keys),
        grid=(batch,),
        in_specs=[pl.BlockSpec((tq, MLA_HEADS * HEAD_PAD), lambda b: (b, 0)),
                  pl.BlockSpec((1, tk, MLA_HEADS * HEAD_PAD), lambda b: (b, 0, 0)),
                  pl.BlockSpec((1, tk, MLA_WIDTH), lambda b: (b, 0, 0)),
                  pl.BlockSpec((tq, MLA_WIDTH), lambda b: (b, 0))],
        out_specs=pl.BlockSpec((tq, MLA_WIDTH), lambda b: (b, 0)),
        out_shape=jax.ShapeDtypeStruct((batch * tq, MLA_WIDTH), BF16),
        compiler_params=_cparams("arbitrary"),
        name="attn_step",
    )(q, k, v, gate)


def _pad_cols(a, width):
    return jnp.pad(a, ((0, 0), (0, width - a.shape[1])))


def _ssd_weights(j, in_w, conv_w, conv_b, dt_bias, a_log, d, norm_w, out_w):
    zx = SSD_D_INNER + SSD_CONV_DIM
    return {
        "in_w": jnp.concatenate([in_w[j][:, :zx], _pad_cols(in_w[j][:, zx:], LANES)], axis=1).astype(BF16),
        "conv_w": conv_w[j],
        "conv_b": conv_b[j][None, :],
        "dt_bias": _pad_cols(dt_bias[j][None, :], LANES),
        "a_log": _pad_cols(a_log[j][None, :], LANES),
        "d_exp": jnp.repeat(d[j], SSD_HEAD_DIM)[None, :],
        "norm_w": norm_w[j][None, :],
        "out_w": out_w[j].astype(BF16),
    }


def _head_pad_cols(w, per_head, take, offset=0):
    k = w.shape[0]
    w = w.reshape(k, MLA_HEADS, per_head)[:, :, offset:offset + take]
    return jnp.pad(w, ((0, 0), (0, 0), (0, HEAD_PAD - take))).reshape(k, MLA_HEADS * HEAD_PAD)


def _mla_weights(j, in_w, q_a_norm, q_b_w, kv_a_norm, kv_b_w, qn, qr, kn, kr, out_w):
    c1, c2, c3 = MLA_Q_LORA, MLA_Q_LORA + MLA_KV_LORA, MLA_Q_LORA + MLA_KV_LORA + MLA_ROPE
    w = in_w[j]
    kr_cols = jnp.pad(w[:, c2:c3], ((0, 0), (MLA_NOPE, HEAD_PAD - MLA_QK)))
    zeros = lambda n: jnp.zeros((n,), F32)
    return {
        "in_w": jnp.concatenate([w[:, :c2], kr_cols, w[:, c3:]], axis=1).astype(BF16),
        "q_a_norm": q_a_norm[j][None, :],
        "q_b_w": _head_pad_cols(q_b_w[j], MLA_QK, MLA_QK).astype(BF16),
        "kv_a_norm": kv_a_norm[j][None, :],
        "kv_k_w": _head_pad_cols(kv_b_w[j], MLA_NOPE + MLA_V, MLA_NOPE).astype(BF16),
        "kv_v_w": kv_b_w[j].reshape(MLA_KV_LORA, MLA_HEADS, MLA_NOPE + MLA_V)[:, :, MLA_NOPE:]
                  .reshape(MLA_KV_LORA, MLA_WIDTH).astype(BF16),
        "gain_q": jnp.concatenate([qn[j], qr[j], zeros(HEAD_PAD - MLA_QK)])[None, :],
        "gain_kr": jnp.concatenate([zeros(MLA_NOPE), kr[j], zeros(HEAD_PAD - MLA_QK)])[None, :],
        "gain_kn": jnp.concatenate([kn[j], zeros(HEAD_PAD - MLA_NOPE)])[None, :],
        "out_w": out_w[j].astype(BF16),
    }


def _rope_tables(pos0, seq, rows):
    half = MLA_ROPE // 2
    inv = 1.0 / (ROPE_BASE ** (jnp.arange(0, MLA_ROPE, 2, dtype=F32) / MLA_ROPE))
    ang = (pos0 + jnp.arange(seq)).astype(F32)[:, None] * inv[None, :]
    cos, sin = jnp.cos(ang), jnp.sin(ang)
    z = lambda n: jnp.zeros((seq, n), F32)
    cos_t = jnp.concatenate([jnp.ones((seq, MLA_NOPE), F32), cos, cos, z(HEAD_PAD - MLA_QK)], axis=1)
    sin_lo = jnp.concatenate([z(MLA_NOPE), -sin, z(HEAD_PAD - MLA_NOPE - half)], axis=1)
    sin_hi = jnp.concatenate([z(MLA_NOPE + half), sin, z(HEAD_PAD - MLA_QK)], axis=1)
    reps = max(1, rows // seq)
    return tuple(jnp.tile(t, (reps, 1)) for t in (cos_t, sin_lo, sin_hi))


def _row_tile(m, want):
    tm = min(m, want)
    assert m % tm == 0
    return tm


def _ssd_layer(h, p, lnw, ple, w, batch, seq, chunk, conv_prev, h0):
    tm = _row_tile(h.shape[0], 256)
    z, x, bc, dt = _norm_proj(h, lnw, w["in_w"], (SSD_D_INNER, SSD_D_INNER, 2 * SSD_BC, LANES), tm)
    yn, conv_new, ht = _ssd_scan(z, x, bc, dt, conv_prev, h0, w, batch, seq, chunk)
    h = _out_ple(yn, h, p, w["out_w"], *ple, tm)
    return h, conv_new, ht.reshape(batch, SSD_HEADS, SSD_HEAD_DIM, SSD_STATE)


def _mla_layer(h, p, lnw, ple, w, batch, seq, lat_past, kr_past):
    tm = _row_tile(h.shape[0], 256)
    pos0 = 0 if lat_past is None else lat_past.shape[1]
    rope = _rope_tables(pos0, seq, tm)
    q, lat, kr, gate = _mla_in(h, lnw, w, rope, tm)
    if lat_past is None:
        k, v = _mla_kv(lat, kr, w, tm)
        og = _attn_prompt(q, k, v, gate, batch, seq, 256)
    else:
        n_keys = pos0 + seq
        tk = -(-n_keys // LANES) * LANES
        pad = ((0, 0), (0, tk - n_keys), (0, 0))
        lat_all = jnp.pad(jnp.concatenate([lat_past, lat.reshape(batch, seq, -1)], axis=1), pad)
        kr_all = jnp.pad(jnp.concatenate([kr_past, kr.reshape(batch, seq, -1)], axis=1), pad)
        k, v = _mla_kv(lat_all.reshape(batch * tk, -1), kr_all.reshape(batch * tk, -1), w,
                       math.gcd(batch * tk, 512))
        og = _attn_step(q, k.reshape(batch, tk, -1), v.reshape(batch, tk, -1), gate, batch, seq, n_keys)
    h = _out_ple(og, h, p, w["out_w"], *ple, tm)
    return h, lat.reshape(batch, seq, -1), kr.reshape(batch, seq, -1)


def kernel(x_prompt, x_sample, cache_conv, state_ssm, cache_kv_latent, cache_k_rope, p_prompt, p_sample, ln_w, ssd_in_w, ssd_conv_w, ssd_conv_b, ssd_dt_bias, ssd_A_log, ssd_D, ssd_norm_w, ssd_out_w, mla_in_w, mla_q_a_norm, mla_q_b_w, mla_kv_a_norm, mla_kv_b_w, mla_q_nope_norm, mla_q_rope_norm, mla_k_nope_norm, mla_k_rope_norm, mla_out_w, ple_up_w, ple_norm_w, ple_gate_w):
    bp, tp, d = x_prompt.shape
    bs, ts, _ = x_sample.shape
    hp = x_prompt.reshape(bp * tp, d)
    hs = x_sample.reshape(bs * ts, d)
    conv_p, ssm_p, lat_p, kr_p = [], [], [], []
    conv_s, ssm_s, lat_s, kr_s = [], [], [], []
    for i in range(DEPTH):
        j = i // 2
        lnw = ln_w[i][None, :]
        ple = (ple_norm_w[i][None, :], ple_gate_w[i].astype(BF16), ple_up_w[i].astype(BF16))
        pp = p_prompt[i].reshape(bp * tp, PLE_DIM)
        ps = p_sample[i].reshape(bs * ts, PLE_DIM)
        if i % 2 == 0:
            w = _ssd_weights(j, ssd_in_w, ssd_conv_w, ssd_conv_b, ssd_dt_bias, ssd_A_log, ssd_D,
                             ssd_norm_w, ssd_out_w)
            hp, cp, sp = _ssd_layer(hp, pp, lnw, ple, w, bp, tp, CHUNK, None, None)
            hs, cs, ss = _ssd_layer(hs, ps, lnw, ple, w, bs, ts, ts, cache_conv[j], state_ssm[j])
            conv_p.append(cp); ssm_p.append(sp); conv_s.append(cs); ssm_s.append(ss)
        else:
            w = _mla_weights(j, mla_in_w, mla_q_a_norm, mla_q_b_w, mla_kv_a_norm, mla_kv_b_w,
                             mla_q_nope_norm, mla_q_rope_norm, mla_k_nope_norm, mla_k_rope_norm, mla_out_w)
            hp, lp, rp = _mla_layer(hp, pp, lnw, ple, w, bp, tp, None, None)
            hs, ls, rs = _mla_layer(hs, ps, lnw, ple, w, bs, ts, cache_kv_latent[j], cache_k_rope[j])
            lat_p.append(lp); kr_p.append(rp); lat_s.append(ls); kr_s.append(rs)
    return (hp.reshape(bp, tp, d), hs.reshape(bs, ts, d),
            jnp.stack(conv_p), jnp.stack(ssm_p), jnp.stack(lat_p), jnp.stack(kr_p),
            jnp.stack(conv_s), jnp.stack(ssm_s), jnp.stack(lat_s), jnp.stack(kr_s))
```

```python
import functools
import math

import jax
import jax.numpy as jnp
from jax import lax
from jax.experimental import pallas as pl
from jax.experimental.pallas import tpu as pltpu

F32 = jnp.float32
BF16 = jnp.bfloat16

D_MODEL = 1024
DEPTH = 4
CHUNK = 64
PLE_DIM = 256
EPS = 1e-6

SSD_D_INNER = 2 * D_MODEL
SSD_HEAD_DIM = 64
SSD_HEADS = SSD_D_INNER // SSD_HEAD_DIM
SSD_GROUPS = 8
SSD_HPG = SSD_HEADS // SSD_GROUPS
SSD_STATE = 128
SSD_CONV_W = 4
SSD_BC = SSD_GROUPS * SSD_STATE
SSD_CONV_DIM = SSD_D_INNER + 2 * SSD_BC
SSD_GROUP_W = SSD_HPG * SSD_HEAD_DIM

MLA_HEADS = 16
MLA_NOPE = 64
MLA_ROPE = 32
MLA_V = 64
MLA_Q_LORA = 384
MLA_KV_LORA = 256
MLA_QK = MLA_NOPE + MLA_ROPE
MLA_WIDTH = MLA_HEADS * MLA_V
ROPE_BASE = 10000.0

LANES = 128
SUBLANES = 8
HEAD_PAD = LANES
VMEM_LIMIT_BYTES = 52 * 1024 * 1024
NEG_BIG = -1e30
SOFTMAX_LOG2_SCALE = (MLA_QK ** -0.5) * math.log2(math.e)
COL_CHUNK = 512
CONV_ROWS = 128
SSD_BLOCK_ROWS = 256
SSD_SCAN_CHUNK = 128


def _cparams(*sem):
    return pltpu.CompilerParams(dimension_semantics=sem, vmem_limit_bytes=VMEM_LIMIT_BYTES)


def _const_spec(shape):
    nd = len(shape)
    return pl.BlockSpec(shape, lambda *_: (0,) * nd, pipeline_mode=pl.Buffered(1))


def _rms(x, w):
    return x * lax.rsqrt(jnp.mean(x * x, axis=-1, keepdims=True) + EPS) * w


def _dot(a, b):
    return jnp.dot(a, b, preferred_element_type=F32)


def _dot_nt(a, b):
    return lax.dot_general(a, b, (((1,), (1,)), ((), ())), preferred_element_type=F32)


def _dot_tn(a, b):
    return lax.dot_general(a, b, (((0,), (0,)), ((), ())), preferred_element_type=F32)


def _store_dot(o_ref, a, w_ref, col0=0):
    width = o_ref.shape[1]
    for c in range(0, width, COL_CHUNK):
        cw = min(COL_CHUNK, width - c)
        o_ref[:, c:c + cw] = _dot(a, w_ref[:, col0 + c:col0 + c + cw]).astype(o_ref.dtype)


def _ssd_in_kernel(*refs, tiles_per_seq, has_init):
    if has_init:
        (x_ref, lnw_ref, w_ref, convw_ref, convb_ref, cprev_ref,
         z_ref, xs_ref, b_ref, c_ref, dt_ref, convnew_ref, ext) = refs
    else:
        (x_ref, lnw_ref, w_ref, convw_ref, convb_ref,
         z_ref, xs_ref, b_ref, c_ref, dt_ref, convnew_ref, ext) = refs
    tm = x_ref.shape[0]
    tail = SSD_CONV_W - 1
    t = pl.program_id(0) % tiles_per_seq
    xn = _rms(x_ref[...], lnw_ref[...]).astype(BF16)

    @pl.when(t == 0)
    def _init():
        ext[0:SUBLANES, :] = jnp.zeros((SUBLANES, SSD_CONV_DIM), F32)
        if has_init:
            ext[SUBLANES - tail:SUBLANES, :] = cprev_ref[0]

    dt_ref[...] = _dot(xn, w_ref[:, SSD_D_INNER + SSD_CONV_DIM:])
    def project(c0):
        cs = slice(c0, c0 + COL_CHUNK)
        ext[SUBLANES:SUBLANES + tm, cs] = _dot(xn, w_ref[:, SSD_D_INNER + c0:SSD_D_INNER + c0 + COL_CHUNK])
        convnew_ref[0, :, cs] = ext[pl.ds(SUBLANES + tm - tail, tail), cs]

    project(0)
    for c0 in range(0, SSD_CONV_DIM, COL_CHUNK):
        cs = slice(c0, c0 + COL_CHUNK)
        if c0 + COL_CHUNK < SSD_CONV_DIM:
            project(c0 + COL_CHUNK)
        if c0 % (2 * COL_CHUNK) == 0:
            zc = slice(c0 // 2, c0 // 2 + COL_CHUNK)
            z = _dot(xn, w_ref[:, zc])
            z_ref[:, zc] = z * jax.nn.sigmoid(z)
        if c0 < SSD_D_INNER:
            o_ref, o0 = xs_ref, c0
        elif c0 < SSD_D_INNER + SSD_BC:
            o_ref, o0 = b_ref, c0 - SSD_D_INNER
        else:
            o_ref, o0 = c_ref, c0 - SSD_D_INNER - SSD_BC
        for r0 in range(0, tm, CONV_ROWS):
            rb = min(CONV_ROWS, tm - r0)
            for l0 in range(0, COL_CHUNK, LANES):
                ls = slice(c0 + l0, c0 + l0 + LANES)
                e = ext[r0:r0 + rb + SUBLANES, ls]
                a = convb_ref[:, ls]
                for k in range(SSD_CONV_W):
                    shifted = e if k == tail else pltpu.roll(e, tail - k, 0)
                    a = a + shifted[SUBLANES:SUBLANES + rb] * convw_ref[k:k + 1, ls]
                o_ref[r0:r0 + rb, o0 + l0:o0 + l0 + LANES] = (a * jax.nn.sigmoid(a)).astype(o_ref.dtype)
    ext[0:SUBLANES, :] = ext[tm:tm + SUBLANES, :]


def _ssd_in(x, lnw, w, conv_prev, batch, seq, tm):
    m, d = x.shape
    assert seq % tm == 0 and tm % SUBLANES == 0 and m == batch * seq
    tps = seq // tm
    has_init = conv_prev is not None
    tail = SSD_CONV_W - 1
    consts = [lnw, w["in_w"], w["conv_w"], w["conv_b"]]
    row = lambda wd: pl.BlockSpec((tm, wd), lambda i: (i, 0))
    conv_spec = pl.BlockSpec((1, tail, SSD_CONV_DIM), lambda i: (i // tps, 0, 0))
    in_specs = [row(d)] + [_const_spec(a.shape) for a in consts]
    args = [x] + consts
    if has_init:
        in_specs.append(conv_spec)
        args.append(conv_prev)
    return pl.pallas_call(
        functools.partial(_ssd_in_kernel, tiles_per_seq=tps, has_init=has_init),
        grid=(m // tm,),
        in_specs=in_specs,
        out_specs=[row(SSD_D_INNER), row(SSD_D_INNER), row(SSD_BC), row(SSD_BC), row(LANES), conv_spec],
        out_shape=[jax.ShapeDtypeStruct((m, SSD_D_INNER), F32),
                   jax.ShapeDtypeStruct((m, SSD_D_INNER), F32),
                   jax.ShapeDtypeStruct((m, SSD_BC), BF16),
                   jax.ShapeDtypeStruct((m, SSD_BC), BF16),
                   jax.ShapeDtypeStruct((m, LANES), F32),
                   jax.ShapeDtypeStruct((batch, tail, SSD_CONV_DIM), F32)],
        scratch_shapes=[pltpu.VMEM((tm + SUBLANES, SSD_CONV_DIM), F32)],
        compiler_params=_cparams("arbitrary"),
        name="ssd_in",
    )(*args)


def _dot_exact_lhs(a, x):
    x1 = x.astype(BF16)
    r1 = x - x1.astype(F32)
    x2 = r1.astype(BF16)
    x3 = (r1 - x2.astype(F32)).astype(BF16)
    return _dot(a, x1) + _dot(a, x2) + _dot(a, x3)


def _expand_heads(cols, g, lane_head):
    shape = (cols.shape[0], SSD_GROUP_W)
    h0 = SSD_HPG * g
    out = jnp.broadcast_to(cols[:, h0 + SSD_HPG - 1:h0 + SSD_HPG], shape)
    for r in range(SSD_HPG - 2, -1, -1):
        out = jnp.where(lane_head[:shape[0]] == r, jnp.broadcast_to(cols[:, h0 + r:h0 + r + 1], shape), out)
    return out


def _ssd_kernel(*refs, L, has_init):
    if has_init:
        (zs_ref, xs_ref, b_ref, c_ref, dt_ref, h0_ref, dtb_ref, alog_ref,
         dexp_ref, normw_ref, yn_ref, ht_ref, st) = refs
    else:
        (zs_ref, xs_ref, b_ref, c_ref, dt_ref, dtb_ref, alog_ref,
         dexp_ref, normw_ref, yn_ref, ht_ref, st) = refs
    c = pl.program_id(1)
    last = pl.num_programs(1) - 1

    @pl.when(c == 0)
    def _init():
        if has_init:
            for g in range(SSD_GROUPS):
                st[g] = h0_ref[0, g * SSD_GROUP_W:(g + 1) * SSD_GROUP_W, :].T
        else:
            st[...] = jnp.zeros(st.shape, F32)

    rows = dt_ref.shape[0]
    n_chunks = rows // L
    dt = jax.nn.softplus(dt_ref[...] + dtb_ref[...])
    dta = dt * (-jnp.exp(alog_ref[...]))
    row = lax.broadcasted_iota(jnp.int32, (rows, rows), 0)
    col = lax.broadcasted_iota(jnp.int32, (rows, rows), 1)
    same_chunk_tri = jnp.logical_and(row >= col, row // L == col // L)
    acs = _dot_exact_lhs(same_chunk_tri.astype(BF16), dta)
    acs_t = acs.T
    dt_t = dt.T
    e_acs = jnp.exp(acs)
    tri = lax.broadcasted_iota(jnp.int32, (L, L), 0) >= lax.broadcasted_iota(jnp.int32, (L, L), 1)
    lane_head = lax.broadcasted_iota(jnp.int32, (L, SSD_GROUP_W), 1) // SSD_HEAD_DIM

    for ci in range(n_chunks):
        rs = slice(ci * L, (ci + 1) * L)
        acs_c = acs[rs]
        a_last = acs_c[L - 1:L, :]
        e_acs_c = e_acs[rs]
        wdt = jnp.exp(a_last - acs_c) * dt[rs]
        cdec = jnp.exp(a_last)
        for g in range(SSD_GROUPS):
            gs = slice(g * SSD_GROUP_W, (g + 1) * SSD_GROUP_W)
            xg = xs_ref[rs, gs]
            bg = b_ref[rs, g * SSD_STATE:(g + 1) * SSD_STATE]
            cg = c_ref[rs, g * SSD_STATE:(g + 1) * SSD_STATE]
            cb = _dot_nt(cg, bg)
            sg = st[g]
            y = _dot(cg, sg.astype(BF16)) * _expand_heads(e_acs_c, g, lane_head)
            for r in range(SSD_HPG):
                h = SSD_HPG * g + r
                seg = acs_c[:, h:h + 1] - acs_t[h:h + 1, rs]
                m = cb * jnp.exp(jnp.where(tri, seg, -jnp.inf)) * dt_t[h:h + 1, rs]
                xh = jnp.where(lane_head == r, xg, 0.0).astype(BF16)
                y = y + _dot(m.astype(BF16), xh)
            y = y + dexp_ref[:, gs] * xg
            y = y * zs_ref[rs, gs]
            yn_ref[rs, gs] = _rms(y, normw_ref[:, gs]).astype(yn_ref.dtype)
            xw = (xg * _expand_heads(wdt, g, lane_head)).astype(BF16)
            st[g] = sg * _expand_heads(cdec, g, lane_head) + _dot_tn(bg, xw)

    @pl.when(c == last)
    def _state_out():
        for g in range(SSD_GROUPS):
            ht_ref[0, g * SSD_GROUP_W:(g + 1) * SSD_GROUP_W, :] = st[g].T


def _ssd_scan(zs, xs, b, c, dt, h0, w, batch, seq, L):
    rows = min(seq, SSD_BLOCK_ROWS)
    nc = seq // rows
    assert nc * rows == seq and rows % L == 0 and L % SUBLANES == 0
    has_init = h0 is not None
    row_spec = lambda wd: pl.BlockSpec((rows, wd), lambda bi, ci: (bi * nc + ci, 0))
    state_spec = pl.BlockSpec((1, SSD_D_INNER, SSD_STATE), lambda bi, ci: (bi, 0, 0))
    in_specs = [row_spec(SSD_D_INNER), row_spec(SSD_D_INNER), row_spec(SSD_BC), row_spec(SSD_BC), row_spec(LANES)]
    args = [zs, xs, b, c, dt]
    if has_init:
        in_specs.append(state_spec)
        args.append(h0.reshape(batch, SSD_D_INNER, SSD_STATE))
    consts = [w["dt_bias"], w["a_log"], w["d_exp"], w["norm_w"]]
    in_specs += [_const_spec(a.shape) for a in consts]
    args += consts
    return pl.pallas_call(
        functools.partial(_ssd_kernel, L=L, has_init=has_init),
        grid=(batch, nc),
        in_specs=in_specs,
        out_specs=[row_spec(SSD_D_INNER), state_spec],
        out_shape=[jax.ShapeDtypeStruct((batch * seq, SSD_D_INNER), BF16),
                   jax.ShapeDtypeStruct((batch, SSD_D_INNER, SSD_STATE), F32)],
        scratch_shapes=[pltpu.VMEM((SSD_GROUPS, SSD_STATE, SSD_GROUP_W), F32)],
        compiler_params=_cparams("arbitrary", "arbitrary"),
        name="ssd_scan",
    )(*args)


def _out_ple_kernel(y_ref, h_ref, p_ref, wout_ref, plenorm_ref, wgate_ref, wup_ref, o_ref):
    h1 = h_ref[...] + _dot(y_ref[...], wout_ref[...])
    hn = _rms(h1, plenorm_ref[...]).astype(BF16)
    pb = p_ref[...].astype(BF16)
    for c in range(0, D_MODEL, COL_CHUNK):
        cs = slice(c, c + COL_CHUNK)
        gate = jax.nn.sigmoid(_dot(hn, wgate_ref[:, cs]))
        o_ref[:, cs] = h1[:, cs] + _dot(pb, wup_ref[:, cs]) * gate


def _out_ple(y, h, p, wout, plenorm, wgate, wup, tm):
    m, kd = y.shape
    assert m % tm == 0
    return pl.pallas_call(
        _out_ple_kernel,
        grid=(m // tm,),
        in_specs=[pl.BlockSpec((tm, kd), lambda i: (i, 0)),
                  pl.BlockSpec((tm, D_MODEL), lambda i: (i, 0)),
                  pl.BlockSpec((tm, PLE_DIM), lambda i: (i, 0)),
                  _const_spec(wout.shape), _const_spec(plenorm.shape),
                  _const_spec(wgate.shape), _const_spec(wup.shape)],
        out_specs=pl.BlockSpec((tm, D_MODEL), lambda i: (i, 0)),
        out_shape=jax.ShapeDtypeStruct((m, D_MODEL), F32),
        compiler_params=_cparams("arbitrary"),
        name="out_ple",
    )(y, h, p, wout, plenorm, wgate, wup)


def _dot_exact_rhs(x, a):
    x1 = x.astype(BF16)
    r1 = x - x1.astype(F32)
    x2 = r1.astype(BF16)
    x3 = (r1 - x2.astype(F32)).astype(BF16)
    return _dot(x1, a) + _dot(x2, a) + _dot(x3, a)


def _head_part_matrix(width):
    def part(idx):
        within = idx % HEAD_PAD
        return (idx // HEAD_PAD) * 4 + jnp.where(within < MLA_NOPE, 0, jnp.where(within < MLA_QK, 1, 2))
    rows = part(lax.broadcasted_iota(jnp.int32, (width, width), 0))
    cols_i = lax.broadcasted_iota(jnp.int32, (width, width), 1)
    cols = part(cols_i)
    same = jnp.logical_and(rows == cols, cols_i % HEAD_PAD < MLA_QK)
    lane = lax.broadcasted_iota(jnp.int32, (1, width), 1) % HEAD_PAD
    inv_count = jnp.where(lane < MLA_NOPE, 1.0 / MLA_NOPE, jnp.where(lane < MLA_QK, 1.0 / MLA_ROPE, 0.0))
    return jnp.where(same, 1.0, 0.0).astype(BF16), inv_count


def _head_norm_rope(x, x_rot, pmat, inv_count, gain_cos, gain_sin):
    inv = lax.rsqrt(_dot_exact_rhs(x * x, pmat) * inv_count + EPS)
    return (x * inv) * gain_cos + (x_rot * inv) * gain_sin


def _mla_in_kernel(h_ref, lnw_ref, win_ref, qan_ref, wqb_ref, kvan_ref, gq_ref, gqrot_ref, gkr_ref, gkrrot_ref,
                   cos_ref, sin_ref, q_ref, lat_ref, kr_ref, gate_ref):
    xn = _rms(h_ref[...], lnw_ref[...]).astype(BF16)
    c_q, c_kv, c_kr = 0, MLA_Q_LORA, MLA_Q_LORA + MLA_KV_LORA
    c_gate = c_kr + 2 * HEAD_PAD
    pair_w = 2 * HEAD_PAD
    _store_dot(gate_ref, xn, win_ref, c_gate)
    lat_ref[...] = _rms(_dot(xn, win_ref[:, c_kv:c_kr]), kvan_ref[...])
    pmat, inv_count = _head_part_matrix(pair_w)
    cos_t, sin_t = cos_ref[...], sin_ref[...]
    kr_raw = _dot(xn, win_ref[:, c_kr:c_gate])
    kr = _head_norm_rope(kr_raw[:, :HEAD_PAD], kr_raw[:, HEAD_PAD:], pmat[:HEAD_PAD, :HEAD_PAD],
                         inv_count[:, :HEAD_PAD], gkr_ref[...] * cos_t[:, :HEAD_PAD],
                         gkrrot_ref[...] * sin_t[:, :HEAD_PAD])
    kr_ref[...] = kr[:, MLA_NOPE:MLA_QK]
    qa = _rms(_dot(xn, win_ref[:, c_q:c_kv]), qan_ref[...]).astype(BF16)
    gain_cos = gq_ref[...] * cos_t
    gain_sin = gqrot_ref[...] * sin_t
    for pair in range(MLA_HEADS // 2):
        qq = _dot(qa, wqb_ref[:, 2 * pair * pair_w:2 * (pair + 1) * pair_w])
        qh = _head_norm_rope(qq[:, :pair_w], qq[:, pair_w:], pmat, inv_count, gain_cos, gain_sin)
        q_ref[:, pair * pair_w:(pair + 1) * pair_w] = qh.astype(q_ref.dtype)


def _mla_in(h, lnw, w, rope, tm):
    m = h.shape[0]
    assert m % tm == 0
    cos_t, sin_t = rope
    period = cos_t.shape[0] // tm
    assert period * tm == cos_t.shape[0]
    rope_spec = pl.BlockSpec((tm, 2 * HEAD_PAD), lambda i: (i % period, 0))
    consts = [lnw, w["in_w"], w["q_a_norm"], w["q_b_w"], w["kv_a_norm"],
              w["gain_q"], w["gain_q_rot"], w["gain_kr"], w["gain_kr_rot"]]
    row = lambda wd: pl.BlockSpec((tm, wd), lambda i: (i, 0))
    return pl.pallas_call(
        _mla_in_kernel,
        grid=(m // tm,),
        in_specs=[row(D_MODEL)] + [_const_spec(a.shape) for a in consts] + [rope_spec] * 2,
        out_specs=[row(MLA_HEADS * HEAD_PAD), row(MLA_KV_LORA), row(MLA_ROPE), row(MLA_WIDTH)],
        out_shape=[jax.ShapeDtypeStruct((m, MLA_HEADS * HEAD_PAD), BF16),
                   jax.ShapeDtypeStruct((m, MLA_KV_LORA), F32),
                   jax.ShapeDtypeStruct((m, MLA_ROPE), F32),
                   jax.ShapeDtypeStruct((m, MLA_WIDTH), F32)],
        compiler_params=_cparams("arbitrary"),
        name="mla_in",
    )(h, *consts, cos_t, sin_t)


def _mla_kv_kernel(lat_ref, kr_ref, wk_ref, wv_ref, gkn_ref, k_ref, v_ref):
    latb = lat_ref[...].astype(BF16)
    _store_dot(v_ref, latb, wv_ref)
    rows = latb.shape[0]
    kr = jnp.concatenate([jnp.zeros((rows, MLA_NOPE), F32), kr_ref[...],
                          jnp.zeros((rows, HEAD_PAD - MLA_QK), F32)], axis=1)
    kr = jnp.concatenate([kr, kr], axis=1)
    pair_w = 2 * HEAD_PAD
    pmat, inv_count = _head_part_matrix(pair_w)
    gain = jnp.concatenate([gkn_ref[...], gkn_ref[...]], axis=1)
    for pair in range(MLA_HEADS // 2):
        ps = slice(pair * pair_w, (pair + 1) * pair_w)
        kk = _dot(latb, wk_ref[:, ps])
        inv = lax.rsqrt(_dot_exact_rhs(kk * kk, pmat) * inv_count + EPS)
        k_ref[:, ps] = (kk * inv * gain + kr).astype(k_ref.dtype)


def _mla_kv(lat, kr, w, tm):
    m = lat.shape[0]
    assert m % tm == 0
    consts = [w["kv_k_w"], w["kv_v_w"], w["gain_kn"]]
    row = lambda wd: pl.BlockSpec((tm, wd), lambda i: (i, 0))
    return pl.pallas_call(
        _mla_kv_kernel,
        grid=(m // tm,),
        in_specs=[row(MLA_KV_LORA), row(MLA_ROPE)] + [_const_spec(a.shape) for a in consts],
        out_specs=[row(MLA_HEADS * HEAD_PAD), row(MLA_WIDTH)],
        out_shape=[jax.ShapeDtypeStruct((m, MLA_HEADS * HEAD_PAD), BF16),
                   jax.ShapeDtypeStruct((m, MLA_WIDTH), BF16)],
        compiler_params=_cparams("arbitrary"),
        name="mla_kv",
    )(lat, kr, *consts)


def _softmax_step(q, k, v, carry, mask):
    m_prev, l_prev, acc = carry
    s = _dot_nt(q, k) * (MLA_QK ** -0.5)
    if mask is not None:
        s = jnp.where(mask, s, -jnp.inf)
    m_new = jnp.maximum(m_prev, jnp.max(s, axis=-1, keepdims=True))
    alpha = jnp.exp(m_prev - m_new)
    p = jnp.exp(s - m_new)
    l_new = alpha * l_prev + jnp.sum(p, axis=-1, keepdims=True)
    return m_new, l_new, alpha * acc + _dot(p.astype(BF16), v)


def _gated_pair_out(o_ref, gate_ref, outs, pair):
    lane = lax.broadcasted_iota(jnp.int32, outs[0].shape, 1)
    ps = slice(pair * LANES, (pair + 1) * LANES)
    g = gate_ref[:, ps]
    o_ref[:, ps] = (jnp.where(lane < MLA_V, outs[0], outs[1]) * (g * jax.nn.sigmoid(g))).astype(o_ref.dtype)


def _attn_kernel(q_ref, k_ref, v_ref, gate_ref, o_ref, m_s, l_s, acc_s, *, tq):
    i = pl.program_id(1)
    row = lax.broadcasted_iota(jnp.int32, (tq, tq), 0) // CHUNK
    col = lax.broadcasted_iota(jnp.int32, (tq, tq), 1) // CHUNK
    diag_mask = col <= row
    m_s[...] = jnp.full(m_s.shape, NEG_BIG, F32)
    l_s[...] = jnp.zeros(l_s.shape, F32)
    acc_s[...] = jnp.zeros(acc_s.shape, F32)

    def tile_step(j, mask):
        rows = pl.ds(pl.multiple_of(j * tq, tq), tq)
        for h in range(MLA_HEADS):
            hs = slice(h * HEAD_PAD, (h + 1) * HEAD_PAD)
            vs = slice((h // 2) * LANES, (h // 2 + 1) * LANES)
            s = _dot_nt(q_ref[:, hs], k_ref[rows, hs])
            if mask is not None:
                s = jnp.where(mask, s, -jnp.inf)
            m_prev = m_s[h]
            m_new = jnp.maximum(m_prev, jnp.max(s, axis=-1, keepdims=True))
            alpha = jnp.exp2((m_prev - m_new) * SOFTMAX_LOG2_SCALE)
            p = jnp.exp2((s - jnp.concatenate([m_new] * (tq // LANES), axis=1)) * SOFTMAX_LOG2_SCALE)
            l_s[h] = alpha * l_s[h] + jnp.sum(p, axis=-1, keepdims=True)
            acc_s[h] = alpha * acc_s[h] + _dot(p.astype(BF16), v_ref[rows, vs])
            m_s[h] = m_new

    def body(j, carry):
        tile_step(j, None)
        return carry

    lax.fori_loop(0, i, body, 0)
    tile_step(i, diag_mask)
    for pair in range(MLA_HEADS // 2):
        outs = [acc_s[2 * pair + sub] / l_s[2 * pair + sub] for sub in range(2)]
        _gated_pair_out(o_ref, gate_ref, outs, pair)


def _attn_prompt(q, k, v, gate, batch, seq, tq):
    nq = seq // tq
    assert nq * tq == seq and tq % CHUNK == 0
    return pl.pallas_call(
        functools.partial(_attn_kernel, tq=tq),
        grid=(batch, nq),
        in_specs=[pl.BlockSpec((tq, MLA_HEADS * HEAD_PAD), lambda b, i: (b * nq + i, 0)),
                  pl.BlockSpec((seq, MLA_HEADS * HEAD_PAD), lambda b, i: (b, 0)),
                  pl.BlockSpec((seq, MLA_WIDTH), lambda b, i: (b, 0)),
                  pl.BlockSpec((tq, MLA_WIDTH), lambda b, i: (b * nq + i, 0))],
        out_specs=pl.BlockSpec((tq, MLA_WIDTH), lambda b, i: (b * nq + i, 0)),
        out_shape=jax.ShapeDtypeStruct((batch * seq, MLA_WIDTH), BF16),
        scratch_shapes=[pltpu.VMEM((MLA_HEADS, tq, LANES), F32)] * 3,
        compiler_params=_cparams("arbitrary", "arbitrary"),
        name="attn_prompt",
    )(q, k, v, gate)


def _attn_step_kernel(q_ref, k_ref, v_ref, gate_ref, o_ref, *, q_pos0, n_keys):
    tq = q_ref.shape[0]
    tk = k_ref.shape[1]
    k_pos = lax.broadcasted_iota(jnp.int32, (tq, tk), 1)
    q_pos = lax.broadcasted_iota(jnp.int32, (tq, tk), 0) + q_pos0
    mask = jnp.logical_and(k_pos < n_keys, k_pos // CHUNK <= q_pos // CHUNK)
    init = (jnp.full((tq, 1), NEG_BIG, F32), jnp.zeros((tq, 1), F32), jnp.zeros((tq, LANES), F32))
    for pair in range(MLA_HEADS // 2):
        vs = slice(pair * LANES, (pair + 1) * LANES)
        outs = []
        for sub in range(2):
            hs = slice((2 * pair + sub) * HEAD_PAD, (2 * pair + sub + 1) * HEAD_PAD)
            _, l, acc = _softmax_step(q_ref[:, hs], k_ref[0, :, hs], v_ref[0, :, vs], init, mask)
            outs.append(acc / l)
        _gated_pair_out(o_ref, gate_ref, outs, pair)


def _attn_step(q, k, v, gate, batch, tq, n_keys):
    tk = k.shape[1]
    return pl.pallas_call(
        functools.partial(_attn_step_kernel, q_pos0=n_keys - tq, n_keys=n_keys),
        grid=(batch,),
        in_specs=[pl.BlockSpec((tq, MLA_HEADS * HEAD_PAD), lambda b: (b, 0)),
                  pl.BlockSpec((1, tk, MLA_HEADS * HEAD_PAD), lambda b: (b, 0, 0)),
                  pl.BlockSpec((1, tk, MLA_WIDTH), lambda b: (b, 0, 0)),
                  pl.BlockSpec((tq, MLA_WIDTH), lambda b: (b, 0))],
        out_specs=pl.BlockSpec((tq, MLA_WIDTH), lambda b: (b, 0)),
        out_shape=jax.ShapeDtypeStruct((batch * tq, MLA_WIDTH), BF16),
        compiler_params=_cparams("arbitrary"),
        name="attn_step",
    )(q, k, v, gate)


def _pad_cols(a, width):
    return jnp.pad(a, ((0, 0), (0, width - a.shape[1])))


def _ssd_weights(j, in_w, conv_w, conv_b, dt_bias, a_log, d, norm_w, out_w):
    zx = SSD_D_INNER + SSD_CONV_DIM
    return {
        "in_w": jnp.concatenate([in_w[j][:, :zx], _pad_cols(in_w[j][:, zx:], LANES)], axis=1).astype(BF16),
        "conv_w": conv_w[j],
        "conv_b": conv_b[j][None, :],
        "dt_bias": _pad_cols(dt_bias[j][None, :], LANES),
        "a_log": _pad_cols(a_log[j][None, :], LANES),
        "d_exp": jnp.repeat(d[j], SSD_HEAD_DIM)[None, :],
        "norm_w": norm_w[j][None, :],
        "out_w": out_w[j].astype(BF16),
    }


def _head_pad_cols(w, per_head, take, offset=0):
    k = w.shape[0]
    w = w.reshape(k, MLA_HEADS, per_head)[:, :, offset:offset + take]
    return jnp.pad(w, ((0, 0), (0, 0), (0, HEAD_PAD - take))).reshape(k, MLA_HEADS * HEAD_PAD)


def _rotate_half(a):
    half = MLA_ROPE // 2
    return jnp.concatenate([-a[..., half:], a[..., :half]], axis=-1)


def _swap_halves(a):
    half = MLA_ROPE // 2
    return jnp.concatenate([a[..., half:], a[..., :half]], axis=-1)


def _rope_slot(a):
    pad = [(0, 0)] * (a.ndim - 1) + [(MLA_NOPE, HEAD_PAD - MLA_QK)]
    return jnp.pad(a, pad)


def _q_b_cols(w):
    k = w.shape[0]
    w = w.reshape(k, MLA_HEADS, MLA_QK)
    direct = jnp.pad(w, ((0, 0), (0, 0), (0, HEAD_PAD - MLA_QK))).reshape(k, MLA_HEADS // 2, 2 * HEAD_PAD)
    rot = _rope_slot(_rotate_half(w[:, :, MLA_NOPE:])).reshape(k, MLA_HEADS // 2, 2 * HEAD_PAD)
    return jnp.concatenate([direct, rot], axis=-1).reshape(k, MLA_HEADS * 2 * HEAD_PAD)


def _mla_weights(j, in_w, q_a_norm, q_b_w, kv_a_norm, kv_b_w, qn, qr, kn, kr, out_w):
    c1, c2, c3 = MLA_Q_LORA, MLA_Q_LORA + MLA_KV_LORA, MLA_Q_LORA + MLA_KV_LORA + MLA_ROPE
    w = in_w[j]
    kr_cols = jnp.concatenate([_rope_slot(w[:, c2:c3]), _rope_slot(_rotate_half(w[:, c2:c3]))], axis=1)
    zeros = lambda n: jnp.zeros((n,), F32)
    gain_q = jnp.concatenate([qn[j], qr[j], zeros(HEAD_PAD - MLA_QK)])
    return {
        "in_w": jnp.concatenate([w[:, :c2], kr_cols, w[:, c3:]], axis=1).astype(BF16),
        "q_a_norm": q_a_norm[j][None, :],
        "q_b_w": _q_b_cols(q_b_w[j]).astype(BF16),
        "gain_q": jnp.tile(gain_q, 2)[None, :],
        "gain_q_rot": jnp.tile(_rope_slot(_swap_halves(qr[j])), 2)[None, :],
        "gain_kr_rot": _rope_slot(_swap_halves(kr[j]))[None, :],
        "kv_a_norm": kv_a_norm[j][None, :],
        "kv_k_w": _head_pad_cols(kv_b_w[j], MLA_NOPE + MLA_V, MLA_NOPE).astype(BF16),
        "kv_v_w": kv_b_w[j].reshape(MLA_KV_LORA, MLA_HEADS, MLA_NOPE + MLA_V)[:, :, MLA_NOPE:]
                  .reshape(MLA_KV_LORA, MLA_WIDTH).astype(BF16),
        "gain_kr": _rope_slot(kr[j])[None, :],
        "gain_kn": jnp.concatenate([kn[j], zeros(HEAD_PAD - MLA_NOPE)])[None, :],
        "out_w": out_w[j].astype(BF16),
    }


def _rope_tables(pos0, seq, rows):
    inv = 1.0 / (ROPE_BASE ** (jnp.arange(0, MLA_ROPE, 2, dtype=F32) / MLA_ROPE))
    ang = (pos0 + jnp.arange(seq)).astype(F32)[:, None] * inv[None, :]
    cos, sin = jnp.cos(ang), jnp.sin(ang)
    z = lambda n: jnp.zeros((seq, n), F32)
    cos_t = jnp.concatenate([jnp.ones((seq, MLA_NOPE), F32), cos, cos, z(HEAD_PAD - MLA_QK)], axis=1)
    sin_t = jnp.concatenate([z(MLA_NOPE), sin, sin, z(HEAD_PAD - MLA_QK)], axis=1)
    reps = max(1, rows // seq)
    return tuple(jnp.tile(t, (reps, 2)) for t in (cos_t, sin_t))


def _row_tile(m, want):
    tm = min(m, want)
    assert m % tm == 0
    return tm


def _ssd_layer(h, p, lnw, ple, w, batch, seq, chunk, conv_prev, h0):
    tm = _row_tile(h.shape[0], 256)
    zs, xs, b, c, dt, conv_new = _ssd_in(h, lnw, w, conv_prev, batch, seq, min(seq, tm))
    yn, ht = _ssd_scan(zs, xs, b, c, dt, h0, w, batch, seq, chunk)
    h = _out_ple(yn, h, p, w["out_w"], *ple, tm)
    return h, conv_new, ht.reshape(batch, SSD_HEADS, SSD_HEAD_DIM, SSD_STATE)


def _mla_layer(h, p, lnw, ple, w, batch, seq, lat_past, kr_past):
    tm = _row_tile(h.shape[0], 256)
    pos0 = 0 if lat_past is None else lat_past.shape[1]
    rope = _rope_tables(pos0, seq, tm)
    q, lat, kr, gate = _mla_in(h, lnw, w, rope, tm)
    if lat_past is None:
        k, v = _mla_kv(lat, kr, w, tm)
        og = _attn_prompt(q, k, v, gate, batch, seq, 256)
    else:
        n_keys = pos0 + seq
        tk = -(-n_keys // LANES) * LANES
        pad = ((0, 0), (0, tk - n_keys), (0, 0))
        lat_all = jnp.pad(jnp.concatenate([lat_past, lat.reshape(batch, seq, -1)], axis=1), pad)
        kr_all = jnp.pad(jnp.concatenate([kr_past, kr.reshape(batch, seq, -1)], axis=1), pad)
        k, v = _mla_kv(lat_all.reshape(batch * tk, -1), kr_all.reshape(batch * tk, -1), w,
                       math.gcd(batch * tk, 512))
        og = _attn_step(q, k.reshape(batch, tk, -1), v.reshape(batch, tk, -1), gate, batch, seq, n_keys)
    h = _out_ple(og, h, p, w["out_w"], *ple, tm)
    return h, lat.reshape(batch, seq, -1), kr.reshape(batch, seq, -1)


def kernel(x_prompt, x_sample, cache_conv, state_ssm, cache_kv_latent, cache_k_rope, p_prompt, p_sample, ln_w, ssd_in_w, ssd_conv_w, ssd_conv_b, ssd_dt_bias, ssd_A_log, ssd_D, ssd_norm_w, ssd_out_w, mla_in_w, mla_q_a_norm, mla_q_b_w, mla_kv_a_norm, mla_kv_b_w, mla_q_nope_norm, mla_q_rope_norm, mla_k_nope_norm, mla_k_rope_norm, mla_out_w, ple_up_w, ple_norm_w, ple_gate_w):
    bp, tp, d = x_prompt.shape
    bs, ts, _ = x_sample.shape
    hp = x_prompt.reshape(bp * tp, d)
    hs = x_sample.reshape(bs * ts, d)
    conv_p, ssm_p, lat_p, kr_p = [], [], [], []
    conv_s, ssm_s, lat_s, kr_s = [], [], [], []
    for i in range(DEPTH):
        j = i // 2
        lnw = ln_w[i][None, :]
        ple = (ple_norm_w[i][None, :], ple_gate_w[i].astype(BF16), ple_up_w[i].astype(BF16))
        pp = p_prompt[i].reshape(bp * tp, PLE_DIM)
        ps = p_sample[i].reshape(bs * ts, PLE_DIM)
        if i % 2 == 0:
            w = _ssd_weights(j, ssd_in_w, ssd_conv_w, ssd_conv_b, ssd_dt_bias, ssd_A_log, ssd_D,
                             ssd_norm_w, ssd_out_w)
            hp, cp, sp = _ssd_layer(hp, pp, lnw, ple, w, bp, tp, min(tp, SSD_SCAN_CHUNK), None, None)
            hs, cs, ss = _ssd_layer(hs, ps, lnw, ple, w, bs, ts, ts, cache_conv[j], state_ssm[j])
            conv_p.append(cp); ssm_p.append(sp); conv_s.append(cs); ssm_s.append(ss)
        else:
            w = _mla_weights(j, mla_in_w, mla_q_a_norm, mla_q_b_w, mla_kv_a_norm, mla_kv_b_w,
                             mla_q_nope_norm, mla_q_rope_norm, mla_k_nope_norm, mla_k_rope_norm, mla_out_w)
            hp, lp, rp = _mla_layer(hp, pp, lnw, ple, w, bp, tp, None, None)
            hs, ls, rs = _mla_layer(hs, ps, lnw, ple, w, bs, ts, cache_kv_latent[j], cache_k_rope[j])
            lat_p.append(lp); kr_p.append(rp); lat_s.append(ls); kr_s.append(rs)
    return (hp.reshape(bp, tp, d), hs.reshape(bs, ts, d),
            jnp.stack(conv_p), jnp.stack(ssm_p), jnp.stack(lat_p), jnp.stack(kr_p),
            jnp.stack(conv_s), jnp.stack(ssm_s), jnp.stack(lat_s), jnp.stack(kr_s))
```

```python
import functools
import math

import jax
import jax.numpy as jnp
from jax import lax
from jax.experimental import pallas as pl
from jax.experimental.pallas import tpu as pltpu

F32 = jnp.float32
BF16 = jnp.bfloat16

D_MODEL = 1024
DEPTH = 4
CHUNK = 64
PLE_DIM = 256
EPS = 1e-6

SSD_D_INNER = 2 * D_MODEL
SSD_HEAD_DIM = 64
SSD_HEADS = SSD_D_INNER // SSD_HEAD_DIM
SSD_GROUPS = 8
SSD_HPG = SSD_HEADS // SSD_GROUPS
SSD_STATE = 128
SSD_CONV_W = 4
SSD_BC = SSD_GROUPS * SSD_STATE
SSD_CONV_DIM = SSD_D_INNER + 2 * SSD_BC
SSD_GROUP_W = SSD_HPG * SSD_HEAD_DIM

MLA_HEADS = 16
MLA_NOPE = 64
MLA_ROPE = 32
MLA_V = 64
MLA_Q_LORA = 384
MLA_KV_LORA = 256
MLA_QK = MLA_NOPE + MLA_ROPE
MLA_WIDTH = MLA_HEADS * MLA_V
ROPE_BASE = 10000.0

LANES = 128
SUBLANES = 8
HEAD_PAD = LANES
VMEM_LIMIT_BYTES = 52 * 1024 * 1024
NEG_BIG = -1e30
SOFTMAX_LOG2_SCALE = (MLA_QK ** -0.5) * math.log2(math.e)
COL_CHUNK = 512
CONV_ROWS = 128
ATTN_TILE = 256
SSD_BLOCK_ROWS = 256
SSD_SCAN_CHUNK = 128


def _cparams(*sem):
    return pltpu.CompilerParams(dimension_semantics=sem, vmem_limit_bytes=VMEM_LIMIT_BYTES)


def _const_spec(shape):
    nd = len(shape)
    return pl.BlockSpec(shape, lambda *_: (0,) * nd, pipeline_mode=pl.Buffered(1))


def _rms(x, w):
    return x * lax.rsqrt(jnp.mean(x * x, axis=-1, keepdims=True) + EPS) * w


def _dot(a, b):
    return jnp.dot(a, b, preferred_element_type=F32)


def _dot_nt(a, b):
    return lax.dot_general(a, b, (((1,), (1,)), ((), ())), preferred_element_type=F32)


def _dot_tn(a, b):
    return lax.dot_general(a, b, (((0,), (0,)), ((), ())), preferred_element_type=F32)


def _store_dot(o_ref, a, w_ref, col0=0):
    width = o_ref.shape[1]
    for c in range(0, width, COL_CHUNK):
        cw = min(COL_CHUNK, width - c)
        o_ref[:, c:c + cw] = _dot(a, w_ref[:, col0 + c:col0 + c + cw]).astype(o_ref.dtype)


def _ssd_in_kernel(*refs, tiles_per_seq, has_init):
    if has_init:
        (x_ref, lnw_ref, w_ref, convw_ref, convb_ref, cprev_ref,
         z_ref, xs_ref, b_ref, c_ref, dt_ref, convnew_ref, ext) = refs
    else:
        (x_ref, lnw_ref, w_ref, convw_ref, convb_ref,
         z_ref, xs_ref, b_ref, c_ref, dt_ref, convnew_ref, ext) = refs
    tm = x_ref.shape[0]
    tail = SSD_CONV_W - 1
    t = pl.program_id(0) % tiles_per_seq
    xn = _rms(x_ref[...], lnw_ref[...]).astype(BF16)

    @pl.when(t == 0)
    def _init():
        ext[0:SUBLANES, :] = jnp.zeros((SUBLANES, SSD_CONV_DIM), F32)
        if has_init:
            ext[SUBLANES - tail:SUBLANES, :] = cprev_ref[0]

    dt_ref[...] = _dot(xn, w_ref[:, SSD_D_INNER + SSD_CONV_DIM:])
    def project(c0):
        cs = slice(c0, c0 + COL_CHUNK)
        ext[SUBLANES:SUBLANES + tm, cs] = _dot(xn, w_ref[:, SSD_D_INNER + c0:SSD_D_INNER + c0 + COL_CHUNK])
        convnew_ref[0, :, cs] = ext[pl.ds(SUBLANES + tm - tail, tail), cs]

    project(0)
    for c0 in range(0, SSD_CONV_DIM, COL_CHUNK):
        cs = slice(c0, c0 + COL_CHUNK)
        if c0 + COL_CHUNK < SSD_CONV_DIM:
            project(c0 + COL_CHUNK)
        if c0 % (2 * COL_CHUNK) == 0:
            zc = slice(c0 // 2, c0 // 2 + COL_CHUNK)
            z = _dot(xn, w_ref[:, zc])
            z_ref[:, zc] = z * jax.nn.sigmoid(z)
        if c0 < SSD_D_INNER:
            o_ref, o0 = xs_ref, c0
        elif c0 < SSD_D_INNER + SSD_BC:
            o_ref, o0 = b_ref, c0 - SSD_D_INNER
        else:
            o_ref, o0 = c_ref, c0 - SSD_D_INNER - SSD_BC
        for r0 in range(0, tm, CONV_ROWS):
            rb = min(CONV_ROWS, tm - r0)
            for l0 in range(0, COL_CHUNK, LANES):
                ls = slice(c0 + l0, c0 + l0 + LANES)
                e = ext[r0:r0 + rb + SUBLANES, ls]
                a = convb_ref[:, ls]
                for k in range(SSD_CONV_W):
                    shifted = e if k == tail else pltpu.roll(e, tail - k, 0)
                    a = a + shifted[SUBLANES:SUBLANES + rb] * convw_ref[k:k + 1, ls]
                o_ref[r0:r0 + rb, o0 + l0:o0 + l0 + LANES] = (a * jax.nn.sigmoid(a)).astype(o_ref.dtype)
    ext[0:SUBLANES, :] = ext[tm:tm + SUBLANES, :]


def _ssd_in(x, lnw, w, conv_prev, batch, seq, tm):
    m, d = x.shape
    assert seq % tm == 0 and tm % SUBLANES == 0 and m == batch * seq
    tps = seq // tm
    has_init = conv_prev is not None
    tail = SSD_CONV_W - 1
    consts = [lnw, w["in_w"], w["conv_w"], w["conv_b"]]
    row = lambda wd: pl.BlockSpec((tm, wd), lambda i: (i, 0))
    conv_spec = pl.BlockSpec((1, tail, SSD_CONV_DIM), lambda i: (i // tps, 0, 0))
    in_specs = [row(d)] + [_const_spec(a.shape) for a in consts]
    args = [x] + consts
    if has_init:
        in_specs.append(conv_spec)
        args.append(conv_prev)
    return pl.pallas_call(
        functools.partial(_ssd_in_kernel, tiles_per_seq=tps, has_init=has_init),
        grid=(m // tm,),
        in_specs=in_specs,
        out_specs=[row(SSD_D_INNER), row(SSD_D_INNER), row(SSD_BC), row(SSD_BC), row(LANES), conv_spec],
        out_shape=[jax.ShapeDtypeStruct((m, SSD_D_INNER), F32),
                   jax.ShapeDtypeStruct((m, SSD_D_INNER), F32),
                   jax.ShapeDtypeStruct((m, SSD_BC), BF16),
                   jax.ShapeDtypeStruct((m, SSD_BC), BF16),
                   jax.ShapeDtypeStruct((m, LANES), F32),
                   jax.ShapeDtypeStruct((batch, tail, SSD_CONV_DIM), F32)],
        scratch_shapes=[pltpu.VMEM((tm + SUBLANES, SSD_CONV_DIM), F32)],
        compiler_params=_cparams("arbitrary"),
        name="ssd_in",
    )(*args)


def _dot_exact_lhs(a, x):
    x1 = x.astype(BF16)
    r1 = x - x1.astype(F32)
    x2 = r1.astype(BF16)
    x3 = (r1 - x2.astype(F32)).astype(BF16)
    return _dot(a, x1) + _dot(a, x2) + _dot(a, x3)


def _expand_heads(cols, g, lane_head):
    shape = (cols.shape[0], SSD_GROUP_W)
    h0 = SSD_HPG * g
    out = jnp.broadcast_to(cols[:, h0 + SSD_HPG - 1:h0 + SSD_HPG], shape)
    for r in range(SSD_HPG - 2, -1, -1):
        out = jnp.where(lane_head[:shape[0]] == r, jnp.broadcast_to(cols[:, h0 + r:h0 + r + 1], shape), out)
    return out


def _ssd_kernel(*refs, L, has_init):
    if has_init:
        (zs_ref, xs_ref, b_ref, c_ref, dt_ref, h0_ref, dtb_ref, alog_ref,
         dexp_ref, normw_ref, yn_ref, ht_ref, st) = refs
    else:
        (zs_ref, xs_ref, b_ref, c_ref, dt_ref, dtb_ref, alog_ref,
         dexp_ref, normw_ref, yn_ref, ht_ref, st) = refs
    c = pl.program_id(1)
    last = pl.num_programs(1) - 1

    @pl.when(c == 0)
    def _init():
        if has_init:
            for g in range(SSD_GROUPS):
                st[g] = h0_ref[0, g * SSD_GROUP_W:(g + 1) * SSD_GROUP_W, :].T
        else:
            st[...] = jnp.zeros(st.shape, F32)

    rows = dt_ref.shape[0]
    n_chunks = rows // L
    dt = jax.nn.softplus(dt_ref[...] + dtb_ref[...])
    dta = dt * (-jnp.exp(alog_ref[...]))
    row = lax.broadcasted_iota(jnp.int32, (rows, rows), 0)
    col = lax.broadcasted_iota(jnp.int32, (rows, rows), 1)
    same_chunk_tri = jnp.logical_and(row >= col, row // L == col // L)
    acs = _dot_exact_lhs(same_chunk_tri.astype(BF16), dta)
    acs_t = acs.T
    dt_t = dt.T
    e_acs = jnp.exp(acs)
    tri = lax.broadcasted_iota(jnp.int32, (L, L), 0) >= lax.broadcasted_iota(jnp.int32, (L, L), 1)
    lane_head = lax.broadcasted_iota(jnp.int32, (L, SSD_GROUP_W), 1) // SSD_HEAD_DIM

    for ci in range(n_chunks):
        rs = slice(ci * L, (ci + 1) * L)
        acs_c = acs[rs]
        a_last = acs_c[L - 1:L, :]
        e_acs_c = e_acs[rs]
        wdt = jnp.exp(a_last - acs_c) * dt[rs]
        cdec = jnp.exp(a_last)
        for g in range(SSD_GROUPS):
            gs = slice(g * SSD_GROUP_W, (g + 1) * SSD_GROUP_W)
            xg = xs_ref[rs, gs]
            bg = b_ref[rs, g * SSD_STATE:(g + 1) * SSD_STATE]
            cg = c_ref[rs, g * SSD_STATE:(g + 1) * SSD_STATE]
            cb = _dot_nt(cg, bg)
            sg = st[g]
            y = _dot(cg, sg.astype(BF16)) * _expand_heads(e_acs_c, g, lane_head)
            for r in range(SSD_HPG):
                h = SSD_HPG * g + r
                seg = acs_c[:, h:h + 1] - acs_t[h:h + 1, rs]
                m = cb * jnp.exp(jnp.where(tri, seg, -jnp.inf)) * dt_t[h:h + 1, rs]
                xh = jnp.where(lane_head == r, xg, 0.0).astype(BF16)
                y = y + _dot(m.astype(BF16), xh)
            y = y + dexp_ref[:, gs] * xg
            y = y * zs_ref[rs, gs]
            yn_ref[rs, gs] = _rms(y, normw_ref[:, gs]).astype(yn_ref.dtype)
            xw = (xg * _expand_heads(wdt, g, lane_head)).astype(BF16)
            st[g] = sg * _expand_heads(cdec, g, lane_head) + _dot_tn(bg, xw)

    @pl.when(c == last)
    def _state_out():
        for g in range(SSD_GROUPS):
            ht_ref[0, g * SSD_GROUP_W:(g + 1) * SSD_GROUP_W, :] = st[g].T


def _ssd_scan(zs, xs, b, c, dt, h0, w, batch, seq, L):
    rows = min(seq, SSD_BLOCK_ROWS)
    nc = seq // rows
    assert nc * rows == seq and rows % L == 0 and L % SUBLANES == 0
    has_init = h0 is not None
    row_spec = lambda wd: pl.BlockSpec((rows, wd), lambda bi, ci: (bi * nc + ci, 0))
    state_spec = pl.BlockSpec((1, SSD_D_INNER, SSD_STATE), lambda bi, ci: (bi, 0, 0))
    in_specs = [row_spec(SSD_D_INNER), row_spec(SSD_D_INNER), row_spec(SSD_BC), row_spec(SSD_BC), row_spec(LANES)]
    args = [zs, xs, b, c, dt]
    if has_init:
        in_specs.append(state_spec)
        args.append(h0.reshape(batch, SSD_D_INNER, SSD_STATE))
    consts = [w["dt_bias"], w["a_log"], w["d_exp"], w["norm_w"]]
    in_specs += [_const_spec(a.shape) for a in consts]
    args += consts
    return pl.pallas_call(
        functools.partial(_ssd_kernel, L=L, has_init=has_init),
        grid=(batch, nc),
        in_specs=in_specs,
        out_specs=[row_spec(SSD_D_INNER), state_spec],
        out_shape=[jax.ShapeDtypeStruct((batch * seq, SSD_D_INNER), BF16),
                   jax.ShapeDtypeStruct((batch, SSD_D_INNER, SSD_STATE), F32)],
        scratch_shapes=[pltpu.VMEM((SSD_GROUPS, SSD_STATE, SSD_GROUP_W), F32)],
        compiler_params=_cparams("arbitrary", "arbitrary"),
        name="ssd_scan",
    )(*args)


def _out_ple_kernel(y_ref, h_ref, p_ref, wout_ref, plenorm_ref, wgate_ref, wup_ref, o_ref):
    h1 = h_ref[...] + _dot(y_ref[...], wout_ref[...])
    hn = _rms(h1, plenorm_ref[...]).astype(BF16)
    pb = p_ref[...].astype(BF16)
    for c in range(0, D_MODEL, COL_CHUNK):
        cs = slice(c, c + COL_CHUNK)
        gate = jax.nn.sigmoid(_dot(hn, wgate_ref[:, cs]))
        o_ref[:, cs] = h1[:, cs] + _dot(pb, wup_ref[:, cs]) * gate


def _out_ple(y, h, p, wout, plenorm, wgate, wup, tm):
    m, kd = y.shape
    assert m % tm == 0
    return pl.pallas_call(
        _out_ple_kernel,
        grid=(m // tm,),
        in_specs=[pl.BlockSpec((tm, kd), lambda i: (i, 0)),
                  pl.BlockSpec((tm, D_MODEL), lambda i: (i, 0)),
                  pl.BlockSpec((tm, PLE_DIM), lambda i: (i, 0)),
                  _const_spec(wout.shape), _const_spec(plenorm.shape),
                  _const_spec(wgate.shape), _const_spec(wup.shape)],
        out_specs=pl.BlockSpec((tm, D_MODEL), lambda i: (i, 0)),
        out_shape=jax.ShapeDtypeStruct((m, D_MODEL), F32),
        compiler_params=_cparams("arbitrary"),
        name="out_ple",
    )(y, h, p, wout, plenorm, wgate, wup)


def _dot_exact_rhs(x, a):
    x1 = x.astype(BF16)
    r1 = x - x1.astype(F32)
    x2 = r1.astype(BF16)
    x3 = (r1 - x2.astype(F32)).astype(BF16)
    return _dot(x1, a) + _dot(x2, a) + _dot(x3, a)


def _head_part_matrix(width):
    def part(idx):
        within = idx % HEAD_PAD
        return (idx // HEAD_PAD) * 4 + jnp.where(within < MLA_NOPE, 0, jnp.where(within < MLA_QK, 1, 2))
    rows = part(lax.broadcasted_iota(jnp.int32, (width, width), 0))
    cols_i = lax.broadcasted_iota(jnp.int32, (width, width), 1)
    cols = part(cols_i)
    same = jnp.logical_and(rows == cols, cols_i % HEAD_PAD < MLA_QK)
    lane = lax.broadcasted_iota(jnp.int32, (1, width), 1) % HEAD_PAD
    inv_count = jnp.where(lane < MLA_NOPE, 1.0 / MLA_NOPE, jnp.where(lane < MLA_QK, 1.0 / MLA_ROPE, 0.0))
    return jnp.where(same, 1.0, 0.0).astype(BF16), inv_count


def _head_norm_rope(x, x_rot, pmat, inv_count, gain_cos, gain_sin):
    inv = lax.rsqrt(_dot_exact_rhs(x * x, pmat) * inv_count + EPS)
    return (x * inv) * gain_cos + (x_rot * inv) * gain_sin


def _mla_in_kernel(h_ref, lnw_ref, win_ref, qan_ref, wqb_ref, kvan_ref, gq_ref, gqrot_ref, gkr_ref, gkrrot_ref,
                   cos_ref, sin_ref, q_ref, lat_ref, kr_ref, gate_ref, *, q_transposed):
    xn = _rms(h_ref[...], lnw_ref[...]).astype(BF16)
    c_q, c_kv, c_kr = 0, MLA_Q_LORA, MLA_Q_LORA + MLA_KV_LORA
    c_gate = c_kr + 2 * HEAD_PAD
    pair_w = 2 * HEAD_PAD
    _store_dot(gate_ref, xn, win_ref, c_gate)
    lat_ref[...] = _rms(_dot(xn, win_ref[:, c_kv:c_kr]), kvan_ref[...])
    pmat, inv_count = _head_part_matrix(pair_w)
    cos_t, sin_t = cos_ref[...], sin_ref[...]
    kr_raw = _dot(xn, win_ref[:, c_kr:c_gate])
    kr = _head_norm_rope(kr_raw[:, :HEAD_PAD], kr_raw[:, HEAD_PAD:], pmat[:HEAD_PAD, :HEAD_PAD],
                         inv_count[:, :HEAD_PAD], gkr_ref[...] * cos_t[:, :HEAD_PAD],
                         gkrrot_ref[...] * sin_t[:, :HEAD_PAD])
    kr_ref[...] = kr[:, MLA_NOPE:MLA_QK]
    qa = _rms(_dot(xn, win_ref[:, c_q:c_kv]), qan_ref[...]).astype(BF16)
    gain_cos = gq_ref[...] * cos_t
    gain_sin = gqrot_ref[...] * sin_t
    for pair in range(MLA_HEADS // 2):
        qq = _dot(qa, wqb_ref[:, 2 * pair * pair_w:2 * (pair + 1) * pair_w])
        qh = _head_norm_rope(qq[:, :pair_w], qq[:, pair_w:], pmat, inv_count, gain_cos, gain_sin)
        if q_transposed:
            q_ref[0, pair * pair_w:(pair + 1) * pair_w, :] = qh.T.astype(q_ref.dtype)
        else:
            q_ref[:, pair * pair_w:(pair + 1) * pair_w] = qh.astype(q_ref.dtype)


def _mla_in(h, lnw, w, rope, tm, q_transposed):
    m = h.shape[0]
    assert m % tm == 0
    cos_t, sin_t = rope
    period = cos_t.shape[0] // tm
    assert period * tm == cos_t.shape[0]
    rope_spec = pl.BlockSpec((tm, 2 * HEAD_PAD), lambda i: (i % period, 0))
    consts = [lnw, w["in_w"], w["q_a_norm"], w["q_b_w"], w["kv_a_norm"],
              w["gain_q"], w["gain_q_rot"], w["gain_kr"], w["gain_kr_rot"]]
    row = lambda wd: pl.BlockSpec((tm, wd), lambda i: (i, 0))
    qw = MLA_HEADS * HEAD_PAD
    if q_transposed:
        seq = period * tm
        q_spec = pl.BlockSpec((1, qw, tm), lambda i: (i // period, 0, i % period))
        q_shape = jax.ShapeDtypeStruct((m // seq, qw, seq), BF16)
    else:
        q_spec, q_shape = row(qw), jax.ShapeDtypeStruct((m, qw), BF16)
    return pl.pallas_call(
        functools.partial(_mla_in_kernel, q_transposed=q_transposed),
        grid=(m // tm,),
        in_specs=[row(D_MODEL)] + [_const_spec(a.shape) for a in consts] + [rope_spec] * 2,
        out_specs=[q_spec, row(MLA_KV_LORA), row(MLA_ROPE), row(MLA_WIDTH)],
        out_shape=[q_shape,
                   jax.ShapeDtypeStruct((m, MLA_KV_LORA), F32),
                   jax.ShapeDtypeStruct((m, MLA_ROPE), F32),
                   jax.ShapeDtypeStruct((m, MLA_WIDTH), F32)],
        compiler_params=_cparams("arbitrary"),
        name="mla_in",
    )(h, *consts, cos_t, sin_t)


def _mla_kv_kernel(lat_ref, kr_ref, wk_ref, wv_ref, gkn_ref, k_ref, v_ref, *, v_transposed):
    latb = lat_ref[...].astype(BF16)
    if v_transposed:
        for c in range(0, MLA_WIDTH, COL_CHUNK):
            v_ref[0, c:c + COL_CHUNK, :] = _dot(latb, wv_ref[:, c:c + COL_CHUNK]).T.astype(v_ref.dtype)
    else:
        _store_dot(v_ref, latb, wv_ref)
    rows = latb.shape[0]
    kr = jnp.concatenate([jnp.zeros((rows, MLA_NOPE), F32), kr_ref[...],
                          jnp.zeros((rows, HEAD_PAD - MLA_QK), F32)], axis=1)
    for h in range(MLA_HEADS):
        hs = slice(h * HEAD_PAD, (h + 1) * HEAD_PAD)
        kh = _dot(latb, wk_ref[:, hs])
        ms = jnp.sum(kh * kh, axis=-1, keepdims=True) * (1.0 / MLA_NOPE)
        k_ref[:, hs] = (kh * lax.rsqrt(ms + EPS) * gkn_ref[...] + kr).astype(k_ref.dtype)


def _mla_kv(lat, kr, w, tm, v_transposed):
    m = lat.shape[0]
    assert m % tm == 0
    consts = [w["kv_k_w"], w["kv_v_w"], w["gain_kn"]]
    row = lambda wd: pl.BlockSpec((tm, wd), lambda i: (i, 0))
    if v_transposed:
        v_spec = pl.BlockSpec((1, MLA_WIDTH, tm), lambda i: (i, 0, 0))
        v_shape = jax.ShapeDtypeStruct((m // tm, MLA_WIDTH, tm), BF16)
    else:
        v_spec, v_shape = row(MLA_WIDTH), jax.ShapeDtypeStruct((m, MLA_WIDTH), BF16)
    return pl.pallas_call(
        functools.partial(_mla_kv_kernel, v_transposed=v_transposed),
        grid=(m // tm,),
        in_specs=[row(MLA_KV_LORA), row(MLA_ROPE)] + [_const_spec(a.shape) for a in consts],
        out_specs=[row(MLA_HEADS * HEAD_PAD), v_spec],
        out_shape=[jax.ShapeDtypeStruct((m, MLA_HEADS * HEAD_PAD), BF16), v_shape],
        compiler_params=_cparams("arbitrary"),
        name="mla_kv",
    )(lat, kr, *consts)


def _softmax_step(q, k, v, carry, mask):
    m_prev, l_prev, acc = carry
    s = _dot_nt(q, k) * (MLA_QK ** -0.5)
    if mask is not None:
        s = jnp.where(mask, s, -jnp.inf)
    m_new = jnp.maximum(m_prev, jnp.max(s, axis=-1, keepdims=True))
    alpha = jnp.exp(m_prev - m_new)
    p = jnp.exp(s - m_new)
    l_new = alpha * l_prev + jnp.sum(p, axis=-1, keepdims=True)
    return m_new, l_new, alpha * acc + _dot(p.astype(BF16), v)


def _gated_pair_out(o_ref, gate_ref, outs, pair):
    lane = lax.broadcasted_iota(jnp.int32, outs[0].shape, 1)
    ps = slice(pair * LANES, (pair + 1) * LANES)
    g = gate_ref[:, ps]
    o_ref[:, ps] = (jnp.where(lane < MLA_V, outs[0], outs[1]) * (g * jax.nn.sigmoid(g))).astype(o_ref.dtype)


def _attn_kernel(q_ref, k_ref, v_ref, gate_ref, o_ref, m_s, l_s, acc_s, s_s, *, tq):
    i = pl.program_id(1)
    key_c = lax.broadcasted_iota(jnp.int32, (tq, tq), 0) // CHUNK
    qry_c = lax.broadcasted_iota(jnp.int32, (tq, tq), 1) // CHUNK
    diag_mask = key_c <= qry_c
    m_s[...] = jnp.full(m_s.shape, NEG_BIG, F32)
    l_s[...] = jnp.zeros(l_s.shape, F32)
    acc_s[...] = jnp.zeros(acc_s.shape, F32)

    def tile_step(j, mask):
        rows = pl.ds(pl.multiple_of(j * tq, tq), tq)
        for h in range(MLA_HEADS):
            hs = slice(h * HEAD_PAD, (h + 1) * HEAD_PAD)
            s_s[h] = _dot(k_ref[rows, hs], q_ref[0, hs, :])
        for h in range(MLA_HEADS):
            s = s_s[h]
            if mask is not None:
                s = jnp.where(mask, s, -jnp.inf)
            m_prev = m_s[h]
            m_new = jnp.maximum(m_prev, jnp.max(s, axis=0, keepdims=True))
            alpha = jnp.exp2((m_prev - m_new) * SOFTMAX_LOG2_SCALE)
            p = jnp.exp2((s - m_new[0:1, :]) * SOFTMAX_LOG2_SCALE)
            l_s[h] = alpha * l_s[h] + jnp.sum(p, axis=0, keepdims=True)
            pv = _dot(v_ref[j, h * MLA_V:(h + 1) * MLA_V, :], p.astype(BF16))
            acc_s[h] = alpha[0:1, :] * acc_s[h] + pv
            m_s[h] = m_new

    def body(j, carry):
        tile_step(j, None)
        return carry

    lax.fori_loop(0, i, body, 0)
    tile_step(i, diag_mask)
    for pair in range(MLA_HEADS // 2):
        o_t = jnp.concatenate([acc_s[2 * pair + sub] / l_s[2 * pair + sub][0:1, :] for sub in range(2)], axis=0)
        ps = slice(pair * LANES, (pair + 1) * LANES)
        g = gate_ref[:, ps]
        o_ref[:, ps] = (o_t.T * (g * jax.nn.sigmoid(g))).astype(o_ref.dtype)


def _attn_prompt(q_t, k, v_t, gate, batch, seq, tq):
    nq = seq // tq
    assert nq * tq == seq and tq % CHUNK == 0
    return pl.pallas_call(
        functools.partial(_attn_kernel, tq=tq),
        grid=(batch, nq),
        in_specs=[pl.BlockSpec((1, MLA_HEADS * HEAD_PAD, tq), lambda b, i: (b, 0, i)),
                  pl.BlockSpec((seq, MLA_HEADS * HEAD_PAD), lambda b, i: (b, 0)),
                  pl.BlockSpec((nq, MLA_WIDTH, tq), lambda b, i: (b, 0, 0)),
                  pl.BlockSpec((tq, MLA_WIDTH), lambda b, i: (b * nq + i, 0))],
        out_specs=pl.BlockSpec((tq, MLA_WIDTH), lambda b, i: (b * nq + i, 0)),
        out_shape=jax.ShapeDtypeStruct((batch * seq, MLA_WIDTH), BF16),
        scratch_shapes=[pltpu.VMEM((MLA_HEADS, SUBLANES, tq), F32), pltpu.VMEM((MLA_HEADS, SUBLANES, tq), F32),
                        pltpu.VMEM((MLA_HEADS, MLA_V, tq), F32), pltpu.VMEM((MLA_HEADS, tq, tq), F32)],
        compiler_params=_cparams("arbitrary", "arbitrary"),
        name="attn_prompt",
    )(q_t, k, v_t, gate)


def _attn_step_kernel(q_ref, k_ref, v_ref, gate_ref, o_ref, *, q_pos0, n_keys):
    tq = q_ref.shape[0]
    tk = k_ref.shape[1]
    k_pos = lax.broadcasted_iota(jnp.int32, (tq, tk), 1)
    q_pos = lax.broadcasted_iota(jnp.int32, (tq, tk), 0) + q_pos0
    mask = jnp.logical_and(k_pos < n_keys, k_pos // CHUNK <= q_pos // CHUNK)
    init = (jnp.full((tq, 1), NEG_BIG, F32), jnp.zeros((tq, 1), F32), jnp.zeros((tq, LANES), F32))
    for pair in range(MLA_HEADS // 2):
        vs = slice(pair * LANES, (pair + 1) * LANES)
        outs = []
        for sub in range(2):
            hs = slice((2 * pair + sub) * HEAD_PAD, (2 * pair + sub + 1) * HEAD_PAD)
            _, l, acc = _softmax_step(q_ref[:, hs], k_ref[0, :, hs], v_ref[0, :, vs], init, mask)
            outs.append(acc / l)
        _gated_pair_out(o_ref, gate_ref, outs, pair)


def _attn_step(q, k, v, gate, batch, tq, n_keys):
    tk = k.shape[1]
    return pl.pallas_call(
        functools.partial(_attn_step_kernel, q_pos0=n_keys - tq, n_keys=n_keys),
        grid=(batch,),
        in_specs=[pl.BlockSpec((tq, MLA_HEADS * HEAD_PAD), lambda b: (b, 0)),
                  pl.BlockSpec((1, tk, MLA_HEADS * HEAD_PAD), lambda b: (b, 0, 0)),
                  pl.BlockSpec((1, tk, MLA_WIDTH), lambda b: (b, 0, 0)),
                  pl.BlockSpec((tq, MLA_WIDTH), lambda b: (b, 0))],
        out_specs=pl.BlockSpec((tq, MLA_WIDTH), lambda b: (b, 0)),
        out_shape=jax.ShapeDtypeStruct((batch * tq, MLA_WIDTH), BF16),
        compiler_params=_cparams("arbitrary"),
        name="attn_step",
    )(q, k, v, gate)


def _pad_cols(a, width):
    return jnp.pad(a, ((0, 0), (0, width - a.shape[1])))


def _ssd_weights(j, in_w, conv_w, conv_b, dt_bias, a_log, d, norm_w, out_w):
    zx = SSD_D_INNER + SSD_CONV_DIM
    return {
        "in_w": jnp.concatenate([in_w[j][:, :zx], _pad_cols(in_w[j][:, zx:], LANES)], axis=1).astype(BF16),
        "conv_w": conv_w[j],
        "conv_b": conv_b[j][None, :],
        "dt_bias": _pad_cols(dt_bias[j][None, :], LANES),
        "a_log": _pad_cols(a_log[j][None, :], LANES),
        "d_exp": jnp.repeat(d[j], SSD_HEAD_DIM)[None, :],
        "norm_w": norm_w[j][None, :],
        "out_w": out_w[j].astype(BF16),
    }


def _head_pad_cols(w, per_head, take, offset=0):
    k = w.shape[0]
    w = w.reshape(k, MLA_HEADS, per_head)[:, :, offset:offset + take]
    return jnp.pad(w, ((0, 0), (0, 0), (0, HEAD_PAD - take))).reshape(k, MLA_HEADS * HEAD_PAD)


def _rotate_half(a):
    half = MLA_ROPE // 2
    return jnp.concatenate([-a[..., half:], a[..., :half]], axis=-1)


def _swap_halves(a):
    half = MLA_ROPE // 2
    return jnp.concatenate([a[..., half:], a[..., :half]], axis=-1)


def _rope_slot(a):
    pad = [(0, 0)] * (a.ndim - 1) + [(MLA_NOPE, HEAD_PAD - MLA_QK)]
    return jnp.pad(a, pad)


def _q_b_cols(w):
    k = w.shape[0]
    w = w.reshape(k, MLA_HEADS, MLA_QK)
    direct = jnp.pad(w, ((0, 0), (0, 0), (0, HEAD_PAD - MLA_QK))).reshape(k, MLA_HEADS // 2, 2 * HEAD_PAD)
    rot = _rope_slot(_rotate_half(w[:, :, MLA_NOPE:])).reshape(k, MLA_HEADS // 2, 2 * HEAD_PAD)
    return jnp.concatenate([direct, rot], axis=-1).reshape(k, MLA_HEADS * 2 * HEAD_PAD)


def _mla_weights(j, in_w, q_a_norm, q_b_w, kv_a_norm, kv_b_w, qn, qr, kn, kr, out_w):
    c1, c2, c3 = MLA_Q_LORA, MLA_Q_LORA + MLA_KV_LORA, MLA_Q_LORA + MLA_KV_LORA + MLA_ROPE
    w = in_w[j]
    kr_cols = jnp.concatenate([_rope_slot(w[:, c2:c3]), _rope_slot(_rotate_half(w[:, c2:c3]))], axis=1)
    zeros = lambda n: jnp.zeros((n,), F32)
    gain_q = jnp.concatenate([qn[j], qr[j], zeros(HEAD_PAD - MLA_QK)])
    return {
        "in_w": jnp.concatenate([w[:, :c2], kr_cols, w[:, c3:]], axis=1).astype(BF16),
        "q_a_norm": q_a_norm[j][None, :],
        "q_b_w": _q_b_cols(q_b_w[j]).astype(BF16),
        "gain_q": jnp.tile(gain_q, 2)[None, :],
        "gain_q_rot": jnp.tile(_rope_slot(_swap_halves(qr[j])), 2)[None, :],
        "gain_kr_rot": _rope_slot(_swap_halves(kr[j]))[None, :],
        "kv_a_norm": kv_a_norm[j][None, :],
        "kv_k_w": _head_pad_cols(kv_b_w[j], MLA_NOPE + MLA_V, MLA_NOPE).astype(BF16),
        "kv_v_w": kv_b_w[j].reshape(MLA_KV_LORA, MLA_HEADS, MLA_NOPE + MLA_V)[:, :, MLA_NOPE:]
                  .reshape(MLA_KV_LORA, MLA_WIDTH).astype(BF16),
        "gain_kr": _rope_slot(kr[j])[None, :],
        "gain_kn": jnp.concatenate([kn[j], zeros(HEAD_PAD - MLA_NOPE)])[None, :],
        "out_w": out_w[j].astype(BF16),
    }


def _rope_tables(pos0, seq, rows):
    inv = 1.0 / (ROPE_BASE ** (jnp.arange(0, MLA_ROPE, 2, dtype=F32) / MLA_ROPE))
    ang = (pos0 + jnp.arange(seq)).astype(F32)[:, None] * inv[None, :]
    cos, sin = jnp.cos(ang), jnp.sin(ang)
    z = lambda n: jnp.zeros((seq, n), F32)
    cos_t = jnp.concatenate([jnp.ones((seq, MLA_NOPE), F32), cos, cos, z(HEAD_PAD - MLA_QK)], axis=1)
    sin_t = jnp.concatenate([z(MLA_NOPE), sin, sin, z(HEAD_PAD - MLA_QK)], axis=1)
    reps = max(1, rows // seq)
    return tuple(jnp.tile(t, (reps, 2)) for t in (cos_t, sin_t))


def _row_tile(m, want):
    tm = min(m, want)
    assert m % tm == 0
    return tm


def _ssd_layer(h, p, lnw, ple, w, batch, seq, chunk, conv_prev, h0):
    tm = _row_tile(h.shape[0], 256)
    zs, xs, b, c, dt, conv_new = _ssd_in(h, lnw, w, conv_prev, batch, seq, min(seq, tm))
    yn, ht = _ssd_scan(zs, xs, b, c, dt, h0, w, batch, seq, chunk)
    h = _out_ple(yn, h, p, w["out_w"], *ple, tm)
    return h, conv_new, ht.reshape(batch, SSD_HEADS, SSD_HEAD_DIM, SSD_STATE)


def _mla_layer(h, p, lnw, ple, w, batch, seq, lat_past, kr_past):
    tm = _row_tile(h.shape[0], 256)
    pos0 = 0 if lat_past is None else lat_past.shape[1]
    rope = _rope_tables(pos0, seq, tm)
    prompt = lat_past is None
    q, lat, kr, gate = _mla_in(h, lnw, w, rope, tm, q_transposed=prompt)
    if prompt:
        tile = min(seq, ATTN_TILE)
        k, v_t = _mla_kv(lat, kr, w, tile, v_transposed=True)
        og = _attn_prompt(q, k, v_t, gate, batch, seq, tile)
    else:
        n_keys = pos0 + seq
        tk = -(-n_keys // LANES) * LANES
        pad = ((0, 0), (0, tk - n_keys), (0, 0))
        lat_all = jnp.pad(jnp.concatenate([lat_past, lat.reshape(batch, seq, -1)], axis=1), pad)
        kr_all = jnp.pad(jnp.concatenate([kr_past, kr.reshape(batch, seq, -1)], axis=1), pad)
        k, v = _mla_kv(lat_all.reshape(batch * tk, -1), kr_all.reshape(batch * tk, -1), w,
                       math.gcd(batch * tk, 512), v_transposed=False)
        og = _attn_step(q, k.reshape(batch, tk, -1), v.reshape(batch, tk, -1), gate, batch, seq, n_keys)
    h = _out_ple(og, h, p, w["out_w"], *ple, tm)
    return h, lat.reshape(batch, seq, -1), kr.reshape(batch, seq, -1)


def kernel(x_prompt, x_sample, cache_conv, state_ssm, cache_kv_latent, cache_k_rope, p_prompt, p_sample, ln_w, ssd_in_w, ssd_conv_w, ssd_conv_b, ssd_dt_bias, ssd_A_log, ssd_D, ssd_norm_w, ssd_out_w, mla_in_w, mla_q_a_norm, mla_q_b_w, mla_kv_a_norm, mla_kv_b_w, mla_q_nope_norm, mla_q_rope_norm, mla_k_nope_norm, mla_k_rope_norm, mla_out_w, ple_up_w, ple_norm_w, ple_gate_w):
    bp, tp, d = x_prompt.shape
    bs, ts, _ = x_sample.shape
    hp = x_prompt.reshape(bp * tp, d)
    hs = x_sample.reshape(bs * ts, d)
    conv_p, ssm_p, lat_p, kr_p = [], [], [], []
    conv_s, ssm_s, lat_s, kr_s = [], [], [], []
    for i in range(DEPTH):
        j = i // 2
        lnw = ln_w[i][None, :]
        ple = (ple_norm_w[i][None, :], ple_gate_w[i].astype(BF16), ple_up_w[i].astype(BF16))
        pp = p_prompt[i].reshape(bp * tp, PLE_DIM)
        ps = p_sample[i].reshape(bs * ts, PLE_DIM)
        if i % 2 == 0:
            w = _ssd_weights(j, ssd_in_w, ssd_conv_w, ssd_conv_b, ssd_dt_bias, ssd_A_log, ssd_D,
                             ssd_norm_w, ssd_out_w)
            hp, cp, sp = _ssd_layer(hp, pp, lnw, ple, w, bp, tp, min(tp, SSD_SCAN_CHUNK), None, None)
            hs, cs, ss = _ssd_layer(hs, ps, lnw, ple, w, bs, ts, ts, cache_conv[j], state_ssm[j])
            conv_p.append(cp); ssm_p.append(sp); conv_s.append(cs); ssm_s.append(ss)
        else:
            w = _mla_weights(j, mla_in_w, mla_q_a_norm, mla_q_b_w, mla_kv_a_norm, mla_kv_b_w,
                             mla_q_nope_norm, mla_q_rope_norm, mla_k_nope_norm, mla_k_rope_norm, mla_out_w)
            hp, lp, rp = _mla_layer(hp, pp, lnw, ple, w, bp, tp, None, None)
            hs, ls, rs = _mla_layer(hs, ps, lnw, ple, w, bs, ts, cache_kv_latent[j], cache_k_rope[j])
            lat_p.append(lp); kr_p.append(rp); lat_s.append(ls); kr_s.append(rs)
    return (hp.reshape(bp, tp, d), hs.reshape(bs, ts, d),
            jnp.stack(conv_p), jnp.stack(ssm_p), jnp.stack(lat_p), jnp.stack(kr_p),
            jnp.stack(conv_s), jnp.stack(ssm_s), jnp.stack(lat_s), jnp.stack(kr_s))
```

```python
import functools
import math

import jax
import jax.numpy as jnp
from jax import lax
from jax.experimental import pallas as pl
from jax.experimental.pallas import tpu as pltpu

F32 = jnp.float32
BF16 = jnp.bfloat16

D_MODEL = 1024
DEPTH = 4
CHUNK = 64
PLE_DIM = 256
EPS = 1e-6

SSD_D_INNER = 2 * D_MODEL
SSD_HEAD_DIM = 64
SSD_HEADS = SSD_D_INNER // SSD_HEAD_DIM
SSD_GROUPS = 8
SSD_HPG = SSD_HEADS // SSD_GROUPS
SSD_STATE = 128
SSD_CONV_W = 4
SSD_BC = SSD_GROUPS * SSD_STATE
SSD_CONV_DIM = SSD_D_INNER + 2 * SSD_BC
SSD_GROUP_W = SSD_HPG * SSD_HEAD_DIM

MLA_HEADS = 16
MLA_NOPE = 64
MLA_ROPE = 32
MLA_V = 64
MLA_Q_LORA = 384
MLA_KV_LORA = 256
MLA_QK = MLA_NOPE + MLA_ROPE
MLA_WIDTH = MLA_HEADS * MLA_V
ROPE_BASE = 10000.0

LANES = 128
SUBLANES = 8
HEAD_PAD = LANES
VMEM_LIMIT_BYTES = 52 * 1024 * 1024
NEG_BIG = -1e30
SOFTMAX_LOG2_SCALE = (MLA_QK ** -0.5) * math.log2(math.e)
COL_CHUNK = 512
CONV_ROWS = 128
ATTN_TILE = 256
SSD_BLOCK_ROWS = 256
SSD_SCAN_CHUNK = 128


def _cparams(*sem):
    return pltpu.CompilerParams(dimension_semantics=sem, vmem_limit_bytes=VMEM_LIMIT_BYTES)


def _const_spec(shape):
    nd = len(shape)
    return pl.BlockSpec(shape, lambda *_: (0,) * nd, pipeline_mode=pl.Buffered(1))


def _rms(x, w):
    return x * lax.rsqrt(jnp.mean(x * x, axis=-1, keepdims=True) + EPS) * w


def _dot(a, b):
    return jnp.dot(a, b, preferred_element_type=F32)


def _dot_nt(a, b):
    return lax.dot_general(a, b, (((1,), (1,)), ((), ())), preferred_element_type=F32)


def _dot_tn(a, b):
    return lax.dot_general(a, b, (((0,), (0,)), ((), ())), preferred_element_type=F32)


def _store_dot(o_ref, a, w_ref, col0=0):
    width = o_ref.shape[1]
    for c in range(0, width, COL_CHUNK):
        cw = min(COL_CHUNK, width - c)
        o_ref[:, c:c + cw] = _dot(a, w_ref[:, col0 + c:col0 + c + cw]).astype(o_ref.dtype)


def _ssd_in_kernel(*refs, tiles_per_seq, has_init):
    if has_init:
        (x_ref, lnw_ref, w_ref, convw_ref, convb_ref, cprev_ref,
         z_ref, xs_ref, b_ref, c_ref, dt_ref, convnew_ref, ext) = refs
    else:
        (x_ref, lnw_ref, w_ref, convw_ref, convb_ref,
         z_ref, xs_ref, b_ref, c_ref, dt_ref, convnew_ref, ext) = refs
    tm = x_ref.shape[0]
    tail = SSD_CONV_W - 1
    t = pl.program_id(0) % tiles_per_seq
    xn = _rms(x_ref[...], lnw_ref[...]).astype(BF16)

    @pl.when(t == 0)
    def _init():
        ext[0:SUBLANES, :] = jnp.zeros((SUBLANES, SSD_CONV_DIM), F32)
        if has_init:
            ext[SUBLANES - tail:SUBLANES, :] = cprev_ref[0]

    dt_ref[...] = _dot(xn, w_ref[:, SSD_D_INNER + SSD_CONV_DIM:])
    def project(c0):
        cs = slice(c0, c0 + COL_CHUNK)
        ext[SUBLANES:SUBLANES + tm, cs] = _dot(xn, w_ref[:, SSD_D_INNER + c0:SSD_D_INNER + c0 + COL_CHUNK])
        convnew_ref[0, :, cs] = ext[pl.ds(SUBLANES + tm - tail, tail), cs]

    project(0)
    for c0 in range(0, SSD_CONV_DIM, COL_CHUNK):
        cs = slice(c0, c0 + COL_CHUNK)
        if c0 + COL_CHUNK < SSD_CONV_DIM:
            project(c0 + COL_CHUNK)
        if c0 % (2 * COL_CHUNK) == 0:
            zc = slice(c0 // 2, c0 // 2 + COL_CHUNK)
            z = _dot(xn, w_ref[:, zc])
            z_ref[:, zc] = z * jax.nn.sigmoid(z)
        if c0 < SSD_D_INNER:
            o_ref, o0 = xs_ref, c0
        elif c0 < SSD_D_INNER + SSD_BC:
            o_ref, o0 = b_ref, c0 - SSD_D_INNER
        else:
            o_ref, o0 = c_ref, c0 - SSD_D_INNER - SSD_BC
        for r0 in range(0, tm, CONV_ROWS):
            rb = min(CONV_ROWS, tm - r0)
            for l0 in range(0, COL_CHUNK, LANES):
                ls = slice(c0 + l0, c0 + l0 + LANES)
                e = ext[r0:r0 + rb + SUBLANES, ls]
                a = convb_ref[:, ls]
                for k in range(SSD_CONV_W):
                    shifted = e if k == tail else pltpu.roll(e, tail - k, 0)
                    a = a + shifted[SUBLANES:SUBLANES + rb] * convw_ref[k:k + 1, ls]
                y = a * jax.nn.sigmoid(a)
                if o_ref is b_ref:
                    b_ref[0, o0 + l0:o0 + l0 + LANES, r0:r0 + rb] = y.T
                else:
                    o_ref[r0:r0 + rb, o0 + l0:o0 + l0 + LANES] = y.astype(o_ref.dtype)
    ext[0:SUBLANES, :] = ext[tm:tm + SUBLANES, :]


def _ssd_in(x, lnw, w, conv_prev, batch, seq, tm):
    m, d = x.shape
    assert seq % tm == 0 and tm % SUBLANES == 0 and m == batch * seq
    tps = seq // tm
    has_init = conv_prev is not None
    tail = SSD_CONV_W - 1
    consts = [lnw, w["in_w"], w["conv_w"], w["conv_b"]]
    row = lambda wd: pl.BlockSpec((tm, wd), lambda i: (i, 0))
    conv_spec = pl.BlockSpec((1, tail, SSD_CONV_DIM), lambda i: (i // tps, 0, 0))
    in_specs = [row(d)] + [_const_spec(a.shape) for a in consts]
    args = [x] + consts
    if has_init:
        in_specs.append(conv_spec)
        args.append(conv_prev)
    return pl.pallas_call(
        functools.partial(_ssd_in_kernel, tiles_per_seq=tps, has_init=has_init),
        grid=(m // tm,),
        in_specs=in_specs,
        out_specs=[row(SSD_D_INNER), row(SSD_D_INNER), pl.BlockSpec((1, SSD_BC, tm), lambda i: (i, 0, 0)),
                   row(SSD_BC), row(LANES), conv_spec],
        out_shape=[jax.ShapeDtypeStruct((m, SSD_D_INNER), F32),
                   jax.ShapeDtypeStruct((m, SSD_D_INNER), F32),
                   jax.ShapeDtypeStruct((m // tm, SSD_BC, tm), F32),
                   jax.ShapeDtypeStruct((m, SSD_BC), BF16),
                   jax.ShapeDtypeStruct((m, LANES), F32),
                   jax.ShapeDtypeStruct((batch, tail, SSD_CONV_DIM), F32)],
        scratch_shapes=[pltpu.VMEM((tm + SUBLANES, SSD_CONV_DIM), F32)],
        compiler_params=_cparams("arbitrary"),
        name="ssd_in",
    )(*args)


def _dot_exact_lhs(a, x):
    x1 = x.astype(BF16)
    r1 = x - x1.astype(F32)
    x2 = r1.astype(BF16)
    x3 = (r1 - x2.astype(F32)).astype(BF16)
    return _dot(a, x1) + _dot(a, x2) + _dot(a, x3)


def _expand_heads(cols, g, lane_head):
    shape = (cols.shape[0], SSD_GROUP_W)
    h0 = SSD_HPG * g
    out = jnp.broadcast_to(cols[:, h0 + SSD_HPG - 1:h0 + SSD_HPG], shape)
    for r in range(SSD_HPG - 2, -1, -1):
        out = jnp.where(lane_head[:shape[0]] == r, jnp.broadcast_to(cols[:, h0 + r:h0 + r + 1], shape), out)
    return out


def _select_heads(parts, lane_head):
    out = parts[-1]
    for r in range(len(parts) - 2, -1, -1):
        out = jnp.where(lane_head == r, parts[r], out)
    return out


def _ssd_kernel(*refs, L, has_init):
    if has_init:
        (zs_ref, xs_ref, bt_ref, c_ref, dt_ref, h0_ref, dtb_ref, alog_ref,
         dexp_ref, normw_ref, yn_ref, ht_ref, st, cb_s) = refs
    else:
        (zs_ref, xs_ref, bt_ref, c_ref, dt_ref, dtb_ref, alog_ref,
         dexp_ref, normw_ref, yn_ref, ht_ref, st, cb_s) = refs
    c = pl.program_id(1)
    last = pl.num_programs(1) - 1

    @pl.when(c == 0)
    def _init():
        if has_init:
            for g in range(SSD_GROUPS):
                st[g] = h0_ref[0, g * SSD_GROUP_W:(g + 1) * SSD_GROUP_W, :].T
        else:
            st[...] = jnp.zeros(st.shape, F32)

    rows = dt_ref.shape[0]
    n_chunks = rows // L
    fused = L % LANES == 0
    group_n = lambda g: slice(g * SSD_STATE, (g + 1) * SSD_STATE)

    for ci in range(n_chunks):
        rs = slice(ci * L, (ci + 1) * L)
        for g in range(SSD_GROUPS):
            cb_s[ci * SSD_GROUPS + g] = _dot(c_ref[rs, group_n(g)], bt_ref[0, group_n(g), rs].astype(BF16))

    dt = jax.nn.softplus(dt_ref[...] + dtb_ref[...])
    dta = dt * (-jnp.exp(alog_ref[...]))
    row = lax.broadcasted_iota(jnp.int32, (rows, rows), 0)
    col = lax.broadcasted_iota(jnp.int32, (rows, rows), 1)
    same_chunk_tri = jnp.logical_and(row >= col, row // L == col // L)
    acs = _dot_exact_lhs(same_chunk_tri.astype(BF16), dta)
    acs_t = acs.T
    dt_t = dt.T
    tri = lax.broadcasted_iota(jnp.int32, (L, L), 0) >= lax.broadcasted_iota(jnp.int32, (L, L), 1)
    lane_head = lax.broadcasted_iota(jnp.int32, (L, SSD_GROUP_W), 1) // SSD_HEAD_DIM
    lane_head_n = lax.broadcasted_iota(jnp.int32, (SSD_STATE, SSD_GROUP_W), 1) // SSD_HEAD_DIM

    for ci in range(n_chunks):
        rs = slice(ci * L, (ci + 1) * L)
        acs_c = acs[rs]
        acs_tc = acs_t[:, rs]
        dt_tc = dt_t[:, rs]
        cdec = jnp.exp(acs_c[L - 1:L, :])
        wdt_t = jnp.exp(acs_tc[:, L - 1:L] - acs_tc) * dt_tc
        for g in range(SSD_GROUPS):
            gs = slice(g * SSD_GROUP_W, (g + 1) * SSD_GROUP_W)
            xg = xs_ref[rs, gs]
            xgb = xg.astype(BF16)
            cg = c_ref[rs, group_n(g)].astype(F32)
            cb = cb_s[ci * SSD_GROUPS + g]
            sg = st[g]
            sgb = sg.astype(BF16)
            if fused:
                rhs = jnp.concatenate([xgb, sgb], axis=0)
            parts = []
            for r in range(SSD_HPG):
                h = SSD_HPG * g + r
                a_col = jnp.broadcast_to(acs_c[:, h:h + 1], (L, LANES))
                seg = a_col[:, :L] - acs_tc[h:h + 1, :]
                m = (cb * jnp.exp(jnp.where(tri, seg, -jnp.inf)) * dt_tc[h:h + 1, :]).astype(BF16)
                ce = (cg * jnp.exp(a_col)).astype(BF16)
                if fused:
                    parts.append(_dot(jnp.concatenate([m, ce], axis=1), rhs))
                else:
                    parts.append(_dot(m, xgb) + _dot(ce, sgb))
            y = _select_heads(parts, lane_head)
            y = y + dexp_ref[:, gs] * xg
            y = y * zs_ref[rs, gs]
            yn_ref[rs, gs] = _rms(y, normw_ref[:, gs]).astype(yn_ref.dtype)
            btg = bt_ref[0, group_n(g), rs]
            upd = [_dot((btg * wdt_t[SSD_HPG * g + r:SSD_HPG * g + r + 1, :]).astype(BF16), xgb)
                   for r in range(SSD_HPG)]
            st[g] = sg * _expand_heads(cdec, g, lane_head) + _select_heads(upd, lane_head_n)

    @pl.when(c == last)
    def _state_out():
        for g in range(SSD_GROUPS):
            ht_ref[0, g * SSD_GROUP_W:(g + 1) * SSD_GROUP_W, :] = st[g].T


def _ssd_scan(zs, xs, bt, c, dt, h0, w, batch, seq, L):
    rows = bt.shape[2]
    nc = seq // rows
    assert nc * rows == seq and rows % L == 0 and L % SUBLANES == 0
    has_init = h0 is not None
    row_spec = lambda wd: pl.BlockSpec((rows, wd), lambda bi, ci: (bi * nc + ci, 0))
    state_spec = pl.BlockSpec((1, SSD_D_INNER, SSD_STATE), lambda bi, ci: (bi, 0, 0))
    bt_spec = pl.BlockSpec((1, SSD_BC, rows), lambda bi, ci: (bi * nc + ci, 0, 0))
    in_specs = [row_spec(SSD_D_INNER), row_spec(SSD_D_INNER), bt_spec, row_spec(SSD_BC), row_spec(LANES)]
    args = [zs, xs, bt, c, dt]
    if has_init:
        in_specs.append(state_spec)
        args.append(h0.reshape(batch, SSD_D_INNER, SSD_STATE))
    consts = [w["dt_bias"], w["a_log"], w["d_exp"], w["norm_w"]]
    in_specs += [_const_spec(a.shape) for a in consts]
    args += consts
    return pl.pallas_call(
        functools.partial(_ssd_kernel, L=L, has_init=has_init),
        grid=(batch, nc),
        in_specs=in_specs,
        out_specs=[row_spec(SSD_D_INNER), state_spec],
        out_shape=[jax.ShapeDtypeStruct((batch * seq, SSD_D_INNER), BF16),
                   jax.ShapeDtypeStruct((batch, SSD_D_INNER, SSD_STATE), F32)],
        scratch_shapes=[pltpu.VMEM((SSD_GROUPS, SSD_STATE, SSD_GROUP_W), F32),
                        pltpu.VMEM((rows // L * SSD_GROUPS, L, L), F32)],
        compiler_params=_cparams("arbitrary", "arbitrary"),
        name="ssd_scan",
    )(*args)


def _out_ple_kernel(y_ref, h_ref, p_ref, wout_ref, plenorm_ref, wgate_ref, wup_ref, o_ref):
    h1 = h_ref[...] + _dot(y_ref[...], wout_ref[...])
    hn = _rms(h1, plenorm_ref[...]).astype(BF16)
    pb = p_ref[...].astype(BF16)
    for c in range(0, D_MODEL, COL_CHUNK):
        cs = slice(c, c + COL_CHUNK)
        gate = jax.nn.sigmoid(_dot(hn, wgate_ref[:, cs]))
        o_ref[:, cs] = h1[:, cs] + _dot(pb, wup_ref[:, cs]) * gate


def _out_ple(y, h, p, wout, plenorm, wgate, wup, tm):
    m, kd = y.shape
    assert m % tm == 0
    return pl.pallas_call(
        _out_ple_kernel,
        grid=(m // tm,),
        in_specs=[pl.BlockSpec((tm, kd), lambda i: (i, 0)),
                  pl.BlockSpec((tm, D_MODEL), lambda i: (i, 0)),
                  pl.BlockSpec((tm, PLE_DIM), lambda i: (i, 0)),
                  _const_spec(wout.shape), _const_spec(plenorm.shape),
                  _const_spec(wgate.shape), _const_spec(wup.shape)],
        out_specs=pl.BlockSpec((tm, D_MODEL), lambda i: (i, 0)),
        out_shape=jax.ShapeDtypeStruct((m, D_MODEL), F32),
        compiler_params=_cparams("arbitrary"),
        name="out_ple",
    )(y, h, p, wout, plenorm, wgate, wup)


def _dot_exact_rhs(x, a):
    x1 = x.astype(BF16)
    r1 = x - x1.astype(F32)
    x2 = r1.astype(BF16)
    x3 = (r1 - x2.astype(F32)).astype(BF16)
    return _dot(x1, a) + _dot(x2, a) + _dot(x3, a)


def _head_part_matrix(width):
    def part(idx):
        within = idx % HEAD_PAD
        return (idx // HEAD_PAD) * 4 + jnp.where(within < MLA_NOPE, 0, jnp.where(within < MLA_QK, 1, 2))
    rows = part(lax.broadcasted_iota(jnp.int32, (width, width), 0))
    cols_i = lax.broadcasted_iota(jnp.int32, (width, width), 1)
    cols = part(cols_i)
    same = jnp.logical_and(rows == cols, cols_i % HEAD_PAD < MLA_QK)
    lane = lax.broadcasted_iota(jnp.int32, (1, width), 1) % HEAD_PAD
    inv_count = jnp.where(lane < MLA_NOPE, 1.0 / MLA_NOPE, jnp.where(lane < MLA_QK, 1.0 / MLA_ROPE, 0.0))
    return jnp.where(same, 1.0, 0.0).astype(BF16), inv_count


def _head_norm_rope(x, x_rot, pmat, inv_count, gain_cos, gain_sin):
    inv = lax.rsqrt(_dot_exact_rhs(x * x, pmat) * inv_count + EPS)
    return (x * inv) * gain_cos + (x_rot * inv) * gain_sin


def _mla_in_kernel(h_ref, lnw_ref, win_ref, qan_ref, wqb_ref, kvan_ref, gq_ref, gqrot_ref, gkr_ref, gkrrot_ref,
                   cos_ref, sin_ref, q_ref, lat_ref, kr_ref, gate_ref, *, q_transposed):
    xn = _rms(h_ref[...], lnw_ref[...]).astype(BF16)
    c_q, c_kv, c_kr = 0, MLA_Q_LORA, MLA_Q_LORA + MLA_KV_LORA
    c_gate = c_kr + 2 * HEAD_PAD
    pair_w = 2 * HEAD_PAD
    _store_dot(gate_ref, xn, win_ref, c_gate)
    lat_ref[...] = _rms(_dot(xn, win_ref[:, c_kv:c_kr]), kvan_ref[...])
    pmat, inv_count = _head_part_matrix(pair_w)
    cos_t, sin_t = cos_ref[...], sin_ref[...]
    kr_raw = _dot(xn, win_ref[:, c_kr:c_gate])
    kr = _head_norm_rope(kr_raw[:, :HEAD_PAD], kr_raw[:, HEAD_PAD:], pmat[:HEAD_PAD, :HEAD_PAD],
                         inv_count[:, :HEAD_PAD], gkr_ref[...] * cos_t[:, :HEAD_PAD],
                         gkrrot_ref[...] * sin_t[:, :HEAD_PAD])
    kr_ref[...] = kr[:, MLA_NOPE:MLA_QK]
    qa = _rms(_dot(xn, win_ref[:, c_q:c_kv]), qan_ref[...]).astype(BF16)
    gain_cos = gq_ref[...] * cos_t
    gain_sin = gqrot_ref[...] * sin_t
    for pair in range(MLA_HEADS // 2):
        qq = _dot(qa, wqb_ref[:, 2 * pair * pair_w:2 * (pair + 1) * pair_w])
        qh = _head_norm_rope(qq[:, :pair_w], qq[:, pair_w:], pmat, inv_count, gain_cos, gain_sin)
        if q_transposed:
            q_ref[0, pair * pair_w:(pair + 1) * pair_w, :] = qh.T.astype(q_ref.dtype)
        else:
            q_ref[:, pair * pair_w:(pair + 1) * pair_w] = qh.astype(q_ref.dtype)


def _mla_in(h, lnw, w, rope, tm, q_transposed):
    m = h.shape[0]
    assert m % tm == 0
    cos_t, sin_t = rope
    period = cos_t.shape[0] // tm
    assert period * tm == cos_t.shape[0]
    rope_spec = pl.BlockSpec((tm, 2 * HEAD_PAD), lambda i: (i % period, 0))
    consts = [lnw, w["in_w"], w["q_a_norm"], w["q_b_w"], w["kv_a_norm"],
              w["gain_q"], w["gain_q_rot"], w["gain_kr"], w["gain_kr_rot"]]
    row = lambda wd: pl.BlockSpec((tm, wd), lambda i: (i, 0))
    qw = MLA_HEADS * HEAD_PAD
    if q_transposed:
        seq = period * tm
        q_spec = pl.BlockSpec((1, qw, tm), lambda i: (i // period, 0, i % period))
        q_shape = jax.ShapeDtypeStruct((m // seq, qw, seq), BF16)
    else:
        q_spec, q_shape = row(qw), jax.ShapeDtypeStruct((m, qw), BF16)
    return pl.pallas_call(
        functools.partial(_mla_in_kernel, q_transposed=q_transposed),
        grid=(m // tm,),
        in_specs=[row(D_MODEL)] + [_const_spec(a.shape) for a in consts] + [rope_spec] * 2,
        out_specs=[q_spec, row(MLA_KV_LORA), row(MLA_ROPE), row(MLA_WIDTH)],
        out_shape=[q_shape,
                   jax.ShapeDtypeStruct((m, MLA_KV_LORA), F32),
                   jax.ShapeDtypeStruct((m, MLA_ROPE), F32),
                   jax.ShapeDtypeStruct((m, MLA_WIDTH), F32)],
        compiler_params=_cparams("arbitrary"),
        name="mla_in",
    )(h, *consts, cos_t, sin_t)


def _mla_kv_kernel(lat_ref, kr_ref, wk_ref, wv_ref, gkn_ref, k_ref, v_ref, *, v_transposed):
    latb = lat_ref[...].astype(BF16)
    if v_transposed:
        for c in range(0, MLA_WIDTH, COL_CHUNK):
            v_ref[0, c:c + COL_CHUNK, :] = _dot(latb, wv_ref[:, c:c + COL_CHUNK]).T.astype(v_ref.dtype)
    else:
        _store_dot(v_ref, latb, wv_ref)
    rows = latb.shape[0]
    kr = jnp.concatenate([jnp.zeros((rows, MLA_NOPE), F32), kr_ref[...],
                          jnp.zeros((rows, HEAD_PAD - MLA_QK), F32)], axis=1)
    for h in range(MLA_HEADS):
        hs = slice(h * HEAD_PAD, (h + 1) * HEAD_PAD)
        kh = _dot(latb, wk_ref[:, hs])
        ms = jnp.sum(kh * kh, axis=-1, keepdims=True) * (1.0 / MLA_NOPE)
        k_ref[:, hs] = (kh * lax.rsqrt(ms + EPS) * gkn_ref[...] + kr).astype(k_ref.dtype)


def _mla_kv(lat, kr, w, tm, v_transposed):
    m = lat.shape[0]
    assert m % tm == 0
    consts = [w["kv_k_w"], w["kv_v_w"], w["gain_kn"]]
    row = lambda wd: pl.BlockSpec((tm, wd), lambda i: (i, 0))
    if v_transposed:
        v_spec = pl.BlockSpec((1, MLA_WIDTH, tm), lambda i: (i, 0, 0))
        v_shape = jax.ShapeDtypeStruct((m // tm, MLA_WIDTH, tm), BF16)
    else:
        v_spec, v_shape = row(MLA_WIDTH), jax.ShapeDtypeStruct((m, MLA_WIDTH), BF16)
    return pl.pallas_call(
        functools.partial(_mla_kv_kernel, v_transposed=v_transposed),
        grid=(m // tm,),
        in_specs=[row(MLA_KV_LORA), row(MLA_ROPE)] + [_const_spec(a.shape) for a in consts],
        out_specs=[row(MLA_HEADS * HEAD_PAD), v_spec],
        out_shape=[jax.ShapeDtypeStruct((m, MLA_HEADS * HEAD_PAD), BF16), v_shape],
        compiler_params=_cparams("arbitrary"),
        name="mla_kv",
    )(lat, kr, *consts)


def _softmax_step(q, k, v, carry, mask):
    m_prev, l_prev, acc = carry
    s = _dot_nt(q, k) * (MLA_QK ** -0.5)
    if mask is not None:
        s = jnp.where(mask, s, -jnp.inf)
    m_new = jnp.maximum(m_prev, jnp.max(s, axis=-1, keepdims=True))
    alpha = jnp.exp(m_prev - m_new)
    p = jnp.exp(s - m_new)
    l_new = alpha * l_prev + jnp.sum(p, axis=-1, keepdims=True)
    return m_new, l_new, alpha * acc + _dot(p.astype(BF16), v)


def _gated_pair_out(o_ref, gate_ref, outs, pair):
    lane = lax.broadcasted_iota(jnp.int32, outs[0].shape, 1)
    ps = slice(pair * LANES, (pair + 1) * LANES)
    g = gate_ref[:, ps]
    o_ref[:, ps] = (jnp.where(lane < MLA_V, outs[0], outs[1]) * (g * jax.nn.sigmoid(g))).astype(o_ref.dtype)


def _attn_kernel(q_ref, k_ref, v_ref, gate_ref, o_ref, m_s, l_s, acc_s, s_s, *, tq):
    i = pl.program_id(1)
    key_c = lax.broadcasted_iota(jnp.int32, (tq, tq), 0) // CHUNK
    qry_c = lax.broadcasted_iota(jnp.int32, (tq, tq), 1) // CHUNK
    diag_mask = key_c <= qry_c
    m_s[...] = jnp.full(m_s.shape, NEG_BIG, F32)
    l_s[...] = jnp.zeros(l_s.shape, F32)
    acc_s[...] = jnp.zeros(acc_s.shape, F32)

    def tile_step(j, mask):
        rows = pl.ds(pl.multiple_of(j * tq, tq), tq)
        for h in range(MLA_HEADS):
            hs = slice(h * HEAD_PAD, (h + 1) * HEAD_PAD)
            s_s[h] = _dot(k_ref[rows, hs], q_ref[0, hs, :])
        for h in range(MLA_HEADS):
            s = s_s[h]
            if mask is not None:
                s = jnp.where(mask, s, -jnp.inf)
            m_prev = m_s[h]
            m_new = jnp.maximum(m_prev, jnp.max(s, axis=0, keepdims=True))
            alpha = jnp.exp2((m_prev - m_new) * SOFTMAX_LOG2_SCALE)
            p = jnp.exp2((s - m_new[0:1, :]) * SOFTMAX_LOG2_SCALE)
            l_s[h] = alpha * l_s[h] + jnp.sum(p, axis=0, keepdims=True)
            pv = _dot(v_ref[j, h * MLA_V:(h + 1) * MLA_V, :], p.astype(BF16))
            acc_s[h] = alpha[0:1, :] * acc_s[h] + pv
            m_s[h] = m_new

    def body(j, carry):
        tile_step(j, None)
        return carry

    lax.fori_loop(0, i, body, 0)
    tile_step(i, diag_mask)
    for pair in range(MLA_HEADS // 2):
        o_t = jnp.concatenate([acc_s[2 * pair + sub] / l_s[2 * pair + sub][0:1, :] for sub in range(2)], axis=0)
        ps = slice(pair * LANES, (pair + 1) * LANES)
        g = gate_ref[:, ps]
        o_ref[:, ps] = (o_t.T * (g * jax.nn.sigmoid(g))).astype(o_ref.dtype)


def _attn_prompt(q_t, k, v_t, gate, batch, seq, tq):
    nq = seq // tq
    assert nq * tq == seq and tq % CHUNK == 0
    return pl.pallas_call(
        functools.partial(_attn_kernel, tq=tq),
        grid=(batch, nq),
        in_specs=[pl.BlockSpec((1, MLA_HEADS * HEAD_PAD, tq), lambda b, i: (b, 0, i)),
                  pl.BlockSpec((seq, MLA_HEADS * HEAD_PAD), lambda b, i: (b, 0)),
                  pl.BlockSpec((nq, MLA_WIDTH, tq), lambda b, i: (b, 0, 0)),
                  pl.BlockSpec((tq, MLA_WIDTH), lambda b, i: (b * nq + i, 0))],
        out_specs=pl.BlockSpec((tq, MLA_WIDTH), lambda b, i: (b * nq + i, 0)),
        out_shape=jax.ShapeDtypeStruct((batch * seq, MLA_WIDTH), BF16),
        scratch_shapes=[pltpu.VMEM((MLA_HEADS, SUBLANES, tq), F32), pltpu.VMEM((MLA_HEADS, SUBLANES, tq), F32),
                        pltpu.VMEM((MLA_HEADS, MLA_V, tq), F32), pltpu.VMEM((MLA_HEADS, tq, tq), F32)],
        compiler_params=_cparams("arbitrary", "arbitrary"),
        name="attn_prompt",
    )(q_t, k, v_t, gate)


def _attn_step_kernel(q_ref, k_ref, v_ref, gate_ref, o_ref, *, q_pos0, n_keys):
    tq = q_ref.shape[0]
    tk = k_ref.shape[1]
    k_pos = lax.broadcasted_iota(jnp.int32, (tq, tk), 1)
    q_pos = lax.broadcasted_iota(jnp.int32, (tq, tk), 0) + q_pos0
    mask = jnp.logical_and(k_pos < n_keys, k_pos // CHUNK <= q_pos // CHUNK)
    init = (jnp.full((tq, 1), NEG_BIG, F32), jnp.zeros((tq, 1), F32), jnp.zeros((tq, LANES), F32))
    for pair in range(MLA_HEADS // 2):
        vs = slice(pair * LANES, (pair + 1) * LANES)
        outs = []
        for sub in range(2):
            hs = slice((2 * pair + sub) * HEAD_PAD, (2 * pair + sub + 1) * HEAD_PAD)
            _, l, acc = _softmax_step(q_ref[:, hs], k_ref[0, :, hs], v_ref[0, :, vs], init, mask)
            outs.append(acc / l)
        _gated_pair_out(o_ref, gate_ref, outs, pair)


def _attn_step(q, k, v, gate, batch, tq, n_keys):
    tk = k.shape[1]
    return pl.pallas_call(
        functools.partial(_attn_step_kernel, q_pos0=n_keys - tq, n_keys=n_keys),
        grid=(batch,),
        in_specs=[pl.BlockSpec((tq, MLA_HEADS * HEAD_PAD), lambda b: (b, 0)),
                  pl.BlockSpec((1, tk, MLA_HEADS * HEAD_PAD), lambda b: (b, 0, 0)),
                  pl.BlockSpec((1, tk, MLA_WIDTH), lambda b: (b, 0, 0)),
                  pl.BlockSpec((tq, MLA_WIDTH), lambda b: (b, 0))],
        out_specs=pl.BlockSpec((tq, MLA_WIDTH), lambda b: (b, 0)),
        out_shape=jax.ShapeDtypeStruct((batch * tq, MLA_WIDTH), BF16),
        compiler_params=_cparams("arbitrary"),
        name="attn_step",
    )(q, k, v, gate)


def _pad_cols(a, width):
    return jnp.pad(a, ((0, 0), (0, width - a.shape[1])))


def _ssd_weights(j, in_w, conv_w, conv_b, dt_bias, a_log, d, norm_w, out_w):
    zx = SSD_D_INNER + SSD_CONV_DIM
    return {
        "in_w": jnp.concatenate([in_w[j][:, :zx], _pad_cols(in_w[j][:, zx:], LANES)], axis=1).astype(BF16),
        "conv_w": conv_w[j],
        "conv_b": conv_b[j][None, :],
        "dt_bias": _pad_cols(dt_bias[j][None, :], LANES),
        "a_log": _pad_cols(a_log[j][None, :], LANES),
        "d_exp": jnp.repeat(d[j], SSD_HEAD_DIM)[None, :],
        "norm_w": norm_w[j][None, :],
        "out_w": out_w[j].astype(BF16),
    }


def _head_pad_cols(w, per_head, take, offset=0):
    k = w.shape[0]
    w = w.reshape(k, MLA_HEADS, per_head)[:, :, offset:offset + take]
    return jnp.pad(w, ((0, 0), (0, 0), (0, HEAD_PAD - take))).reshape(k, MLA_HEADS * HEAD_PAD)


def _rotate_half(a):
    half = MLA_ROPE // 2
    return jnp.concatenate([-a[..., half:], a[..., :half]], axis=-1)


def _swap_halves(a):
    half = MLA_ROPE // 2
    return jnp.concatenate([a[..., half:], a[..., :half]], axis=-1)


def _rope_slot(a):
    pad = [(0, 0)] * (a.ndim - 1) + [(MLA_NOPE, HEAD_PAD - MLA_QK)]
    return jnp.pad(a, pad)


def _q_b_cols(w):
    k = w.shape[0]
    w = w.reshape(k, MLA_HEADS, MLA_QK)
    direct = jnp.pad(w, ((0, 0), (0, 0), (0, HEAD_PAD - MLA_QK))).reshape(k, MLA_HEADS // 2, 2 * HEAD_PAD)
    rot = _rope_slot(_rotate_half(w[:, :, MLA_NOPE:])).reshape(k, MLA_HEADS // 2, 2 * HEAD_PAD)
    return jnp.concatenate([direct, rot], axis=-1).reshape(k, MLA_HEADS * 2 * HEAD_PAD)


def _mla_weights(j, in_w, q_a_norm, q_b_w, kv_a_norm, kv_b_w, qn, qr, kn, kr, out_w):
    c1, c2, c3 = MLA_Q_LORA, MLA_Q_LORA + MLA_KV_LORA, MLA_Q_LORA + MLA_KV_LORA + MLA_ROPE
    w = in_w[j]
    kr_cols = jnp.concatenate([_rope_slot(w[:, c2:c3]), _rope_slot(_rotate_half(w[:, c2:c3]))], axis=1)
    zeros = lambda n: jnp.zeros((n,), F32)
    gain_q = jnp.concatenate([qn[j], qr[j], zeros(HEAD_PAD - MLA_QK)])
    return {
        "in_w": jnp.concatenate([w[:, :c2], kr_cols, w[:, c3:]], axis=1).astype(BF16),
        "q_a_norm": q_a_norm[j][None, :],
        "q_b_w": _q_b_cols(q_b_w[j]).astype(BF16),
        "gain_q": jnp.tile(gain_q, 2)[None, :],
        "gain_q_rot": jnp.tile(_rope_slot(_swap_halves(qr[j])), 2)[None, :],
        "gain_kr_rot": _rope_slot(_swap_halves(kr[j]))[None, :],
        "kv_a_norm": kv_a_norm[j][None, :],
        "kv_k_w": _head_pad_cols(kv_b_w[j], MLA_NOPE + MLA_V, MLA_NOPE).astype(BF16),
        "kv_v_w": kv_b_w[j].reshape(MLA_KV_LORA, MLA_HEADS, MLA_NOPE + MLA_V)[:, :, MLA_NOPE:]
                  .reshape(MLA_KV_LORA, MLA_WIDTH).astype(BF16),
        "gain_kr": _rope_slot(kr[j])[None, :],
        "gain_kn": jnp.concatenate([kn[j], zeros(HEAD_PAD - MLA_NOPE)])[None, :],
        "out_w": out_w[j].astype(BF16),
    }


def _rope_tables(pos0, seq, rows):
    inv = 1.0 / (ROPE_BASE ** (jnp.arange(0, MLA_ROPE, 2, dtype=F32) / MLA_ROPE))
    ang = (pos0 + jnp.arange(seq)).astype(F32)[:, None] * inv[None, :]
    cos, sin = jnp.cos(ang), jnp.sin(ang)
    z = lambda n: jnp.zeros((seq, n), F32)
    cos_t = jnp.concatenate([jnp.ones((seq, MLA_NOPE), F32), cos, cos, z(HEAD_PAD - MLA_QK)], axis=1)
    sin_t = jnp.concatenate([z(MLA_NOPE), sin, sin, z(HEAD_PAD - MLA_QK)], axis=1)
    reps = max(1, rows // seq)
    return tuple(jnp.tile(t, (reps, 2)) for t in (cos_t, sin_t))


def _row_tile(m, want):
    tm = min(m, want)
    assert m % tm == 0
    return tm


def _ssd_layer(h, p, lnw, ple, w, batch, seq, chunk, conv_prev, h0):
    tm = _row_tile(h.shape[0], 256)
    zs, xs, bt, c, dt, conv_new = _ssd_in(h, lnw, w, conv_prev, batch, seq, min(seq, SSD_BLOCK_ROWS))
    yn, ht = _ssd_scan(zs, xs, bt, c, dt, h0, w, batch, seq, chunk)
    h = _out_ple(yn, h, p, w["out_w"], *ple, tm)
    return h, conv_new, ht.reshape(batch, SSD_HEADS, SSD_HEAD_DIM, SSD_STATE)


def _mla_layer(h, p, lnw, ple, w, batch, seq, lat_past, kr_past):
    tm = _row_tile(h.shape[0], 256)
    pos0 = 0 if lat_past is None else lat_past.shape[1]
    rope = _rope_tables(pos0, seq, tm)
    prompt = lat_past is None
    q, lat, kr, gate = _mla_in(h, lnw, w, rope, tm, q_transposed=prompt)
    if prompt:
        tile = min(seq, ATTN_TILE)
        k, v_t = _mla_kv(lat, kr, w, tile, v_transposed=True)
        og = _attn_prompt(q, k, v_t, gate, batch, seq, tile)
    else:
        n_keys = pos0 + seq
        tk = -(-n_keys // LANES) * LANES
        pad = ((0, 0), (0, tk - n_keys), (0, 0))
        lat_all = jnp.pad(jnp.concatenate([lat_past, lat.reshape(batch, seq, -1)], axis=1), pad)
        kr_all = jnp.pad(jnp.concatenate([kr_past, kr.reshape(batch, seq, -1)], axis=1), pad)
        k, v = _mla_kv(lat_all.reshape(batch * tk, -1), kr_all.reshape(batch * tk, -1), w,
                       math.gcd(batch * tk, 512), v_transposed=False)
        og = _attn_step(q, k.reshape(batch, tk, -1), v.reshape(batch, tk, -1), gate, batch, seq, n_keys)
    h = _out_ple(og, h, p, w["out_w"], *ple, tm)
    return h, lat.reshape(batch, seq, -1), kr.reshape(batch, seq, -1)


def kernel(x_prompt, x_sample, cache_conv, state_ssm, cache_kv_latent, cache_k_rope, p_prompt, p_sample, ln_w, ssd_in_w, ssd_conv_w, ssd_conv_b, ssd_dt_bias, ssd_A_log, ssd_D, ssd_norm_w, ssd_out_w, mla_in_w, mla_q_a_norm, mla_q_b_w, mla_kv_a_norm, mla_kv_b_w, mla_q_nope_norm, mla_q_rope_norm, mla_k_nope_norm, mla_k_rope_norm, mla_out_w, ple_up_w, ple_norm_w, ple_gate_w):
    bp, tp, d = x_prompt.shape
    bs, ts, _ = x_sample.shape
    hp = x_prompt.reshape(bp * tp, d)
    hs = x_sample.reshape(bs * ts, d)
    conv_p, ssm_p, lat_p, kr_p = [], [], [], []
    conv_s, ssm_s, lat_s, kr_s = [], [], [], []
    for i in range(DEPTH):
        j = i // 2
        lnw = ln_w[i][None, :]
        ple = (ple_norm_w[i][None, :], ple_gate_w[i].astype(BF16), ple_up_w[i].astype(BF16))
        pp = p_prompt[i].reshape(bp * tp, PLE_DIM)
        ps = p_sample[i].reshape(bs * ts, PLE_DIM)
        if i % 2 == 0:
            w = _ssd_weights(j, ssd_in_w, ssd_conv_w, ssd_conv_b, ssd_dt_bias, ssd_A_log, ssd_D,
                             ssd_norm_w, ssd_out_w)
            hp, cp, sp = _ssd_layer(hp, pp, lnw, ple, w, bp, tp, min(tp, SSD_SCAN_CHUNK), None, None)
            hs, cs, ss = _ssd_layer(hs, ps, lnw, ple, w, bs, ts, ts, cache_conv[j], state_ssm[j])
            conv_p.append(cp); ssm_p.append(sp); conv_s.append(cs); ssm_s.append(ss)
        else:
            w = _mla_weights(j, mla_in_w, mla_q_a_norm, mla_q_b_w, mla_kv_a_norm, mla_kv_b_w,
                             mla_q_nope_norm, mla_q_rope_norm, mla_k_nope_norm, mla_k_rope_norm, mla_out_w)
            hp, lp, rp = _mla_layer(hp, pp, lnw, ple, w, bp, tp, None, None)
            hs, ls, rs = _mla_layer(hs, ps, lnw, ple, w, bs, ts, cache_kv_latent[j], cache_k_rope[j])
            lat_p.append(lp); kr_p.append(rp); lat_s.append(ls); kr_s.append(rs)
    return (hp.reshape(bp, tp, d), hs.reshape(bs, ts, d),
            jnp.stack(conv_p), jnp.stack(ssm_p), jnp.stack(lat_p), jnp.stack(kr_p),
            jnp.stack(conv_s), jnp.stack(ssm_s), jnp.stack(lat_s), jnp.stack(kr_s))
```

```python
import functools
import math

import jax
import jax.numpy as jnp
from jax import lax
from jax.experimental import pallas as pl
from jax.experimental.pallas import tpu as pltpu

F32 = jnp.float32
BF16 = jnp.bfloat16

D_MODEL = 1024
DEPTH = 4
CHUNK = 64
PLE_DIM = 256
EPS = 1e-6

SSD_D_INNER = 2 * D_MODEL
SSD_HEAD_DIM = 64
SSD_HEADS = SSD_D_INNER // SSD_HEAD_DIM
SSD_GROUPS = 8
SSD_HPG = SSD_HEADS // SSD_GROUPS
SSD_STATE = 128
SSD_CONV_W = 4
SSD_BC = SSD_GROUPS * SSD_STATE
SSD_CONV_DIM = SSD_D_INNER + 2 * SSD_BC
SSD_GROUP_W = SSD_HPG * SSD_HEAD_DIM

MLA_HEADS = 16
MLA_NOPE = 64
MLA_ROPE = 32
MLA_V = 64
MLA_Q_LORA = 384
MLA_KV_LORA = 256
MLA_QK = MLA_NOPE + MLA_ROPE
MLA_WIDTH = MLA_HEADS * MLA_V
ROPE_BASE = 10000.0

LANES = 128
SUBLANES = 8
HEAD_PAD = LANES
VMEM_LIMIT_BYTES = 52 * 1024 * 1024
NEG_BIG = -1e30
SOFTMAX_LOG2_SCALE = (MLA_QK ** -0.5) * math.log2(math.e)
COL_CHUNK = 512
CONV_ROWS = 128
V_ONES_ROWS = 16
V_SLOT_ROWS = MLA_V + V_ONES_ROWS
ATTN_TILE = 256
SSD_BLOCK_ROWS = 256
SSD_SCAN_CHUNK = 128


def _cparams(*sem):
    return pltpu.CompilerParams(dimension_semantics=sem, vmem_limit_bytes=VMEM_LIMIT_BYTES)


def _const_spec(shape):
    nd = len(shape)
    return pl.BlockSpec(shape, lambda *_: (0,) * nd, pipeline_mode=pl.Buffered(1))


def _rms(x, w):
    return x * lax.rsqrt(jnp.mean(x * x, axis=-1, keepdims=True) + EPS) * w


def _dot(a, b):
    return jnp.dot(a, b, preferred_element_type=F32)


def _dot_nt(a, b):
    return lax.dot_general(a, b, (((1,), (1,)), ((), ())), preferred_element_type=F32)


def _dot_tn(a, b):
    return lax.dot_general(a, b, (((0,), (0,)), ((), ())), preferred_element_type=F32)


def _store_dot(o_ref, a, w_ref, col0=0):
    width = o_ref.shape[1]
    for c in range(0, width, COL_CHUNK):
        cw = min(COL_CHUNK, width - c)
        o_ref[:, c:c + cw] = _dot(a, w_ref[:, col0 + c:col0 + c + cw]).astype(o_ref.dtype)


def _ssd_in_kernel(*refs, tiles_per_seq, has_init):
    if has_init:
        (x_ref, lnw_ref, w_ref, convw_ref, convb_ref, cprev_ref,
         z_ref, xs_ref, b_ref, c_ref, dt_ref, convnew_ref, ext) = refs
    else:
        (x_ref, lnw_ref, w_ref, convw_ref, convb_ref,
         z_ref, xs_ref, b_ref, c_ref, dt_ref, convnew_ref, ext) = refs
    tm = x_ref.shape[0]
    tail = SSD_CONV_W - 1
    t = pl.program_id(0) % tiles_per_seq
    xn = _rms(x_ref[...], lnw_ref[...]).astype(BF16)

    @pl.when(t == 0)
    def _init():
        ext[0:SUBLANES, :] = jnp.zeros((SUBLANES, SSD_CONV_DIM), F32)
        if has_init:
            ext[SUBLANES - tail:SUBLANES, :] = cprev_ref[0]

    dt_ref[...] = _dot(xn, w_ref[:, SSD_D_INNER + SSD_CONV_DIM:])
    def project(c0):
        cs = slice(c0, c0 + COL_CHUNK)
        ext[SUBLANES:SUBLANES + tm, cs] = _dot(xn, w_ref[:, SSD_D_INNER + c0:SSD_D_INNER + c0 + COL_CHUNK])
        convnew_ref[0, :, cs] = ext[pl.ds(SUBLANES + tm - tail, tail), cs]

    project(0)
    for c0 in range(0, SSD_CONV_DIM, COL_CHUNK):
        cs = slice(c0, c0 + COL_CHUNK)
        if c0 + COL_CHUNK < SSD_CONV_DIM:
            project(c0 + COL_CHUNK)
        if c0 % (2 * COL_CHUNK) == 0:
            zc = slice(c0 // 2, c0 // 2 + COL_CHUNK)
            hz = _dot(xn, w_ref[:, zc])
            z_ref[:, zc] = hz * jnp.tanh(hz) + hz
        if c0 < SSD_D_INNER:
            o_ref, o0 = xs_ref, c0
        elif c0 < SSD_D_INNER + SSD_BC:
            o_ref, o0 = b_ref, c0 - SSD_D_INNER
        else:
            o_ref, o0 = c_ref, c0 - SSD_D_INNER - SSD_BC
        for r0 in range(0, tm, CONV_ROWS):
            rb = min(CONV_ROWS, tm - r0)
            for l0 in range(0, COL_CHUNK, LANES):
                ls = slice(c0 + l0, c0 + l0 + LANES)
                e = ext[r0:r0 + rb + SUBLANES, ls]
                e1 = pltpu.roll(e, 1, 0)
                far = e * convw_ref[1:2, ls] + e1 * convw_ref[0:1, ls]
                near = e * convw_ref[3:4, ls] + e1 * convw_ref[2:3, ls]
                hv = (convb_ref[:, ls] + near[SUBLANES:SUBLANES + rb]) + pltpu.roll(far, 2, 0)[SUBLANES:SUBLANES + rb]
                y = hv * jnp.tanh(hv) + hv
                if o_ref is b_ref:
                    b_ref[0, o0 + l0:o0 + l0 + LANES, r0:r0 + rb] = y.T
                else:
                    o_ref[r0:r0 + rb, o0 + l0:o0 + l0 + LANES] = y.astype(o_ref.dtype)
    ext[0:SUBLANES, :] = ext[tm:tm + SUBLANES, :]


def _ssd_in(x, lnw, w, conv_prev, batch, seq, tm):
    m, d = x.shape
    assert seq % tm == 0 and tm % SUBLANES == 0 and m == batch * seq
    tps = seq // tm
    has_init = conv_prev is not None
    tail = SSD_CONV_W - 1
    consts = [lnw, w["in_w"], w["conv_w"], w["conv_b"]]
    row = lambda wd: pl.BlockSpec((tm, wd), lambda i: (i, 0))
    conv_spec = pl.BlockSpec((1, tail, SSD_CONV_DIM), lambda i: (i // tps, 0, 0))
    in_specs = [row(d)] + [_const_spec(a.shape) for a in consts]
    args = [x] + consts
    if has_init:
        in_specs.append(conv_spec)
        args.append(conv_prev)
    return pl.pallas_call(
        functools.partial(_ssd_in_kernel, tiles_per_seq=tps, has_init=has_init),
        grid=(m // tm,),
        in_specs=in_specs,
        out_specs=[row(SSD_D_INNER), row(SSD_D_INNER), pl.BlockSpec((1, SSD_BC, tm), lambda i: (i, 0, 0)),
                   row(SSD_BC), row(LANES), conv_spec],
        out_shape=[jax.ShapeDtypeStruct((m, SSD_D_INNER), F32),
                   jax.ShapeDtypeStruct((m, SSD_D_INNER), F32),
                   jax.ShapeDtypeStruct((m // tm, SSD_BC, tm), F32),
                   jax.ShapeDtypeStruct((m, SSD_BC), BF16),
                   jax.ShapeDtypeStruct((m, LANES), F32),
                   jax.ShapeDtypeStruct((batch, tail, SSD_CONV_DIM), F32)],
        scratch_shapes=[pltpu.VMEM((tm + SUBLANES, SSD_CONV_DIM), F32)],
        compiler_params=_cparams("arbitrary"),
        name="ssd_in",
    )(*args)


def _dot_exact_lhs(a, x):
    x1 = x.astype(BF16)
    r1 = x - x1.astype(F32)
    x2 = r1.astype(BF16)
    x3 = (r1 - x2.astype(F32)).astype(BF16)
    return _dot(a, x1) + _dot(a, x2) + _dot(a, x3)


def _expand_heads(cols, g, lane_head):
    shape = (cols.shape[0], SSD_GROUP_W)
    h0 = SSD_HPG * g
    out = jnp.broadcast_to(cols[:, h0 + SSD_HPG - 1:h0 + SSD_HPG], shape)
    for r in range(SSD_HPG - 2, -1, -1):
        out = jnp.where(lane_head[:shape[0]] == r, jnp.broadcast_to(cols[:, h0 + r:h0 + r + 1], shape), out)
    return out


def _select_heads(parts, lane_head):
    out = parts[-1]
    for r in range(len(parts) - 2, -1, -1):
        out = jnp.where(lane_head == r, parts[r], out)
    return out


def _ssd_kernel(*refs, L, has_init):
    if has_init:
        (zs_ref, xs_ref, bt_ref, c_ref, dt_ref, h0_ref, dtb_ref, alog_ref,
         dexp_ref, normw_ref, yn_ref, ht_ref, st, cb_s) = refs
    else:
        (zs_ref, xs_ref, bt_ref, c_ref, dt_ref, dtb_ref, alog_ref,
         dexp_ref, normw_ref, yn_ref, ht_ref, st, cb_s) = refs
    c = pl.program_id(1)
    last = pl.num_programs(1) - 1

    @pl.when(c == 0)
    def _init():
        if has_init:
            for g in range(SSD_GROUPS):
                st[g] = h0_ref[0, g * SSD_GROUP_W:(g + 1) * SSD_GROUP_W, :].T
        else:
            st[...] = jnp.zeros(st.shape, F32)

    rows = dt_ref.shape[0]
    n_chunks = rows // L
    fused = L % LANES == 0
    group_n = lambda g: slice(g * SSD_STATE, (g + 1) * SSD_STATE)

    for ci in range(n_chunks):
        rs = slice(ci * L, (ci + 1) * L)
        for g in range(SSD_GROUPS):
            cb_s[ci * SSD_GROUPS + g] = _dot(c_ref[rs, group_n(g)], bt_ref[0, group_n(g), rs].astype(BF16))

    dt = jax.nn.softplus(dt_ref[...] + dtb_ref[...])
    dta = dt * (-jnp.exp(alog_ref[...]))
    row = lax.broadcasted_iota(jnp.int32, (rows, rows), 0)
    col = lax.broadcasted_iota(jnp.int32, (rows, rows), 1)
    same_chunk_tri = jnp.logical_and(row >= col, row // L == col // L)
    acs = _dot_exact_lhs(same_chunk_tri.astype(BF16), dta)
    acs_t = acs.T
    dt_t = dt.T
    tri = lax.broadcasted_iota(jnp.int32, (L, L), 0) >= lax.broadcasted_iota(jnp.int32, (L, L), 1)
    lane_head = lax.broadcasted_iota(jnp.int32, (L, SSD_GROUP_W), 1) // SSD_HEAD_DIM
    lane_head_n = lax.broadcasted_iota(jnp.int32, (SSD_STATE, SSD_GROUP_W), 1) // SSD_HEAD_DIM

    for ci in range(n_chunks):
        rs = slice(ci * L, (ci + 1) * L)
        acs_c = acs[rs]
        acs_tc = acs_t[:, rs]
        dt_tc = dt_t[:, rs]
        cdec = jnp.exp(acs_c[L - 1:L, :])
        wdt_t = jnp.exp(acs_tc[:, L - 1:L] - acs_tc) * dt_tc
        for g in range(SSD_GROUPS):
            gs = slice(g * SSD_GROUP_W, (g + 1) * SSD_GROUP_W)
            xg = xs_ref[rs, gs]
            xgb = xg.astype(BF16)
            cg = c_ref[rs, group_n(g)].astype(F32)
            cb = cb_s[ci * SSD_GROUPS + g]
            sg = st[g]
            sgb = sg.astype(BF16)
            if fused:
                rhs = jnp.concatenate([xgb, sgb], axis=0)
            parts = []
            for r in range(SSD_HPG):
                h = SSD_HPG * g + r
                a_col = jnp.broadcast_to(acs_c[:, h:h + 1], (L, LANES))
                seg = a_col[:, :L] - acs_tc[h:h + 1, :]
                m = (cb * jnp.exp(jnp.where(tri, seg, -jnp.inf)) * dt_tc[h:h + 1, :]).astype(BF16)
                ce = (cg * jnp.exp(a_col)).astype(BF16)
                if fused:
                    parts.append(_dot(jnp.concatenate([m, ce], axis=1), rhs))
                else:
                    parts.append(_dot(m, xgb) + _dot(ce, sgb))
            y = _select_heads(parts, lane_head)
            y = y + dexp_ref[:, gs] * xg
            y = y * zs_ref[rs, gs]
            yn_ref[rs, gs] = _rms(y, normw_ref[:, gs]).astype(yn_ref.dtype)
            btg = bt_ref[0, group_n(g), rs]
            upd = [_dot((btg * wdt_t[SSD_HPG * g + r:SSD_HPG * g + r + 1, :]).astype(BF16), xgb)
                   for r in range(SSD_HPG)]
            st[g] = sg * _expand_heads(cdec, g, lane_head) + _select_heads(upd, lane_head_n)

    @pl.when(c == last)
    def _state_out():
        for g in range(SSD_GROUPS):
            ht_ref[0, g * SSD_GROUP_W:(g + 1) * SSD_GROUP_W, :] = st[g].T


def _ssd_scan(zs, xs, bt, c, dt, h0, w, batch, seq, L):
    rows = bt.shape[2]
    nc = seq // rows
    assert nc * rows == seq and rows % L == 0 and L % SUBLANES == 0
    has_init = h0 is not None
    row_spec = lambda wd: pl.BlockSpec((rows, wd), lambda bi, ci: (bi * nc + ci, 0))
    state_spec = pl.BlockSpec((1, SSD_D_INNER, SSD_STATE), lambda bi, ci: (bi, 0, 0))
    bt_spec = pl.BlockSpec((1, SSD_BC, rows), lambda bi, ci: (bi * nc + ci, 0, 0))
    in_specs = [row_spec(SSD_D_INNER), row_spec(SSD_D_INNER), bt_spec, row_spec(SSD_BC), row_spec(LANES)]
    args = [zs, xs, bt, c, dt]
    if has_init:
        in_specs.append(state_spec)
        args.append(h0.reshape(batch, SSD_D_INNER, SSD_STATE))
    consts = [w["dt_bias"], w["a_log"], w["d_exp"], w["norm_w"]]
    in_specs += [_const_spec(a.shape) for a in consts]
    args += consts
    return pl.pallas_call(
        functools.partial(_ssd_kernel, L=L, has_init=has_init),
        grid=(batch, nc),
        in_specs=in_specs,
        out_specs=[row_spec(SSD_D_INNER), state_spec],
        out_shape=[jax.ShapeDtypeStruct((batch * seq, SSD_D_INNER), BF16),
                   jax.ShapeDtypeStruct((batch, SSD_D_INNER, SSD_STATE), F32)],
        scratch_shapes=[pltpu.VMEM((SSD_GROUPS, SSD_STATE, SSD_GROUP_W), F32),
                        pltpu.VMEM((rows // L * SSD_GROUPS, L, L), F32)],
        compiler_params=_cparams("arbitrary", "arbitrary"),
        name="ssd_scan",
    )(*args)


def _out_ple_kernel(y_ref, h_ref, p_ref, wout_ref, plenorm_ref, wgate_ref, wup_ref, o_ref):
    h1 = h_ref[...] + _dot(y_ref[...], wout_ref[...])
    hn = _rms(h1, plenorm_ref[...]).astype(BF16)
    pb = p_ref[...].astype(BF16)
    for c in range(0, D_MODEL, COL_CHUNK):
        cs = slice(c, c + COL_CHUNK)
        gate = jax.nn.sigmoid(_dot(hn, wgate_ref[:, cs]))
        o_ref[:, cs] = h1[:, cs] + _dot(pb, wup_ref[:, cs]) * gate


def _out_ple(y, h, p, wout, plenorm, wgate, wup, tm):
    m, kd = y.shape
    assert m % tm == 0
    return pl.pallas_call(
        _out_ple_kernel,
        grid=(m // tm,),
        in_specs=[pl.BlockSpec((tm, kd), lambda i: (i, 0)),
                  pl.BlockSpec((tm, D_MODEL), lambda i: (i, 0)),
                  pl.BlockSpec((tm, PLE_DIM), lambda i: (i, 0)),
                  _const_spec(wout.shape), _const_spec(plenorm.shape),
                  _const_spec(wgate.shape), _const_spec(wup.shape)],
        out_specs=pl.BlockSpec((tm, D_MODEL), lambda i: (i, 0)),
        out_shape=jax.ShapeDtypeStruct((m, D_MODEL), F32),
        compiler_params=_cparams("arbitrary"),
        name="out_ple",
    )(y, h, p, wout, plenorm, wgate, wup)


def _dot_hi_lo_rhs(x, a):
    hi = x.astype(BF16)
    lo = (x - hi.astype(F32)).astype(BF16)
    return _dot(hi, a) + _dot(lo, a)


def _head_part_matrix(width):
    def part(idx):
        within = idx % HEAD_PAD
        return (idx // HEAD_PAD) * 4 + jnp.where(within < MLA_NOPE, 0, jnp.where(within < MLA_QK, 1, 2))
    rows = part(lax.broadcasted_iota(jnp.int32, (width, width), 0))
    cols_i = lax.broadcasted_iota(jnp.int32, (width, width), 1)
    cols = part(cols_i)
    same = jnp.logical_and(rows == cols, cols_i % HEAD_PAD < MLA_QK)
    lane = lax.broadcasted_iota(jnp.int32, (1, width), 1) % HEAD_PAD
    inv_count = jnp.where(lane < MLA_NOPE, 1.0 / MLA_NOPE, jnp.where(lane < MLA_QK, 1.0 / MLA_ROPE, 0.0))
    return jnp.where(same, 1.0, 0.0).astype(BF16), inv_count


def _head_norm_rope(x, pmat, inv_count, gain_cos, gain_sin):
    inv = lax.rsqrt(_dot_hi_lo_rhs(x * x, pmat) * inv_count + EPS)
    shift = HEAD_PAD - MLA_ROPE
    x_rot = jnp.concatenate([pltpu.roll(x[:, s0:s0 + HEAD_PAD], shift, 1)
                             for s0 in range(0, x.shape[1], HEAD_PAD)], axis=1)
    return (x * inv) * gain_cos + (x_rot * inv) * gain_sin


def _mla_in_kernel(h_ref, lnw_ref, win_ref, qan_ref, wqb_ref, kvan_ref, gq_ref, gqrot_ref, gkr_ref, gkrrot_ref,
                   cos_ref, sin_ref, q_ref, lat_ref, kr_ref, gate_ref, *, q_transposed):
    xn = _rms(h_ref[...], lnw_ref[...]).astype(BF16)
    c_q, c_kv, c_kr = 0, MLA_Q_LORA, MLA_Q_LORA + MLA_KV_LORA
    c_gate = c_kr + HEAD_PAD
    pair_w = 2 * HEAD_PAD
    _store_dot(gate_ref, xn, win_ref, c_gate)
    lat_ref[...] = _rms(_dot(xn, win_ref[:, c_kv:c_kr]), kvan_ref[...])
    pmat, inv_count = _head_part_matrix(pair_w)
    cos_t, sin_t = cos_ref[...], sin_ref[...]
    kr = _head_norm_rope(_dot(xn, win_ref[:, c_kr:c_gate]), pmat[:HEAD_PAD, :HEAD_PAD],
                         inv_count[:, :HEAD_PAD], gkr_ref[...] * cos_t[:, :HEAD_PAD],
                         gkrrot_ref[...] * sin_t[:, :HEAD_PAD])
    kr_ref[...] = kr[:, MLA_NOPE:MLA_QK]
    qa = _rms(_dot(xn, win_ref[:, c_q:c_kv]), qan_ref[...]).astype(BF16)
    gain_cos = gq_ref[...] * cos_t
    gain_sin = gqrot_ref[...] * sin_t
    for pair in range(MLA_HEADS // 2):
        qq = _dot(qa, wqb_ref[:, pair * pair_w:(pair + 1) * pair_w])
        qh = _head_norm_rope(qq, pmat, inv_count, gain_cos, gain_sin)
        if q_transposed:
            q_ref[0, pair * pair_w:(pair + 1) * pair_w, :] = qh.T.astype(q_ref.dtype)
        else:
            q_ref[:, pair * pair_w:(pair + 1) * pair_w] = qh.astype(q_ref.dtype)


def _mla_in(h, lnw, w, rope, tm, q_transposed):
    m = h.shape[0]
    assert m % tm == 0
    cos_t, sin_t = rope
    period = cos_t.shape[0] // tm
    assert period * tm == cos_t.shape[0]
    rope_spec = pl.BlockSpec((tm, 2 * HEAD_PAD), lambda i: (i % period, 0))
    consts = [lnw, w["in_w"], w["q_a_norm"], w["q_b_w"], w["kv_a_norm"],
              w["gain_q"], w["gain_q_rot"], w["gain_kr"], w["gain_kr_rot"]]
    row = lambda wd: pl.BlockSpec((tm, wd), lambda i: (i, 0))
    qw = MLA_HEADS * HEAD_PAD
    if q_transposed:
        seq = period * tm
        q_spec = pl.BlockSpec((1, qw, tm), lambda i: (i // period, 0, i % period))
        q_shape = jax.ShapeDtypeStruct((m // seq, qw, seq), BF16)
    else:
        q_spec, q_shape = row(qw), jax.ShapeDtypeStruct((m, qw), BF16)
    return pl.pallas_call(
        functools.partial(_mla_in_kernel, q_transposed=q_transposed),
        grid=(m // tm,),
        in_specs=[row(D_MODEL)] + [_const_spec(a.shape) for a in consts] + [rope_spec] * 2,
        out_specs=[q_spec, row(MLA_KV_LORA), row(MLA_ROPE), row(MLA_WIDTH)],
        out_shape=[q_shape,
                   jax.ShapeDtypeStruct((m, MLA_KV_LORA), F32),
                   jax.ShapeDtypeStruct((m, MLA_ROPE), F32),
                   jax.ShapeDtypeStruct((m, MLA_WIDTH), F32)],
        compiler_params=_cparams("arbitrary"),
        name="mla_in",
    )(h, *consts, cos_t, sin_t)


def _mla_kv_kernel(lat_ref, kr_ref, wk_ref, wv_ref, gkn_ref, k_ref, v_ref, *, v_transposed):
    latb = lat_ref[...].astype(BF16)
    if v_transposed:
        ones = jnp.ones((V_ONES_ROWS, latb.shape[0]), v_ref.dtype)
        for c in range(0, MLA_WIDTH, COL_CHUNK):
            vt = _dot(latb, wv_ref[:, c:c + COL_CHUNK]).T.astype(v_ref.dtype)
            for hh in range(COL_CHUNK // MLA_V):
                r0 = (c // MLA_V + hh) * V_SLOT_ROWS
                v_ref[0, r0:r0 + MLA_V, :] = vt[hh * MLA_V:(hh + 1) * MLA_V]
                v_ref[0, r0 + MLA_V:r0 + V_SLOT_ROWS, :] = ones
    else:
        _store_dot(v_ref, latb, wv_ref)
    rows = latb.shape[0]
    kr = jnp.concatenate([jnp.zeros((rows, MLA_NOPE), F32), kr_ref[...],
                          jnp.zeros((rows, HEAD_PAD - MLA_QK), F32)], axis=1)
    for h in range(MLA_HEADS):
        hs = slice(h * HEAD_PAD, (h + 1) * HEAD_PAD)
        kh = _dot(latb, wk_ref[:, hs])
        ms = jnp.sum(kh * kh, axis=-1, keepdims=True) * (1.0 / MLA_NOPE)
        k_ref[:, hs] = (kh * lax.rsqrt(ms + EPS) * gkn_ref[...] + kr).astype(k_ref.dtype)


def _mla_kv(lat, kr, w, tm, v_transposed):
    m = lat.shape[0]
    assert m % tm == 0
    consts = [w["kv_k_w"], w["kv_v_w"], w["gain_kn"]]
    row = lambda wd: pl.BlockSpec((tm, wd), lambda i: (i, 0))
    if v_transposed:
        v_spec = pl.BlockSpec((1, MLA_HEADS * V_SLOT_ROWS, tm), lambda i: (i, 0, 0))
        v_shape = jax.ShapeDtypeStruct((m // tm, MLA_HEADS * V_SLOT_ROWS, tm), BF16)
    else:
        v_spec, v_shape = row(MLA_WIDTH), jax.ShapeDtypeStruct((m, MLA_WIDTH), BF16)
    return pl.pallas_call(
        functools.partial(_mla_kv_kernel, v_transposed=v_transposed),
        grid=(m // tm,),
        in_specs=[row(MLA_KV_LORA), row(MLA_ROPE)] + [_const_spec(a.shape) for a in consts],
        out_specs=[row(MLA_HEADS * HEAD_PAD), v_spec],
        out_shape=[jax.ShapeDtypeStruct((m, MLA_HEADS * HEAD_PAD), BF16), v_shape],
        compiler_params=_cparams("arbitrary"),
        name="mla_kv",
    )(lat, kr, *consts)


def _softmax_step(q, k, v, carry, mask):
    m_prev, l_prev, acc = carry
    s = _dot_nt(q, k) * (MLA_QK ** -0.5)
    if mask is not None:
        s = jnp.where(mask, s, -jnp.inf)
    m_new = jnp.maximum(m_prev, jnp.max(s, axis=-1, keepdims=True))
    alpha = jnp.exp(m_prev - m_new)
    p = jnp.exp(s - m_new)
    l_new = alpha * l_prev + jnp.sum(p, axis=-1, keepdims=True)
    return m_new, l_new, alpha * acc + _dot(p.astype(BF16), v)


def _gated_pair_out(o_ref, gate_ref, outs, pair):
    lane = lax.broadcasted_iota(jnp.int32, outs[0].shape, 1)
    ps = slice(pair * LANES, (pair + 1) * LANES)
    g = gate_ref[:, ps]
    o_ref[:, ps] = (jnp.where(lane < MLA_V, outs[0], outs[1]) * (g * jax.nn.sigmoid(g))).astype(o_ref.dtype)


def _attn_kernel(q_ref, k_ref, v_ref, gate_ref, o_ref, m_s, acc_s, s_s, *, tq):
    i = pl.program_id(1)
    key_c = lax.broadcasted_iota(jnp.int32, (tq, tq), 0) // CHUNK
    qry_c = lax.broadcasted_iota(jnp.int32, (tq, tq), 1) // CHUNK
    diag_mask = key_c <= qry_c
    m_s[...] = jnp.full(m_s.shape, NEG_BIG, F32)
    acc_s[...] = jnp.zeros(acc_s.shape, F32)

    def tile_step(j, mask):
        rows = pl.ds(pl.multiple_of(j * tq, tq), tq)
        for h in range(MLA_HEADS):
            hs = slice(h * HEAD_PAD, (h + 1) * HEAD_PAD)
            s_s[h] = _dot(k_ref[rows, hs], q_ref[0, hs, :])
        for h in range(MLA_HEADS):
            alphas, probs = [], []
            for q0 in range(0, tq, LANES):
                qs = slice(q0, q0 + LANES)
                s = s_s[h, :, qs]
                if mask is not None:
                    s = jnp.where(mask[:, qs], s, -jnp.inf)
                m_prev = m_s[h, :, qs]
                m_new = jnp.maximum(m_prev, jnp.max(s, axis=0, keepdims=True))
                alphas.append(jnp.exp2((m_prev[0:1, :] - m_new[0:1, :]) * SOFTMAX_LOG2_SCALE))
                probs.append(jnp.exp2((s - m_new[0:1, :]) * SOFTMAX_LOG2_SCALE).astype(BF16))
                m_s[h, :, qs] = m_new
            p = jnp.concatenate(probs, axis=1)
            pv = _dot(v_ref[j, h * V_SLOT_ROWS:(h + 1) * V_SLOT_ROWS, :], p)
            acc_s[h] = jnp.concatenate(alphas, axis=1) * acc_s[h] + pv

    def body(j, carry):
        tile_step(j, None)
        return carry

    lax.fori_loop(0, i, body, 0)
    tile_step(i, diag_mask)
    for pair in range(MLA_HEADS // 2):
        o_t = jnp.concatenate([acc_s[2 * pair + sub, 0:MLA_V, :] / acc_s[2 * pair + sub, MLA_V:MLA_V + 1, :]
                               for sub in range(2)], axis=0)
        ps = slice(pair * LANES, (pair + 1) * LANES)
        g = gate_ref[:, ps]
        o_ref[:, ps] = (o_t.T * (g * jax.nn.sigmoid(g))).astype(o_ref.dtype)


def _attn_prompt(q_t, k, v_t, gate, batch, seq, tq):
    nq = seq // tq
    assert nq * tq == seq and tq % CHUNK == 0
    return pl.pallas_call(
        functools.partial(_attn_kernel, tq=tq),
        grid=(batch, nq),
        in_specs=[pl.BlockSpec((1, MLA_HEADS * HEAD_PAD, tq), lambda b, i: (b, 0, i)),
                  pl.BlockSpec((seq, MLA_HEADS * HEAD_PAD), lambda b, i: (b, 0)),
                  pl.BlockSpec((nq, MLA_HEADS * V_SLOT_ROWS, tq), lambda b, i: (b, 0, 0)),
                  pl.BlockSpec((tq, MLA_WIDTH), lambda b, i: (b * nq + i, 0))],
        out_specs=pl.BlockSpec((tq, MLA_WIDTH), lambda b, i: (b * nq + i, 0)),
        out_shape=jax.ShapeDtypeStruct((batch * seq, MLA_WIDTH), BF16),
        scratch_shapes=[pltpu.VMEM((MLA_HEADS, SUBLANES, tq), F32),
                        pltpu.VMEM((MLA_HEADS, V_SLOT_ROWS, tq), F32), pltpu.VMEM((MLA_HEADS, tq, tq), F32)],
        compiler_params=_cparams("arbitrary", "arbitrary"),
        name="attn_prompt",
    )(q_t, k, v_t, gate)


def _attn_step_kernel(q_ref, k_ref, v_ref, gate_ref, o_ref, *, q_pos0, n_keys):
    tq = q_ref.shape[0]
    tk = k_ref.shape[1]
    k_pos = lax.broadcasted_iota(jnp.int32, (tq, tk), 1)
    q_pos = lax.broadcasted_iota(jnp.int32, (tq, tk), 0) + q_pos0
    mask = jnp.logical_and(k_pos < n_keys, k_pos // CHUNK <= q_pos // CHUNK)
    init = (jnp.full((tq, 1), NEG_BIG, F32), jnp.zeros((tq, 1), F32), jnp.zeros((tq, LANES), F32))
    for pair in range(MLA_HEADS // 2):
        vs = slice(pair * LANES, (pair + 1) * LANES)
        outs = []
        for sub in range(2):
            hs = slice((2 * pair + sub) * HEAD_PAD, (2 * pair + sub + 1) * HEAD_PAD)
            _, l, acc = _softmax_step(q_ref[:, hs], k_ref[0, :, hs], v_ref[0, :, vs], init, mask)
            outs.append(acc / l)
        _gated_pair_out(o_ref, gate_ref, outs, pair)


def _attn_step(q, k, v, gate, batch, tq, n_keys):
    tk = k.shape[1]
    return pl.pallas_call(
        functools.partial(_attn_step_kernel, q_pos0=n_keys - tq, n_keys=n_keys),
        grid=(batch,),
        in_specs=[pl.BlockSpec((tq, MLA_HEADS * HEAD_PAD), lambda b: (b, 0)),
                  pl.BlockSpec((1, tk, MLA_HEADS * HEAD_PAD), lambda b: (b, 0, 0)),
                  pl.BlockSpec((1, tk, MLA_WIDTH), lambda b: (b, 0, 0)),
                  pl.BlockSpec((tq, MLA_WIDTH), lambda b: (b, 0))],
        out_specs=pl.BlockSpec((tq, MLA_WIDTH), lambda b: (b, 0)),
        out_shape=jax.ShapeDtypeStruct((batch * tq, MLA_WIDTH), BF16),
        compiler_params=_cparams("arbitrary"),
        name="attn_step",
    )(q, k, v, gate)


def _pad_cols(a, width):
    return jnp.pad(a, ((0, 0), (0, width - a.shape[1])))


def _ssd_weights(j, in_w, conv_w, conv_b, dt_bias, a_log, d, norm_w, out_w):
    zx = SSD_D_INNER + SSD_CONV_DIM
    return {
        "in_w": jnp.concatenate([0.5 * in_w[j][:, :SSD_D_INNER], in_w[j][:, SSD_D_INNER:zx],
                                 _pad_cols(in_w[j][:, zx:], LANES)], axis=1).astype(BF16),
        "conv_w": 0.5 * conv_w[j],
        "conv_b": 0.5 * conv_b[j][None, :],
        "dt_bias": _pad_cols(dt_bias[j][None, :], LANES),
        "a_log": _pad_cols(a_log[j][None, :], LANES),
        "d_exp": jnp.repeat(d[j], SSD_HEAD_DIM)[None, :],
        "norm_w": norm_w[j][None, :],
        "out_w": out_w[j].astype(BF16),
    }


def _head_pad_cols(w, per_head, take, offset=0):
    k = w.shape[0]
    w = w.reshape(k, MLA_HEADS, per_head)[:, :, offset:offset + take]
    return jnp.pad(w, ((0, 0), (0, 0), (0, HEAD_PAD - take))).reshape(k, MLA_HEADS * HEAD_PAD)


def _rotate_half(a):
    half = MLA_ROPE // 2
    return jnp.concatenate([-a[..., half:], a[..., :half]], axis=-1)


def _swap_halves(a):
    half = MLA_ROPE // 2
    return jnp.concatenate([a[..., half:], a[..., :half]], axis=-1)


def _rope_slot(a):
    pad = [(0, 0)] * (a.ndim - 1) + [(MLA_NOPE, HEAD_PAD - MLA_QK)]
    return jnp.pad(a, pad)


def _q_b_cols(w):
    k = w.shape[0]
    w = w.reshape(k, MLA_HEADS, MLA_QK)
    return jnp.concatenate([w, _rotate_half(w[:, :, MLA_NOPE:])], axis=-1).reshape(k, MLA_HEADS * HEAD_PAD)


def _mla_weights(j, in_w, q_a_norm, q_b_w, kv_a_norm, kv_b_w, qn, qr, kn, kr, out_w):
    c1, c2, c3 = MLA_Q_LORA, MLA_Q_LORA + MLA_KV_LORA, MLA_Q_LORA + MLA_KV_LORA + MLA_ROPE
    w = in_w[j]
    kr_cols = jnp.concatenate([jnp.zeros((w.shape[0], MLA_NOPE), F32), w[:, c2:c3],
                               _rotate_half(w[:, c2:c3])], axis=1)
    zeros = lambda n: jnp.zeros((n,), F32)
    gain_q = jnp.concatenate([qn[j], qr[j], zeros(HEAD_PAD - MLA_QK)])
    return {
        "in_w": jnp.concatenate([w[:, :c2], kr_cols, w[:, c3:]], axis=1).astype(BF16),
        "q_a_norm": q_a_norm[j][None, :],
        "q_b_w": _q_b_cols(q_b_w[j]).astype(BF16),
        "gain_q": jnp.tile(gain_q, 2)[None, :],
        "gain_q_rot": jnp.tile(_rope_slot(_swap_halves(qr[j])), 2)[None, :],
        "gain_kr_rot": _rope_slot(_swap_halves(kr[j]))[None, :],
        "kv_a_norm": kv_a_norm[j][None, :],
        "kv_k_w": _head_pad_cols(kv_b_w[j], MLA_NOPE + MLA_V, MLA_NOPE).astype(BF16),
        "kv_v_w": kv_b_w[j].reshape(MLA_KV_LORA, MLA_HEADS, MLA_NOPE + MLA_V)[:, :, MLA_NOPE:]
                  .reshape(MLA_KV_LORA, MLA_WIDTH).astype(BF16),
        "gain_kr": _rope_slot(kr[j])[None, :],
        "gain_kn": jnp.concatenate([kn[j], zeros(HEAD_PAD - MLA_NOPE)])[None, :],
        "out_w": out_w[j].astype(BF16),
    }


def _rope_tables(pos0, seq, rows):
    inv = 1.0 / (ROPE_BASE ** (jnp.arange(0, MLA_ROPE, 2, dtype=F32) / MLA_ROPE))
    ang = (pos0 + jnp.arange(seq)).astype(F32)[:, None] * inv[None, :]
    cos, sin = jnp.cos(ang), jnp.sin(ang)
    z = lambda n: jnp.zeros((seq, n), F32)
    cos_t = jnp.concatenate([jnp.ones((seq, MLA_NOPE), F32), cos, cos, z(HEAD_PAD - MLA_QK)], axis=1)
    sin_t = jnp.concatenate([z(MLA_NOPE), sin, sin, z(HEAD_PAD - MLA_QK)], axis=1)
    reps = max(1, rows // seq)
    return tuple(jnp.tile(t, (reps, 2)) for t in (cos_t, sin_t))


def _row_tile(m, want):
    tm = min(m, want)
    assert m % tm == 0
    return tm


def _ssd_layer(h, p, lnw, ple, w, batch, seq, chunk, conv_prev, h0):
    tm = _row_tile(h.shape[0], 256)
    zs, xs, bt, c, dt, conv_new = _ssd_in(h, lnw, w, conv_prev, batch, seq, min(seq, SSD_BLOCK_ROWS))
    yn, ht = _ssd_scan(zs, xs, bt, c, dt, h0, w, batch, seq, chunk)
    h = _out_ple(yn, h, p, w["out_w"], *ple, tm)
    return h, conv_new, ht.reshape(batch, SSD_HEADS, SSD_HEAD_DIM, SSD_STATE)


def _mla_layer(h, p, lnw, ple, w, batch, seq, lat_past, kr_past):
    tm = _row_tile(h.shape[0], 256)
    pos0 = 0 if lat_past is None else lat_past.shape[1]
    rope = _rope_tables(pos0, seq, tm)
    prompt = lat_past is None
    q, lat, kr, gate = _mla_in(h, lnw, w, rope, tm, q_transposed=prompt)
    if prompt:
        tile = min(seq, ATTN_TILE)
        k, v_t = _mla_kv(lat, kr, w, tile, v_transposed=True)
        og = _attn_prompt(q, k, v_t, gate, batch, seq, tile)
    else:
        n_keys = pos0 + seq
        tk = -(-n_keys // LANES) * LANES
        pad = ((0, 0), (0, tk - n_keys), (0, 0))
        lat_all = jnp.pad(jnp.concatenate([lat_past, lat.reshape(batch, seq, -1)], axis=1), pad)
        kr_all = jnp.pad(jnp.concatenate([kr_past, kr.reshape(batch, seq, -1)], axis=1), pad)
        k, v = _mla_kv(lat_all.reshape(batch * tk, -1), kr_all.reshape(batch * tk, -1), w,
                       math.gcd(batch * tk, 512), v_transposed=False)
        og = _attn_step(q, k.reshape(batch, tk, -1), v.reshape(batch, tk, -1), gate, batch, seq, n_keys)
    h = _out_ple(og, h, p, w["out_w"], *ple, tm)
    return h, lat.reshape(batch, seq, -1), kr.reshape(batch, seq, -1)


def kernel(x_prompt, x_sample, cache_conv, state_ssm, cache_kv_latent, cache_k_rope, p_prompt, p_sample, ln_w, ssd_in_w, ssd_conv_w, ssd_conv_b, ssd_dt_bias, ssd_A_log, ssd_D, ssd_norm_w, ssd_out_w, mla_in_w, mla_q_a_norm, mla_q_b_w, mla_kv_a_norm, mla_kv_b_w, mla_q_nope_norm, mla_q_rope_norm, mla_k_nope_norm, mla_k_rope_norm, mla_out_w, ple_up_w, ple_norm_w, ple_gate_w):
    bp, tp, d = x_prompt.shape
    bs, ts, _ = x_sample.shape
    hp = x_prompt.reshape(bp * tp, d)
    hs = x_sample.reshape(bs * ts, d)
    conv_p, ssm_p, lat_p, kr_p = [], [], [], []
    conv_s, ssm_s, lat_s, kr_s = [], [], [], []
    for i in range(DEPTH):
        j = i // 2
        lnw = ln_w[i][None, :]
        ple = (ple_norm_w[i][None, :], ple_gate_w[i].astype(BF16), ple_up_w[i].astype(BF16))
        pp = p_prompt[i].reshape(bp * tp, PLE_DIM)
        ps = p_sample[i].reshape(bs * ts, PLE_DIM)
        if i % 2 == 0:
            w = _ssd_weights(j, ssd_in_w, ssd_conv_w, ssd_conv_b, ssd_dt_bias, ssd_A_log, ssd_D,
                             ssd_norm_w, ssd_out_w)
            hp, cp, sp = _ssd_layer(hp, pp, lnw, ple, w, bp, tp, min(tp, SSD_SCAN_CHUNK), None, None)
            hs, cs, ss = _ssd_layer(hs, ps, lnw, ple, w, bs, ts, ts, cache_conv[j], state_ssm[j])
            conv_p.append(cp); ssm_p.append(sp); conv_s.append(cs); ssm_s.append(ss)
        else:
            w = _mla_weights(j, mla_in_w, mla_q_a_norm, mla_q_b_w, mla_kv_a_norm, mla_kv_b_w,
                             mla_q_nope_norm, mla_q_rope_norm, mla_k_nope_norm, mla_k_rope_norm, mla_out_w)
            hp, lp, rp = _mla_layer(hp, pp, lnw, ple, w, bp, tp, None, None)
            hs, ls, rs = _mla_layer(hs, ps, lnw, ple, w, bs, ts, cache_kv_latent[j], cache_k_rope[j])
            lat_p.append(lp); kr_p.append(rp); lat_s.append(ls); kr_s.append(rs)
    return (hp.reshape(bp, tp, d), hs.reshape(bs, ts, d),
            jnp.stack(conv_p), jnp.stack(ssm_p), jnp.stack(lat_p), jnp.stack(kr_p),
            jnp.stack(conv_s), jnp.stack(ssm_s), jnp.stack(lat_s), jnp.stack(kr_s))
```

```python
import functools
import math

import jax
import jax.numpy as jnp
from jax import lax
from jax.experimental import pallas as pl
from jax.experimental.pallas import tpu as pltpu

F32 = jnp.float32
BF16 = jnp.bfloat16

D_MODEL = 1024
DEPTH = 4
CHUNK = 64
PLE_DIM = 256
EPS = 1e-6

SSD_D_INNER = 2 * D_MODEL
SSD_HEAD_DIM = 64
SSD_HEADS = SSD_D_INNER // SSD_HEAD_DIM
SSD_GROUPS = 8
SSD_HPG = SSD_HEADS // SSD_GROUPS
SSD_STATE = 128
SSD_CONV_W = 4
SSD_BC = SSD_GROUPS * SSD_STATE
SSD_CONV_DIM = SSD_D_INNER + 2 * SSD_BC
SSD_GROUP_W = SSD_HPG * SSD_HEAD_DIM

MLA_HEADS = 16
MLA_NOPE = 64
MLA_ROPE = 32
MLA_V = 64
MLA_Q_LORA = 384
MLA_KV_LORA = 256
MLA_QK = MLA_NOPE + MLA_ROPE
MLA_WIDTH = MLA_HEADS * MLA_V
ROPE_BASE = 10000.0

LANES = 128
SUBLANES = 8
HEAD_PAD = LANES
VMEM_LIMIT_BYTES = 52 * 1024 * 1024
NEG_BIG = -1e30
SOFTMAX_LOG2_SCALE = (MLA_QK ** -0.5) * math.log2(math.e)
COL_CHUNK = 512
CONV_ROWS = 128
V_ONES_ROWS = 16
V_SLOT_ROWS = MLA_V + V_ONES_ROWS
ROW_TILE = 512
ATTN_TILE = 256
SSD_BLOCK_ROWS = 256
SSD_SCAN_CHUNK = 128


def _cparams(*sem):
    return pltpu.CompilerParams(dimension_semantics=sem, vmem_limit_bytes=VMEM_LIMIT_BYTES)


def _const_spec(shape):
    nd = len(shape)
    return pl.BlockSpec(shape, lambda *_: (0,) * nd, pipeline_mode=pl.Buffered(1))


def _rms(x, w):
    return x * lax.rsqrt(jnp.mean(x * x, axis=-1, keepdims=True) + EPS) * w


def _dot(a, b):
    return jnp.dot(a, b, preferred_element_type=F32)


def _dot_nt(a, b):
    return lax.dot_general(a, b, (((1,), (1,)), ((), ())), preferred_element_type=F32)


def _dot_tn(a, b):
    return lax.dot_general(a, b, (((0,), (0,)), ((), ())), preferred_element_type=F32)


def _store_dot(o_ref, a, w_ref, col0=0):
    width = o_ref.shape[1]
    for c in range(0, width, COL_CHUNK):
        cw = min(COL_CHUNK, width - c)
        o_ref[:, c:c + cw] = _dot(a, w_ref[:, col0 + c:col0 + c + cw]).astype(o_ref.dtype)


def _ssd_in_kernel(*refs, tiles_per_seq, has_init):
    if has_init:
        (x_ref, lnw_ref, w_ref, convw_ref, convb_ref, cprev_ref,
         z_ref, xs_ref, b_ref, c_ref, dt_ref, convnew_ref, ext) = refs
    else:
        (x_ref, lnw_ref, w_ref, convw_ref, convb_ref,
         z_ref, xs_ref, b_ref, c_ref, dt_ref, convnew_ref, ext) = refs
    tm = x_ref.shape[0]
    tail = SSD_CONV_W - 1
    t = pl.program_id(0) % tiles_per_seq
    xn = _rms(x_ref[...], lnw_ref[...]).astype(BF16)

    @pl.when(t == 0)
    def _init():
        ext[0:SUBLANES, :] = jnp.zeros((SUBLANES, SSD_CONV_DIM), F32)
        if has_init:
            ext[SUBLANES - tail:SUBLANES, :] = cprev_ref[0]

    dt_ref[...] = _dot(xn, w_ref[:, SSD_D_INNER + SSD_CONV_DIM:])
    def project(c0):
        cs = slice(c0, c0 + COL_CHUNK)
        ext[SUBLANES:SUBLANES + tm, cs] = _dot(xn, w_ref[:, SSD_D_INNER + c0:SSD_D_INNER + c0 + COL_CHUNK])
        convnew_ref[0, :, cs] = ext[pl.ds(SUBLANES + tm - tail, tail), cs]

    project(0)
    for c0 in range(0, SSD_CONV_DIM, COL_CHUNK):
        cs = slice(c0, c0 + COL_CHUNK)
        if c0 + COL_CHUNK < SSD_CONV_DIM:
            project(c0 + COL_CHUNK)
        if c0 % (2 * COL_CHUNK) == 0:
            zc = slice(c0 // 2, c0 // 2 + COL_CHUNK)
            hz = _dot(xn, w_ref[:, zc])
            z_ref[:, zc] = hz * jnp.tanh(hz) + hz
        if c0 < SSD_D_INNER:
            o_ref, o0 = xs_ref, c0
        elif c0 < SSD_D_INNER + SSD_BC:
            o_ref, o0 = b_ref, c0 - SSD_D_INNER
        else:
            o_ref, o0 = c_ref, c0 - SSD_D_INNER - SSD_BC
        for r0 in range(0, tm, CONV_ROWS):
            rb = min(CONV_ROWS, tm - r0)
            for l0 in range(0, COL_CHUNK, LANES):
                ls = slice(c0 + l0, c0 + l0 + LANES)
                e = ext[r0:r0 + rb + SUBLANES, ls]
                e1 = pltpu.roll(e, 1, 0)
                far = e * convw_ref[1:2, ls] + e1 * convw_ref[0:1, ls]
                near = e * convw_ref[3:4, ls] + e1 * convw_ref[2:3, ls]
                hv = (convb_ref[:, ls] + near[SUBLANES:SUBLANES + rb]) + pltpu.roll(far, 2, 0)[SUBLANES:SUBLANES + rb]
                y = hv * jnp.tanh(hv) + hv
                if o_ref is b_ref:
                    b_ref[0, o0 + l0:o0 + l0 + LANES, r0:r0 + rb] = y.T
                else:
                    o_ref[r0:r0 + rb, o0 + l0:o0 + l0 + LANES] = y.astype(o_ref.dtype)
    ext[0:SUBLANES, :] = ext[tm:tm + SUBLANES, :]


def _ssd_in(x, lnw, w, conv_prev, batch, seq, tm):
    m, d = x.shape
    assert seq % tm == 0 and tm % SUBLANES == 0 and m == batch * seq
    tps = seq // tm
    has_init = conv_prev is not None
    tail = SSD_CONV_W - 1
    consts = [lnw, w["in_w"], w["conv_w"], w["conv_b"]]
    row = lambda wd: pl.BlockSpec((tm, wd), lambda i: (i, 0))
    conv_spec = pl.BlockSpec((1, tail, SSD_CONV_DIM), lambda i: (i // tps, 0, 0))
    in_specs = [row(d)] + [_const_spec(a.shape) for a in consts]
    args = [x] + consts
    if has_init:
        in_specs.append(conv_spec)
        args.append(conv_prev)
    return pl.pallas_call(
        functools.partial(_ssd_in_kernel, tiles_per_seq=tps, has_init=has_init),
        grid=(m // tm,),
        in_specs=in_specs,
        out_specs=[row(SSD_D_INNER), row(SSD_D_INNER), pl.BlockSpec((1, SSD_BC, tm), lambda i: (i, 0, 0)),
                   row(SSD_BC), row(LANES), conv_spec],
        out_shape=[jax.ShapeDtypeStruct((m, SSD_D_INNER), F32),
                   jax.ShapeDtypeStruct((m, SSD_D_INNER), F32),
                   jax.ShapeDtypeStruct((m // tm, SSD_BC, tm), F32),
                   jax.ShapeDtypeStruct((m, SSD_BC), BF16),
                   jax.ShapeDtypeStruct((m, LANES), F32),
                   jax.ShapeDtypeStruct((batch, tail, SSD_CONV_DIM), F32)],
        scratch_shapes=[pltpu.VMEM((tm + SUBLANES, SSD_CONV_DIM), F32)],
        compiler_params=_cparams("arbitrary"),
        name="ssd_in",
    )(*args)


def _dot_exact_lhs(a, x):
    x1 = x.astype(BF16)
    r1 = x - x1.astype(F32)
    x2 = r1.astype(BF16)
    x3 = (r1 - x2.astype(F32)).astype(BF16)
    return _dot(a, x1) + _dot(a, x2) + _dot(a, x3)


def _expand_heads(cols, g, lane_head):
    shape = (cols.shape[0], SSD_GROUP_W)
    h0 = SSD_HPG * g
    out = jnp.broadcast_to(cols[:, h0 + SSD_HPG - 1:h0 + SSD_HPG], shape)
    for r in range(SSD_HPG - 2, -1, -1):
        out = jnp.where(lane_head[:shape[0]] == r, jnp.broadcast_to(cols[:, h0 + r:h0 + r + 1], shape), out)
    return out


def _select_heads(parts, lane_head):
    out = parts[-1]
    for r in range(len(parts) - 2, -1, -1):
        out = jnp.where(lane_head == r, parts[r], out)
    return out


def _ssd_kernel(*refs, L, has_init):
    if has_init:
        (zs_ref, xs_ref, bt_ref, c_ref, dt_ref, h0_ref, dtb_ref, alog_ref,
         dexp_ref, normw_ref, yn_ref, ht_ref, st, cb_s) = refs
    else:
        (zs_ref, xs_ref, bt_ref, c_ref, dt_ref, dtb_ref, alog_ref,
         dexp_ref, normw_ref, yn_ref, ht_ref, st, cb_s) = refs
    c = pl.program_id(1)
    last = pl.num_programs(1) - 1

    @pl.when(c == 0)
    def _init():
        if has_init:
            for g in range(SSD_GROUPS):
                st[g] = h0_ref[0, g * SSD_GROUP_W:(g + 1) * SSD_GROUP_W, :].T
        else:
            st[...] = jnp.zeros(st.shape, F32)

    rows = dt_ref.shape[0]
    n_chunks = rows // L
    fused = L % LANES == 0
    group_n = lambda g: slice(g * SSD_STATE, (g + 1) * SSD_STATE)

    for ci in range(n_chunks):
        rs = slice(ci * L, (ci + 1) * L)
        for g in range(SSD_GROUPS):
            cb_s[ci * SSD_GROUPS + g] = _dot(c_ref[rs, group_n(g)], bt_ref[0, group_n(g), rs].astype(BF16))

    dt = jax.nn.softplus(dt_ref[...] + dtb_ref[...])
    dta = dt * (-jnp.exp(alog_ref[...]))
    row = lax.broadcasted_iota(jnp.int32, (rows, rows), 0)
    col = lax.broadcasted_iota(jnp.int32, (rows, rows), 1)
    same_chunk_tri = jnp.logical_and(row >= col, row // L == col // L)
    acs = _dot_exact_lhs(same_chunk_tri.astype(BF16), dta)
    acs_t = acs.T
    dt_t = dt.T
    tri = lax.broadcasted_iota(jnp.int32, (L, L), 0) >= lax.broadcasted_iota(jnp.int32, (L, L), 1)
    lane_head = lax.broadcasted_iota(jnp.int32, (L, SSD_GROUP_W), 1) // SSD_HEAD_DIM
    lane_head_n = lax.broadcasted_iota(jnp.int32, (SSD_STATE, SSD_GROUP_W), 1) // SSD_HEAD_DIM

    for ci in range(n_chunks):
        rs = slice(ci * L, (ci + 1) * L)
        acs_c = acs[rs]
        acs_tc = acs_t[:, rs]
        dt_tc = dt_t[:, rs]
        cdec = jnp.exp(acs_c[L - 1:L, :])
        wdt_t = jnp.exp(acs_tc[:, L - 1:L] - acs_tc) * dt_tc
        for g in range(SSD_GROUPS):
            gs = slice(g * SSD_GROUP_W, (g + 1) * SSD_GROUP_W)
            xg = xs_ref[rs, gs]
            xgb = xg.astype(BF16)
            cg = c_ref[rs, group_n(g)].astype(F32)
            cb = cb_s[ci * SSD_GROUPS + g]
            sg = st[g]
            sgb = sg.astype(BF16)
            if fused:
                rhs = jnp.concatenate([xgb, sgb], axis=0)
            parts = []
            for r in range(SSD_HPG):
                h = SSD_HPG * g + r
                a_col = jnp.broadcast_to(acs_c[:, h:h + 1], (L, LANES))
                seg = a_col[:, :L] - acs_tc[h:h + 1, :]
                m = (cb * jnp.exp(jnp.where(tri, seg, -jnp.inf)) * dt_tc[h:h + 1, :]).astype(BF16)
                ce = (cg * jnp.exp(a_col)).astype(BF16)
                if fused:
                    parts.append(_dot(jnp.concatenate([m, ce], axis=1), rhs))
                else:
                    parts.append(_dot(m, xgb) + _dot(ce, sgb))
            y = _select_heads(parts, lane_head)
            y = y + dexp_ref[:, gs] * xg
            y = y * zs_ref[rs, gs]
            yn_ref[rs, gs] = _rms(y, normw_ref[:, gs]).astype(yn_ref.dtype)
            btg = bt_ref[0, group_n(g), rs]
            upd = [_dot((btg * wdt_t[SSD_HPG * g + r:SSD_HPG * g + r + 1, :]).astype(BF16), xgb)
                   for r in range(SSD_HPG)]
            st[g] = sg * _expand_heads(cdec, g, lane_head) + _select_heads(upd, lane_head_n)

    @pl.when(c == last)
    def _state_out():
        for g in range(SSD_GROUPS):
            ht_ref[0, g * SSD_GROUP_W:(g + 1) * SSD_GROUP_W, :] = st[g].T


def _ssd_scan(zs, xs, bt, c, dt, h0, w, batch, seq, L):
    rows = bt.shape[2]
    nc = seq // rows
    assert nc * rows == seq and rows % L == 0 and L % SUBLANES == 0
    has_init = h0 is not None
    row_spec = lambda wd: pl.BlockSpec((rows, wd), lambda bi, ci: (bi * nc + ci, 0))
    state_spec = pl.BlockSpec((1, SSD_D_INNER, SSD_STATE), lambda bi, ci: (bi, 0, 0))
    bt_spec = pl.BlockSpec((1, SSD_BC, rows), lambda bi, ci: (bi * nc + ci, 0, 0))
    in_specs = [row_spec(SSD_D_INNER), row_spec(SSD_D_INNER), bt_spec, row_spec(SSD_BC), row_spec(LANES)]
    args = [zs, xs, bt, c, dt]
    if has_init:
        in_specs.append(state_spec)
        args.append(h0.reshape(batch, SSD_D_INNER, SSD_STATE))
    consts = [w["dt_bias"], w["a_log"], w["d_exp"], w["norm_w"]]
    in_specs += [_const_spec(a.shape) for a in consts]
    args += consts
    return pl.pallas_call(
        functools.partial(_ssd_kernel, L=L, has_init=has_init),
        grid=(batch, nc),
        in_specs=in_specs,
        out_specs=[row_spec(SSD_D_INNER), state_spec],
        out_shape=[jax.ShapeDtypeStruct((batch * seq, SSD_D_INNER), BF16),
                   jax.ShapeDtypeStruct((batch, SSD_D_INNER, SSD_STATE), F32)],
        scratch_shapes=[pltpu.VMEM((SSD_GROUPS, SSD_STATE, SSD_GROUP_W), F32),
                        pltpu.VMEM((rows // L * SSD_GROUPS, L, L), F32)],
        compiler_params=_cparams("arbitrary", "arbitrary"),
        name="ssd_scan",
    )(*args)


def _out_ple_kernel(y_ref, h_ref, p_ref, wout_ref, plenorm_ref, wgate_ref, wup_ref, o_ref):
    h1 = h_ref[...] + _dot(y_ref[...], wout_ref[...])
    hn = _rms(h1, plenorm_ref[...]).astype(BF16)
    pb = p_ref[0].astype(BF16)
    for c in range(0, D_MODEL, COL_CHUNK):
        cs = slice(c, c + COL_CHUNK)
        gate = jax.nn.sigmoid(_dot(hn, wgate_ref[:, cs]))
        o_ref[:, cs] = h1[:, cs] + _dot(pb, wup_ref[:, cs]) * gate


def _out_ple(y, h, p, wout, plenorm, wgate, wup, tm):
    m, kd = y.shape
    assert m % tm == 0
    p_all, layer = p
    return pl.pallas_call(
        _out_ple_kernel,
        grid=(m // tm,),
        in_specs=[pl.BlockSpec((tm, kd), lambda i: (i, 0)),
                  pl.BlockSpec((tm, D_MODEL), lambda i: (i, 0)),
                  pl.BlockSpec((1, tm, PLE_DIM), lambda i: (layer, i, 0)),
                  _const_spec(wout.shape), _const_spec(plenorm.shape),
                  _const_spec(wgate.shape), _const_spec(wup.shape)],
        out_specs=pl.BlockSpec((tm, D_MODEL), lambda i: (i, 0)),
        out_shape=jax.ShapeDtypeStruct((m, D_MODEL), F32),
        compiler_params=_cparams("arbitrary"),
        name="out_ple",
    )(y, h, p_all, wout, plenorm, wgate, wup)


def _dot_hi_lo_rhs(x, a):
    hi = x.astype(BF16)
    lo = (x - hi.astype(F32)).astype(BF16)
    return _dot(hi, a) + _dot(lo, a)


def _head_part_matrix(width):
    def part(idx):
        within = idx % HEAD_PAD
        return (idx // HEAD_PAD) * 4 + jnp.where(within < MLA_NOPE, 0, jnp.where(within < MLA_QK, 1, 2))
    rows = part(lax.broadcasted_iota(jnp.int32, (width, width), 0))
    cols_i = lax.broadcasted_iota(jnp.int32, (width, width), 1)
    cols = part(cols_i)
    same = jnp.logical_and(rows == cols, cols_i % HEAD_PAD < MLA_QK)
    lane = lax.broadcasted_iota(jnp.int32, (1, width), 1) % HEAD_PAD
    inv_count = jnp.where(lane < MLA_NOPE, 1.0 / MLA_NOPE, jnp.where(lane < MLA_QK, 1.0 / MLA_ROPE, 0.0))
    return jnp.where(same, 1.0, 0.0).astype(BF16), inv_count


def _head_norm_rope(x, pmat, inv_count, gain_cos, gain_sin):
    inv = lax.rsqrt(_dot_hi_lo_rhs(x * x, pmat) * inv_count + EPS)
    shift = HEAD_PAD - MLA_ROPE
    x_rot = jnp.concatenate([pltpu.roll(x[:, s0:s0 + HEAD_PAD], shift, 1)
                             for s0 in range(0, x.shape[1], HEAD_PAD)], axis=1)
    return (x * inv) * gain_cos + (x_rot * inv) * gain_sin


def _mla_in_kernel(h_ref, lnw_ref, win_ref, qan_ref, wqb_ref, kvan_ref, gq_ref, gqrot_ref, gkr_ref, gkrrot_ref,
                   cos_ref, sin_ref, q_ref, lat_ref, kr_ref, gate_ref, *, q_transposed):
    xn = _rms(h_ref[...], lnw_ref[...]).astype(BF16)
    c_q, c_kv, c_kr = 0, MLA_Q_LORA, MLA_Q_LORA + MLA_KV_LORA
    c_gate = c_kr + HEAD_PAD
    pair_w = 2 * HEAD_PAD
    _store_dot(gate_ref, xn, win_ref, c_gate)
    lat_ref[...] = _rms(_dot(xn, win_ref[:, c_kv:c_kr]), kvan_ref[...])
    pmat, inv_count = _head_part_matrix(pair_w)
    cos_t, sin_t = cos_ref[...], sin_ref[...]
    kr = _head_norm_rope(_dot(xn, win_ref[:, c_kr:c_gate]), pmat[:HEAD_PAD, :HEAD_PAD],
                         inv_count[:, :HEAD_PAD], gkr_ref[...] * cos_t[:, :HEAD_PAD],
                         gkrrot_ref[...] * sin_t[:, :HEAD_PAD])
    kr_ref[...] = kr[:, MLA_NOPE:MLA_QK]
    qa = _rms(_dot(xn, win_ref[:, c_q:c_kv]), qan_ref[...]).astype(BF16)
    gain_cos = gq_ref[...] * cos_t
    gain_sin = gqrot_ref[...] * sin_t
    for pair in range(MLA_HEADS // 2):
        qq = _dot(qa, wqb_ref[:, pair * pair_w:(pair + 1) * pair_w])
        qh = _head_norm_rope(qq, pmat, inv_count, gain_cos, gain_sin)
        if q_transposed:
            q_ref[0, pair * pair_w:(pair + 1) * pair_w, :] = qh.T.astype(q_ref.dtype)
        else:
            q_ref[:, pair * pair_w:(pair + 1) * pair_w] = qh.astype(q_ref.dtype)


def _mla_in(h, lnw, w, rope, tm, q_transposed):
    m = h.shape[0]
    assert m % tm == 0
    cos_t, sin_t = rope
    period = cos_t.shape[0] // tm
    assert period * tm == cos_t.shape[0]
    rope_spec = pl.BlockSpec((tm, 2 * HEAD_PAD), lambda i: (i % period, 0))
    consts = [lnw, w["in_w"], w["q_a_norm"], w["q_b_w"], w["kv_a_norm"],
              w["gain_q"], w["gain_q_rot"], w["gain_kr"], w["gain_kr_rot"]]
    row = lambda wd: pl.BlockSpec((tm, wd), lambda i: (i, 0))
    qw = MLA_HEADS * HEAD_PAD
    if q_transposed:
        seq = period * tm
        q_spec = pl.BlockSpec((1, qw, tm), lambda i: (i // period, 0, i % period))
        q_shape = jax.ShapeDtypeStruct((m // seq, qw, seq), BF16)
    else:
        q_spec, q_shape = row(qw), jax.ShapeDtypeStruct((m, qw), BF16)
    return pl.pallas_call(
        functools.partial(_mla_in_kernel, q_transposed=q_transposed),
        grid=(m // tm,),
        in_specs=[row(D_MODEL)] + [_const_spec(a.shape) for a in consts] + [rope_spec] * 2,
        out_specs=[q_spec, row(MLA_KV_LORA), row(MLA_ROPE), row(MLA_WIDTH)],
        out_shape=[q_shape,
                   jax.ShapeDtypeStruct((m, MLA_KV_LORA), F32),
                   jax.ShapeDtypeStruct((m, MLA_ROPE), F32),
                   jax.ShapeDtypeStruct((m, MLA_WIDTH), F32)],
        compiler_params=_cparams("arbitrary"),
        name="mla_in",
    )(h, *consts, cos_t, sin_t)


def _mla_kv_kernel(lat_ref, kr_ref, wk_ref, wv_ref, gkn_ref, k_ref, v_ref, *, v_transposed):
    latb = lat_ref[...].astype(BF16)
    if v_transposed:
        ones = jnp.ones((V_ONES_ROWS, latb.shape[0]), v_ref.dtype)
        for c in range(0, MLA_WIDTH, COL_CHUNK):
            vt = _dot(latb, wv_ref[:, c:c + COL_CHUNK]).T.astype(v_ref.dtype)
            for hh in range(COL_CHUNK // MLA_V):
                r0 = (c // MLA_V + hh) * V_SLOT_ROWS
                v_ref[0, r0:r0 + MLA_V, :] = vt[hh * MLA_V:(hh + 1) * MLA_V]
                v_ref[0, r0 + MLA_V:r0 + V_SLOT_ROWS, :] = ones
    else:
        _store_dot(v_ref, latb, wv_ref)
    rows = latb.shape[0]
    kr = jnp.concatenate([jnp.zeros((rows, MLA_NOPE), F32), kr_ref[...],
                          jnp.zeros((rows, HEAD_PAD - MLA_QK), F32)], axis=1)
    for h in range(MLA_HEADS):
        hs = slice(h * HEAD_PAD, (h + 1) * HEAD_PAD)
        kh = _dot(latb, wk_ref[:, hs])
        ms = jnp.sum(kh * kh, axis=-1, keepdims=True) * (1.0 / MLA_NOPE)
        k_ref[:, hs] = (kh * lax.rsqrt(ms + EPS) * gkn_ref[...] + kr).astype(k_ref.dtype)


def _mla_kv(lat, kr, w, tm, v_transposed):
    m = lat.shape[0]
    assert m % tm == 0
    consts = [w["kv_k_w"], w["kv_v_w"], w["gain_kn"]]
    row = lambda wd: pl.BlockSpec((tm, wd), lambda i: (i, 0))
    if v_transposed:
        v_spec = pl.BlockSpec((1, MLA_HEADS * V_SLOT_ROWS, tm), lambda i: (i, 0, 0))
        v_shape = jax.ShapeDtypeStruct((m // tm, MLA_HEADS * V_SLOT_ROWS, tm), BF16)
    else:
        v_spec, v_shape = row(MLA_WIDTH), jax.ShapeDtypeStruct((m, MLA_WIDTH), BF16)
    return pl.pallas_call(
        functools.partial(_mla_kv_kernel, v_transposed=v_transposed),
        grid=(m // tm,),
        in_specs=[row(MLA_KV_LORA), row(MLA_ROPE)] + [_const_spec(a.shape) for a in consts],
        out_specs=[row(MLA_HEADS * HEAD_PAD), v_spec],
        out_shape=[jax.ShapeDtypeStruct((m, MLA_HEADS * HEAD_PAD), BF16), v_shape],
        compiler_params=_cparams("arbitrary"),
        name="mla_kv",
    )(lat, kr, *consts)


def _gated_pair_out(o_ref, gate_ref, outs, pair):
    lane = lax.broadcasted_iota(jnp.int32, outs[0].shape, 1)
    ps = slice(pair * LANES, (pair + 1) * LANES)
    g = gate_ref[:, ps]
    o_ref[:, ps] = (jnp.where(lane < MLA_V, outs[0], outs[1]) * (g * jax.nn.sigmoid(g))).astype(o_ref.dtype)


def _attn_kernel(q_ref, k_ref, v_ref, gate_ref, o_ref, m_s, acc_s, s_s, *, tq):
    i = pl.program_id(1)
    key_c = lax.broadcasted_iota(jnp.int32, (tq, tq), 0) // CHUNK
    qry_c = lax.broadcasted_iota(jnp.int32, (tq, tq), 1) // CHUNK
    diag_mask = key_c <= qry_c
    m_s[...] = jnp.full(m_s.shape, NEG_BIG, F32)
    acc_s[...] = jnp.zeros(acc_s.shape, F32)

    def tile_step(j, mask):
        rows = pl.ds(pl.multiple_of(j * tq, tq), tq)
        for h in range(MLA_HEADS):
            hs = slice(h * HEAD_PAD, (h + 1) * HEAD_PAD)
            s_s[h] = _dot(k_ref[rows, hs], q_ref[0, hs, :])
        for h in range(MLA_HEADS):
            alphas, probs = [], []
            for q0 in range(0, tq, LANES):
                qs = slice(q0, q0 + LANES)
                s = s_s[h, :, qs]
                if mask is not None:
                    s = jnp.where(mask[:, qs], s, -jnp.inf)
                m_prev = m_s[h, :, qs]
                m_new = jnp.maximum(m_prev, jnp.max(s, axis=0, keepdims=True))
                alphas.append(jnp.exp2((m_prev[0:1, :] - m_new[0:1, :]) * SOFTMAX_LOG2_SCALE))
                probs.append(jnp.exp2((s - m_new[0:1, :]) * SOFTMAX_LOG2_SCALE).astype(BF16))
                m_s[h, :, qs] = m_new
            p = jnp.concatenate(probs, axis=1)
            pv = _dot(v_ref[j, h * V_SLOT_ROWS:(h + 1) * V_SLOT_ROWS, :], p)
            acc_s[h] = jnp.concatenate(alphas, axis=1) * acc_s[h] + pv

    def body(j, carry):
        tile_step(j, None)
        return carry

    lax.fori_loop(0, i, body, 0)
    tile_step(i, diag_mask)
    for pair in range(MLA_HEADS // 2):
        o_t = jnp.concatenate([acc_s[2 * pair + sub, 0:MLA_V, :] / acc_s[2 * pair + sub, MLA_V:MLA_V + 1, :]
                               for sub in range(2)], axis=0)
        ps = slice(pair * LANES, (pair + 1) * LANES)
        g = gate_ref[:, ps]
        o_ref[:, ps] = (o_t.T * (g * jax.nn.sigmoid(g))).astype(o_ref.dtype)


def _attn_prompt(q_t, k, v_t, gate, batch, seq, tq):
    nq = seq // tq
    assert nq * tq == seq and tq % CHUNK == 0
    return pl.pallas_call(
        functools.partial(_attn_kernel, tq=tq),
        grid=(batch, nq),
        in_specs=[pl.BlockSpec((1, MLA_HEADS * HEAD_PAD, tq), lambda b, i: (b, 0, i)),
                  pl.BlockSpec((seq, MLA_HEADS * HEAD_PAD), lambda b, i: (b, 0)),
                  pl.BlockSpec((nq, MLA_HEADS * V_SLOT_ROWS, tq), lambda b, i: (b, 0, 0)),
                  pl.BlockSpec((tq, MLA_WIDTH), lambda b, i: (b * nq + i, 0))],
        out_specs=pl.BlockSpec((tq, MLA_WIDTH), lambda b, i: (b * nq + i, 0)),
        out_shape=jax.ShapeDtypeStruct((batch * seq, MLA_WIDTH), BF16),
        scratch_shapes=[pltpu.VMEM((MLA_HEADS, SUBLANES, tq), F32),
                        pltpu.VMEM((MLA_HEADS, V_SLOT_ROWS, tq), F32), pltpu.VMEM((MLA_HEADS, tq, tq), F32)],
        compiler_params=_cparams("arbitrary", "arbitrary"),
        name="attn_prompt",
    )(q_t, k, v_t, gate)


def _attn_step_kernel(q_ref, k_ref, v_ref, gate_ref, o_ref, s_s, *, q_pos0, n_keys):
    tq = q_ref.shape[0]
    tk = k_ref.shape[1]
    k_pos = lax.broadcasted_iota(jnp.int32, (tq, tk), 1)
    q_pos = lax.broadcasted_iota(jnp.int32, (tq, tk), 0) + q_pos0
    mask = jnp.logical_and(k_pos < n_keys, k_pos // CHUNK <= q_pos // CHUNK)
    for h in range(MLA_HEADS):
        hs = slice(h * HEAD_PAD, (h + 1) * HEAD_PAD)
        s_s[h] = _dot_nt(q_ref[:, hs], k_ref[0, :, hs])
    for pair in range(MLA_HEADS // 2):
        vs = slice(pair * LANES, (pair + 1) * LANES)
        outs = []
        for sub in range(2):
            s = jnp.where(mask, s_s[2 * pair + sub], -jnp.inf)
            p = jnp.exp2((s - jnp.max(s, axis=-1, keepdims=True)) * SOFTMAX_LOG2_SCALE)
            outs.append(_dot(p.astype(BF16), v_ref[0, :, vs]) / jnp.sum(p, axis=-1, keepdims=True))
        _gated_pair_out(o_ref, gate_ref, outs, pair)


def _attn_step(q, k, v, gate, batch, tq, n_keys):
    tk = k.shape[1]
    return pl.pallas_call(
        functools.partial(_attn_step_kernel, q_pos0=n_keys - tq, n_keys=n_keys),
        grid=(batch,),
        in_specs=[pl.BlockSpec((tq, MLA_HEADS * HEAD_PAD), lambda b: (b, 0)),
                  pl.BlockSpec((1, tk, MLA_HEADS * HEAD_PAD), lambda b: (b, 0, 0)),
                  pl.BlockSpec((1, tk, MLA_WIDTH), lambda b: (b, 0, 0)),
                  pl.BlockSpec((tq, MLA_WIDTH), lambda b: (b, 0))],
        out_specs=pl.BlockSpec((tq, MLA_WIDTH), lambda b: (b, 0)),
        out_shape=jax.ShapeDtypeStruct((batch * tq, MLA_WIDTH), BF16),
        scratch_shapes=[pltpu.VMEM((MLA_HEADS, tq, tk), F32)],
        compiler_params=_cparams("arbitrary"),
        name="attn_step",
    )(q, k, v, gate)


def _pad_cols(a, width):
    return jnp.pad(a, ((0, 0), (0, width - a.shape[1])))


def _ssd_weights(j, in_w, conv_w, conv_b, dt_bias, a_log, d, norm_w, out_w):
    zx = SSD_D_INNER + SSD_CONV_DIM
    return {
        "in_w": jnp.concatenate([0.5 * in_w[j][:, :SSD_D_INNER], in_w[j][:, SSD_D_INNER:zx],
                                 _pad_cols(in_w[j][:, zx:], LANES)], axis=1).astype(BF16),
        "conv_w": 0.5 * conv_w[j],
        "conv_b": 0.5 * conv_b[j][None, :],
        "dt_bias": _pad_cols(dt_bias[j][None, :], LANES),
        "a_log": _pad_cols(a_log[j][None, :], LANES),
        "d_exp": jnp.repeat(d[j], SSD_HEAD_DIM)[None, :],
        "norm_w": norm_w[j][None, :],
        "out_w": out_w[j].astype(BF16),
    }


def _head_pad_cols(w, per_head, take, offset=0):
    k = w.shape[0]
    w = w.reshape(k, MLA_HEADS, per_head)[:, :, offset:offset + take]
    return jnp.pad(w, ((0, 0), (0, 0), (0, HEAD_PAD - take))).reshape(k, MLA_HEADS * HEAD_PAD)


def _rotate_half(a):
    half = MLA_ROPE // 2
    return jnp.concatenate([-a[..., half:], a[..., :half]], axis=-1)


def _swap_halves(a):
    half = MLA_ROPE // 2
    return jnp.concatenate([a[..., half:], a[..., :half]], axis=-1)


def _rope_slot(a):
    pad = [(0, 0)] * (a.ndim - 1) + [(MLA_NOPE, HEAD_PAD - MLA_QK)]
    return jnp.pad(a, pad)


def _q_b_cols(w):
    k = w.shape[0]
    w = w.reshape(k, MLA_HEADS, MLA_QK)
    return jnp.concatenate([w, _rotate_half(w[:, :, MLA_NOPE:])], axis=-1).reshape(k, MLA_HEADS * HEAD_PAD)


def _mla_weights(j, in_w, q_a_norm, q_b_w, kv_a_norm, kv_b_w, qn, qr, kn, kr, out_w):
    c1, c2, c3 = MLA_Q_LORA, MLA_Q_LORA + MLA_KV_LORA, MLA_Q_LORA + MLA_KV_LORA + MLA_ROPE
    w = in_w[j]
    kr_cols = jnp.concatenate([jnp.zeros((w.shape[0], MLA_NOPE), F32), w[:, c2:c3],
                               _rotate_half(w[:, c2:c3])], axis=1)
    zeros = lambda n: jnp.zeros((n,), F32)
    gain_q = jnp.concatenate([qn[j], qr[j], zeros(HEAD_PAD - MLA_QK)])
    return {
        "in_w": jnp.concatenate([w[:, :c2], kr_cols, w[:, c3:]], axis=1).astype(BF16),
        "q_a_norm": q_a_norm[j][None, :],
        "q_b_w": _q_b_cols(q_b_w[j]).astype(BF16),
        "gain_q": jnp.tile(gain_q, 2)[None, :],
        "gain_q_rot": jnp.tile(_rope_slot(_swap_halves(qr[j])), 2)[None, :],
        "gain_kr_rot": _rope_slot(_swap_halves(kr[j]))[None, :],
        "kv_a_norm": kv_a_norm[j][None, :],
        "kv_k_w": _head_pad_cols(kv_b_w[j], MLA_NOPE + MLA_V, MLA_NOPE).astype(BF16),
        "kv_v_w": kv_b_w[j].reshape(MLA_KV_LORA, MLA_HEADS, MLA_NOPE + MLA_V)[:, :, MLA_NOPE:]
                  .reshape(MLA_KV_LORA, MLA_WIDTH).astype(BF16),
        "gain_kr": _rope_slot(kr[j])[None, :],
        "gain_kn": jnp.concatenate([kn[j], zeros(HEAD_PAD - MLA_NOPE)])[None, :],
        "out_w": out_w[j].astype(BF16),
    }


def _rope_tables(pos0, seq, rows):
    inv = 1.0 / (ROPE_BASE ** (jnp.arange(0, MLA_ROPE, 2, dtype=F32) / MLA_ROPE))
    ang = (pos0 + jnp.arange(seq)).astype(F32)[:, None] * inv[None, :]
    cos, sin = jnp.cos(ang), jnp.sin(ang)
    z = lambda n: jnp.zeros((seq, n), F32)
    cos_t = jnp.concatenate([jnp.ones((seq, MLA_NOPE), F32), cos, cos, z(HEAD_PAD - MLA_QK)], axis=1)
    sin_t = jnp.concatenate([z(MLA_NOPE), sin, sin, z(HEAD_PAD - MLA_QK)], axis=1)
    reps = max(1, rows // seq)
    return tuple(jnp.tile(t, (reps, 2)) for t in (cos_t, sin_t))


def _row_tile(m, want):
    tm = min(m, want)
    assert m % tm == 0
    return tm


def _ssd_layer(h, p, lnw, ple, w, batch, seq, chunk, conv_prev, h0):
    tm = _row_tile(h.shape[0], ROW_TILE)
    zs, xs, bt, c, dt, conv_new = _ssd_in(h, lnw, w, conv_prev, batch, seq, min(seq, SSD_BLOCK_ROWS))
    yn, ht = _ssd_scan(zs, xs, bt, c, dt, h0, w, batch, seq, chunk)
    h = _out_ple(yn, h, p, w["out_w"], *ple, tm)
    return h, conv_new, ht.reshape(batch, SSD_HEADS, SSD_HEAD_DIM, SSD_STATE)


def _mla_layer(h, p, lnw, ple, w, batch, seq, lat_past, kr_past):
    tm = _row_tile(h.shape[0], ROW_TILE)
    pos0 = 0 if lat_past is None else lat_past.shape[1]
    rope = _rope_tables(pos0, seq, tm)
    prompt = lat_past is None
    q, lat, kr, gate = _mla_in(h, lnw, w, rope, tm, q_transposed=prompt)
    if prompt:
        tile = min(seq, ATTN_TILE)
        k, v_t = _mla_kv(lat, kr, w, tile, v_transposed=True)
        og = _attn_prompt(q, k, v_t, gate, batch, seq, tile)
    else:
        n_keys = pos0 + seq
        tk = -(-n_keys // LANES) * LANES
        pad = ((0, 0), (0, tk - n_keys), (0, 0))
        lat_all = jnp.pad(jnp.concatenate([lat_past, lat.reshape(batch, seq, -1)], axis=1), pad)
        kr_all = jnp.pad(jnp.concatenate([kr_past, kr.reshape(batch, seq, -1)], axis=1), pad)
        k, v = _mla_kv(lat_all.reshape(batch * tk, -1), kr_all.reshape(batch * tk, -1), w,
                       math.gcd(batch * tk, 512), v_transposed=False)
        og = _attn_step(q, k.reshape(batch, tk, -1), v.reshape(batch, tk, -1), gate, batch, seq, n_keys)
    h = _out_ple(og, h, p, w["out_w"], *ple, tm)
    return h, lat.reshape(batch, seq, -1), kr.reshape(batch, seq, -1)


def kernel(x_prompt, x_sample, cache_conv, state_ssm, cache_kv_latent, cache_k_rope, p_prompt, p_sample, ln_w, ssd_in_w, ssd_conv_w, ssd_conv_b, ssd_dt_bias, ssd_A_log, ssd_D, ssd_norm_w, ssd_out_w, mla_in_w, mla_q_a_norm, mla_q_b_w, mla_kv_a_norm, mla_kv_b_w, mla_q_nope_norm, mla_q_rope_norm, mla_k_nope_norm, mla_k_rope_norm, mla_out_w, ple_up_w, ple_norm_w, ple_gate_w):
    bp, tp, d = x_prompt.shape
    bs, ts, _ = x_sample.shape
    hp = x_prompt.reshape(bp * tp, d)
    hs = x_sample.reshape(bs * ts, d)
    conv_p, ssm_p, lat_p, kr_p = [], [], [], []
    conv_s, ssm_s, lat_s, kr_s = [], [], [], []
    for i in range(DEPTH):
        j = i // 2
        lnw = ln_w[i][None, :]
        ple = (ple_norm_w[i][None, :], ple_gate_w[i].astype(BF16), ple_up_w[i].astype(BF16))
        pp = (p_prompt.reshape(DEPTH, bp * tp, PLE_DIM), i)
        ps = (p_sample.reshape(DEPTH, bs * ts, PLE_DIM), i)
        if i % 2 == 0:
            w = _ssd_weights(j, ssd_in_w, ssd_conv_w, ssd_conv_b, ssd_dt_bias, ssd_A_log, ssd_D,
                             ssd_norm_w, ssd_out_w)
            hp, cp, sp = _ssd_layer(hp, pp, lnw, ple, w, bp, tp, min(tp, SSD_SCAN_CHUNK), None, None)
            hs, cs, ss = _ssd_layer(hs, ps, lnw, ple, w, bs, ts, ts, cache_conv[j], state_ssm[j])
            conv_p.append(cp); ssm_p.append(sp); conv_s.append(cs); ssm_s.append(ss)
        else:
            w = _mla_weights(j, mla_in_w, mla_q_a_norm, mla_q_b_w, mla_kv_a_norm, mla_kv_b_w,
                             mla_q_nope_norm, mla_q_rope_norm, mla_k_nope_norm, mla_k_rope_norm, mla_out_w)
            hp, lp, rp = _mla_layer(hp, pp, lnw, ple, w, bp, tp, None, None)
            hs, ls, rs = _mla_layer(hs, ps, lnw, ple, w, bs, ts, cache_kv_latent[j], cache_k_rope[j])
            lat_p.append(lp); kr_p.append(rp); lat_s.append(ls); kr_s.append(rs)
    return (hp.reshape(bp, tp, d), hs.reshape(bs, ts, d),
            jnp.stack(conv_p), jnp.stack(ssm_p), jnp.stack(lat_p), jnp.stack(kr_p),
            jnp.stack(conv_s), jnp.stack(ssm_s), jnp.stack(lat_s), jnp.stack(kr_s))
```

```python
import functools
import math

import jax
import jax.numpy as jnp
from jax import lax
from jax.experimental import pallas as pl
from jax.experimental.pallas import tpu as pltpu

F32 = jnp.float32
BF16 = jnp.bfloat16

D_MODEL = 1024
DEPTH = 4
CHUNK = 64
PLE_DIM = 256
EPS = 1e-6

SSD_D_INNER = 2 * D_MODEL
SSD_HEAD_DIM = 64
SSD_HEADS = SSD_D_INNER // SSD_HEAD_DIM
SSD_GROUPS = 8
SSD_HPG = SSD_HEADS // SSD_GROUPS
SSD_STATE = 128
SSD_CONV_W = 4
SSD_BC = SSD_GROUPS * SSD_STATE
SSD_CONV_DIM = SSD_D_INNER + 2 * SSD_BC
SSD_GROUP_W = SSD_HPG * SSD_HEAD_DIM

MLA_HEADS = 16
MLA_NOPE = 64
MLA_ROPE = 32
MLA_V = 64
MLA_Q_LORA = 384
MLA_KV_LORA = 256
MLA_QK = MLA_NOPE + MLA_ROPE
MLA_WIDTH = MLA_HEADS * MLA_V
ROPE_BASE = 10000.0

LANES = 128
SUBLANES = 8
HEAD_PAD = LANES
VMEM_LIMIT_BYTES = 52 * 1024 * 1024
NEG_BIG = -1e30
SOFTMAX_LOG2_SCALE = (MLA_QK ** -0.5) * math.log2(math.e)
COL_CHUNK = 512
CONV_ROWS = 128
V_ONES_ROWS = 16
V_SLOT_ROWS = MLA_V + V_ONES_ROWS
ROW_TILE = 512
MLA_IN_TILE = 1024
ATTN_TILE = 256
SSD_BLOCK_ROWS = 256
SSD_SCAN_CHUNK = 128


def _cparams(*sem):
    return pltpu.CompilerParams(dimension_semantics=sem, vmem_limit_bytes=VMEM_LIMIT_BYTES)


def _const_spec(shape):
    nd = len(shape)
    return pl.BlockSpec(shape, lambda *_: (0,) * nd, pipeline_mode=pl.Buffered(1))


def _rms(x, w):
    return x * lax.rsqrt(jnp.mean(x * x, axis=-1, keepdims=True) + EPS) * w


def _dot(a, b):
    return jnp.dot(a, b, preferred_element_type=F32)


def _dot_nt(a, b):
    return lax.dot_general(a, b, (((1,), (1,)), ((), ())), preferred_element_type=F32)


def _store_dot(o_ref, a, w_ref, col0=0):
    width = o_ref.shape[1]
    for c in range(0, width, COL_CHUNK):
        cw = min(COL_CHUNK, width - c)
        o_ref[:, c:c + cw] = _dot(a, w_ref[:, col0 + c:col0 + c + cw]).astype(o_ref.dtype)


def _ssd_in_kernel(*refs, tiles_per_seq, has_init):
    if has_init:
        (x_ref, lnw_ref, w_ref, convw_ref, convb_ref, cprev_ref,
         z_ref, xs_ref, b_ref, c_ref, dt_ref, convnew_ref, ext) = refs
    else:
        (x_ref, lnw_ref, w_ref, convw_ref, convb_ref,
         z_ref, xs_ref, b_ref, c_ref, dt_ref, convnew_ref, ext) = refs
    tm = x_ref.shape[0]
    tail = SSD_CONV_W - 1
    t = pl.program_id(0) % tiles_per_seq
    xn = _rms(x_ref[...], lnw_ref[...]).astype(BF16)

    @pl.when(t == 0)
    def _init():
        ext[0:SUBLANES, :] = jnp.zeros((SUBLANES, SSD_CONV_DIM), F32)
        if has_init:
            ext[SUBLANES - tail:SUBLANES, :] = cprev_ref[0]

    dt_ref[...] = _dot(xn, w_ref[:, SSD_D_INNER + SSD_CONV_DIM:])
    def project(c0):
        cs = slice(c0, c0 + COL_CHUNK)
        ext[SUBLANES:SUBLANES + tm, cs] = _dot(xn, w_ref[:, SSD_D_INNER + c0:SSD_D_INNER + c0 + COL_CHUNK])
        convnew_ref[0, :, cs] = ext[pl.ds(SUBLANES + tm - tail, tail), cs]

    project(0)
    for c0 in range(0, SSD_CONV_DIM, COL_CHUNK):
        cs = slice(c0, c0 + COL_CHUNK)
        if c0 + COL_CHUNK < SSD_CONV_DIM:
            project(c0 + COL_CHUNK)
        if c0 % (2 * COL_CHUNK) == 0:
            zc = slice(c0 // 2, c0 // 2 + COL_CHUNK)
            hz = _dot(xn, w_ref[:, zc])
            z_ref[:, zc] = hz * jnp.tanh(hz) + hz
        if c0 < SSD_D_INNER:
            o_ref, o0 = xs_ref, c0
        elif c0 < SSD_D_INNER + SSD_BC:
            o_ref, o0 = b_ref, c0 - SSD_D_INNER
        else:
            o_ref, o0 = c_ref, c0 - SSD_D_INNER - SSD_BC
        for r0 in range(0, tm, CONV_ROWS):
            rb = min(CONV_ROWS, tm - r0)
            for l0 in range(0, COL_CHUNK, LANES):
                ls = slice(c0 + l0, c0 + l0 + LANES)
                e = ext[r0:r0 + rb + SUBLANES, ls]
                e1 = pltpu.roll(e, 1, 0)
                far = e * convw_ref[1:2, ls] + e1 * convw_ref[0:1, ls]
                near = e * convw_ref[3:4, ls] + e1 * convw_ref[2:3, ls]
                hv = (convb_ref[:, ls] + near[SUBLANES:SUBLANES + rb]) + pltpu.roll(far, 2, 0)[SUBLANES:SUBLANES + rb]
                y = hv * jnp.tanh(hv) + hv
                if o_ref is b_ref:
                    b_ref[0, o0 + l0:o0 + l0 + LANES, r0:r0 + rb] = y.T
                else:
                    o_ref[r0:r0 + rb, o0 + l0:o0 + l0 + LANES] = y.astype(o_ref.dtype)
    ext[0:SUBLANES, :] = ext[tm:tm + SUBLANES, :]


def _ssd_in(x, lnw, w, conv_prev, batch, seq, tm):
    m, d = x.shape
    assert seq % tm == 0 and tm % SUBLANES == 0 and m == batch * seq
    tps = seq // tm
    has_init = conv_prev is not None
    tail = SSD_CONV_W - 1
    consts = [lnw, w["in_w"], w["conv_w"], w["conv_b"]]
    row = lambda wd: pl.BlockSpec((tm, wd), lambda i: (i, 0))
    conv_spec = pl.BlockSpec((1, tail, SSD_CONV_DIM), lambda i: (i // tps, 0, 0))
    in_specs = [row(d)] + [_const_spec(a.shape) for a in consts]
    args = [x] + consts
    if has_init:
        in_specs.append(conv_spec)
        args.append(conv_prev)
    return pl.pallas_call(
        functools.partial(_ssd_in_kernel, tiles_per_seq=tps, has_init=has_init),
        grid=(m // tm,),
        in_specs=in_specs,
        out_specs=[row(SSD_D_INNER), row(SSD_D_INNER), pl.BlockSpec((1, SSD_BC, tm), lambda i: (i, 0, 0)),
                   row(SSD_BC), row(LANES), conv_spec],
        out_shape=[jax.ShapeDtypeStruct((m, SSD_D_INNER), F32),
                   jax.ShapeDtypeStruct((m, SSD_D_INNER), F32),
                   jax.ShapeDtypeStruct((m // tm, SSD_BC, tm), F32),
                   jax.ShapeDtypeStruct((m, SSD_BC), BF16),
                   jax.ShapeDtypeStruct((m, LANES), F32),
                   jax.ShapeDtypeStruct((batch, tail, SSD_CONV_DIM), F32)],
        scratch_shapes=[pltpu.VMEM((tm + SUBLANES, SSD_CONV_DIM), F32)],
        compiler_params=_cparams("arbitrary"),
        name="ssd_in",
    )(*args)


def _dot_exact_lhs(a, x):
    x1 = x.astype(BF16)
    r1 = x - x1.astype(F32)
    x2 = r1.astype(BF16)
    x3 = (r1 - x2.astype(F32)).astype(BF16)
    return _dot(a, x1) + _dot(a, x2) + _dot(a, x3)


def _expand_heads(cols, g, lane_head):
    shape = (cols.shape[0], SSD_GROUP_W)
    h0 = SSD_HPG * g
    out = jnp.broadcast_to(cols[:, h0 + SSD_HPG - 1:h0 + SSD_HPG], shape)
    for r in range(SSD_HPG - 2, -1, -1):
        out = jnp.where(lane_head[:shape[0]] == r, jnp.broadcast_to(cols[:, h0 + r:h0 + r + 1], shape), out)
    return out


def _select_heads(parts, lane_head):
    out = parts[0]
    for r in range(1, len(parts)):
        out = jnp.where(lane_head == r, parts[r], out)
    return out


def _ssd_kernel(*refs, L, has_init):
    if has_init:
        (zs_ref, xs_ref, bt_ref, c_ref, dt_ref, h0_ref, dtb_ref, alog_ref,
         dexp_ref, normw_ref, yn_ref, ht_ref, st, cb_s) = refs
    else:
        (zs_ref, xs_ref, bt_ref, c_ref, dt_ref, dtb_ref, alog_ref,
         dexp_ref, normw_ref, yn_ref, ht_ref, st, cb_s) = refs
    c = pl.program_id(1)
    last = pl.num_programs(1) - 1

    @pl.when(c == 0)
    def _init():
        if has_init:
            for g in range(SSD_GROUPS):
                st[g] = h0_ref[0, g * SSD_GROUP_W:(g + 1) * SSD_GROUP_W, :].T
        else:
            st[...] = jnp.zeros(st.shape, F32)

    rows = dt_ref.shape[0]
    n_chunks = rows // L
    fused = L % LANES == 0
    group_n = lambda g: slice(g * SSD_STATE, (g + 1) * SSD_STATE)

    for ci in range(n_chunks):
        rs = slice(ci * L, (ci + 1) * L)
        for g in range(SSD_GROUPS):
            cb_s[ci * SSD_GROUPS + g] = _dot(c_ref[rs, group_n(g)], bt_ref[0, group_n(g), rs].astype(BF16))

    dt = jax.nn.softplus(dt_ref[...] + dtb_ref[...])
    dta = dt * (-jnp.exp(alog_ref[...]))
    row = lax.broadcasted_iota(jnp.int32, (rows, rows), 0)
    col = lax.broadcasted_iota(jnp.int32, (rows, rows), 1)
    same_chunk_tri = jnp.logical_and(row >= col, row // L == col // L)
    acs = _dot_exact_lhs(same_chunk_tri.astype(BF16), dta)
    acs_t = acs.T
    dt_t = dt.T
    tri = lax.broadcasted_iota(jnp.int32, (L, L), 0) >= lax.broadcasted_iota(jnp.int32, (L, L), 1)
    lane_head = lax.broadcasted_iota(jnp.int32, (L, SSD_GROUP_W), 1) // SSD_HEAD_DIM
    lane_head_n = lax.broadcasted_iota(jnp.int32, (SSD_STATE, SSD_GROUP_W), 1) // SSD_HEAD_DIM

    for ci in range(n_chunks):
        rs = slice(ci * L, (ci + 1) * L)
        acs_c = acs[rs]
        acs_tc = acs_t[:, rs]
        dt_tc = dt_t[:, rs]
        cdec = jnp.exp(acs_c[L - 1:L, :])
        wdt_t = jnp.exp(acs_tc[:, L - 1:L] - acs_tc) * dt_tc
        for g in range(SSD_GROUPS):
            gs = slice(g * SSD_GROUP_W, (g + 1) * SSD_GROUP_W)
            xg = xs_ref[rs, gs]
            xgb = xg.astype(BF16)
            cg = c_ref[rs, group_n(g)].astype(F32)
            cb = cb_s[ci * SSD_GROUPS + g]
            sg = st[g]
            sgb = sg.astype(BF16)
            if fused:
                rhs = jnp.concatenate([xgb, sgb], axis=0)
            parts = []
            for r in range(SSD_HPG):
                h = SSD_HPG * g + r
                a_col = jnp.broadcast_to(acs_c[:, h:h + 1], (L, LANES))
                seg = a_col[:, :L] - acs_tc[h:h + 1, :]
                m = (cb * jnp.exp(jnp.where(tri, seg, -jnp.inf)) * dt_tc[h:h + 1, :]).astype(BF16)
                ce = (cg * jnp.exp(a_col)).astype(BF16)
                if fused:
                    parts.append(_dot(jnp.concatenate([m, ce], axis=1), rhs))
                else:
                    parts.append(_dot(m, xgb) + _dot(ce, sgb))
            y = _select_heads(parts, lane_head)
            y = y + dexp_ref[:, gs] * xg
            y = y * zs_ref[rs, gs]
            yn_ref[rs, gs] = _rms(y, normw_ref[:, gs]).astype(yn_ref.dtype)
            btg = bt_ref[0, group_n(g), rs]
            upd = [_dot((btg * wdt_t[SSD_HPG * g + r:SSD_HPG * g + r + 1, :]).astype(BF16), xgb)
                   for r in range(SSD_HPG)]
            st[g] = sg * _expand_heads(cdec, g, lane_head) + _select_heads(upd, lane_head_n)

    @pl.when(c == last)
    def _state_out():
        for g in range(SSD_GROUPS):
            ht_ref[0, g * SSD_GROUP_W:(g + 1) * SSD_GROUP_W, :] = st[g].T


def _ssd_scan(zs, xs, bt, c, dt, h0, w, batch, seq, L):
    rows = bt.shape[2]
    nc = seq // rows
    assert nc * rows == seq and rows % L == 0 and L % SUBLANES == 0
    has_init = h0 is not None
    row_spec = lambda wd: pl.BlockSpec((rows, wd), lambda bi, ci: (bi * nc + ci, 0))
    state_spec = pl.BlockSpec((1, SSD_D_INNER, SSD_STATE), lambda bi, ci: (bi, 0, 0))
    bt_spec = pl.BlockSpec((1, SSD_BC, rows), lambda bi, ci: (bi * nc + ci, 0, 0))
    in_specs = [row_spec(SSD_D_INNER), row_spec(SSD_D_INNER), bt_spec, row_spec(SSD_BC), row_spec(LANES)]
    args = [zs, xs, bt, c, dt]
    if has_init:
        in_specs.append(state_spec)
        args.append(h0.reshape(batch, SSD_D_INNER, SSD_STATE))
    consts = [w["dt_bias"], w["a_log"], w["d_exp"], w["norm_w"]]
    in_specs += [_const_spec(a.shape) for a in consts]
    args += consts
    return pl.pallas_call(
        functools.partial(_ssd_kernel, L=L, has_init=has_init),
        grid=(batch, nc),
        in_specs=in_specs,
        out_specs=[row_spec(SSD_D_INNER), state_spec],
        out_shape=[jax.ShapeDtypeStruct((batch * seq, SSD_D_INNER), BF16),
                   jax.ShapeDtypeStruct((batch, SSD_D_INNER, SSD_STATE), F32)],
        scratch_shapes=[pltpu.VMEM((SSD_GROUPS, SSD_STATE, SSD_GROUP_W), F32),
                        pltpu.VMEM((rows // L * SSD_GROUPS, L, L), F32)],
        compiler_params=_cparams("arbitrary", "arbitrary"),
        name="ssd_scan",
    )(*args)


def _out_ple_kernel(y_ref, h_ref, p_ref, wout_ref, plenorm_ref, wgate_ref, wup_ref, o_ref):
    h1 = h_ref[...] + _dot(y_ref[...], wout_ref[...])
    hn = _rms(h1, plenorm_ref[...]).astype(BF16)
    pb = p_ref[0].astype(BF16)
    for c in range(0, D_MODEL, COL_CHUNK):
        cs = slice(c, c + COL_CHUNK)
        gate = jax.nn.sigmoid(_dot(hn, wgate_ref[:, cs]))
        o_ref[:, cs] = h1[:, cs] + _dot(pb, wup_ref[:, cs]) * gate


def _out_ple(y, h, p, wout, plenorm, wgate, wup, tm):
    m, kd = y.shape
    assert m % tm == 0
    p_all, layer = p
    return pl.pallas_call(
        _out_ple_kernel,
        grid=(m // tm,),
        in_specs=[pl.BlockSpec((tm, kd), lambda i: (i, 0)),
                  pl.BlockSpec((tm, D_MODEL), lambda i: (i, 0)),
                  pl.BlockSpec((1, tm, PLE_DIM), lambda i: (layer, i, 0)),
                  _const_spec(wout.shape), _const_spec(plenorm.shape),
                  _const_spec(wgate.shape), _const_spec(wup.shape)],
        out_specs=pl.BlockSpec((tm, D_MODEL), lambda i: (i, 0)),
        out_shape=jax.ShapeDtypeStruct((m, D_MODEL), F32),
        compiler_params=_cparams("arbitrary"),
        name="out_ple",
    )(y, h, p_all, wout, plenorm, wgate, wup)


def _dot_hi_lo_rhs(x, a):
    hi = x.astype(BF16)
    lo = (x - hi.astype(F32)).astype(BF16)
    return _dot(hi, a) + _dot(lo, a)


def _head_part_matrix(width):
    def part(idx):
        within = idx % HEAD_PAD
        return (idx // HEAD_PAD) * 4 + jnp.where(within < MLA_NOPE, 0, jnp.where(within < MLA_QK, 1, 2))
    rows = part(lax.broadcasted_iota(jnp.int32, (width, width), 0))
    cols_i = lax.broadcasted_iota(jnp.int32, (width, width), 1)
    cols = part(cols_i)
    same = jnp.logical_and(rows == cols, cols_i % HEAD_PAD < MLA_QK)
    lane = lax.broadcasted_iota(jnp.int32, (1, width), 1) % HEAD_PAD
    inv_count = jnp.where(lane < MLA_NOPE, 1.0 / MLA_NOPE, jnp.where(lane < MLA_QK, 1.0 / MLA_ROPE, 0.0))
    return jnp.where(same, 1.0, 0.0).astype(BF16), inv_count


def _head_norm_rope(x, pmat, inv_count, gain_cos, gain_sin):
    inv = lax.rsqrt(_dot_hi_lo_rhs(x * x, pmat) * inv_count + EPS)
    shift = HEAD_PAD - MLA_ROPE
    x_rot = jnp.concatenate([pltpu.roll(x[:, s0:s0 + HEAD_PAD], shift, 1)
                             for s0 in range(0, x.shape[1], HEAD_PAD)], axis=1)
    return (x * inv) * gain_cos + (x_rot * inv) * gain_sin


def _mla_in_kernel(h_ref, lnw_ref, win_ref, qan_ref, wqb_ref, kvan_ref, gq_ref, gqrot_ref, gkr_ref, gkrrot_ref,
                   cos_ref, sin_ref, *rest, q_transposed):
    q_ref, lat_ref, kr_ref, gate_ref = rest[-4:]
    xn = _rms(h_ref[...], lnw_ref[...]).astype(BF16)
    c_q, c_kv, c_kr = 0, MLA_Q_LORA, MLA_Q_LORA + MLA_KV_LORA
    c_gate = c_kr + HEAD_PAD
    pair_w = 2 * HEAD_PAD
    _store_dot(gate_ref, xn, win_ref, c_gate)
    lat_ref[0] = _rms(_dot(xn, win_ref[:, c_kv:c_kr]), kvan_ref[...])
    pmat, inv_count = _head_part_matrix(pair_w)
    cos_t, sin_t = cos_ref[...], sin_ref[...]
    kr = _head_norm_rope(_dot(xn, win_ref[:, c_kr:c_gate]), pmat[:HEAD_PAD, :HEAD_PAD],
                         inv_count[:, :HEAD_PAD], gkr_ref[...] * cos_t[:, :HEAD_PAD],
                         gkrrot_ref[...] * sin_t[:, :HEAD_PAD])
    kr_ref[...] = kr[:, MLA_NOPE:MLA_QK]
    qa = _rms(_dot(xn, win_ref[:, c_q:c_kv]), qan_ref[...]).astype(BF16)
    gain_cos = gq_ref[...] * cos_t
    gain_sin = gqrot_ref[...] * sin_t
    for pair in range(MLA_HEADS // 2):
        qq = _dot(qa, wqb_ref[:, pair * pair_w:(pair + 1) * pair_w])
        qh = _head_norm_rope(qq, pmat, inv_count, gain_cos, gain_sin)
        if q_transposed:
            q_ref[0, pair * pair_w:(pair + 1) * pair_w, :] = qh.T.astype(q_ref.dtype)
        else:
            q_ref[:, pair * pair_w:(pair + 1) * pair_w] = qh.astype(q_ref.dtype)


def _stacked_out(prev, layer, count, block, index_tail, shape_tail, dtype):
    spec = pl.BlockSpec((1,) + block, lambda *g: (layer,) + index_tail(*g))
    shape = jax.ShapeDtypeStruct((count,) + shape_tail, dtype)
    if prev is None:
        return [], [], spec, shape
    return [pl.BlockSpec(memory_space=pl.ANY)], [prev], spec, shape


def _mla_in(h, lnw, w, rope, tm, q_transposed, lat_stack):
    m = h.shape[0]
    assert m % tm == 0
    cos_t, sin_t = rope
    period = cos_t.shape[0] // tm
    assert period * tm == cos_t.shape[0]
    rope_spec = pl.BlockSpec((tm, 2 * HEAD_PAD), lambda i: (i % period, 0))
    consts = [lnw, w["in_w"], w["q_a_norm"], w["q_b_w"], w["kv_a_norm"],
              w["gain_q"], w["gain_q_rot"], w["gain_kr"], w["gain_kr_rot"]]
    row = lambda wd: pl.BlockSpec((tm, wd), lambda i: (i, 0))
    qw = MLA_HEADS * HEAD_PAD
    if q_transposed:
        seq = period * tm
        assert m % seq == 0
        q_spec = pl.BlockSpec((1, qw, tm), lambda i: (i // period, 0, i % period))
        q_shape = jax.ShapeDtypeStruct((m // seq, qw, seq), BF16)
    else:
        q_spec, q_shape = row(qw), jax.ShapeDtypeStruct((m, qw), BF16)
    prev, layer, count = lat_stack
    extra_specs, extra_args, lat_spec, lat_shape = _stacked_out(
        prev, layer, count, (tm, MLA_KV_LORA), lambda i: (i, 0), (m, MLA_KV_LORA), F32)
    n_in = 1 + len(consts) + 2
    return pl.pallas_call(
        functools.partial(_mla_in_kernel, q_transposed=q_transposed),
        grid=(m // tm,),
        in_specs=[row(D_MODEL)] + [_const_spec(a.shape) for a in consts] + [rope_spec] * 2 + extra_specs,
        out_specs=[q_spec, lat_spec, row(MLA_ROPE), row(MLA_WIDTH)],
        input_output_aliases={n_in: 1} if extra_args else {},
        out_shape=[q_shape,
                   lat_shape,
                   jax.ShapeDtypeStruct((m, MLA_ROPE), F32),
                   jax.ShapeDtypeStruct((m, MLA_WIDTH), F32)],
        compiler_params=_cparams("arbitrary"),
        name="mla_in",
    )(h, *consts, cos_t, sin_t, *extra_args)


def _mla_kv_kernel(lat_ref, kr_ref, wk_ref, wv_ref, gkn_ref, k_ref, v_ref, *, v_transposed):
    latb = lat_ref[0].astype(BF16)
    if v_transposed:
        ones = jnp.ones((V_ONES_ROWS, latb.shape[0]), v_ref.dtype)
        for c in range(0, MLA_WIDTH, COL_CHUNK):
            vt = _dot(latb, wv_ref[:, c:c + COL_CHUNK]).T.astype(v_ref.dtype)
            for hh in range(COL_CHUNK // MLA_V):
                r0 = (c // MLA_V + hh) * V_SLOT_ROWS
                v_ref[0, r0:r0 + MLA_V, :] = vt[hh * MLA_V:(hh + 1) * MLA_V]
                v_ref[0, r0 + MLA_V:r0 + V_SLOT_ROWS, :] = ones
    else:
        _store_dot(v_ref, latb, wv_ref)
    rows = latb.shape[0]
    kr = jnp.concatenate([jnp.zeros((rows, MLA_NOPE), F32), kr_ref[...],
                          jnp.zeros((rows, HEAD_PAD - MLA_QK), F32)], axis=1)
    for h in range(MLA_HEADS):
        hs = slice(h * HEAD_PAD, (h + 1) * HEAD_PAD)
        kh = _dot(latb, wk_ref[:, hs])
        ms = jnp.sum(kh * kh, axis=-1, keepdims=True) * (1.0 / MLA_NOPE)
        k_ref[:, hs] = (kh * lax.rsqrt(ms + EPS) * gkn_ref[...] + kr).astype(k_ref.dtype)


def _mla_kv(lat, layer, kr, w, tm, v_transposed):
    m = lat.shape[1]
    assert m % tm == 0
    consts = [w["kv_k_w"], w["kv_v_w"], w["gain_kn"]]
    row = lambda wd: pl.BlockSpec((tm, wd), lambda i: (i, 0))
    if v_transposed:
        v_spec = pl.BlockSpec((1, MLA_HEADS * V_SLOT_ROWS, tm), lambda i: (i, 0, 0))
        v_shape = jax.ShapeDtypeStruct((m // tm, MLA_HEADS * V_SLOT_ROWS, tm), BF16)
    else:
        v_spec, v_shape = row(MLA_WIDTH), jax.ShapeDtypeStruct((m, MLA_WIDTH), BF16)
    return pl.pallas_call(
        functools.partial(_mla_kv_kernel, v_transposed=v_transposed),
        grid=(m // tm,),
        in_specs=[pl.BlockSpec((1, tm, MLA_KV_LORA), lambda i: (layer, i, 0)), row(MLA_ROPE)]
                 + [_const_spec(a.shape) for a in consts],
        out_specs=[row(MLA_HEADS * HEAD_PAD), v_spec],
        out_shape=[jax.ShapeDtypeStruct((m, MLA_HEADS * HEAD_PAD), BF16), v_shape],
        compiler_params=_cparams("arbitrary"),
        name="mla_kv",
    )(lat, kr, *consts)


def _gated_pair_out(o_ref, gate_ref, outs, pair):
    lane = lax.broadcasted_iota(jnp.int32, outs[0].shape, 1)
    ps = slice(pair * LANES, (pair + 1) * LANES)
    g = gate_ref[:, ps]
    o_ref[:, ps] = (jnp.where(lane < MLA_V, outs[0], outs[1]) * (g * jax.nn.sigmoid(g))).astype(o_ref.dtype)


def _attn_kernel(q_ref, k_ref, v_ref, gate_ref, o_ref, m_s, acc_s, s_s, *, tq):
    i = pl.program_id(1)
    key_c = lax.broadcasted_iota(jnp.int32, (tq, tq), 0) // CHUNK
    qry_c = lax.broadcasted_iota(jnp.int32, (tq, tq), 1) // CHUNK
    diag_mask = key_c <= qry_c
    m_s[...] = jnp.full(m_s.shape, NEG_BIG, F32)
    acc_s[...] = jnp.zeros(acc_s.shape, F32)

    def tile_step(j, mask):
        rows = pl.ds(pl.multiple_of(j * tq, tq), tq)
        for h in range(MLA_HEADS):
            hs = slice(h * HEAD_PAD, (h + 1) * HEAD_PAD)
            s_s[h] = _dot(k_ref[rows, hs], q_ref[0, hs, :])
        for h in range(MLA_HEADS):
            alphas, probs = [], []
            for q0 in range(0, tq, LANES):
                qs = slice(q0, q0 + LANES)
                s = s_s[h, :, qs]
                if mask is not None:
                    s = jnp.where(mask[:, qs], s, -jnp.inf)
                m_prev = m_s[h, :, qs]
                m_new = jnp.maximum(m_prev, jnp.max(s, axis=0, keepdims=True))
                alphas.append(jnp.exp2((m_prev[0:1, :] - m_new[0:1, :]) * SOFTMAX_LOG2_SCALE))
                probs.append(jnp.exp2((s - m_new[0:1, :]) * SOFTMAX_LOG2_SCALE).astype(BF16))
                m_s[h, :, qs] = m_new
            p = jnp.concatenate(probs, axis=1)
            pv = _dot(v_ref[j, h * V_SLOT_ROWS:(h + 1) * V_SLOT_ROWS, :], p)
            acc_s[h] = jnp.concatenate(alphas, axis=1) * acc_s[h] + pv

    def body(j, carry):
        tile_step(j, None)
        return carry

    lax.fori_loop(0, i, body, 0)
    tile_step(i, diag_mask)
    for pair in range(MLA_HEADS // 2):
        o_t = jnp.concatenate([acc_s[2 * pair + sub, 0:MLA_V, :] / acc_s[2 * pair + sub, MLA_V:MLA_V + 1, :]
                               for sub in range(2)], axis=0)
        ps = slice(pair * LANES, (pair + 1) * LANES)
        g = gate_ref[:, ps]
        o_ref[:, ps] = (o_t.T * (g * jax.nn.sigmoid(g))).astype(o_ref.dtype)


def _attn_prompt(q_t, k, v_t, gate, batch, seq, tq):
    nq = seq // tq
    assert nq * tq == seq and tq % CHUNK == 0
    return pl.pallas_call(
        functools.partial(_attn_kernel, tq=tq),
        grid=(batch, nq),
        in_specs=[pl.BlockSpec((1, MLA_HEADS * HEAD_PAD, tq), lambda b, i: (b, 0, i)),
                  pl.BlockSpec((seq, MLA_HEADS * HEAD_PAD), lambda b, i: (b, 0)),
                  pl.BlockSpec((nq, MLA_HEADS * V_SLOT_ROWS, tq), lambda b, i: (b, 0, 0)),
                  pl.BlockSpec((tq, MLA_WIDTH), lambda b, i: (b * nq + i, 0))],
        out_specs=pl.BlockSpec((tq, MLA_WIDTH), lambda b, i: (b * nq + i, 0)),
        out_shape=jax.ShapeDtypeStruct((batch * seq, MLA_WIDTH), BF16),
        scratch_shapes=[pltpu.VMEM((MLA_HEADS, SUBLANES, tq), F32),
                        pltpu.VMEM((MLA_HEADS, V_SLOT_ROWS, tq), F32), pltpu.VMEM((MLA_HEADS, tq, tq), F32)],
        compiler_params=_cparams("arbitrary", "arbitrary"),
        name="attn_prompt",
    )(q_t, k, v_t, gate)


def _attn_step_kernel(q_ref, k_ref, v_ref, gate_ref, o_ref, s_s, *, q_pos0, n_keys):
    tq = q_ref.shape[0]
    tk = k_ref.shape[1]
    k_pos = lax.broadcasted_iota(jnp.int32, (tq, tk), 1)
    q_pos = lax.broadcasted_iota(jnp.int32, (tq, tk), 0) + q_pos0
    mask = jnp.logical_and(k_pos < n_keys, k_pos // CHUNK <= q_pos // CHUNK)
    for h in range(MLA_HEADS):
        hs = slice(h * HEAD_PAD, (h + 1) * HEAD_PAD)
        s_s[h] = _dot_nt(q_ref[:, hs], k_ref[0, :, hs])
    for pair in range(MLA_HEADS // 2):
        vs = slice(pair * LANES, (pair + 1) * LANES)
        outs = []
        for sub in range(2):
            s = jnp.where(mask, s_s[2 * pair + sub], -jnp.inf)
            p = jnp.exp2((s - jnp.max(s, axis=-1, keepdims=True)) * SOFTMAX_LOG2_SCALE)
            outs.append(_dot(p.astype(BF16), v_ref[0, :, vs]) / jnp.sum(p, axis=-1, keepdims=True))
        _gated_pair_out(o_ref, gate_ref, outs, pair)


def _attn_step(q, k, v, gate, batch, tq, n_keys):
    tk = k.shape[1]
    return pl.pallas_call(
        functools.partial(_attn_step_kernel, q_pos0=n_keys - tq, n_keys=n_keys),
        grid=(batch,),
        in_specs=[pl.BlockSpec((tq, MLA_HEADS * HEAD_PAD), lambda b: (b, 0)),
                  pl.BlockSpec((1, tk, MLA_HEADS * HEAD_PAD), lambda b: (b, 0, 0)),
                  pl.BlockSpec((1, tk, MLA_WIDTH), lambda b: (b, 0, 0)),
                  pl.BlockSpec((tq, MLA_WIDTH), lambda b: (b, 0))],
        out_specs=pl.BlockSpec((tq, MLA_WIDTH), lambda b: (b, 0)),
        out_shape=jax.ShapeDtypeStruct((batch * tq, MLA_WIDTH), BF16),
        scratch_shapes=[pltpu.VMEM((MLA_HEADS, tq, tk), F32)],
        compiler_params=_cparams("arbitrary"),
        name="attn_step",
    )(q, k, v, gate)


def _pad_cols(a, width):
    return jnp.pad(a, ((0, 0), (0, width - a.shape[1])))


def _ssd_weights(j, in_w, conv_w, conv_b, dt_bias, a_log, d, norm_w, out_w):
    zx = SSD_D_INNER + SSD_CONV_DIM
    return {
        "in_w": jnp.concatenate([0.5 * in_w[j][:, :SSD_D_INNER], in_w[j][:, SSD_D_INNER:zx],
                                 _pad_cols(in_w[j][:, zx:], LANES)], axis=1).astype(BF16),
        "conv_w": 0.5 * conv_w[j],
        "conv_b": 0.5 * conv_b[j][None, :],
        "dt_bias": _pad_cols(dt_bias[j][None, :], LANES),
        "a_log": _pad_cols(a_log[j][None, :], LANES),
        "d_exp": jnp.repeat(d[j], SSD_HEAD_DIM)[None, :],
        "norm_w": norm_w[j][None, :],
        "out_w": out_w[j].astype(BF16),
    }


def _head_pad_cols(w, per_head, take, offset=0):
    k = w.shape[0]
    w = w.reshape(k, MLA_HEADS, per_head)[:, :, offset:offset + take]
    return jnp.pad(w, ((0, 0), (0, 0), (0, HEAD_PAD - take))).reshape(k, MLA_HEADS * HEAD_PAD)


def _rotate_half(a):
    half = MLA_ROPE // 2
    return jnp.concatenate([-a[..., half:], a[..., :half]], axis=-1)


def _swap_halves(a):
    half = MLA_ROPE // 2
    return jnp.concatenate([a[..., half:], a[..., :half]], axis=-1)


def _rope_slot(a):
    pad = [(0, 0)] * (a.ndim - 1) + [(MLA_NOPE, HEAD_PAD - MLA_QK)]
    return jnp.pad(a, pad)


def _q_b_cols(w):
    k = w.shape[0]
    w = w.reshape(k, MLA_HEADS, MLA_QK)
    return jnp.concatenate([w, _rotate_half(w[:, :, MLA_NOPE:])], axis=-1).reshape(k, MLA_HEADS * HEAD_PAD)


def _mla_weights(j, in_w, q_a_norm, q_b_w, kv_a_norm, kv_b_w, qn, qr, kn, kr, out_w):
    c1, c2, c3 = MLA_Q_LORA, MLA_Q_LORA + MLA_KV_LORA, MLA_Q_LORA + MLA_KV_LORA + MLA_ROPE
    w = in_w[j]
    kr_cols = jnp.concatenate([jnp.zeros((w.shape[0], MLA_NOPE), F32), w[:, c2:c3],
                               _rotate_half(w[:, c2:c3])], axis=1)
    zeros = lambda n: jnp.zeros((n,), F32)
    gain_q = jnp.concatenate([qn[j], qr[j], zeros(HEAD_PAD - MLA_QK)])
    return {
        "in_w": jnp.concatenate([w[:, :c2], kr_cols, w[:, c3:]], axis=1).astype(BF16),
        "q_a_norm": q_a_norm[j][None, :],
        "q_b_w": _q_b_cols(q_b_w[j]).astype(BF16),
        "gain_q": jnp.tile(gain_q, 2)[None, :],
        "gain_q_rot": jnp.tile(_rope_slot(_swap_halves(qr[j])), 2)[None, :],
        "gain_kr_rot": _rope_slot(_swap_halves(kr[j]))[None, :],
        "kv_a_norm": kv_a_norm[j][None, :],
        "kv_k_w": _head_pad_cols(kv_b_w[j], MLA_NOPE + MLA_V, MLA_NOPE).astype(BF16),
        "kv_v_w": kv_b_w[j].reshape(MLA_KV_LORA, MLA_HEADS, MLA_NOPE + MLA_V)[:, :, MLA_NOPE:]
                  .reshape(MLA_KV_LORA, MLA_WIDTH).astype(BF16),
        "gain_kr": _rope_slot(kr[j])[None, :],
        "gain_kn": jnp.concatenate([kn[j], zeros(HEAD_PAD - MLA_NOPE)])[None, :],
        "out_w": out_w[j].astype(BF16),
    }


def _rope_tables(pos0, seq, rows):
    inv = 1.0 / (ROPE_BASE ** (jnp.arange(0, MLA_ROPE, 2, dtype=F32) / MLA_ROPE))
    ang = (pos0 + jnp.arange(seq)).astype(F32)[:, None] * inv[None, :]
    cos, sin = jnp.cos(ang), jnp.sin(ang)
    z = lambda n: jnp.zeros((seq, n), F32)
    cos_t = jnp.concatenate([jnp.ones((seq, MLA_NOPE), F32), cos, cos, z(HEAD_PAD - MLA_QK)], axis=1)
    sin_t = jnp.concatenate([z(MLA_NOPE), sin, sin, z(HEAD_PAD - MLA_QK)], axis=1)
    reps = max(1, rows // seq)
    return tuple(jnp.tile(t, (reps, 2)) for t in (cos_t, sin_t))


def _row_tile(m, want):
    tm = min(m, want)
    assert m % tm == 0
    return tm


def _ssd_layer(h, p, lnw, ple, w, batch, seq, chunk, conv_prev, h0):
    tm = _row_tile(h.shape[0], ROW_TILE)
    zs, xs, bt, c, dt, conv_new = _ssd_in(h, lnw, w, conv_prev, batch, seq, min(seq, SSD_BLOCK_ROWS))
    yn, ht = _ssd_scan(zs, xs, bt, c, dt, h0, w, batch, seq, chunk)
    h = _out_ple(yn, h, p, w["out_w"], *ple, tm)
    return h, conv_new, ht.reshape(batch, SSD_HEADS, SSD_HEAD_DIM, SSD_STATE)


def _mla_layer(h, p, lnw, ple, w, batch, seq, lat_past, kr_past, lat_stack):
    tm = _row_tile(h.shape[0], ROW_TILE)
    prompt = lat_past is None
    tm_in = _row_tile(h.shape[0], min(seq, MLA_IN_TILE) if prompt else MLA_IN_TILE)
    pos0 = 0 if prompt else lat_past.shape[1]
    rope = _rope_tables(pos0, seq, tm_in)
    q, lats, kr, gate = _mla_in(h, lnw, w, rope, tm_in, q_transposed=prompt, lat_stack=lat_stack)
    layer = lat_stack[1]
    if prompt:
        tile = min(seq, ATTN_TILE)
        k, v_t = _mla_kv(lats, layer, kr, w, tile, v_transposed=True)
        og = _attn_prompt(q, k, v_t, gate, batch, seq, tile)
    else:
        n_keys = pos0 + seq
        tk = -(-n_keys // LANES) * LANES
        pad = ((0, 0), (0, tk - n_keys), (0, 0))
        lat_all = jnp.pad(jnp.concatenate([lat_past, lats[layer].reshape(batch, seq, -1)], axis=1), pad)
        kr_all = jnp.pad(jnp.concatenate([kr_past, kr.reshape(batch, seq, -1)], axis=1), pad)
        k, v = _mla_kv(lat_all.reshape(1, batch * tk, -1), 0, kr_all.reshape(batch * tk, -1), w,
                       math.gcd(batch * tk, 512), v_transposed=False)
        og = _attn_step(q, k.reshape(batch, tk, -1), v.reshape(batch, tk, -1), gate, batch, seq, n_keys)
    h = _out_ple(og, h, p, w["out_w"], *ple, tm)
    return h, lats, kr.reshape(batch, seq, -1)


def kernel(x_prompt, x_sample, cache_conv, state_ssm, cache_kv_latent, cache_k_rope, p_prompt, p_sample, ln_w, ssd_in_w, ssd_conv_w, ssd_conv_b, ssd_dt_bias, ssd_A_log, ssd_D, ssd_norm_w, ssd_out_w, mla_in_w, mla_q_a_norm, mla_q_b_w, mla_kv_a_norm, mla_kv_b_w, mla_q_nope_norm, mla_q_rope_norm, mla_k_nope_norm, mla_k_rope_norm, mla_out_w, ple_up_w, ple_norm_w, ple_gate_w):
    bp, tp, d = x_prompt.shape
    bs, ts, _ = x_sample.shape
    hp = x_prompt.reshape(bp * tp, d)
    hs = x_sample.reshape(bs * ts, d)
    conv_p, ssm_p, kr_p = [], [], []
    conv_s, ssm_s, kr_s = [], [], []
    lat_p = lat_s = None
    for i in range(DEPTH):
        j = i // 2
        lnw = ln_w[i][None, :]
        ple = (ple_norm_w[i][None, :], ple_gate_w[i].astype(BF16), ple_up_w[i].astype(BF16))
        pp = (p_prompt.reshape(DEPTH, bp * tp, PLE_DIM), i)
        ps = (p_sample.reshape(DEPTH, bs * ts, PLE_DIM), i)
        if i % 2 == 0:
            w = _ssd_weights(j, ssd_in_w, ssd_conv_w, ssd_conv_b, ssd_dt_bias, ssd_A_log, ssd_D,
                             ssd_norm_w, ssd_out_w)
            hp, cp, sp = _ssd_layer(hp, pp, lnw, ple, w, bp, tp, min(tp, SSD_SCAN_CHUNK), None, None)
            hs, cs, ss = _ssd_layer(hs, ps, lnw, ple, w, bs, ts, ts, cache_conv[j], state_ssm[j])
            conv_p.append(cp); ssm_p.append(sp); conv_s.append(cs); ssm_s.append(ss)
        else:
            w = _mla_weights(j, mla_in_w, mla_q_a_norm, mla_q_b_w, mla_kv_a_norm, mla_kv_b_w,
                             mla_q_nope_norm, mla_q_rope_norm, mla_k_nope_norm, mla_k_rope_norm, mla_out_w)
            n_mla = DEPTH // 2
            hp, lat_p, rp = _mla_layer(hp, pp, lnw, ple, w, bp, tp, None, None, (lat_p, j, n_mla))
            hs, lat_s, rs = _mla_layer(hs, ps, lnw, ple, w, bs, ts, cache_kv_latent[j], cache_k_rope[j],
                                       (lat_s, j, n_mla))
            kr_p.append(rp); kr_s.append(rs)
    return (hp.reshape(bp, tp, d), hs.reshape(bs, ts, d),
            jnp.stack(conv_p), jnp.stack(ssm_p), lat_p.reshape(-1, bp, tp, MLA_KV_LORA), jnp.stack(kr_p),
            jnp.stack(conv_s), jnp.stack(ssm_s), lat_s.reshape(-1, bs, ts, MLA_KV_LORA), jnp.stack(kr_s))
```

```python
import functools
import math

import jax
import jax.numpy as jnp
from jax import lax
from jax.experimental import pallas as pl
from jax.experimental.pallas import tpu as pltpu

F32 = jnp.float32
BF16 = jnp.bfloat16

D_MODEL = 1024
DEPTH = 4
CHUNK = 64
PLE_DIM = 256
EPS = 1e-6

SSD_D_INNER = 2 * D_MODEL
SSD_HEAD_DIM = 64
SSD_HEADS = SSD_D_INNER // SSD_HEAD_DIM
SSD_GROUPS = 8
SSD_HPG = SSD_HEADS // SSD_GROUPS
SSD_STATE = 128
SSD_CONV_W = 4
SSD_BC = SSD_GROUPS * SSD_STATE
SSD_CONV_DIM = SSD_D_INNER + 2 * SSD_BC
SSD_GROUP_W = SSD_HPG * SSD_HEAD_DIM

MLA_HEADS = 16
MLA_NOPE = 64
MLA_ROPE = 32
MLA_V = 64
MLA_Q_LORA = 384
MLA_KV_LORA = 256
MLA_QK = MLA_NOPE + MLA_ROPE
MLA_WIDTH = MLA_HEADS * MLA_V
ROPE_BASE = 10000.0

LANES = 128
SUBLANES = 8
HEAD_PAD = LANES
VMEM_LIMIT_BYTES = 52 * 1024 * 1024
NEG_BIG = -1e30
LOG2_E = math.log2(math.e)
SOFTMAX_LOG2_SCALE = (MLA_QK ** -0.5) * math.log2(math.e)
COL_CHUNK = 512
CONV_ROWS = 128
V_ONES_ROWS = 16
V_SLOT_ROWS = MLA_V + V_ONES_ROWS
ROW_TILE = 512
MLA_IN_TILE = 1024
ATTN_TILE = 256
SSD_BLOCK_ROWS = 256
SSD_SCAN_CHUNK = 128


def _cparams(*sem):
    return pltpu.CompilerParams(dimension_semantics=sem, vmem_limit_bytes=VMEM_LIMIT_BYTES)


def _const_spec(shape):
    nd = len(shape)
    return pl.BlockSpec(shape, lambda *_: (0,) * nd, pipeline_mode=pl.Buffered(1))


def _rms(x, w):
    return x * lax.rsqrt(jnp.mean(x * x, axis=-1, keepdims=True) + EPS) * w


def _dot(a, b):
    return jnp.dot(a, b, preferred_element_type=F32)


def _dot_nt(a, b):
    return lax.dot_general(a, b, (((1,), (1,)), ((), ())), preferred_element_type=F32)


def _store_dot(o_ref, a, w_ref, col0=0):
    width = o_ref.shape[1]
    for c in range(0, width, COL_CHUNK):
        cw = min(COL_CHUNK, width - c)
        o_ref[:, c:c + cw] = _dot(a, w_ref[:, col0 + c:col0 + c + cw]).astype(o_ref.dtype)


def _ssd_in_kernel(*refs, tiles_per_seq, has_init):
    if has_init:
        (x_ref, lnw_ref, w_ref, convw_ref, convb_ref, cprev_ref,
         z_ref, xs_ref, b_ref, c_ref, dt_ref, convnew_ref, ext) = refs
    else:
        (x_ref, lnw_ref, w_ref, convw_ref, convb_ref,
         z_ref, xs_ref, b_ref, c_ref, dt_ref, convnew_ref, ext) = refs
    tm = x_ref.shape[0]
    tail = SSD_CONV_W - 1
    t = pl.program_id(0) % tiles_per_seq
    xn = _rms(x_ref[...], lnw_ref[...]).astype(BF16)

    @pl.when(t == 0)
    def _init():
        ext[0:SUBLANES, :] = jnp.zeros((SUBLANES, SSD_CONV_DIM), F32)
        if has_init:
            ext[SUBLANES - tail:SUBLANES, :] = cprev_ref[0]

    dt_ref[...] = _dot(xn, w_ref[:, SSD_D_INNER + SSD_CONV_DIM:])
    def project(c0):
        cs = slice(c0, c0 + COL_CHUNK)
        ext[SUBLANES:SUBLANES + tm, cs] = _dot(xn, w_ref[:, SSD_D_INNER + c0:SSD_D_INNER + c0 + COL_CHUNK])
        convnew_ref[0, :, cs] = ext[pl.ds(SUBLANES + tm - tail, tail), cs]

    project(0)
    for c0 in range(0, SSD_CONV_DIM, COL_CHUNK):
        cs = slice(c0, c0 + COL_CHUNK)
        if c0 + COL_CHUNK < SSD_CONV_DIM:
            project(c0 + COL_CHUNK)
        if c0 % (2 * COL_CHUNK) == 0:
            zc = slice(c0 // 2, c0 // 2 + COL_CHUNK)
            hz = _dot(xn, w_ref[:, zc])
            z_ref[:, zc] = hz * jnp.tanh(hz) + hz
        if c0 < SSD_D_INNER:
            o_ref, o0 = xs_ref, c0
        elif c0 < SSD_D_INNER + SSD_BC:
            o_ref, o0 = b_ref, c0 - SSD_D_INNER
        else:
            o_ref, o0 = c_ref, c0 - SSD_D_INNER - SSD_BC
        for r0 in range(0, tm, CONV_ROWS):
            rb = min(CONV_ROWS, tm - r0)
            for l0 in range(0, COL_CHUNK, LANES):
                ls = slice(c0 + l0, c0 + l0 + LANES)
                e = ext[r0:r0 + rb + SUBLANES, ls]
                e1 = pltpu.roll(e, 1, 0)
                far = e * convw_ref[1:2, ls] + e1 * convw_ref[0:1, ls]
                near = e * convw_ref[3:4, ls] + e1 * convw_ref[2:3, ls]
                hv = (convb_ref[:, ls] + near[SUBLANES:SUBLANES + rb]) + pltpu.roll(far, 2, 0)[SUBLANES:SUBLANES + rb]
                y = hv * jnp.tanh(hv) + hv
                if o_ref is b_ref:
                    b_ref[0, o0 + l0:o0 + l0 + LANES, r0:r0 + rb] = y.T
                else:
                    o_ref[r0:r0 + rb, o0 + l0:o0 + l0 + LANES] = y.astype(o_ref.dtype)
    ext[0:SUBLANES, :] = ext[tm:tm + SUBLANES, :]


def _ssd_in(x, lnw, w, conv_prev, batch, seq, tm):
    m, d = x.shape
    assert seq % tm == 0 and tm % SUBLANES == 0 and m == batch * seq
    tps = seq // tm
    has_init = conv_prev is not None
    tail = SSD_CONV_W - 1
    consts = [lnw, w["in_w"], w["conv_w"], w["conv_b"]]
    row = lambda wd: pl.BlockSpec((tm, wd), lambda i: (i, 0))
    conv_spec = pl.BlockSpec((1, tail, SSD_CONV_DIM), lambda i: (i // tps, 0, 0))
    in_specs = [row(d)] + [_const_spec(a.shape) for a in consts]
    args = [x] + consts
    if has_init:
        in_specs.append(conv_spec)
        args.append(conv_prev)
    return pl.pallas_call(
        functools.partial(_ssd_in_kernel, tiles_per_seq=tps, has_init=has_init),
        grid=(m // tm,),
        in_specs=in_specs,
        out_specs=[row(SSD_D_INNER), row(SSD_D_INNER), pl.BlockSpec((1, SSD_BC, tm), lambda i: (i, 0, 0)),
                   row(SSD_BC), row(LANES), conv_spec],
        out_shape=[jax.ShapeDtypeStruct((m, SSD_D_INNER), F32),
                   jax.ShapeDtypeStruct((m, SSD_D_INNER), F32),
                   jax.ShapeDtypeStruct((m // tm, SSD_BC, tm), F32),
                   jax.ShapeDtypeStruct((m, SSD_BC), BF16),
                   jax.ShapeDtypeStruct((m, LANES), F32),
                   jax.ShapeDtypeStruct((batch, tail, SSD_CONV_DIM), F32)],
        scratch_shapes=[pltpu.VMEM((tm + SUBLANES, SSD_CONV_DIM), F32)],
        compiler_params=_cparams("arbitrary"),
        name="ssd_in",
    )(*args)


def _dot_exact_lhs(a, x):
    x1 = x.astype(BF16)
    r1 = x - x1.astype(F32)
    x2 = r1.astype(BF16)
    x3 = (r1 - x2.astype(F32)).astype(BF16)
    return _dot(a, x1) + _dot(a, x2) + _dot(a, x3)


def _expand_heads(cols, g, lane_head):
    shape = (cols.shape[0], SSD_GROUP_W)
    h0 = SSD_HPG * g
    out = jnp.broadcast_to(cols[:, h0 + SSD_HPG - 1:h0 + SSD_HPG], shape)
    for r in range(SSD_HPG - 2, -1, -1):
        out = jnp.where(lane_head[:shape[0]] == r, jnp.broadcast_to(cols[:, h0 + r:h0 + r + 1], shape), out)
    return out


def _select_heads(parts, lane_head):
    out = parts[0]
    for r in range(1, len(parts)):
        out = jnp.where(lane_head == r, parts[r], out)
    return out


def _ssd_kernel(*refs, L, has_init):
    if has_init:
        (zs_ref, xs_ref, bt_ref, c_ref, dt_ref, h0_ref, dtb_ref, alog_ref,
         dexp_ref, normw_ref, yn_ref, ht_ref, st, cb_s) = refs
    else:
        (zs_ref, xs_ref, bt_ref, c_ref, dt_ref, dtb_ref, alog_ref,
         dexp_ref, normw_ref, yn_ref, ht_ref, st, cb_s) = refs
    c = pl.program_id(1)
    last = pl.num_programs(1) - 1

    @pl.when(c == 0)
    def _init():
        if has_init:
            for g in range(SSD_GROUPS):
                st[g] = h0_ref[0, g * SSD_GROUP_W:(g + 1) * SSD_GROUP_W, :].T
        else:
            st[...] = jnp.zeros(st.shape, F32)

    rows = dt_ref.shape[0]
    n_chunks = rows // L
    fused = L % LANES == 0
    group_n = lambda g: slice(g * SSD_STATE, (g + 1) * SSD_STATE)

    for ci in range(n_chunks):
        rs = slice(ci * L, (ci + 1) * L)
        for g in range(SSD_GROUPS):
            cb_s[ci * SSD_GROUPS + g] = _dot(c_ref[rs, group_n(g)], bt_ref[0, group_n(g), rs].astype(BF16))

    dt = jax.nn.softplus(dt_ref[...] + dtb_ref[...])
    dta = dt * (-jnp.exp(alog_ref[...]))
    row = lax.broadcasted_iota(jnp.int32, (rows, rows), 0)
    col = lax.broadcasted_iota(jnp.int32, (rows, rows), 1)
    same_chunk_tri = jnp.logical_and(row >= col, row // L == col // L)
    acs = _dot_exact_lhs(same_chunk_tri.astype(BF16), dta) * LOG2_E
    acs_t = acs.T
    dt_t = dt.T
    tri = lax.broadcasted_iota(jnp.int32, (L, L), 0) >= lax.broadcasted_iota(jnp.int32, (L, L), 1)
    lane_head = lax.broadcasted_iota(jnp.int32, (L, SSD_GROUP_W), 1) // SSD_HEAD_DIM
    lane_head_n = lax.broadcasted_iota(jnp.int32, (SSD_STATE, SSD_GROUP_W), 1) // SSD_HEAD_DIM

    for ci in range(n_chunks):
        rs = slice(ci * L, (ci + 1) * L)
        acs_c = acs[rs]
        acs_tc = acs_t[:, rs]
        dt_tc = dt_t[:, rs]
        cdec = jnp.exp2(acs_c[L - 1:L, :])
        wdt_t = jnp.exp2(acs_tc[:, L - 1:L] - acs_tc) * dt_tc
        for g in range(SSD_GROUPS):
            gs = slice(g * SSD_GROUP_W, (g + 1) * SSD_GROUP_W)
            xg = xs_ref[rs, gs]
            xgb = xg.astype(BF16)
            cg = c_ref[rs, group_n(g)].astype(F32)
            cb = cb_s[ci * SSD_GROUPS + g]
            sg = st[g]
            sgb = sg.astype(BF16)
            if fused:
                rhs = jnp.concatenate([xgb, sgb], axis=0)
            parts = []
            for r in range(SSD_HPG):
                h = SSD_HPG * g + r
                a_col = jnp.broadcast_to(acs_c[:, h:h + 1], (L, LANES))
                seg = a_col[:, :L] - acs_tc[h:h + 1, :]
                m = (cb * jnp.exp2(jnp.where(tri, seg, -jnp.inf)) * dt_tc[h:h + 1, :]).astype(BF16)
                ce = (cg * jnp.exp2(a_col)).astype(BF16)
                if fused:
                    parts.append(_dot(jnp.concatenate([m, ce], axis=1), rhs))
                else:
                    parts.append(_dot(m, xgb) + _dot(ce, sgb))
            y = _select_heads(parts, lane_head)
            y = y + dexp_ref[:, gs] * xg
            y = y * zs_ref[rs, gs]
            yn_ref[rs, gs] = _rms(y, normw_ref[:, gs]).astype(yn_ref.dtype)
            btg = bt_ref[0, group_n(g), rs]
            upd = [_dot((btg * wdt_t[SSD_HPG * g + r:SSD_HPG * g + r + 1, :]).astype(BF16), xgb)
                   for r in range(SSD_HPG)]
            st[g] = sg * _expand_heads(cdec, g, lane_head) + _select_heads(upd, lane_head_n)

    @pl.when(c == last)
    def _state_out():
        for g in range(SSD_GROUPS):
            ht_ref[0, g * SSD_GROUP_W:(g + 1) * SSD_GROUP_W, :] = st[g].T


def _ssd_scan(zs, xs, bt, c, dt, h0, w, batch, seq, L):
    rows = bt.shape[2]
    nc = seq // rows
    assert nc * rows == seq and rows % L == 0 and L % SUBLANES == 0
    has_init = h0 is not None
    row_spec = lambda wd: pl.BlockSpec((rows, wd), lambda bi, ci: (bi * nc + ci, 0))
    state_spec = pl.BlockSpec((1, SSD_D_INNER, SSD_STATE), lambda bi, ci: (bi, 0, 0))
    bt_spec = pl.BlockSpec((1, SSD_BC, rows), lambda bi, ci: (bi * nc + ci, 0, 0))
    in_specs = [row_spec(SSD_D_INNER), row_spec(SSD_D_INNER), bt_spec, row_spec(SSD_BC), row_spec(LANES)]
    args = [zs, xs, bt, c, dt]
    if has_init:
        in_specs.append(state_spec)
        args.append(h0.reshape(batch, SSD_D_INNER, SSD_STATE))
    consts = [w["dt_bias"], w["a_log"], w["d_exp"], w["norm_w"]]
    in_specs += [_const_spec(a.shape) for a in consts]
    args += consts
    return pl.pallas_call(
        functools.partial(_ssd_kernel, L=L, has_init=has_init),
        grid=(batch, nc),
        in_specs=in_specs,
        out_specs=[row_spec(SSD_D_INNER), state_spec],
        out_shape=[jax.ShapeDtypeStruct((batch * seq, SSD_D_INNER), BF16),
                   jax.ShapeDtypeStruct((batch, SSD_D_INNER, SSD_STATE), F32)],
        scratch_shapes=[pltpu.VMEM((SSD_GROUPS, SSD_STATE, SSD_GROUP_W), F32),
                        pltpu.VMEM((rows // L * SSD_GROUPS, L, L), F32)],
        compiler_params=_cparams("arbitrary", "arbitrary"),
        name="ssd_scan",
    )(*args)


def _out_ple_kernel(y_ref, h_ref, p_ref, wout_ref, plenorm_ref, wgate_ref, wup_ref, o_ref):
    h1 = h_ref[...] + _dot(y_ref[...], wout_ref[...])
    hn = _rms(h1, plenorm_ref[...]).astype(BF16)
    pb = p_ref[0].astype(BF16)
    for c in range(0, D_MODEL, COL_CHUNK):
        cs = slice(c, c + COL_CHUNK)
        gate = jax.nn.sigmoid(_dot(hn, wgate_ref[:, cs]))
        o_ref[:, cs] = h1[:, cs] + _dot(pb, wup_ref[:, cs]) * gate


def _out_ple(y, h, p, wout, plenorm, wgate, wup, tm):
    m, kd = y.shape
    assert m % tm == 0
    p_all, layer = p
    return pl.pallas_call(
        _out_ple_kernel,
        grid=(m // tm,),
        in_specs=[pl.BlockSpec((tm, kd), lambda i: (i, 0)),
                  pl.BlockSpec((tm, D_MODEL), lambda i: (i, 0)),
                  pl.BlockSpec((1, tm, PLE_DIM), lambda i: (layer, i, 0)),
                  _const_spec(wout.shape), _const_spec(plenorm.shape),
                  _const_spec(wgate.shape), _const_spec(wup.shape)],
        out_specs=pl.BlockSpec((tm, D_MODEL), lambda i: (i, 0)),
        out_shape=jax.ShapeDtypeStruct((m, D_MODEL), F32),
        compiler_params=_cparams("arbitrary"),
        name="out_ple",
    )(y, h, p_all, wout, plenorm, wgate, wup)


def _dot_hi_lo_rhs(x, a):
    hi = x.astype(BF16)
    lo = (x - hi.astype(F32)).astype(BF16)
    return _dot(hi, a) + _dot(lo, a)


def _head_part_matrix(width):
    def part(idx):
        within = idx % HEAD_PAD
        return (idx // HEAD_PAD) * 4 + jnp.where(within < MLA_NOPE, 0, jnp.where(within < MLA_QK, 1, 2))
    rows = part(lax.broadcasted_iota(jnp.int32, (width, width), 0))
    cols_i = lax.broadcasted_iota(jnp.int32, (width, width), 1)
    cols = part(cols_i)
    same = jnp.logical_and(rows == cols, cols_i % HEAD_PAD < MLA_QK)
    lane = lax.broadcasted_iota(jnp.int32, (1, width), 1) % HEAD_PAD
    inv_count = jnp.where(lane < MLA_NOPE, 1.0 / MLA_NOPE, jnp.where(lane < MLA_QK, 1.0 / MLA_ROPE, 0.0))
    return jnp.where(same, 1.0, 0.0).astype(BF16), inv_count


def _head_norm_rope(x, pmat, inv_count, gain_cos, gain_sin):
    inv = lax.rsqrt(_dot_hi_lo_rhs(x * x, pmat) * inv_count + EPS)
    shift = HEAD_PAD - MLA_ROPE
    x_rot = jnp.concatenate([pltpu.roll(x[:, s0:s0 + HEAD_PAD], shift, 1)
                             for s0 in range(0, x.shape[1], HEAD_PAD)], axis=1)
    return (x * inv) * gain_cos + (x_rot * inv) * gain_sin


def _mla_in_kernel(h_ref, lnw_ref, win_ref, qan_ref, wqb_ref, kvan_ref, gq_ref, gqrot_ref, gkr_ref, gkrrot_ref,
                   cos_ref, sin_ref, *rest, q_transposed):
    q_ref, lat_ref, kr_ref, gate_ref = rest[-4:]
    xn = _rms(h_ref[...], lnw_ref[...]).astype(BF16)
    c_q, c_kv, c_kr = 0, MLA_Q_LORA, MLA_Q_LORA + MLA_KV_LORA
    c_gate = c_kr + HEAD_PAD
    pair_w = 2 * HEAD_PAD
    _store_dot(gate_ref, xn, win_ref, c_gate)
    lat_ref[0] = _rms(_dot(xn, win_ref[:, c_kv:c_kr]), kvan_ref[...])
    pmat, inv_count = _head_part_matrix(pair_w)
    cos_t, sin_t = cos_ref[...], sin_ref[...]
    kr = _head_norm_rope(_dot(xn, win_ref[:, c_kr:c_gate]), pmat[:HEAD_PAD, :HEAD_PAD],
                         inv_count[:, :HEAD_PAD], gkr_ref[...] * cos_t[:, :HEAD_PAD],
                         gkrrot_ref[...] * sin_t[:, :HEAD_PAD])
    kr_ref[...] = kr[:, MLA_NOPE:MLA_QK]
    qa = _rms(_dot(xn, win_ref[:, c_q:c_kv]), qan_ref[...]).astype(BF16)
    gain_cos = gq_ref[...] * cos_t * SOFTMAX_LOG2_SCALE
    gain_sin = gqrot_ref[...] * sin_t * SOFTMAX_LOG2_SCALE
    for pair in range(MLA_HEADS // 2):
        qq = _dot(qa, wqb_ref[:, pair * pair_w:(pair + 1) * pair_w])
        qh = _head_norm_rope(qq, pmat, inv_count, gain_cos, gain_sin)
        if q_transposed:
            q_ref[0, pair * pair_w:(pair + 1) * pair_w, :] = qh.T.astype(q_ref.dtype)
        else:
            q_ref[:, pair * pair_w:(pair + 1) * pair_w] = qh.astype(q_ref.dtype)


def _stacked_out(prev, layer, count, block, index_tail, shape_tail, dtype):
    spec = pl.BlockSpec((1,) + block, lambda *g: (layer,) + index_tail(*g))
    shape = jax.ShapeDtypeStruct((count,) + shape_tail, dtype)
    if prev is None:
        return [], [], spec, shape
    return [pl.BlockSpec(memory_space=pl.ANY)], [prev], spec, shape


def _mla_in(h, lnw, w, rope, tm, q_transposed, lat_stack):
    m = h.shape[0]
    assert m % tm == 0
    cos_t, sin_t = rope
    period = cos_t.shape[0] // tm
    assert period * tm == cos_t.shape[0]
    rope_spec = pl.BlockSpec((tm, 2 * HEAD_PAD), lambda i: (i % period, 0))
    consts = [lnw, w["in_w"], w["q_a_norm"], w["q_b_w"], w["kv_a_norm"],
              w["gain_q"], w["gain_q_rot"], w["gain_kr"], w["gain_kr_rot"]]
    row = lambda wd: pl.BlockSpec((tm, wd), lambda i: (i, 0))
    qw = MLA_HEADS * HEAD_PAD
    if q_transposed:
        seq = period * tm
        assert m % seq == 0
        q_spec = pl.BlockSpec((1, qw, tm), lambda i: (i // period, 0, i % period))
        q_shape = jax.ShapeDtypeStruct((m // seq, qw, seq), BF16)
    else:
        q_spec, q_shape = row(qw), jax.ShapeDtypeStruct((m, qw), BF16)
    prev, layer, count = lat_stack
    extra_specs, extra_args, lat_spec, lat_shape = _stacked_out(
        prev, layer, count, (tm, MLA_KV_LORA), lambda i: (i, 0), (m, MLA_KV_LORA), F32)
    n_in = 1 + len(consts) + 2
    return pl.pallas_call(
        functools.partial(_mla_in_kernel, q_transposed=q_transposed),
        grid=(m // tm,),
        in_specs=[row(D_MODEL)] + [_const_spec(a.shape) for a in consts] + [rope_spec] * 2 + extra_specs,
        out_specs=[q_spec, lat_spec, row(MLA_ROPE), row(MLA_WIDTH)],
        input_output_aliases={n_in: 1} if extra_args else {},
        out_shape=[q_shape,
                   lat_shape,
                   jax.ShapeDtypeStruct((m, MLA_ROPE), F32),
                   jax.ShapeDtypeStruct((m, MLA_WIDTH), F32)],
        compiler_params=_cparams("arbitrary"),
        name="mla_in",
    )(h, *consts, cos_t, sin_t, *extra_args)


def _mla_kv_kernel(lat_ref, kr_ref, wk_ref, wv_ref, gkn_ref, k_ref, v_ref, *, v_transposed):
    latb = lat_ref[0].astype(BF16)
    if v_transposed:
        ones = jnp.ones((V_ONES_ROWS, latb.shape[0]), v_ref.dtype)
        for c in range(0, MLA_WIDTH, COL_CHUNK):
            vt = _dot(latb, wv_ref[:, c:c + COL_CHUNK]).T.astype(v_ref.dtype)
            for hh in range(COL_CHUNK // MLA_V):
                r0 = (c // MLA_V + hh) * V_SLOT_ROWS
                v_ref[0, r0:r0 + MLA_V, :] = vt[hh * MLA_V:(hh + 1) * MLA_V]
                v_ref[0, r0 + MLA_V:r0 + V_SLOT_ROWS, :] = ones
    else:
        _store_dot(v_ref, latb, wv_ref)
    rows = latb.shape[0]
    kr = jnp.concatenate([jnp.zeros((rows, MLA_NOPE), F32), kr_ref[...],
                          jnp.zeros((rows, HEAD_PAD - MLA_QK), F32)], axis=1)
    for h in range(MLA_HEADS):
        hs = slice(h * HEAD_PAD, (h + 1) * HEAD_PAD)
        kh = _dot(latb, wk_ref[:, hs])
        ms = jnp.sum(kh * kh, axis=-1, keepdims=True) * (1.0 / MLA_NOPE)
        k_ref[:, hs] = (kh * lax.rsqrt(ms + EPS) * gkn_ref[...] + kr).astype(k_ref.dtype)


def _mla_kv(lat, layer, kr, w, tm, v_transposed):
    m = lat.shape[1]
    assert m % tm == 0
    consts = [w["kv_k_w"], w["kv_v_w"], w["gain_kn"]]
    row = lambda wd: pl.BlockSpec((tm, wd), lambda i: (i, 0))
    if v_transposed:
        v_spec = pl.BlockSpec((1, MLA_HEADS * V_SLOT_ROWS, tm), lambda i: (i, 0, 0))
        v_shape = jax.ShapeDtypeStruct((m // tm, MLA_HEADS * V_SLOT_ROWS, tm), BF16)
    else:
        v_spec, v_shape = row(MLA_WIDTH), jax.ShapeDtypeStruct((m, MLA_WIDTH), BF16)
    return pl.pallas_call(
        functools.partial(_mla_kv_kernel, v_transposed=v_transposed),
        grid=(m // tm,),
        in_specs=[pl.BlockSpec((1, tm, MLA_KV_LORA), lambda i: (layer, i, 0)), row(MLA_ROPE)]
                 + [_const_spec(a.shape) for a in consts],
        out_specs=[row(MLA_HEADS * HEAD_PAD), v_spec],
        out_shape=[jax.ShapeDtypeStruct((m, MLA_HEADS * HEAD_PAD), BF16), v_shape],
        compiler_params=_cparams("arbitrary"),
        name="mla_kv",
    )(lat, kr, *consts)


def _gated_pair_out(o_ref, gate_ref, outs, pair):
    lane = lax.broadcasted_iota(jnp.int32, outs[0].shape, 1)
    ps = slice(pair * LANES, (pair + 1) * LANES)
    g = gate_ref[:, ps]
    o_ref[:, ps] = (jnp.where(lane < MLA_V, outs[0], outs[1]) * (g * jax.nn.sigmoid(g))).astype(o_ref.dtype)


def _attn_kernel(q_ref, k_ref, v_ref, gate_ref, o_ref, m_s, acc_s, s_s, *, tq):
    i = pl.program_id(1)
    key_c = lax.broadcasted_iota(jnp.int32, (tq, tq), 0) // CHUNK
    qry_c = lax.broadcasted_iota(jnp.int32, (tq, tq), 1) // CHUNK
    diag_mask = key_c <= qry_c
    m_s[...] = jnp.full(m_s.shape, NEG_BIG, F32)
    acc_s[...] = jnp.zeros(acc_s.shape, F32)

    def tile_step(j, mask):
        rows = pl.ds(pl.multiple_of(j * tq, tq), tq)
        for h in range(MLA_HEADS):
            hs = slice(h * HEAD_PAD, (h + 1) * HEAD_PAD)
            s_s[h] = _dot(k_ref[rows, hs], q_ref[0, hs, :])
        for h in range(MLA_HEADS):
            alphas, probs = [], []
            for q0 in range(0, tq, LANES):
                qs = slice(q0, q0 + LANES)
                s = s_s[h, :, qs]
                if mask is not None:
                    s = jnp.where(mask[:, qs], s, -jnp.inf)
                m_prev = m_s[h, :, qs]
                m_new = jnp.maximum(m_prev, jnp.max(s, axis=0, keepdims=True))
                alphas.append(jnp.exp2(m_prev[0:1, :] - m_new[0:1, :]))
                probs.append(jnp.exp2(s - m_new[0:1, :]).astype(BF16))
                m_s[h, :, qs] = m_new
            p = jnp.concatenate(probs, axis=1)
            pv = _dot(v_ref[j, h * V_SLOT_ROWS:(h + 1) * V_SLOT_ROWS, :], p)
            acc_s[h] = jnp.concatenate(alphas, axis=1) * acc_s[h] + pv

    def body(j, carry):
        tile_step(j, None)
        return carry

    lax.fori_loop(0, i, body, 0)
    tile_step(i, diag_mask)
    for pair in range(MLA_HEADS // 2):
        o_t = jnp.concatenate([acc_s[2 * pair + sub, 0:MLA_V, :] / acc_s[2 * pair + sub, MLA_V:MLA_V + 1, :]
                               for sub in range(2)], axis=0)
        ps = slice(pair * LANES, (pair + 1) * LANES)
        g = gate_ref[:, ps]
        o_ref[:, ps] = (o_t.T * (g * jax.nn.sigmoid(g))).astype(o_ref.dtype)


def _attn_prompt(q_t, k, v_t, gate, batch, seq, tq):
    nq = seq // tq
    assert nq * tq == seq and tq % CHUNK == 0
    return pl.pallas_call(
        functools.partial(_attn_kernel, tq=tq),
        grid=(batch, nq),
        in_specs=[pl.BlockSpec((1, MLA_HEADS * HEAD_PAD, tq), lambda b, i: (b, 0, i)),
                  pl.BlockSpec((seq, MLA_HEADS * HEAD_PAD), lambda b, i: (b, 0)),
                  pl.BlockSpec((nq, MLA_HEADS * V_SLOT_ROWS, tq), lambda b, i: (b, 0, 0)),
                  pl.BlockSpec((tq, MLA_WIDTH), lambda b, i: (b * nq + i, 0))],
        out_specs=pl.BlockSpec((tq, MLA_WIDTH), lambda b, i: (b * nq + i, 0)),
        out_shape=jax.ShapeDtypeStruct((batch * seq, MLA_WIDTH), BF16),
        scratch_shapes=[pltpu.VMEM((MLA_HEADS, SUBLANES, tq), F32),
                        pltpu.VMEM((MLA_HEADS, V_SLOT_ROWS, tq), F32), pltpu.VMEM((MLA_HEADS, tq, tq), F32)],
        compiler_params=_cparams("arbitrary", "arbitrary"),
        name="attn_prompt",
    )(q_t, k, v_t, gate)


def _attn_step_kernel(q_ref, k_ref, v_ref, gate_ref, o_ref, s_s, *, q_pos0, n_keys):
    tq = q_ref.shape[0]
    tk = k_ref.shape[1]
    k_pos = lax.broadcasted_iota(jnp.int32, (tq, tk), 1)
    q_pos = lax.broadcasted_iota(jnp.int32, (tq, tk), 0) + q_pos0
    mask = jnp.logical_and(k_pos < n_keys, k_pos // CHUNK <= q_pos // CHUNK)
    for h in range(MLA_HEADS):
        hs = slice(h * HEAD_PAD, (h + 1) * HEAD_PAD)
        s_s[h] = _dot_nt(q_ref[:, hs], k_ref[0, :, hs])
    for pair in range(MLA_HEADS // 2):
        vs = slice(pair * LANES, (pair + 1) * LANES)
        outs = []
        for sub in range(2):
            s = jnp.where(mask, s_s[2 * pair + sub], -jnp.inf)
            p = jnp.exp2(s - jnp.max(s, axis=-1, keepdims=True))
            outs.append(_dot(p.astype(BF16), v_ref[0, :, vs]) / jnp.sum(p, axis=-1, keepdims=True))
        _gated_pair_out(o_ref, gate_ref, outs, pair)


def _attn_step(q, k, v, gate, batch, tq, n_keys):
    tk = k.shape[1]
    return pl.pallas_call(
        functools.partial(_attn_step_kernel, q_pos0=n_keys - tq, n_keys=n_keys),
        grid=(batch,),
        in_specs=[pl.BlockSpec((tq, MLA_HEADS * HEAD_PAD), lambda b: (b, 0)),
                  pl.BlockSpec((1, tk, MLA_HEADS * HEAD_PAD), lambda b: (b, 0, 0)),
                  pl.BlockSpec((1, tk, MLA_WIDTH), lambda b: (b, 0, 0)),
                  pl.BlockSpec((tq, MLA_WIDTH), lambda b: (b, 0))],
        out_specs=pl.BlockSpec((tq, MLA_WIDTH), lambda b: (b, 0)),
        out_shape=jax.ShapeDtypeStruct((batch * tq, MLA_WIDTH), BF16),
        scratch_shapes=[pltpu.VMEM((MLA_HEADS, tq, tk), F32)],
        compiler_params=_cparams("arbitrary"),
        name="attn_step",
    )(q, k, v, gate)


def _pad_cols(a, width):
    return jnp.pad(a, ((0, 0), (0, width - a.shape[1])))


def _ssd_weights(j, in_w, conv_w, conv_b, dt_bias, a_log, d, norm_w, out_w):
    zx = SSD_D_INNER + SSD_CONV_DIM
    return {
        "in_w": jnp.concatenate([0.5 * in_w[j][:, :SSD_D_INNER], in_w[j][:, SSD_D_INNER:zx],
                                 _pad_cols(in_w[j][:, zx:], LANES)], axis=1).astype(BF16),
        "conv_w": 0.5 * conv_w[j],
        "conv_b": 0.5 * conv_b[j][None, :],
        "dt_bias": _pad_cols(dt_bias[j][None, :], LANES),
        "a_log": _pad_cols(a_log[j][None, :], LANES),
        "d_exp": jnp.repeat(d[j], SSD_HEAD_DIM)[None, :],
        "norm_w": norm_w[j][None, :],
        "out_w": out_w[j].astype(BF16),
    }


def _head_pad_cols(w, per_head, take, offset=0):
    k = w.shape[0]
    w = w.reshape(k, MLA_HEADS, per_head)[:, :, offset:offset + take]
    return jnp.pad(w, ((0, 0), (0, 0), (0, HEAD_PAD - take))).reshape(k, MLA_HEADS * HEAD_PAD)


def _rotate_half(a):
    half = MLA_ROPE // 2
    return jnp.concatenate([-a[..., half:], a[..., :half]], axis=-1)


def _swap_halves(a):
    half = MLA_ROPE // 2
    return jnp.concatenate([a[..., half:], a[..., :half]], axis=-1)


def _rope_slot(a):
    pad = [(0, 0)] * (a.ndim - 1) + [(MLA_NOPE, HEAD_PAD - MLA_QK)]
    return jnp.pad(a, pad)


def _q_b_cols(w):
    k = w.shape[0]
    w = w.reshape(k, MLA_HEADS, MLA_QK)
    return jnp.concatenate([w, _rotate_half(w[:, :, MLA_NOPE:])], axis=-1).reshape(k, MLA_HEADS * HEAD_PAD)


def _mla_weights(j, in_w, q_a_norm, q_b_w, kv_a_norm, kv_b_w, qn, qr, kn, kr, out_w):
    c1, c2, c3 = MLA_Q_LORA, MLA_Q_LORA + MLA_KV_LORA, MLA_Q_LORA + MLA_KV_LORA + MLA_ROPE
    w = in_w[j]
    kr_cols = jnp.concatenate([jnp.zeros((w.shape[0], MLA_NOPE), F32), w[:, c2:c3],
                               _rotate_half(w[:, c2:c3])], axis=1)
    zeros = lambda n: jnp.zeros((n,), F32)
    gain_q = jnp.concatenate([qn[j], qr[j], zeros(HEAD_PAD - MLA_QK)])
    return {
        "in_w": jnp.concatenate([w[:, :c2], kr_cols, w[:, c3:]], axis=1).astype(BF16),
        "q_a_norm": q_a_norm[j][None, :],
        "q_b_w": _q_b_cols(q_b_w[j]).astype(BF16),
        "gain_q": jnp.tile(gain_q, 2)[None, :],
        "gain_q_rot": jnp.tile(_rope_slot(_swap_halves(qr[j])), 2)[None, :],
        "gain_kr_rot": _rope_slot(_swap_halves(kr[j]))[None, :],
        "kv_a_norm": kv_a_norm[j][None, :],
        "kv_k_w": _head_pad_cols(kv_b_w[j], MLA_NOPE + MLA_V, MLA_NOPE).astype(BF16),
        "kv_v_w": kv_b_w[j].reshape(MLA_KV_LORA, MLA_HEADS, MLA_NOPE + MLA_V)[:, :, MLA_NOPE:]
                  .reshape(MLA_KV_LORA, MLA_WIDTH).astype(BF16),
        "gain_kr": _rope_slot(kr[j])[None, :],
        "gain_kn": jnp.concatenate([kn[j], zeros(HEAD_PAD - MLA_NOPE)])[None, :],
        "out_w": out_w[j].astype(BF16),
    }


def _rope_tables(pos0, seq, rows):
    inv = 1.0 / (ROPE_BASE ** (jnp.arange(0, MLA_ROPE, 2, dtype=F32) / MLA_ROPE))
    ang = (pos0 + jnp.arange(seq)).astype(F32)[:, None] * inv[None, :]
    cos, sin = jnp.cos(ang), jnp.sin(ang)
    z = lambda n: jnp.zeros((seq, n), F32)
    cos_t = jnp.concatenate([jnp.ones((seq, MLA_NOPE), F32), cos, cos, z(HEAD_PAD - MLA_QK)], axis=1)
    sin_t = jnp.concatenate([z(MLA_NOPE), sin, sin, z(HEAD_PAD - MLA_QK)], axis=1)
    reps = max(1, rows // seq)
    return tuple(jnp.tile(t, (reps, 2)) for t in (cos_t, sin_t))


def _row_tile(m, want):
    tm = min(m, want)
    assert m % tm == 0
    return tm


def _ssd_layer(h, p, lnw, ple, w, batch, seq, chunk, conv_prev, h0):
    tm = _row_tile(h.shape[0], ROW_TILE)
    zs, xs, bt, c, dt, conv_new = _ssd_in(h, lnw, w, conv_prev, batch, seq, min(seq, SSD_BLOCK_ROWS))
    yn, ht = _ssd_scan(zs, xs, bt, c, dt, h0, w, batch, seq, chunk)
    h = _out_ple(yn, h, p, w["out_w"], *ple, tm)
    return h, conv_new, ht.reshape(batch, SSD_HEADS, SSD_HEAD_DIM, SSD_STATE)


def _mla_layer(h, p, lnw, ple, w, batch, seq, lat_past, kr_past, lat_stack):
    tm = _row_tile(h.shape[0], ROW_TILE)
    prompt = lat_past is None
    tm_in = _row_tile(h.shape[0], min(seq, MLA_IN_TILE) if prompt else MLA_IN_TILE)
    pos0 = 0 if prompt else lat_past.shape[1]
    rope = _rope_tables(pos0, seq, tm_in)
    q, lats, kr, gate = _mla_in(h, lnw, w, rope, tm_in, q_transposed=prompt, lat_stack=lat_stack)
    layer = lat_stack[1]
    if prompt:
        tile = min(seq, ATTN_TILE)
        k, v_t = _mla_kv(lats, layer, kr, w, tile, v_transposed=True)
        og = _attn_prompt(q, k, v_t, gate, batch, seq, tile)
    else:
        n_keys = pos0 + seq
        tk = -(-n_keys // LANES) * LANES
        pad = ((0, 0), (0, tk - n_keys), (0, 0))
        lat_all = jnp.pad(jnp.concatenate([lat_past, lats[layer].reshape(batch, seq, -1)], axis=1), pad)
        kr_all = jnp.pad(jnp.concatenate([kr_past, kr.reshape(batch, seq, -1)], axis=1), pad)
        k, v = _mla_kv(lat_all.reshape(1, batch * tk, -1), 0, kr_all.reshape(batch * tk, -1), w,
                       math.gcd(batch * tk, 512), v_transposed=False)
        og = _attn_step(q, k.reshape(batch, tk, -1), v.reshape(batch, tk, -1), gate, batch, seq, n_keys)
    h = _out_ple(og, h, p, w["out_w"], *ple, tm)
    return h, lats, kr.reshape(batch, seq, -1)


def kernel(x_prompt, x_sample, cache_conv, state_ssm, cache_kv_latent, cache_k_rope, p_prompt, p_sample, ln_w, ssd_in_w, ssd_conv_w, ssd_conv_b, ssd_dt_bias, ssd_A_log, ssd_D, ssd_norm_w, ssd_out_w, mla_in_w, mla_q_a_norm, mla_q_b_w, mla_kv_a_norm, mla_kv_b_w, mla_q_nope_norm, mla_q_rope_norm, mla_k_nope_norm, mla_k_rope_norm, mla_out_w, ple_up_w, ple_norm_w, ple_gate_w):
    bp, tp, d = x_prompt.shape
    bs, ts, _ = x_sample.shape
    hp = x_prompt.reshape(bp * tp, d)
    hs = x_sample.reshape(bs * ts, d)
    conv_p, ssm_p, kr_p = [], [], []
    conv_s, ssm_s, kr_s = [], [], []
    lat_p = lat_s = None
    for i in range(DEPTH):
        j = i // 2
        lnw = ln_w[i][None, :]
        ple = (ple_norm_w[i][None, :], ple_gate_w[i].astype(BF16), ple_up_w[i].astype(BF16))
        pp = (p_prompt.reshape(DEPTH, bp * tp, PLE_DIM), i)
        ps = (p_sample.reshape(DEPTH, bs * ts, PLE_DIM), i)
        if i % 2 == 0:
            w = _ssd_weights(j, ssd_in_w, ssd_conv_w, ssd_conv_b, ssd_dt_bias, ssd_A_log, ssd_D,
                             ssd_norm_w, ssd_out_w)
            hp, cp, sp = _ssd_layer(hp, pp, lnw, ple, w, bp, tp, min(tp, SSD_SCAN_CHUNK), None, None)
            hs, cs, ss = _ssd_layer(hs, ps, lnw, ple, w, bs, ts, ts, cache_conv[j], state_ssm[j])
            conv_p.append(cp); ssm_p.append(sp); conv_s.append(cs); ssm_s.append(ss)
        else:
            w = _mla_weights(j, mla_in_w, mla_q_a_norm, mla_q_b_w, mla_kv_a_norm, mla_kv_b_w,
                             mla_q_nope_norm, mla_q_rope_norm, mla_k_nope_norm, mla_k_rope_norm, mla_out_w)
            n_mla = DEPTH // 2
            hp, lat_p, rp = _mla_layer(hp, pp, lnw, ple, w, bp, tp, None, None, (lat_p, j, n_mla))
            hs, lat_s, rs = _mla_layer(hs, ps, lnw, ple, w, bs, ts, cache_kv_latent[j], cache_k_rope[j],
                                       (lat_s, j, n_mla))
            kr_p.append(rp); kr_s.append(rs)
    return (hp.reshape(bp, tp, d), hs.reshape(bs, ts, d),
            jnp.stack(conv_p), jnp.stack(ssm_p), lat_p.reshape(-1, bp, tp, MLA_KV_LORA), jnp.stack(kr_p),
            jnp.stack(conv_s), jnp.stack(ssm_s), lat_s.reshape(-1, bs, ts, MLA_KV_LORA), jnp.stack(kr_s))
```

```python
import functools
import math

import jax
import jax.numpy as jnp
from jax import lax
from jax.experimental import pallas as pl
from jax.experimental.pallas import tpu as pltpu

F32 = jnp.float32
BF16 = jnp.bfloat16

D_MODEL = 1024
DEPTH = 4
CHUNK = 64
PLE_DIM = 256
EPS = 1e-6

SSD_D_INNER = 2 * D_MODEL
SSD_HEAD_DIM = 64
SSD_HEADS = SSD_D_INNER // SSD_HEAD_DIM
SSD_GROUPS = 8
SSD_HPG = SSD_HEADS // SSD_GROUPS
SSD_STATE = 128
SSD_CONV_W = 4
SSD_BC = SSD_GROUPS * SSD_STATE
SSD_CONV_DIM = SSD_D_INNER + 2 * SSD_BC
SSD_GROUP_W = SSD_HPG * SSD_HEAD_DIM

MLA_HEADS = 16
MLA_NOPE = 64
MLA_ROPE = 32
MLA_V = 64
MLA_Q_LORA = 384
MLA_KV_LORA = 256
MLA_QK = MLA_NOPE + MLA_ROPE
MLA_WIDTH = MLA_HEADS * MLA_V
ROPE_BASE = 10000.0

LANES = 128
SUBLANES = 8
HEAD_PAD = LANES
VMEM_V7X_BYTES = 64 * 1024 * 1024
VMEM_LIMIT_BYTES = VMEM_V7X_BYTES * 13 // 16
NEG_BIG = -1e30
LOG2_E = math.log2(math.e)
SOFTMAX_LOG2_SCALE = (MLA_QK ** -0.5) * LOG2_E
COL_CHUNK = 512
CONV_ROWS = 128
V_ONES_ROWS = 16
V_SLOT_ROWS = MLA_V + V_ONES_ROWS
ROW_TILE = 512
MLA_IN_TILE = 1024
ATTN_TILE = 256
SSD_BLOCK_ROWS = 256
SSD_SCAN_CHUNK = 128


def _cparams(*sem):
    return pltpu.CompilerParams(dimension_semantics=sem, vmem_limit_bytes=VMEM_LIMIT_BYTES)


def _const_spec(shape):
    nd = len(shape)
    return pl.BlockSpec(shape, lambda *_: (0,) * nd, pipeline_mode=pl.Buffered(1))


def _rms(x, w):
    return x * lax.rsqrt(jnp.mean(x * x, axis=-1, keepdims=True) + EPS) * w


def _dot(a, b):
    return jnp.dot(a, b, preferred_element_type=F32)


def _dot_nt(a, b):
    return lax.dot_general(a, b, (((1,), (1,)), ((), ())), preferred_element_type=F32)


def _store_dot(o_ref, a, w_ref, col0=0):
    width = o_ref.shape[1]
    for c in range(0, width, COL_CHUNK):
        cw = min(COL_CHUNK, width - c)
        o_ref[:, c:c + cw] = _dot(a, w_ref[:, col0 + c:col0 + c + cw]).astype(o_ref.dtype)


def _ssd_in_kernel(*refs, tiles_per_seq, has_init):
    if has_init:
        (x_ref, lnw_ref, w_ref, convw_ref, convb_ref, cprev_ref,
         z_ref, xs_ref, b_ref, c_ref, dt_ref, convnew_ref, ext) = refs
    else:
        (x_ref, lnw_ref, w_ref, convw_ref, convb_ref,
         z_ref, xs_ref, b_ref, c_ref, dt_ref, convnew_ref, ext) = refs
    tm = x_ref.shape[0]
    tail = SSD_CONV_W - 1
    t = pl.program_id(0) % tiles_per_seq
    xn = _rms(x_ref[...], lnw_ref[...]).astype(BF16)

    @pl.when(t == 0)
    def _init():
        ext[0:SUBLANES, :] = jnp.zeros((SUBLANES, SSD_CONV_DIM), F32)
        if has_init:
            ext[SUBLANES - tail:SUBLANES, :] = cprev_ref[0]

    dt_ref[...] = _dot(xn, w_ref[:, SSD_D_INNER + SSD_CONV_DIM:])
    def project(c0):
        cs = slice(c0, c0 + COL_CHUNK)
        ext[SUBLANES:SUBLANES + tm, cs] = _dot(xn, w_ref[:, SSD_D_INNER + c0:SSD_D_INNER + c0 + COL_CHUNK])
        convnew_ref[0, :, cs] = ext[pl.ds(SUBLANES + tm - tail, tail), cs]

    project(0)
    for c0 in range(0, SSD_CONV_DIM, COL_CHUNK):
        cs = slice(c0, c0 + COL_CHUNK)
        if c0 + COL_CHUNK < SSD_CONV_DIM:
            project(c0 + COL_CHUNK)
        if c0 % (2 * COL_CHUNK) == 0:
            zc = slice(c0 // 2, c0 // 2 + COL_CHUNK)
            hz = _dot(xn, w_ref[:, zc])
            z_ref[:, zc] = hz * jnp.tanh(hz) + hz
        if c0 < SSD_D_INNER:
            o_ref, o0 = xs_ref, c0
        elif c0 < SSD_D_INNER + SSD_BC:
            o_ref, o0 = b_ref, c0 - SSD_D_INNER
        else:
            o_ref, o0 = c_ref, c0 - SSD_D_INNER - SSD_BC
        for r0 in range(0, tm, CONV_ROWS):
            rb = min(CONV_ROWS, tm - r0)
            for l0 in range(0, COL_CHUNK, LANES):
                ls = slice(c0 + l0, c0 + l0 + LANES)
                e = ext[r0:r0 + rb + SUBLANES, ls]
                e1 = pltpu.roll(e, 1, 0)
                far = e * convw_ref[1:2, ls] + e1 * convw_ref[0:1, ls]
                near = e * convw_ref[3:4, ls] + e1 * convw_ref[2:3, ls]
                hv = (convb_ref[:, ls] + near[SUBLANES:SUBLANES + rb]) + pltpu.roll(far, 2, 0)[SUBLANES:SUBLANES + rb]
                y = hv * jnp.tanh(hv) + hv
                if o_ref is b_ref:
                    b_ref[0, o0 + l0:o0 + l0 + LANES, r0:r0 + rb] = y.T
                else:
                    o_ref[r0:r0 + rb, o0 + l0:o0 + l0 + LANES] = y.astype(o_ref.dtype)
    ext[0:SUBLANES, :] = ext[tm:tm + SUBLANES, :]


def _ssd_in(x, lnw, w, conv_prev, batch, seq, tm):
    m, d = x.shape
    assert seq % tm == 0 and tm % SUBLANES == 0 and m == batch * seq
    tps = seq // tm
    has_init = conv_prev is not None
    tail = SSD_CONV_W - 1
    consts = [lnw, w["in_w"], w["conv_w"], w["conv_b"]]
    row = lambda wd: pl.BlockSpec((tm, wd), lambda i: (i, 0))
    conv_spec = pl.BlockSpec((1, tail, SSD_CONV_DIM), lambda i: (i // tps, 0, 0))
    in_specs = [row(d)] + [_const_spec(a.shape) for a in consts]
    args = [x] + consts
    if has_init:
        in_specs.append(conv_spec)
        args.append(conv_prev)
    return pl.pallas_call(
        functools.partial(_ssd_in_kernel, tiles_per_seq=tps, has_init=has_init),
        grid=(m // tm,),
        in_specs=in_specs,
        out_specs=[row(SSD_D_INNER), row(SSD_D_INNER), pl.BlockSpec((1, SSD_BC, tm), lambda i: (i, 0, 0)),
                   row(SSD_BC), row(LANES), conv_spec],
        out_shape=[jax.ShapeDtypeStruct((m, SSD_D_INNER), F32),
                   jax.ShapeDtypeStruct((m, SSD_D_INNER), F32),
                   jax.ShapeDtypeStruct((m // tm, SSD_BC, tm), F32),
                   jax.ShapeDtypeStruct((m, SSD_BC), BF16),
                   jax.ShapeDtypeStruct((m, LANES), F32),
                   jax.ShapeDtypeStruct((batch, tail, SSD_CONV_DIM), F32)],
        scratch_shapes=[pltpu.VMEM((tm + SUBLANES, SSD_CONV_DIM), F32)],
        compiler_params=_cparams("arbitrary"),
        name="ssd_in",
    )(*args)


def _dot_exact_lhs(a, x):
    x1 = x.astype(BF16)
    r1 = x - x1.astype(F32)
    x2 = r1.astype(BF16)
    x3 = (r1 - x2.astype(F32)).astype(BF16)
    return _dot(a, x1) + _dot(a, x2) + _dot(a, x3)


def _expand_heads(cols, g, lane_head):
    shape = (cols.shape[0], SSD_GROUP_W)
    h0 = SSD_HPG * g
    out = jnp.broadcast_to(cols[:, h0 + SSD_HPG - 1:h0 + SSD_HPG], shape)
    for r in range(SSD_HPG - 2, -1, -1):
        out = jnp.where(lane_head[:shape[0]] == r, jnp.broadcast_to(cols[:, h0 + r:h0 + r + 1], shape), out)
    return out


def _select_heads(parts, lane_head):
    out = parts[0]
    for r in range(1, len(parts)):
        out = jnp.where(lane_head == r, parts[r], out)
    return out


def _ssd_kernel(*refs, L, has_init):
    zs_ref, xs_ref, bt_ref, c_ref, dt_ref = refs[:5]
    h0_ref = refs[5] if has_init else None
    dtb_ref, alog_ref, dexp_ref, normw_ref = refs[5 + has_init:9 + has_init]
    yn_ref, ht_ref, st, cb_s = refs[-4:]
    c = pl.program_id(1)
    last = pl.num_programs(1) - 1

    @pl.when(c == 0)
    def _init():
        if has_init:
            for g in range(SSD_GROUPS):
                st[g] = h0_ref[0, g * SSD_GROUP_W:(g + 1) * SSD_GROUP_W, :].T
        else:
            st[...] = jnp.zeros(st.shape, F32)

    rows = dt_ref.shape[0]
    n_chunks = rows // L
    fused = L % LANES == 0
    group_n = lambda g: slice(g * SSD_STATE, (g + 1) * SSD_STATE)

    for ci in range(n_chunks):
        rs = slice(ci * L, (ci + 1) * L)
        for g in range(SSD_GROUPS):
            cb_s[ci * SSD_GROUPS + g] = _dot(c_ref[rs, group_n(g)], bt_ref[0, group_n(g), rs].astype(BF16))

    dt = jax.nn.softplus(dt_ref[...] + dtb_ref[...])
    dta = dt * (-jnp.exp(alog_ref[...]))
    row = lax.broadcasted_iota(jnp.int32, (rows, rows), 0)
    col = lax.broadcasted_iota(jnp.int32, (rows, rows), 1)
    same_chunk_tri = jnp.logical_and(row >= col, row // L == col // L)
    acs = _dot_exact_lhs(same_chunk_tri.astype(BF16), dta) * LOG2_E
    acs_t = acs.T
    dt_t = dt.T
    tri = lax.broadcasted_iota(jnp.int32, (L, L), 0) >= lax.broadcasted_iota(jnp.int32, (L, L), 1)
    lane_head = lax.broadcasted_iota(jnp.int32, (L, SSD_GROUP_W), 1) // SSD_HEAD_DIM
    lane_head_n = lax.broadcasted_iota(jnp.int32, (SSD_STATE, SSD_GROUP_W), 1) // SSD_HEAD_DIM

    for ci in range(n_chunks):
        rs = slice(ci * L, (ci + 1) * L)
        acs_c = acs[rs]
        acs_tc = acs_t[:, rs]
        dt_tc = dt_t[:, rs]
        cdec = jnp.exp2(acs_c[L - 1:L, :])
        wdt_t = jnp.exp2(acs_tc[:, L - 1:L] - acs_tc) * dt_tc
        for g in range(SSD_GROUPS):
            gs = slice(g * SSD_GROUP_W, (g + 1) * SSD_GROUP_W)
            xg = xs_ref[rs, gs]
            xgb = xg.astype(BF16)
            cg = c_ref[rs, group_n(g)].astype(F32)
            cb = cb_s[ci * SSD_GROUPS + g]
            sg = st[g]
            sgb = sg.astype(BF16)
            if fused:
                rhs = jnp.concatenate([xgb, sgb], axis=0)
            parts = []
            for r in range(SSD_HPG):
                h = SSD_HPG * g + r
                a_col = jnp.broadcast_to(acs_c[:, h:h + 1], (L, LANES))
                seg = a_col[:, :L] - acs_tc[h:h + 1, :]
                m = (cb * jnp.exp2(jnp.where(tri, seg, -jnp.inf)) * dt_tc[h:h + 1, :]).astype(BF16)
                ce = (cg * jnp.exp2(a_col)).astype(BF16)
                if fused:
                    parts.append(_dot(jnp.concatenate([m, ce], axis=1), rhs))
                else:
                    parts.append(_dot(m, xgb) + _dot(ce, sgb))
            y = _select_heads(parts, lane_head)
            y = y + dexp_ref[:, gs] * xg
            y = y * zs_ref[rs, gs]
            yn_ref[rs, gs] = _rms(y, normw_ref[:, gs]).astype(yn_ref.dtype)
            btg = bt_ref[0, group_n(g), rs]
            upd = [_dot((btg * wdt_t[SSD_HPG * g + r:SSD_HPG * g + r + 1, :]).astype(BF16), xgb)
                   for r in range(SSD_HPG)]
            st[g] = sg * _expand_heads(cdec, g, lane_head) + _select_heads(upd, lane_head_n)

    @pl.when(c == last)
    def _state_out():
        for g in range(SSD_GROUPS):
            ht_ref[0, 0, g * SSD_GROUP_W:(g + 1) * SSD_GROUP_W, :] = st[g].T


def _ssd_scan(zs, xs, bt, c, dt, h0, w, batch, seq, L, state_stack):
    rows = bt.shape[2]
    nc = seq // rows
    assert nc * rows == seq and rows % L == 0 and L % SUBLANES == 0
    has_init = h0 is not None
    row_spec = lambda wd: pl.BlockSpec((rows, wd), lambda bi, ci: (bi * nc + ci, 0))
    state_spec = pl.BlockSpec((1, SSD_D_INNER, SSD_STATE), lambda bi, ci: (bi, 0, 0))
    bt_spec = pl.BlockSpec((1, SSD_BC, rows), lambda bi, ci: (bi * nc + ci, 0, 0))
    in_specs = [row_spec(SSD_D_INNER), row_spec(SSD_D_INNER), bt_spec, row_spec(SSD_BC), row_spec(LANES)]
    args = [zs, xs, bt, c, dt]
    if has_init:
        in_specs.append(state_spec)
        args.append(h0.reshape(batch, SSD_D_INNER, SSD_STATE))
    consts = [w["dt_bias"], w["a_log"], w["d_exp"], w["norm_w"]]
    prev, layer, count = state_stack
    extra_specs, extra_args, ht_spec, ht_shape = _stacked_out(
        prev, layer, count, (1, SSD_D_INNER, SSD_STATE), lambda bi, ci: (bi, 0, 0),
        (batch, SSD_D_INNER, SSD_STATE), F32)
    in_specs += [_const_spec(a.shape) for a in consts] + extra_specs
    args += consts + extra_args
    return pl.pallas_call(
        functools.partial(_ssd_kernel, L=L, has_init=has_init),
        grid=(batch, nc),
        in_specs=in_specs,
        out_specs=[row_spec(SSD_D_INNER), ht_spec],
        input_output_aliases={len(args) - 1: 1} if extra_args else {},
        out_shape=[jax.ShapeDtypeStruct((batch * seq, SSD_D_INNER), BF16), ht_shape],
        scratch_shapes=[pltpu.VMEM((SSD_GROUPS, SSD_STATE, SSD_GROUP_W), F32),
                        pltpu.VMEM((rows // L * SSD_GROUPS, L, L), F32)],
        compiler_params=_cparams("arbitrary", "arbitrary"),
        name="ssd_scan",
    )(*args)


def _out_ple_kernel(y_ref, h_ref, p_ref, wout_ref, plenorm_ref, wgate_ref, wup_ref, o_ref):
    h1 = h_ref[...] + _dot(y_ref[...], wout_ref[...])
    hn = _rms(h1, plenorm_ref[...]).astype(BF16)
    pb = p_ref[0].astype(BF16)
    for c in range(0, D_MODEL, COL_CHUNK):
        cs = slice(c, c + COL_CHUNK)
        gate = jax.nn.sigmoid(_dot(hn, wgate_ref[:, cs]))
        o_ref[:, cs] = h1[:, cs] + _dot(pb, wup_ref[:, cs]) * gate


def _out_ple(y, h, p, wout, plenorm, wgate, wup, tm):
    m, kd = y.shape
    assert m % tm == 0
    p_all, layer = p
    return pl.pallas_call(
        _out_ple_kernel,
        grid=(m // tm,),
        in_specs=[pl.BlockSpec((tm, kd), lambda i: (i, 0)),
                  pl.BlockSpec((tm, D_MODEL), lambda i: (i, 0)),
                  pl.BlockSpec((1, tm, PLE_DIM), lambda i: (layer, i, 0)),
                  _const_spec(wout.shape), _const_spec(plenorm.shape),
                  _const_spec(wgate.shape), _const_spec(wup.shape)],
        out_specs=pl.BlockSpec((tm, D_MODEL), lambda i: (i, 0)),
        out_shape=jax.ShapeDtypeStruct((m, D_MODEL), F32),
        compiler_params=_cparams("arbitrary"),
        name="out_ple",
    )(y, h, p_all, wout, plenorm, wgate, wup)


def _dot_hi_lo_rhs(x, a):
    hi = x.astype(BF16)
    lo = (x - hi.astype(F32)).astype(BF16)
    return _dot(hi, a) + _dot(lo, a)


def _head_part_matrix(width):
    def part(idx):
        within = idx % HEAD_PAD
        return (idx // HEAD_PAD) * 4 + jnp.where(within < MLA_NOPE, 0, jnp.where(within < MLA_QK, 1, 2))
    rows = part(lax.broadcasted_iota(jnp.int32, (width, width), 0))
    cols_i = lax.broadcasted_iota(jnp.int32, (width, width), 1)
    cols = part(cols_i)
    same = jnp.logical_and(rows == cols, cols_i % HEAD_PAD < MLA_QK)
    lane = lax.broadcasted_iota(jnp.int32, (1, width), 1) % HEAD_PAD
    inv_count = jnp.where(lane < MLA_NOPE, 1.0 / MLA_NOPE, jnp.where(lane < MLA_QK, 1.0 / MLA_ROPE, 0.0))
    return jnp.where(same, 1.0, 0.0).astype(BF16), inv_count


def _head_norm_rope(x, pmat, inv_count, gain_cos, gain_sin):
    inv = lax.rsqrt(_dot_hi_lo_rhs(x * x, pmat) * inv_count + EPS)
    shift = HEAD_PAD - MLA_ROPE
    x_rot = jnp.concatenate([pltpu.roll(x[:, s0:s0 + HEAD_PAD], shift, 1)
                             for s0 in range(0, x.shape[1], HEAD_PAD)], axis=1)
    return (x * inv) * gain_cos + (x_rot * inv) * gain_sin


def _mla_in_kernel(h_ref, lnw_ref, win_ref, qan_ref, wqb_ref, kvan_ref, gq_ref, gqrot_ref, gkr_ref, gkrrot_ref,
                   cos_ref, sin_ref, *rest, q_transposed):
    q_ref, lat_ref, kr_ref, gate_ref = rest[-4:]
    xn = _rms(h_ref[...], lnw_ref[...]).astype(BF16)
    c_q, c_kv, c_kr = 0, MLA_Q_LORA, MLA_Q_LORA + MLA_KV_LORA
    c_gate = c_kr + HEAD_PAD
    pair_w = 2 * HEAD_PAD
    _store_dot(gate_ref, xn, win_ref, c_gate)
    lat_ref[0] = _rms(_dot(xn, win_ref[:, c_kv:c_kr]), kvan_ref[...])
    pmat, inv_count = _head_part_matrix(pair_w)
    cos_t, sin_t = cos_ref[...], sin_ref[...]
    kr = _head_norm_rope(_dot(xn, win_ref[:, c_kr:c_gate]), pmat[:HEAD_PAD, :HEAD_PAD],
                         inv_count[:, :HEAD_PAD], gkr_ref[...] * cos_t[:, :HEAD_PAD],
                         gkrrot_ref[...] * sin_t[:, :HEAD_PAD])
    kr_ref[...] = kr[:, MLA_NOPE:MLA_QK]
    qa = _rms(_dot(xn, win_ref[:, c_q:c_kv]), qan_ref[...]).astype(BF16)
    gain_cos = gq_ref[...] * cos_t * SOFTMAX_LOG2_SCALE
    gain_sin = gqrot_ref[...] * sin_t * SOFTMAX_LOG2_SCALE
    for pair in range(MLA_HEADS // 2):
        qq = _dot(qa, wqb_ref[:, pair * pair_w:(pair + 1) * pair_w])
        qh = _head_norm_rope(qq, pmat, inv_count, gain_cos, gain_sin)
        if q_transposed:
            q_ref[0, pair * pair_w:(pair + 1) * pair_w, :] = qh.T.astype(q_ref.dtype)
        else:
            q_ref[:, pair * pair_w:(pair + 1) * pair_w] = qh.astype(q_ref.dtype)


def _stacked_out(prev, layer, count, block, index_tail, shape_tail, dtype):
    spec = pl.BlockSpec((1,) + block, lambda *g: (layer,) + index_tail(*g))
    shape = jax.ShapeDtypeStruct((count,) + shape_tail, dtype)
    if prev is None:
        return [], [], spec, shape
    return [pl.BlockSpec(memory_space=pl.ANY)], [prev], spec, shape


def _mla_in(h, lnw, w, rope, tm, q_transposed, lat_stack):
    m = h.shape[0]
    assert m % tm == 0
    cos_t, sin_t = rope
    period = cos_t.shape[0] // tm
    assert period * tm == cos_t.shape[0]
    rope_spec = pl.BlockSpec((tm, 2 * HEAD_PAD), lambda i: (i % period, 0))
    consts = [lnw, w["in_w"], w["q_a_norm"], w["q_b_w"], w["kv_a_norm"],
              w["gain_q"], w["gain_q_rot"], w["gain_kr"], w["gain_kr_rot"]]
    row = lambda wd: pl.BlockSpec((tm, wd), lambda i: (i, 0))
    qw = MLA_HEADS * HEAD_PAD
    if q_transposed:
        seq = period * tm
        assert m % seq == 0
        q_spec = pl.BlockSpec((1, qw, tm), lambda i: (i // period, 0, i % period))
        q_shape = jax.ShapeDtypeStruct((m // seq, qw, seq), BF16)
    else:
        q_spec, q_shape = row(qw), jax.ShapeDtypeStruct((m, qw), BF16)
    prev, layer, count = lat_stack
    extra_specs, extra_args, lat_spec, lat_shape = _stacked_out(
        prev, layer, count, (tm, MLA_KV_LORA), lambda i: (i, 0), (m, MLA_KV_LORA), F32)
    n_in = 1 + len(consts) + 2
    return pl.pallas_call(
        functools.partial(_mla_in_kernel, q_transposed=q_transposed),
        grid=(m // tm,),
        in_specs=[row(D_MODEL)] + [_const_spec(a.shape) for a in consts] + [rope_spec] * 2 + extra_specs,
        out_specs=[q_spec, lat_spec, row(MLA_ROPE), row(MLA_WIDTH)],
        input_output_aliases={n_in: 1} if extra_args else {},
        out_shape=[q_shape,
                   lat_shape,
                   jax.ShapeDtypeStruct((m, MLA_ROPE), F32),
                   jax.ShapeDtypeStruct((m, MLA_WIDTH), F32)],
        compiler_params=_cparams("arbitrary"),
        name="mla_in",
    )(h, *consts, cos_t, sin_t, *extra_args)


def _mla_kv_kernel(lat_ref, kr_ref, wk_ref, wv_ref, gkn_ref, k_ref, v_ref, *, v_transposed):
    latb = lat_ref[0].astype(BF16)
    if v_transposed:
        ones = jnp.ones((V_ONES_ROWS, latb.shape[0]), v_ref.dtype)
        for c in range(0, MLA_WIDTH, COL_CHUNK):
            vt = _dot(latb, wv_ref[:, c:c + COL_CHUNK]).T.astype(v_ref.dtype)
            for hh in range(COL_CHUNK // MLA_V):
                r0 = (c // MLA_V + hh) * V_SLOT_ROWS
                v_ref[0, r0:r0 + MLA_V, :] = vt[hh * MLA_V:(hh + 1) * MLA_V]
                v_ref[0, r0 + MLA_V:r0 + V_SLOT_ROWS, :] = ones
    else:
        _store_dot(v_ref, latb, wv_ref)
    rows = latb.shape[0]
    kr = jnp.concatenate([jnp.zeros((rows, MLA_NOPE), F32), kr_ref[...],
                          jnp.zeros((rows, HEAD_PAD - MLA_QK), F32)], axis=1)
    for h in range(MLA_HEADS):
        hs = slice(h * HEAD_PAD, (h + 1) * HEAD_PAD)
        kh = _dot(latb, wk_ref[:, hs])
        ms = jnp.sum(kh * kh, axis=-1, keepdims=True) * (1.0 / MLA_NOPE)
        k_ref[:, hs] = (kh * lax.rsqrt(ms + EPS) * gkn_ref[...] + kr).astype(k_ref.dtype)


def _mla_kv(lat, layer, kr, w, tm, v_transposed):
    m = lat.shape[1]
    assert m % tm == 0
    consts = [w["kv_k_w"], w["kv_v_w"], w["gain_kn"]]
    row = lambda wd: pl.BlockSpec((tm, wd), lambda i: (i, 0))
    if v_transposed:
        v_spec = pl.BlockSpec((1, MLA_HEADS * V_SLOT_ROWS, tm), lambda i: (i, 0, 0))
        v_shape = jax.ShapeDtypeStruct((m // tm, MLA_HEADS * V_SLOT_ROWS, tm), BF16)
    else:
        v_spec, v_shape = row(MLA_WIDTH), jax.ShapeDtypeStruct((m, MLA_WIDTH), BF16)
    return pl.pallas_call(
        functools.partial(_mla_kv_kernel, v_transposed=v_transposed),
        grid=(m // tm,),
        in_specs=[pl.BlockSpec((1, tm, MLA_KV_LORA), lambda i: (layer, i, 0)), row(MLA_ROPE)]
                 + [_const_spec(a.shape) for a in consts],
        out_specs=[row(MLA_HEADS * HEAD_PAD), v_spec],
        out_shape=[jax.ShapeDtypeStruct((m, MLA_HEADS * HEAD_PAD), BF16), v_shape],
        compiler_params=_cparams("arbitrary"),
        name="mla_kv",
    )(lat, kr, *consts)


def _gated_pair_out(o_ref, gate_ref, outs, pair):
    lane = lax.broadcasted_iota(jnp.int32, outs[0].shape, 1)
    ps = slice(pair * LANES, (pair + 1) * LANES)
    g = gate_ref[:, ps]
    o_ref[:, ps] = (jnp.where(lane < MLA_V, outs[0], outs[1]) * (g * jax.nn.sigmoid(g))).astype(o_ref.dtype)


def _attn_kernel(q_ref, k_ref, v_ref, gate_ref, o_ref, m_s, acc_s, s_s, *, tq):
    i = pl.program_id(1)
    key_c = lax.broadcasted_iota(jnp.int32, (tq, tq), 0) // CHUNK
    qry_c = lax.broadcasted_iota(jnp.int32, (tq, tq), 1) // CHUNK
    diag_mask = key_c <= qry_c
    m_s[...] = jnp.full(m_s.shape, NEG_BIG, F32)
    acc_s[...] = jnp.zeros(acc_s.shape, F32)

    def tile_step(j, mask):
        rows = pl.ds(pl.multiple_of(j * tq, tq), tq)
        for h in range(MLA_HEADS):
            hs = slice(h * HEAD_PAD, (h + 1) * HEAD_PAD)
            s_s[h] = _dot(k_ref[rows, hs], q_ref[0, hs, :])
        for h in range(MLA_HEADS):
            s = s_s[h]
            if mask is not None:
                s = jnp.where(mask, s, -jnp.inf)
            m_prev = m_s[h]
            m_new = jnp.maximum(m_prev, jnp.max(s, axis=0, keepdims=True))
            alpha = jnp.exp2(m_prev[0:1, :] - m_new[0:1, :])
            p = jnp.exp2(s - m_new[0:1, :]).astype(BF16)
            pv = _dot(v_ref[j, h * V_SLOT_ROWS:(h + 1) * V_SLOT_ROWS, :], p)
            acc_s[h] = alpha * acc_s[h] + pv
            m_s[h] = m_new

    def body(j, carry):
        tile_step(j, None)
        return carry

    lax.fori_loop(0, i, body, 0)
    tile_step(i, diag_mask)
    for pair in range(MLA_HEADS // 2):
        o_t = jnp.concatenate([acc_s[2 * pair + sub, 0:MLA_V, :] / acc_s[2 * pair + sub, MLA_V:MLA_V + 1, :]
                               for sub in range(2)], axis=0)
        ps = slice(pair * LANES, (pair + 1) * LANES)
        g = gate_ref[:, ps]
        o_ref[:, ps] = (o_t.T * (g * jax.nn.sigmoid(g))).astype(o_ref.dtype)


def _attn_prompt(q_t, k, v_t, gate, batch, seq, tq):
    nq = seq // tq
    assert nq * tq == seq and tq % CHUNK == 0
    return pl.pallas_call(
        functools.partial(_attn_kernel, tq=tq),
        grid=(batch, nq),
        in_specs=[pl.BlockSpec((1, MLA_HEADS * HEAD_PAD, tq), lambda b, i: (b, 0, i)),
                  pl.BlockSpec((seq, MLA_HEADS * HEAD_PAD), lambda b, i: (b, 0)),
                  pl.BlockSpec((nq, MLA_HEADS * V_SLOT_ROWS, tq), lambda b, i: (b, 0, 0)),
                  pl.BlockSpec((tq, MLA_WIDTH), lambda b, i: (b * nq + i, 0))],
        out_specs=pl.BlockSpec((tq, MLA_WIDTH), lambda b, i: (b * nq + i, 0)),
        out_shape=jax.ShapeDtypeStruct((batch * seq, MLA_WIDTH), BF16),
        scratch_shapes=[pltpu.VMEM((MLA_HEADS, SUBLANES, tq), F32),
                        pltpu.VMEM((MLA_HEADS, V_SLOT_ROWS, tq), F32), pltpu.VMEM((MLA_HEADS, tq, tq), F32)],
        compiler_params=_cparams("arbitrary", "arbitrary"),
        name="attn_prompt",
    )(q_t, k, v_t, gate)


def _attn_step_kernel(q_ref, k_ref, v_ref, gate_ref, o_ref, s_s, *, q_pos0, n_keys):
    tq = q_ref.shape[0]
    tk = k_ref.shape[1]
    k_pos = lax.broadcasted_iota(jnp.int32, (tq, tk), 1)
    q_pos = lax.broadcasted_iota(jnp.int32, (tq, tk), 0) + q_pos0
    mask = jnp.logical_and(k_pos < n_keys, k_pos // CHUNK <= q_pos // CHUNK)
    for h in range(MLA_HEADS):
        hs = slice(h * HEAD_PAD, (h + 1) * HEAD_PAD)
        s_s[h] = _dot_nt(q_ref[:, hs], k_ref[0, :, hs])
    for pair in range(MLA_HEADS // 2):
        vs = slice(pair * LANES, (pair + 1) * LANES)
        outs = []
        for sub in range(2):
            s = jnp.where(mask, s_s[2 * pair + sub], -jnp.inf)
            p = jnp.exp2(s - jnp.max(s, axis=-1, keepdims=True))
            outs.append(_dot(p.astype(BF16), v_ref[0, :, vs]) / jnp.sum(p, axis=-1, keepdims=True))
        _gated_pair_out(o_ref, gate_ref, outs, pair)


def _attn_step(q, k, v, gate, batch, tq, n_keys):
    tk = k.shape[1]
    return pl.pallas_call(
        functools.partial(_attn_step_kernel, q_pos0=n_keys - tq, n_keys=n_keys),
        grid=(batch,),
        in_specs=[pl.BlockSpec((tq, MLA_HEADS * HEAD_PAD), lambda b: (b, 0)),
                  pl.BlockSpec((1, tk, MLA_HEADS * HEAD_PAD), lambda b: (b, 0, 0)),
                  pl.BlockSpec((1, tk, MLA_WIDTH), lambda b: (b, 0, 0)),
                  pl.BlockSpec((tq, MLA_WIDTH), lambda b: (b, 0))],
        out_specs=pl.BlockSpec((tq, MLA_WIDTH), lambda b: (b, 0)),
        out_shape=jax.ShapeDtypeStruct((batch * tq, MLA_WIDTH), BF16),
        scratch_shapes=[pltpu.VMEM((MLA_HEADS, tq, tk), F32)],
        compiler_params=_cparams("arbitrary"),
        name="attn_step",
    )(q, k, v, gate)


def _pad_cols(a, width):
    return jnp.pad(a, ((0, 0), (0, width - a.shape[1])))


def _ssd_weights(j, in_w, conv_w, conv_b, dt_bias, a_log, d, norm_w, out_w):
    zx = SSD_D_INNER + SSD_CONV_DIM
    return {
        "in_w": jnp.concatenate([0.5 * in_w[j][:, :SSD_D_INNER], in_w[j][:, SSD_D_INNER:zx],
                                 _pad_cols(in_w[j][:, zx:], LANES)], axis=1).astype(BF16),
        "conv_w": 0.5 * conv_w[j],
        "conv_b": 0.5 * conv_b[j][None, :],
        "dt_bias": _pad_cols(dt_bias[j][None, :], LANES),
        "a_log": _pad_cols(a_log[j][None, :], LANES),
        "d_exp": jnp.repeat(d[j], SSD_HEAD_DIM)[None, :],
        "norm_w": norm_w[j][None, :],
        "out_w": out_w[j].astype(BF16),
    }


def _head_pad_cols(w, per_head, take, offset=0):
    k = w.shape[0]
    w = w.reshape(k, MLA_HEADS, per_head)[:, :, offset:offset + take]
    return jnp.pad(w, ((0, 0), (0, 0), (0, HEAD_PAD - take))).reshape(k, MLA_HEADS * HEAD_PAD)


def _rotate_half(a):
    half = MLA_ROPE // 2
    return jnp.concatenate([-a[..., half:], a[..., :half]], axis=-1)


def _swap_halves(a):
    half = MLA_ROPE // 2
    return jnp.concatenate([a[..., half:], a[..., :half]], axis=-1)


def _rope_slot(a):
    pad = [(0, 0)] * (a.ndim - 1) + [(MLA_NOPE, HEAD_PAD - MLA_QK)]
    return jnp.pad(a, pad)


def _q_b_cols(w):
    k = w.shape[0]
    w = w.reshape(k, MLA_HEADS, MLA_QK)
    return jnp.concatenate([w, _rotate_half(w[:, :, MLA_NOPE:])], axis=-1).reshape(k, MLA_HEADS * HEAD_PAD)


def _mla_weights(j, in_w, q_a_norm, q_b_w, kv_a_norm, kv_b_w, qn, qr, kn, kr, out_w):
    c1, c2, c3 = MLA_Q_LORA, MLA_Q_LORA + MLA_KV_LORA, MLA_Q_LORA + MLA_KV_LORA + MLA_ROPE
    w = in_w[j]
    kr_cols = jnp.concatenate([jnp.zeros((w.shape[0], MLA_NOPE), F32), w[:, c2:c3],
                               _rotate_half(w[:, c2:c3])], axis=1)
    zeros = lambda n: jnp.zeros((n,), F32)
    gain_q = jnp.concatenate([qn[j], qr[j], zeros(HEAD_PAD - MLA_QK)])
    return {
        "in_w": jnp.concatenate([w[:, :c2], kr_cols, w[:, c3:]], axis=1).astype(BF16),
        "q_a_norm": q_a_norm[j][None, :],
        "q_b_w": _q_b_cols(q_b_w[j]).astype(BF16),
        "gain_q": jnp.tile(gain_q, 2)[None, :],
        "gain_q_rot": jnp.tile(_rope_slot(_swap_halves(qr[j])), 2)[None, :],
        "gain_kr_rot": _rope_slot(_swap_halves(kr[j]))[None, :],
        "kv_a_norm": kv_a_norm[j][None, :],
        "kv_k_w": _head_pad_cols(kv_b_w[j], MLA_NOPE + MLA_V, MLA_NOPE).astype(BF16),
        "kv_v_w": kv_b_w[j].reshape(MLA_KV_LORA, MLA_HEADS, MLA_NOPE + MLA_V)[:, :, MLA_NOPE:]
                  .reshape(MLA_KV_LORA, MLA_WIDTH).astype(BF16),
        "gain_kr": _rope_slot(kr[j])[None, :],
        "gain_kn": jnp.concatenate([kn[j], zeros(HEAD_PAD - MLA_NOPE)])[None, :],
        "out_w": out_w[j].astype(BF16),
    }


def _rope_tables(pos0, seq, rows):
    inv = 1.0 / (ROPE_BASE ** (jnp.arange(0, MLA_ROPE, 2, dtype=F32) / MLA_ROPE))
    ang = (pos0 + jnp.arange(seq)).astype(F32)[:, None] * inv[None, :]
    cos, sin = jnp.cos(ang), jnp.sin(ang)
    z = lambda n: jnp.zeros((seq, n), F32)
    cos_t = jnp.concatenate([jnp.ones((seq, MLA_NOPE), F32), cos, cos, z(HEAD_PAD - MLA_QK)], axis=1)
    sin_t = jnp.concatenate([z(MLA_NOPE), sin, sin, z(HEAD_PAD - MLA_QK)], axis=1)
    reps = max(1, rows // seq)
    return tuple(jnp.tile(t, (reps, 2)) for t in (cos_t, sin_t))


def _row_tile(m, want):
    tm = min(m, want)
    assert m % tm == 0
    return tm


def _ssd_layer(h, p, lnw, ple, w, batch, seq, chunk, conv_prev, h0, state_stack):
    tm = _row_tile(h.shape[0], ROW_TILE)
    zs, xs, bt, c, dt, conv_new = _ssd_in(h, lnw, w, conv_prev, batch, seq, min(seq, SSD_BLOCK_ROWS))
    yn, states = _ssd_scan(zs, xs, bt, c, dt, h0, w, batch, seq, chunk, state_stack)
    h = _out_ple(yn, h, p, w["out_w"], *ple, tm)
    return h, conv_new, states


def _mla_layer(h, p, lnw, ple, w, batch, seq, lat_past, kr_past, lat_stack):
    tm = _row_tile(h.shape[0], ROW_TILE)
    prompt = lat_past is None
    tm_in = _row_tile(h.shape[0], min(seq, MLA_IN_TILE) if prompt else MLA_IN_TILE)
    pos0 = 0 if prompt else lat_past.shape[1]
    rope = _rope_tables(pos0, seq, tm_in)
    q, lats, kr, gate = _mla_in(h, lnw, w, rope, tm_in, q_transposed=prompt, lat_stack=lat_stack)
    layer = lat_stack[1]
    if prompt:
        tile = min(seq, ATTN_TILE)
        k, v_t = _mla_kv(lats, layer, kr, w, tile, v_transposed=True)
        og = _attn_prompt(q, k, v_t, gate, batch, seq, tile)
    else:
        n_keys = pos0 + seq
        tk = -(-n_keys // LANES) * LANES
        pad = ((0, 0), (0, tk - n_keys), (0, 0))
        lat_all = jnp.pad(jnp.concatenate([lat_past, lats[layer].reshape(batch, seq, -1)], axis=1), pad)
        kr_all = jnp.pad(jnp.concatenate([kr_past, kr.reshape(batch, seq, -1)], axis=1), pad)
        k, v = _mla_kv(lat_all.reshape(1, batch * tk, -1), 0, kr_all.reshape(batch * tk, -1), w,
                       math.gcd(batch * tk, 512), v_transposed=False)
        og = _attn_step(q, k.reshape(batch, tk, -1), v.reshape(batch, tk, -1), gate, batch, seq, n_keys)
    h = _out_ple(og, h, p, w["out_w"], *ple, tm)
    return h, lats, kr.reshape(batch, seq, -1)


def kernel(x_prompt, x_sample, cache_conv, state_ssm, cache_kv_latent, cache_k_rope, p_prompt, p_sample, ln_w, ssd_in_w, ssd_conv_w, ssd_conv_b, ssd_dt_bias, ssd_A_log, ssd_D, ssd_norm_w, ssd_out_w, mla_in_w, mla_q_a_norm, mla_q_b_w, mla_kv_a_norm, mla_kv_b_w, mla_q_nope_norm, mla_q_rope_norm, mla_k_nope_norm, mla_k_rope_norm, mla_out_w, ple_up_w, ple_norm_w, ple_gate_w):
    bp, tp, d = x_prompt.shape
    bs, ts, _ = x_sample.shape
    hp = x_prompt.reshape(bp * tp, d)
    hs = x_sample.reshape(bs * ts, d)
    conv_p, kr_p, conv_s, kr_s = [], [], [], []
    lat_p = lat_s = ssm_p = ssm_s = None
    for i in range(DEPTH):
        j = i // 2
        lnw = ln_w[i][None, :]
        ple = (ple_norm_w[i][None, :], ple_gate_w[i].astype(BF16), ple_up_w[i].astype(BF16))
        pp = (p_prompt.reshape(DEPTH, bp * tp, PLE_DIM), i)
        ps = (p_sample.reshape(DEPTH, bs * ts, PLE_DIM), i)
        if i % 2 == 0:
            w = _ssd_weights(j, ssd_in_w, ssd_conv_w, ssd_conv_b, ssd_dt_bias, ssd_A_log, ssd_D,
                             ssd_norm_w, ssd_out_w)
            n_ssd = (DEPTH + 1) // 2
            hp, cp, ssm_p = _ssd_layer(hp, pp, lnw, ple, w, bp, tp, min(tp, SSD_SCAN_CHUNK), None, None,
                                       (ssm_p, j, n_ssd))
            hs, cs, ssm_s = _ssd_layer(hs, ps, lnw, ple, w, bs, ts, ts, cache_conv[j], state_ssm[j],
                                       (ssm_s, j, n_ssd))
            conv_p.append(cp); conv_s.append(cs)
        else:
            w = _mla_weights(j, mla_in_w, mla_q_a_norm, mla_q_b_w, mla_kv_a_norm, mla_kv_b_w,
                             mla_q_nope_norm, mla_q_rope_norm, mla_k_nope_norm, mla_k_rope_norm, mla_out_w)
            n_mla = DEPTH // 2
            hp, lat_p, rp = _mla_layer(hp, pp, lnw, ple, w, bp, tp, None, None, (lat_p, j, n_mla))
            hs, lat_s, rs = _mla_layer(hs, ps, lnw, ple, w, bs, ts, cache_kv_latent[j], cache_k_rope[j],
                                       (lat_s, j, n_mla))
            kr_p.append(rp); kr_s.append(rs)
    return (hp.reshape(bp, tp, d), hs.reshape(bs, ts, d),
            jnp.stack(conv_p), ssm_p.reshape(-1, bp, SSD_HEADS, SSD_HEAD_DIM, SSD_STATE),
            lat_p.reshape(-1, bp, tp, MLA_KV_LORA), jnp.stack(kr_p),
            jnp.stack(conv_s), ssm_s.reshape(-1, bs, SSD_HEADS, SSD_HEAD_DIM, SSD_STATE),
            lat_s.reshape(-1, bs, ts, MLA_KV_LORA), jnp.stack(kr_s))
```

```python
import functools
import math

import jax
import jax.numpy as jnp
from jax import lax
from jax.experimental import pallas as pl
from jax.experimental.pallas import tpu as pltpu

F32 = jnp.float32
BF16 = jnp.bfloat16

D_MODEL = 1024
DEPTH = 4
CHUNK = 64
PLE_DIM = 256
EPS = 1e-6

SSD_D_INNER = 2 * D_MODEL
SSD_HEAD_DIM = 64
SSD_HEADS = SSD_D_INNER // SSD_HEAD_DIM
SSD_GROUPS = 8
SSD_HPG = SSD_HEADS // SSD_GROUPS
SSD_STATE = 128
SSD_CONV_W = 4
SSD_BC = SSD_GROUPS * SSD_STATE
SSD_CONV_DIM = SSD_D_INNER + 2 * SSD_BC
SSD_GROUP_W = SSD_HPG * SSD_HEAD_DIM

MLA_HEADS = 16
MLA_NOPE = 64
MLA_ROPE = 32
MLA_V = 64
MLA_Q_LORA = 384
MLA_KV_LORA = 256
MLA_QK = MLA_NOPE + MLA_ROPE
MLA_WIDTH = MLA_HEADS * MLA_V
ROPE_BASE = 10000.0

LANES = 128
SUBLANES = 8
HEAD_PAD = LANES
VMEM_V7X_BYTES = 64 * 1024 * 1024
VMEM_LIMIT_BYTES = VMEM_V7X_BYTES * 13 // 16
NEG_BIG = -1e30
LOG2_E = math.log2(math.e)
SOFTMAX_LOG2_SCALE = (MLA_QK ** -0.5) * LOG2_E
COL_CHUNK = 512
CONV_ROWS = 128
V_ONES_ROWS = 16
V_SLOT_ROWS = MLA_V + V_ONES_ROWS
ROW_TILE = 512
MLA_IN_TILE = 1024
ATTN_TILE = 256
SSD_BLOCK_ROWS = 256
SSD_SCAN_ROWS = 512
SSD_SCAN_CHUNK = 128


def _cparams(*sem):
    return pltpu.CompilerParams(dimension_semantics=sem, vmem_limit_bytes=VMEM_LIMIT_BYTES)


def _const_spec(shape):
    nd = len(shape)
    return pl.BlockSpec(shape, lambda *_: (0,) * nd, pipeline_mode=pl.Buffered(1))


def _rms(x, w):
    return x * lax.rsqrt(jnp.mean(x * x, axis=-1, keepdims=True) + EPS) * w


def _dot(a, b):
    return jnp.dot(a, b, preferred_element_type=F32)


def _dot_nt(a, b):
    return lax.dot_general(a, b, (((1,), (1,)), ((), ())), preferred_element_type=F32)


def _store_dot(o_ref, a, w_ref, col0=0):
    width = o_ref.shape[1]
    for c in range(0, width, COL_CHUNK):
        cw = min(COL_CHUNK, width - c)
        o_ref[:, c:c + cw] = _dot(a, w_ref[:, col0 + c:col0 + c + cw]).astype(o_ref.dtype)


def _ssd_in_kernel(*refs, tiles_per_seq, has_init):
    if has_init:
        (x_ref, lnw_ref, w_ref, convw_ref, convb_ref, cprev_ref,
         z_ref, xs_ref, b_ref, c_ref, dt_ref, convnew_ref, ext) = refs
    else:
        (x_ref, lnw_ref, w_ref, convw_ref, convb_ref,
         z_ref, xs_ref, b_ref, c_ref, dt_ref, convnew_ref, ext) = refs
    tm = x_ref.shape[0]
    tail = SSD_CONV_W - 1
    t = pl.program_id(0) % tiles_per_seq
    xn = _rms(x_ref[...], lnw_ref[...]).astype(BF16)

    @pl.when(t == 0)
    def _init():
        ext[0:SUBLANES, :] = jnp.zeros((SUBLANES, SSD_CONV_DIM), F32)
        if has_init:
            ext[SUBLANES - tail:SUBLANES, :] = cprev_ref[0]

    dt_ref[...] = _dot(xn, w_ref[:, SSD_D_INNER + SSD_CONV_DIM:])
    def project(c0):
        cs = slice(c0, c0 + COL_CHUNK)
        ext[SUBLANES:SUBLANES + tm, cs] = _dot(xn, w_ref[:, SSD_D_INNER + c0:SSD_D_INNER + c0 + COL_CHUNK])
        convnew_ref[0, :, cs] = ext[pl.ds(SUBLANES + tm - tail, tail), cs]

    project(0)
    for c0 in range(0, SSD_CONV_DIM, COL_CHUNK):
        cs = slice(c0, c0 + COL_CHUNK)
        if c0 + COL_CHUNK < SSD_CONV_DIM:
            project(c0 + COL_CHUNK)
        if c0 % (2 * COL_CHUNK) == 0:
            zc = slice(c0 // 2, c0 // 2 + COL_CHUNK)
            hz = _dot(xn, w_ref[:, zc])
            z_ref[:, zc] = hz * jnp.tanh(hz) + hz
        if c0 < SSD_D_INNER:
            o_ref, o0 = xs_ref, c0
        elif c0 < SSD_D_INNER + SSD_BC:
            o_ref, o0 = b_ref, c0 - SSD_D_INNER
        else:
            o_ref, o0 = c_ref, c0 - SSD_D_INNER - SSD_BC
        for r0 in range(0, tm, CONV_ROWS):
            rb = min(CONV_ROWS, tm - r0)
            for l0 in range(0, COL_CHUNK, LANES):
                ls = slice(c0 + l0, c0 + l0 + LANES)
                e = ext[r0:r0 + rb + SUBLANES, ls]
                e1 = pltpu.roll(e, 1, 0)
                far = e * convw_ref[1:2, ls] + e1 * convw_ref[0:1, ls]
                near = e * convw_ref[3:4, ls] + e1 * convw_ref[2:3, ls]
                hv = (convb_ref[:, ls] + near[SUBLANES:SUBLANES + rb]) + pltpu.roll(far, 2, 0)[SUBLANES:SUBLANES + rb]
                y = hv * jnp.tanh(hv) + hv
                if o_ref is b_ref:
                    b_ref[0, o0 + l0:o0 + l0 + LANES, r0:r0 + rb] = y.T
                else:
                    o_ref[r0:r0 + rb, o0 + l0:o0 + l0 + LANES] = y.astype(o_ref.dtype)
    ext[0:SUBLANES, :] = ext[tm:tm + SUBLANES, :]


def _ssd_in(x, lnw, w, conv_prev, batch, seq, tm):
    m, d = x.shape
    assert seq % tm == 0 and tm % SUBLANES == 0 and m == batch * seq
    tps = seq // tm
    has_init = conv_prev is not None
    tail = SSD_CONV_W - 1
    consts = [lnw, w["in_w"], w["conv_w"], w["conv_b"]]
    row = lambda wd: pl.BlockSpec((tm, wd), lambda i: (i, 0))
    conv_spec = pl.BlockSpec((1, tail, SSD_CONV_DIM), lambda i: (i // tps, 0, 0))
    in_specs = [row(d)] + [_const_spec(a.shape) for a in consts]
    args = [x] + consts
    if has_init:
        in_specs.append(conv_spec)
        args.append(conv_prev)
    return pl.pallas_call(
        functools.partial(_ssd_in_kernel, tiles_per_seq=tps, has_init=has_init),
        grid=(m // tm,),
        in_specs=in_specs,
        out_specs=[row(SSD_D_INNER), row(SSD_D_INNER), pl.BlockSpec((1, SSD_BC, tm), lambda i: (i, 0, 0)),
                   row(SSD_BC), row(LANES), conv_spec],
        out_shape=[jax.ShapeDtypeStruct((m, SSD_D_INNER), F32),
                   jax.ShapeDtypeStruct((m, SSD_D_INNER), F32),
                   jax.ShapeDtypeStruct((m // tm, SSD_BC, tm), F32),
                   jax.ShapeDtypeStruct((m, SSD_BC), BF16),
                   jax.ShapeDtypeStruct((m, LANES), F32),
                   jax.ShapeDtypeStruct((batch, tail, SSD_CONV_DIM), F32)],
        scratch_shapes=[pltpu.VMEM((tm + SUBLANES, SSD_CONV_DIM), F32)],
        compiler_params=_cparams("arbitrary"),
        name="ssd_in",
    )(*args)


def _dot_exact_lhs(a, x):
    x1 = x.astype(BF16)
    r1 = x - x1.astype(F32)
    x2 = r1.astype(BF16)
    x3 = (r1 - x2.astype(F32)).astype(BF16)
    return _dot(a, x1) + _dot(a, x2) + _dot(a, x3)


def _expand_heads(cols, g, lane_head):
    shape = (cols.shape[0], SSD_GROUP_W)
    h0 = SSD_HPG * g
    out = jnp.broadcast_to(cols[:, h0 + SSD_HPG - 1:h0 + SSD_HPG], shape)
    for r in range(SSD_HPG - 2, -1, -1):
        out = jnp.where(lane_head[:shape[0]] == r, jnp.broadcast_to(cols[:, h0 + r:h0 + r + 1], shape), out)
    return out


def _select_heads(parts, lane_head):
    out = parts[0]
    for r in range(1, len(parts)):
        out = jnp.where(lane_head == r, parts[r], out)
    return out


def _ssd_kernel(*refs, L, has_init):
    zs_ref, xs_ref, bt_ref, c_ref, dt_ref = refs[:5]
    h0_ref = refs[5] if has_init else None
    dtb_ref, alog_ref, dexp_ref, normw_ref = refs[5 + has_init:9 + has_init]
    yn_ref, ht_ref, st, cb_s = refs[-4:]
    c = pl.program_id(1)
    last = pl.num_programs(1) - 1

    @pl.when(c == 0)
    def _init():
        if has_init:
            for g in range(SSD_GROUPS):
                st[g] = h0_ref[0, g * SSD_GROUP_W:(g + 1) * SSD_GROUP_W, :].T
        else:
            st[...] = jnp.zeros(st.shape, F32)

    rows = dt_ref.shape[0]
    n_chunks = rows // L
    fused = L % LANES == 0
    group_n = lambda g: slice(g * SSD_STATE, (g + 1) * SSD_STATE)
    bt_rows = bt_ref.shape[2]

    def bt_chunk(g, ci):
        r0 = ci * L
        return bt_ref[r0 // bt_rows, group_n(g), r0 % bt_rows:r0 % bt_rows + L]

    for ci in range(n_chunks):
        rs = slice(ci * L, (ci + 1) * L)
        for g in range(SSD_GROUPS):
            cb_s[ci * SSD_GROUPS + g] = _dot(c_ref[rs, group_n(g)], bt_chunk(g, ci).astype(BF16))

    dt = jax.nn.softplus(dt_ref[...] + dtb_ref[...])
    dta = dt * (-jnp.exp(alog_ref[...]))
    row = lax.broadcasted_iota(jnp.int32, (rows, rows), 0)
    col = lax.broadcasted_iota(jnp.int32, (rows, rows), 1)
    same_chunk_tri = jnp.logical_and(row >= col, row // L == col // L)
    acs = _dot_exact_lhs(same_chunk_tri.astype(BF16), dta) * LOG2_E
    acs_t = acs.T
    dt_t = dt.T
    tri = lax.broadcasted_iota(jnp.int32, (L, L), 0) >= lax.broadcasted_iota(jnp.int32, (L, L), 1)
    lane_head = lax.broadcasted_iota(jnp.int32, (L, SSD_GROUP_W), 1) // SSD_HEAD_DIM
    lane_head_n = lax.broadcasted_iota(jnp.int32, (SSD_STATE, SSD_GROUP_W), 1) // SSD_HEAD_DIM

    for ci in range(n_chunks):
        rs = slice(ci * L, (ci + 1) * L)
        acs_c = acs[rs]
        acs_tc = acs_t[:, rs]
        dt_tc = dt_t[:, rs]
        cdec = jnp.exp2(acs_c[L - 1:L, :])
        wdt_t = jnp.exp2(acs_tc[:, L - 1:L] - acs_tc) * dt_tc
        for g in range(SSD_GROUPS):
            gs = slice(g * SSD_GROUP_W, (g + 1) * SSD_GROUP_W)
            xg = xs_ref[rs, gs]
            xgb = xg.astype(BF16)
            cg = c_ref[rs, group_n(g)].astype(F32)
            cb = cb_s[ci * SSD_GROUPS + g]
            sg = st[g]
            sgb = sg.astype(BF16)
            if fused:
                rhs = jnp.concatenate([xgb, sgb], axis=0)
            parts = []
            for r in range(SSD_HPG):
                h = SSD_HPG * g + r
                a_col = jnp.broadcast_to(acs_c[:, h:h + 1], (L, LANES))
                seg = a_col[:, :L] - acs_tc[h:h + 1, :]
                m = (cb * jnp.exp2(jnp.where(tri, seg, -jnp.inf)) * dt_tc[h:h + 1, :]).astype(BF16)
                ce = (cg * jnp.exp2(a_col)).astype(BF16)
                if fused:
                    parts.append(_dot(jnp.concatenate([m, ce], axis=1), rhs))
                else:
                    parts.append(_dot(m, xgb) + _dot(ce, sgb))
            y = _select_heads(parts, lane_head)
            y = y + dexp_ref[:, gs] * xg
            y = y * zs_ref[rs, gs]
            yn_ref[rs, gs] = _rms(y, normw_ref[:, gs]).astype(yn_ref.dtype)
            btg = bt_chunk(g, ci)
            upd = [_dot((btg * wdt_t[SSD_HPG * g + r:SSD_HPG * g + r + 1, :]).astype(BF16), xgb)
                   for r in range(SSD_HPG)]
            st[g] = sg * _expand_heads(cdec, g, lane_head) + _select_heads(upd, lane_head_n)

    @pl.when(c == last)
    def _state_out():
        for g in range(SSD_GROUPS):
            ht_ref[0, 0, g * SSD_GROUP_W:(g + 1) * SSD_GROUP_W, :] = st[g].T


def _ssd_scan(zs, xs, bt, c, dt, h0, w, batch, seq, L, state_stack):
    slab = bt.shape[2]
    rows = min(seq, max(slab, SSD_SCAN_ROWS))
    nc = seq // rows
    assert nc * rows == seq and rows % L == 0 and L % SUBLANES == 0 and rows % slab == 0 and slab % L == 0
    has_init = h0 is not None
    row_spec = lambda wd: pl.BlockSpec((rows, wd), lambda bi, ci: (bi * nc + ci, 0))
    state_spec = pl.BlockSpec((1, SSD_D_INNER, SSD_STATE), lambda bi, ci: (bi, 0, 0))
    bt_spec = pl.BlockSpec((rows // slab, SSD_BC, slab), lambda bi, ci: (bi * nc + ci, 0, 0))
    in_specs = [row_spec(SSD_D_INNER), row_spec(SSD_D_INNER), bt_spec, row_spec(SSD_BC), row_spec(LANES)]
    args = [zs, xs, bt, c, dt]
    if has_init:
        in_specs.append(state_spec)
        args.append(h0.reshape(batch, SSD_D_INNER, SSD_STATE))
    consts = [w["dt_bias"], w["a_log"], w["d_exp"], w["norm_w"]]
    prev, layer, count = state_stack
    extra_specs, extra_args, ht_spec, ht_shape = _stacked_out(
        prev, layer, count, (1, SSD_D_INNER, SSD_STATE), lambda bi, ci: (bi, 0, 0),
        (batch, SSD_D_INNER, SSD_STATE), F32)
    in_specs += [_const_spec(a.shape) for a in consts] + extra_specs
    args += consts + extra_args
    return pl.pallas_call(
        functools.partial(_ssd_kernel, L=L, has_init=has_init),
        grid=(batch, nc),
        in_specs=in_specs,
        out_specs=[row_spec(SSD_D_INNER), ht_spec],
        input_output_aliases={len(args) - 1: 1} if extra_args else {},
        out_shape=[jax.ShapeDtypeStruct((batch * seq, SSD_D_INNER), BF16), ht_shape],
        scratch_shapes=[pltpu.VMEM((SSD_GROUPS, SSD_STATE, SSD_GROUP_W), F32),
                        pltpu.VMEM((rows // L * SSD_GROUPS, L, L), F32)],
        compiler_params=_cparams("arbitrary", "arbitrary"),
        name="ssd_scan",
    )(*args)


def _out_ple_kernel(y_ref, h_ref, p_ref, wout_ref, plenorm_ref, wgate_ref, wup_ref, o_ref):
    h1 = h_ref[...] + _dot(y_ref[...], wout_ref[...])
    hn = _rms(h1, plenorm_ref[...]).astype(BF16)
    pb = p_ref[0].astype(BF16)
    for c in range(0, D_MODEL, COL_CHUNK):
        cs = slice(c, c + COL_CHUNK)
        gate = jax.nn.sigmoid(_dot(hn, wgate_ref[:, cs]))
        o_ref[:, cs] = h1[:, cs] + _dot(pb, wup_ref[:, cs]) * gate


def _out_ple(y, h, p, wout, plenorm, wgate, wup, tm):
    m, kd = y.shape
    assert m % tm == 0
    p_all, layer = p
    return pl.pallas_call(
        _out_ple_kernel,
        grid=(m // tm,),
        in_specs=[pl.BlockSpec((tm, kd), lambda i: (i, 0)),
                  pl.BlockSpec((tm, D_MODEL), lambda i: (i, 0)),
                  pl.BlockSpec((1, tm, PLE_DIM), lambda i: (layer, i, 0)),
                  _const_spec(wout.shape), _const_spec(plenorm.shape),
                  _const_spec(wgate.shape), _const_spec(wup.shape)],
        out_specs=pl.BlockSpec((tm, D_MODEL), lambda i: (i, 0)),
        out_shape=jax.ShapeDtypeStruct((m, D_MODEL), F32),
        compiler_params=_cparams("arbitrary"),
        name="out_ple",
    )(y, h, p_all, wout, plenorm, wgate, wup)


def _dot_hi_lo_rhs(x, a):
    hi = x.astype(BF16)
    lo = (x - hi.astype(F32)).astype(BF16)
    return _dot(hi, a) + _dot(lo, a)


def _head_part_matrix(width):
    def part(idx):
        within = idx % HEAD_PAD
        return (idx // HEAD_PAD) * 4 + jnp.where(within < MLA_NOPE, 0, jnp.where(within < MLA_QK, 1, 2))
    rows = part(lax.broadcasted_iota(jnp.int32, (width, width), 0))
    cols_i = lax.broadcasted_iota(jnp.int32, (width, width), 1)
    cols = part(cols_i)
    same = jnp.logical_and(rows == cols, cols_i % HEAD_PAD < MLA_QK)
    lane = lax.broadcasted_iota(jnp.int32, (1, width), 1) % HEAD_PAD
    inv_count = jnp.where(lane < MLA_NOPE, 1.0 / MLA_NOPE, jnp.where(lane < MLA_QK, 1.0 / MLA_ROPE, 0.0))
    return jnp.where(same, 1.0, 0.0).astype(BF16), inv_count


def _head_norm_rope(x, pmat, inv_count, gain_cos, gain_sin):
    inv = lax.rsqrt(_dot_hi_lo_rhs(x * x, pmat) * inv_count + EPS)
    shift = HEAD_PAD - MLA_ROPE
    x_rot = jnp.concatenate([pltpu.roll(x[:, s0:s0 + HEAD_PAD], shift, 1)
                             for s0 in range(0, x.shape[1], HEAD_PAD)], axis=1)
    return (x * inv) * gain_cos + (x_rot * inv) * gain_sin


def _mla_in_kernel(h_ref, lnw_ref, win_ref, qan_ref, wqb_ref, kvan_ref, gq_ref, gqrot_ref, gkr_ref, gkrrot_ref,
                   cos_ref, sin_ref, *rest, q_transposed):
    q_ref, lat_ref, kr_ref, gate_ref = rest[-4:]
    xn = _rms(h_ref[...], lnw_ref[...]).astype(BF16)
    c_q, c_kv, c_kr = 0, MLA_Q_LORA, MLA_Q_LORA + MLA_KV_LORA
    c_gate = c_kr + HEAD_PAD
    pair_w = 2 * HEAD_PAD
    _store_dot(gate_ref, xn, win_ref, c_gate)
    lat_ref[0] = _rms(_dot(xn, win_ref[:, c_kv:c_kr]), kvan_ref[...])
    pmat, inv_count = _head_part_matrix(pair_w)
    cos_t, sin_t = cos_ref[...], sin_ref[...]
    kr = _head_norm_rope(_dot(xn, win_ref[:, c_kr:c_gate]), pmat[:HEAD_PAD, :HEAD_PAD],
                         inv_count[:, :HEAD_PAD], gkr_ref[...] * cos_t[:, :HEAD_PAD],
                         gkrrot_ref[...] * sin_t[:, :HEAD_PAD])
    kr_ref[...] = kr[:, MLA_NOPE:MLA_QK]
    qa = _rms(_dot(xn, win_ref[:, c_q:c_kv]), qan_ref[...]).astype(BF16)
    gain_cos = gq_ref[...] * cos_t * SOFTMAX_LOG2_SCALE
    gain_sin = gqrot_ref[...] * sin_t * SOFTMAX_LOG2_SCALE
    for pair in range(MLA_HEADS // 2):
        qq = _dot(qa, wqb_ref[:, pair * pair_w:(pair + 1) * pair_w])
        qh = _head_norm_rope(qq, pmat, inv_count, gain_cos, gain_sin)
        if q_transposed:
            q_ref[0, pair * pair_w:(pair + 1) * pair_w, :] = qh.T.astype(q_ref.dtype)
        else:
            q_ref[:, pair * pair_w:(pair + 1) * pair_w] = qh.astype(q_ref.dtype)


def _stacked_out(prev, layer, count, block, index_tail, shape_tail, dtype):
    spec = pl.BlockSpec((1,) + block, lambda *g: (layer,) + index_tail(*g))
    shape = jax.ShapeDtypeStruct((count,) + shape_tail, dtype)
    if prev is None:
        return [], [], spec, shape
    return [pl.BlockSpec(memory_space=pl.ANY)], [prev], spec, shape


def _mla_in(h, lnw, w, rope, tm, q_transposed, lat_stack):
    m = h.shape[0]
    assert m % tm == 0
    cos_t, sin_t = rope
    period = cos_t.shape[0] // tm
    assert period * tm == cos_t.shape[0]
    rope_spec = pl.BlockSpec((tm, 2 * HEAD_PAD), lambda i: (i % period, 0))
    consts = [lnw, w["in_w"], w["q_a_norm"], w["q_b_w"], w["kv_a_norm"],
              w["gain_q"], w["gain_q_rot"], w["gain_kr"], w["gain_kr_rot"]]
    row = lambda wd: pl.BlockSpec((tm, wd), lambda i: (i, 0))
    qw = MLA_HEADS * HEAD_PAD
    if q_transposed:
        seq = period * tm
        assert m % seq == 0
        q_spec = pl.BlockSpec((1, qw, tm), lambda i: (i // period, 0, i % period))
        q_shape = jax.ShapeDtypeStruct((m // seq, qw, seq), BF16)
    else:
        q_spec, q_shape = row(qw), jax.ShapeDtypeStruct((m, qw), BF16)
    prev, layer, count = lat_stack
    extra_specs, extra_args, lat_spec, lat_shape = _stacked_out(
        prev, layer, count, (tm, MLA_KV_LORA), lambda i: (i, 0), (m, MLA_KV_LORA), F32)
    n_in = 1 + len(consts) + 2
    return pl.pallas_call(
        functools.partial(_mla_in_kernel, q_transposed=q_transposed),
        grid=(m // tm,),
        in_specs=[row(D_MODEL)] + [_const_spec(a.shape) for a in consts] + [rope_spec] * 2 + extra_specs,
        out_specs=[q_spec, lat_spec, row(MLA_ROPE), row(MLA_WIDTH)],
        input_output_aliases={n_in: 1} if extra_args else {},
        out_shape=[q_shape,
                   lat_shape,
                   jax.ShapeDtypeStruct((m, MLA_ROPE), F32),
                   jax.ShapeDtypeStruct((m, MLA_WIDTH), F32)],
        compiler_params=_cparams("arbitrary"),
        name="mla_in",
    )(h, *consts, cos_t, sin_t, *extra_args)


def _mla_kv_kernel(lat_ref, kr_ref, wk_ref, wv_ref, gkn_ref, k_ref, v_ref, *, v_transposed):
    latb = lat_ref[0].astype(BF16)
    if v_transposed:
        ones = jnp.ones((V_ONES_ROWS, latb.shape[0]), v_ref.dtype)
        for c in range(0, MLA_WIDTH, COL_CHUNK):
            vt = _dot(latb, wv_ref[:, c:c + COL_CHUNK]).T.astype(v_ref.dtype)
            for hh in range(COL_CHUNK // MLA_V):
                r0 = (c // MLA_V + hh) * V_SLOT_ROWS
                v_ref[0, r0:r0 + MLA_V, :] = vt[hh * MLA_V:(hh + 1) * MLA_V]
                v_ref[0, r0 + MLA_V:r0 + V_SLOT_ROWS, :] = ones
    else:
        _store_dot(v_ref, latb, wv_ref)
    rows = latb.shape[0]
    kr = jnp.concatenate([jnp.zeros((rows, MLA_NOPE), F32), kr_ref[...],
                          jnp.zeros((rows, HEAD_PAD - MLA_QK), F32)], axis=1)
    for h in range(MLA_HEADS):
        hs = slice(h * HEAD_PAD, (h + 1) * HEAD_PAD)
        kh = _dot(latb, wk_ref[:, hs])
        ms = jnp.sum(kh * kh, axis=-1, keepdims=True) * (1.0 / MLA_NOPE)
        k_ref[:, hs] = (kh * lax.rsqrt(ms + EPS) * gkn_ref[...] + kr).astype(k_ref.dtype)


def _mla_kv(lat, layer, kr, w, tm, v_transposed):
    m = lat.shape[1]
    assert m % tm == 0
    consts = [w["kv_k_w"], w["kv_v_w"], w["gain_kn"]]
    row = lambda wd: pl.BlockSpec((tm, wd), lambda i: (i, 0))
    if v_transposed:
        v_spec = pl.BlockSpec((1, MLA_HEADS * V_SLOT_ROWS, tm), lambda i: (i, 0, 0))
        v_shape = jax.ShapeDtypeStruct((m // tm, MLA_HEADS * V_SLOT_ROWS, tm), BF16)
    else:
        v_spec, v_shape = row(MLA_WIDTH), jax.ShapeDtypeStruct((m, MLA_WIDTH), BF16)
    return pl.pallas_call(
        functools.partial(_mla_kv_kernel, v_transposed=v_transposed),
        grid=(m // tm,),
        in_specs=[pl.BlockSpec((1, tm, MLA_KV_LORA), lambda i: (layer, i, 0)), row(MLA_ROPE)]
                 + [_const_spec(a.shape) for a in consts],
        out_specs=[row(MLA_HEADS * HEAD_PAD), v_spec],
        out_shape=[jax.ShapeDtypeStruct((m, MLA_HEADS * HEAD_PAD), BF16), v_shape],
        compiler_params=_cparams("arbitrary"),
        name="mla_kv",
    )(lat, kr, *consts)


def _gated_pair_out(o_ref, gate_ref, outs, pair):
    lane = lax.broadcasted_iota(jnp.int32, outs[0].shape, 1)
    ps = slice(pair * LANES, (pair + 1) * LANES)
    g = gate_ref[:, ps]
    o_ref[:, ps] = (jnp.where(lane < MLA_V, outs[0], outs[1]) * (g * jax.nn.sigmoid(g))).astype(o_ref.dtype)


def _attn_kernel(q_ref, k_ref, v_ref, gate_ref, o_ref, m_s, acc_s, s_s, *, tq):
    i = pl.program_id(1)
    key_c = lax.broadcasted_iota(jnp.int32, (tq, tq), 0) // CHUNK
    qry_c = lax.broadcasted_iota(jnp.int32, (tq, tq), 1) // CHUNK
    diag_mask = key_c <= qry_c
    m_s[...] = jnp.full(m_s.shape, NEG_BIG, F32)
    acc_s[...] = jnp.zeros(acc_s.shape, F32)

    def tile_step(j, mask):
        rows = pl.ds(pl.multiple_of(j * tq, tq), tq)
        for h in range(MLA_HEADS):
            hs = slice(h * HEAD_PAD, (h + 1) * HEAD_PAD)
            s_s[h] = _dot(k_ref[rows, hs], q_ref[0, hs, :])
        for h in range(MLA_HEADS):
            s = s_s[h]
            if mask is not None:
                s = jnp.where(mask, s, -jnp.inf)
            m_prev = m_s[h]
            m_new = jnp.maximum(m_prev, jnp.max(s, axis=0, keepdims=True))
            alpha = jnp.exp2(m_prev[0:1, :] - m_new[0:1, :])
            p = jnp.exp2(s - m_new[0:1, :]).astype(BF16)
            pv = _dot(v_ref[j, h * V_SLOT_ROWS:(h + 1) * V_SLOT_ROWS, :], p)
            acc_s[h] = alpha * acc_s[h] + pv
            m_s[h] = m_new

    def body(j, carry):
        tile_step(j, None)
        return carry

    lax.fori_loop(0, i, body, 0)
    tile_step(i, diag_mask)
    for pair in range(MLA_HEADS // 2):
        o_t = jnp.concatenate([acc_s[2 * pair + sub, 0:MLA_V, :] / acc_s[2 * pair + sub, MLA_V:MLA_V + 1, :]
                               for sub in range(2)], axis=0)
        ps = slice(pair * LANES, (pair + 1) * LANES)
        g = gate_ref[:, ps]
        o_ref[:, ps] = (o_t.T * (g * jax.nn.sigmoid(g))).astype(o_ref.dtype)


def _attn_prompt(q_t, k, v_t, gate, batch, seq, tq):
    nq = seq // tq
    assert nq * tq == seq and tq % CHUNK == 0
    return pl.pallas_call(
        functools.partial(_attn_kernel, tq=tq),
        grid=(batch, nq),
        in_specs=[pl.BlockSpec((1, MLA_HEADS * HEAD_PAD, tq), lambda b, i: (b, 0, i)),
                  pl.BlockSpec((seq, MLA_HEADS * HEAD_PAD), lambda b, i: (b, 0)),
                  pl.BlockSpec((nq, MLA_HEADS * V_SLOT_ROWS, tq), lambda b, i: (b, 0, 0)),
                  pl.BlockSpec((tq, MLA_WIDTH), lambda b, i: (b * nq + i, 0))],
        out_specs=pl.BlockSpec((tq, MLA_WIDTH), lambda b, i: (b * nq + i, 0)),
        out_shape=jax.ShapeDtypeStruct((batch * seq, MLA_WIDTH), BF16),
        scratch_shapes=[pltpu.VMEM((MLA_HEADS, SUBLANES, tq), F32),
                        pltpu.VMEM((MLA_HEADS, V_SLOT_ROWS, tq), F32), pltpu.VMEM((MLA_HEADS, tq, tq), F32)],
        compiler_params=_cparams("arbitrary", "arbitrary"),
        name="attn_prompt",
    )(q_t, k, v_t, gate)


def _attn_step_kernel(q_ref, k_ref, v_ref, gate_ref, o_ref, s_s, *, q_pos0, n_keys):
    tq = q_ref.shape[0]
    tk = k_ref.shape[1]
    k_pos = lax.broadcasted_iota(jnp.int32, (tq, tk), 1)
    q_pos = lax.broadcasted_iota(jnp.int32, (tq, tk), 0) + q_pos0
    mask = jnp.logical_and(k_pos < n_keys, k_pos // CHUNK <= q_pos // CHUNK)
    for h in range(MLA_HEADS):
        hs = slice(h * HEAD_PAD, (h + 1) * HEAD_PAD)
        s_s[h] = _dot_nt(q_ref[:, hs], k_ref[0, :, hs])
    for pair in range(MLA_HEADS // 2):
        vs = slice(pair * LANES, (pair + 1) * LANES)
        outs = []
        for sub in range(2):
            s = jnp.where(mask, s_s[2 * pair + sub], -jnp.inf)
            p = jnp.exp2(s - jnp.max(s, axis=-1, keepdims=True))
            outs.append(_dot(p.astype(BF16), v_ref[0, :, vs]) / jnp.sum(p, axis=-1, keepdims=True))
        _gated_pair_out(o_ref, gate_ref, outs, pair)


def _attn_step(q, k, v, gate, batch, tq, n_keys):
    tk = k.shape[1]
    return pl.pallas_call(
        functools.partial(_attn_step_kernel, q_pos0=n_keys - tq, n_keys=n_keys),
        grid=(batch,),
        in_specs=[pl.BlockSpec((tq, MLA_HEADS * HEAD_PAD), lambda b: (b, 0)),
                  pl.BlockSpec((1, tk, MLA_HEADS * HEAD_PAD), lambda b: (b, 0, 0)),
                  pl.BlockSpec((1, tk, MLA_WIDTH), lambda b: (b, 0, 0)),
                  pl.BlockSpec((tq, MLA_WIDTH), lambda b: (b, 0))],
        out_specs=pl.BlockSpec((tq, MLA_WIDTH), lambda b: (b, 0)),
        out_shape=jax.ShapeDtypeStruct((batch * tq, MLA_WIDTH), BF16),
        scratch_shapes=[pltpu.VMEM((MLA_HEADS, tq, tk), F32)],
        compiler_params=_cparams("arbitrary"),
        name="attn_step",
    )(q, k, v, gate)


def _pad_cols(a, width):
    return jnp.pad(a, ((0, 0), (0, width - a.shape[1])))


def _ssd_weights(j, in_w, conv_w, conv_b, dt_bias, a_log, d, norm_w, out_w):
    zx = SSD_D_INNER + SSD_CONV_DIM
    return {
        "in_w": jnp.concatenate([0.5 * in_w[j][:, :SSD_D_INNER], in_w[j][:, SSD_D_INNER:zx],
                                 _pad_cols(in_w[j][:, zx:], LANES)], axis=1).astype(BF16),
        "conv_w": 0.5 * conv_w[j],
        "conv_b": 0.5 * conv_b[j][None, :],
        "dt_bias": _pad_cols(dt_bias[j][None, :], LANES),
        "a_log": _pad_cols(a_log[j][None, :], LANES),
        "d_exp": jnp.repeat(d[j], SSD_HEAD_DIM)[None, :],
        "norm_w": norm_w[j][None, :],
        "out_w": out_w[j].astype(BF16),
    }


def _head_pad_cols(w, per_head, take, offset=0):
    k = w.shape[0]
    w = w.reshape(k, MLA_HEADS, per_head)[:, :, offset:offset + take]
    return jnp.pad(w, ((0, 0), (0, 0), (0, HEAD_PAD - take))).reshape(k, MLA_HEADS * HEAD_PAD)


def _rotate_half(a):
    half = MLA_ROPE // 2
    return jnp.concatenate([-a[..., half:], a[..., :half]], axis=-1)


def _swap_halves(a):
    half = MLA_ROPE // 2
    return jnp.concatenate([a[..., half:], a[..., :half]], axis=-1)


def _rope_slot(a):
    pad = [(0, 0)] * (a.ndim - 1) + [(MLA_NOPE, HEAD_PAD - MLA_QK)]
    return jnp.pad(a, pad)


def _q_b_cols(w):
    k = w.shape[0]
    w = w.reshape(k, MLA_HEADS, MLA_QK)
    return jnp.concatenate([w, _rotate_half(w[:, :, MLA_NOPE:])], axis=-1).reshape(k, MLA_HEADS * HEAD_PAD)


def _mla_weights(j, in_w, q_a_norm, q_b_w, kv_a_norm, kv_b_w, qn, qr, kn, kr, out_w):
    c1, c2, c3 = MLA_Q_LORA, MLA_Q_LORA + MLA_KV_LORA, MLA_Q_LORA + MLA_KV_LORA + MLA_ROPE
    w = in_w[j]
    kr_cols = jnp.concatenate([jnp.zeros((w.shape[0], MLA_NOPE), F32), w[:, c2:c3],
                               _rotate_half(w[:, c2:c3])], axis=1)
    zeros = lambda n: jnp.zeros((n,), F32)
    gain_q = jnp.concatenate([qn[j], qr[j], zeros(HEAD_PAD - MLA_QK)])
    return {
        "in_w": jnp.concatenate([w[:, :c2], kr_cols, w[:, c3:]], axis=1).astype(BF16),
        "q_a_norm": q_a_norm[j][None, :],
        "q_b_w": _q_b_cols(q_b_w[j]).astype(BF16),
        "gain_q": jnp.tile(gain_q, 2)[None, :],
        "gain_q_rot": jnp.tile(_rope_slot(_swap_halves(qr[j])), 2)[None, :],
        "gain_kr_rot": _rope_slot(_swap_halves(kr[j]))[None, :],
        "kv_a_norm": kv_a_norm[j][None, :],
        "kv_k_w": _head_pad_cols(kv_b_w[j], MLA_NOPE + MLA_V, MLA_NOPE).astype(BF16),
        "kv_v_w": kv_b_w[j].reshape(MLA_KV_LORA, MLA_HEADS, MLA_NOPE + MLA_V)[:, :, MLA_NOPE:]
                  .reshape(MLA_KV_LORA, MLA_WIDTH).astype(BF16),
        "gain_kr": _rope_slot(kr[j])[None, :],
        "gain_kn": jnp.concatenate([kn[j], zeros(HEAD_PAD - MLA_NOPE)])[None, :],
        "out_w": out_w[j].astype(BF16),
    }


def _rope_tables(pos0, seq, rows):
    inv = 1.0 / (ROPE_BASE ** (jnp.arange(0, MLA_ROPE, 2, dtype=F32) / MLA_ROPE))
    ang = (pos0 + jnp.arange(seq)).astype(F32)[:, None] * inv[None, :]
    cos, sin = jnp.cos(ang), jnp.sin(ang)
    z = lambda n: jnp.zeros((seq, n), F32)
    cos_t = jnp.concatenate([jnp.ones((seq, MLA_NOPE), F32), cos, cos, z(HEAD_PAD - MLA_QK)], axis=1)
    sin_t = jnp.concatenate([z(MLA_NOPE), sin, sin, z(HEAD_PAD - MLA_QK)], axis=1)
    reps = max(1, rows // seq)
    return tuple(jnp.tile(t, (reps, 2)) for t in (cos_t, sin_t))


def _row_tile(m, want):
    tm = min(m, want)
    assert m % tm == 0
    return tm


def _ssd_layer(h, p, lnw, ple, w, batch, seq, chunk, conv_prev, h0, state_stack):
    tm = _row_tile(h.shape[0], ROW_TILE)
    zs, xs, bt, c, dt, conv_new = _ssd_in(h, lnw, w, conv_prev, batch, seq, min(seq, SSD_BLOCK_ROWS))
    yn, states = _ssd_scan(zs, xs, bt, c, dt, h0, w, batch, seq, chunk, state_stack)
    h = _out_ple(yn, h, p, w["out_w"], *ple, tm)
    return h, conv_new, states


def _mla_layer(h, p, lnw, ple, w, batch, seq, lat_past, kr_past, lat_stack):
    tm = _row_tile(h.shape[0], ROW_TILE)
    prompt = lat_past is None
    tm_in = _row_tile(h.shape[0], min(seq, MLA_IN_TILE) if prompt else MLA_IN_TILE)
    pos0 = 0 if prompt else lat_past.shape[1]
    rope = _rope_tables(pos0, seq, tm_in)
    q, lats, kr, gate = _mla_in(h, lnw, w, rope, tm_in, q_transposed=prompt, lat_stack=lat_stack)
    layer = lat_stack[1]
    if prompt:
        tile = min(seq, ATTN_TILE)
        k, v_t = _mla_kv(lats, layer, kr, w, tile, v_transposed=True)
        og = _attn_prompt(q, k, v_t, gate, batch, seq, tile)
    else:
        n_keys = pos0 + seq
        tk = -(-n_keys // LANES) * LANES
        pad = ((0, 0), (0, tk - n_keys), (0, 0))
        lat_all = jnp.pad(jnp.concatenate([lat_past, lats[layer].reshape(batch, seq, -1)], axis=1), pad)
        kr_all = jnp.pad(jnp.concatenate([kr_past, kr.reshape(batch, seq, -1)], axis=1), pad)
        k, v = _mla_kv(lat_all.reshape(1, batch * tk, -1), 0, kr_all.reshape(batch * tk, -1), w,
                       math.gcd(batch * tk, 512), v_transposed=False)
        og = _attn_step(q, k.reshape(batch, tk, -1), v.reshape(batch, tk, -1), gate, batch, seq, n_keys)
    h = _out_ple(og, h, p, w["out_w"], *ple, tm)
    return h, lats, kr.reshape(batch, seq, -1)


def kernel(x_prompt, x_sample, cache_conv, state_ssm, cache_kv_latent, cache_k_rope, p_prompt, p_sample, ln_w, ssd_in_w, ssd_conv_w, ssd_conv_b, ssd_dt_bias, ssd_A_log, ssd_D, ssd_norm_w, ssd_out_w, mla_in_w, mla_q_a_norm, mla_q_b_w, mla_kv_a_norm, mla_kv_b_w, mla_q_nope_norm, mla_q_rope_norm, mla_k_nope_norm, mla_k_rope_norm, mla_out_w, ple_up_w, ple_norm_w, ple_gate_w):
    bp, tp, d = x_prompt.shape
    bs, ts, _ = x_sample.shape
    hp = x_prompt.reshape(bp * tp, d)
    hs = x_sample.reshape(bs * ts, d)
    conv_p, kr_p, conv_s, kr_s = [], [], [], []
    lat_p = lat_s = ssm_p = ssm_s = None
    for i in range(DEPTH):
        j = i // 2
        lnw = ln_w[i][None, :]
        ple = (ple_norm_w[i][None, :], ple_gate_w[i].astype(BF16), ple_up_w[i].astype(BF16))
        pp = (p_prompt.reshape(DEPTH, bp * tp, PLE_DIM), i)
        ps = (p_sample.reshape(DEPTH, bs * ts, PLE_DIM), i)
        if i % 2 == 0:
            w = _ssd_weights(j, ssd_in_w, ssd_conv_w, ssd_conv_b, ssd_dt_bias, ssd_A_log, ssd_D,
                             ssd_norm_w, ssd_out_w)
            n_ssd = (DEPTH + 1) // 2
            hp, cp, ssm_p = _ssd_layer(hp, pp, lnw, ple, w, bp, tp, min(tp, SSD_SCAN_CHUNK), None, None,
                                       (ssm_p, j, n_ssd))
            hs, cs, ssm_s = _ssd_layer(hs, ps, lnw, ple, w, bs, ts, ts, cache_conv[j], state_ssm[j],
                                       (ssm_s, j, n_ssd))
            conv_p.append(cp); conv_s.append(cs)
        else:
            w = _mla_weights(j, mla_in_w, mla_q_a_norm, mla_q_b_w, mla_kv_a_norm, mla_kv_b_w,
                             mla_q_nope_norm, mla_q_rope_norm, mla_k_nope_norm, mla_k_rope_norm, mla_out_w)
            n_mla = DEPTH // 2
            hp, lat_p, rp = _mla_layer(hp, pp, lnw, ple, w, bp, tp, None, None, (lat_p, j, n_mla))
            hs, lat_s, rs = _mla_layer(hs, ps, lnw, ple, w, bs, ts, cache_kv_latent[j], cache_k_rope[j],
                                       (lat_s, j, n_mla))
            kr_p.append(rp); kr_s.append(rs)
    return (hp.reshape(bp, tp, d), hs.reshape(bs, ts, d),
            jnp.stack(conv_p), ssm_p.reshape(-1, bp, SSD_HEADS, SSD_HEAD_DIM, SSD_STATE),
            lat_p.reshape(-1, bp, tp, MLA_KV_LORA), jnp.stack(kr_p),
            jnp.stack(conv_s), ssm_s.reshape(-1, bs, SSD_HEADS, SSD_HEAD_DIM, SSD_STATE),
            lat_s.reshape(-1, bs, ts, MLA_KV_LORA), jnp.stack(kr_s))
```

```python
import functools
import math

import jax
import jax.numpy as jnp
from jax import lax
from jax.experimental import pallas as pl
from jax.experimental.pallas import tpu as pltpu

F32 = jnp.float32
BF16 = jnp.bfloat16

D_MODEL = 1024
DEPTH = 4
CHUNK = 64
PLE_DIM = 256
EPS = 1e-6

SSD_D_INNER = 2 * D_MODEL
SSD_HEAD_DIM = 64
SSD_HEADS = SSD_D_INNER // SSD_HEAD_DIM
SSD_GROUPS = 8
SSD_HPG = SSD_HEADS // SSD_GROUPS
SSD_STATE = 128
SSD_CONV_W = 4
SSD_BC = SSD_GROUPS * SSD_STATE
SSD_CONV_DIM = SSD_D_INNER + 2 * SSD_BC
SSD_GROUP_W = SSD_HPG * SSD_HEAD_DIM

MLA_HEADS = 16
MLA_NOPE = 64
MLA_ROPE = 32
MLA_V = 64
MLA_Q_LORA = 384
MLA_KV_LORA = 256
MLA_QK = MLA_NOPE + MLA_ROPE
MLA_WIDTH = MLA_HEADS * MLA_V
ROPE_BASE = 10000.0

LANES = 128
SUBLANES = 8
HEAD_PAD = LANES
VMEM_V7X_BYTES = 64 * 1024 * 1024
VMEM_LIMIT_BYTES = VMEM_V7X_BYTES * 13 // 16
NEG_BIG = -1e30
LOG2_E = math.log2(math.e)
SOFTMAX_LOG2_SCALE = (MLA_QK ** -0.5) * LOG2_E
COL_CHUNK = 512
CONV_ROWS = 128
V_ONES_ROWS = 16
V_SLOT_ROWS = MLA_V + V_ONES_ROWS
ROW_TILE = 512
MLA_IN_TILE = 1024
ATTN_TILE = 256
SSD_BLOCK_ROWS = 256
SSD_SCAN_ROWS = 512
SSD_SCAN_CHUNK = 128


def _cparams(*sem):
    return pltpu.CompilerParams(dimension_semantics=sem, vmem_limit_bytes=VMEM_LIMIT_BYTES)


def _const_spec(shape):
    nd = len(shape)
    return pl.BlockSpec(shape, lambda *_: (0,) * nd, pipeline_mode=pl.Buffered(1))


def _rms(x, w):
    return x * lax.rsqrt(jnp.mean(x * x, axis=-1, keepdims=True) + EPS) * w


def _dot(a, b):
    return jnp.dot(a, b, preferred_element_type=F32)


def _dot_nt(a, b):
    return lax.dot_general(a, b, (((1,), (1,)), ((), ())), preferred_element_type=F32)


def _store_dot(o_ref, a, w_ref, col0=0):
    width = o_ref.shape[1]
    for c in range(0, width, COL_CHUNK):
        cw = min(COL_CHUNK, width - c)
        o_ref[:, c:c + cw] = _dot(a, w_ref[:, col0 + c:col0 + c + cw]).astype(o_ref.dtype)


def _ssd_in_kernel(*refs, tiles_per_seq, has_init):
    if has_init:
        (x_ref, lnw_ref, w_ref, convw_ref, convb_ref, cprev_ref,
         z_ref, xs_ref, b_ref, c_ref, dt_ref, convnew_ref, ext) = refs
    else:
        (x_ref, lnw_ref, w_ref, convw_ref, convb_ref,
         z_ref, xs_ref, b_ref, c_ref, dt_ref, convnew_ref, ext) = refs
    tm = x_ref.shape[0]
    tail = SSD_CONV_W - 1
    t = pl.program_id(0) % tiles_per_seq
    xn = _rms(x_ref[...], lnw_ref[...]).astype(BF16)

    @pl.when(t == 0)
    def _init():
        ext[0:SUBLANES, :] = jnp.zeros((SUBLANES, SSD_CONV_DIM), F32)
        if has_init:
            ext[SUBLANES - tail:SUBLANES, :] = cprev_ref[0]

    dt_ref[...] = _dot(xn, w_ref[:, SSD_D_INNER + SSD_CONV_DIM:])
    def project(c0):
        cs = slice(c0, c0 + COL_CHUNK)
        ext[SUBLANES:SUBLANES + tm, cs] = _dot(xn, w_ref[:, SSD_D_INNER + c0:SSD_D_INNER + c0 + COL_CHUNK])
        convnew_ref[0, :, cs] = ext[pl.ds(SUBLANES + tm - tail, tail), cs]

    project(0)
    for c0 in range(0, SSD_CONV_DIM, COL_CHUNK):
        cs = slice(c0, c0 + COL_CHUNK)
        if c0 + COL_CHUNK < SSD_CONV_DIM:
            project(c0 + COL_CHUNK)
        if c0 % (2 * COL_CHUNK) == 0:
            zc = slice(c0 // 2, c0 // 2 + COL_CHUNK)
            hz = _dot(xn, w_ref[:, zc])
            z_ref[:, zc] = hz * jnp.tanh(hz) + hz
        if c0 < SSD_D_INNER:
            o_ref, o0 = xs_ref, c0
        elif c0 < SSD_D_INNER + SSD_BC:
            o_ref, o0 = b_ref, c0 - SSD_D_INNER
        else:
            o_ref, o0 = c_ref, c0 - SSD_D_INNER - SSD_BC
        for r0 in range(0, tm, CONV_ROWS):
            rb = min(CONV_ROWS, tm - r0)
            for l0 in range(0, COL_CHUNK, LANES):
                ls = slice(c0 + l0, c0 + l0 + LANES)
                e = ext[r0:r0 + rb + SUBLANES, ls]
                e1 = pltpu.roll(e, 1, 0)
                far = e * convw_ref[1:2, ls] + e1 * convw_ref[0:1, ls]
                near = e * convw_ref[3:4, ls] + e1 * convw_ref[2:3, ls]
                hv = (convb_ref[:, ls] + near[SUBLANES:SUBLANES + rb]) + pltpu.roll(far, 2, 0)[SUBLANES:SUBLANES + rb]
                y = hv * jnp.tanh(hv) + hv
                if o_ref is b_ref:
                    b_ref[0, o0 + l0:o0 + l0 + LANES, r0:r0 + rb] = y.T
                else:
                    o_ref[r0:r0 + rb, o0 + l0:o0 + l0 + LANES] = y.astype(o_ref.dtype)
    ext[0:SUBLANES, :] = ext[tm:tm + SUBLANES, :]


def _ssd_in(x, lnw, w, conv_prev, batch, seq, tm):
    m, d = x.shape
    assert seq % tm == 0 and tm % SUBLANES == 0 and m == batch * seq
    tps = seq // tm
    has_init = conv_prev is not None
    tail = SSD_CONV_W - 1
    consts = [lnw, w["in_w"], w["conv_w"], w["conv_b"]]
    row = lambda wd: pl.BlockSpec((tm, wd), lambda i: (i, 0))
    conv_spec = pl.BlockSpec((1, tail, SSD_CONV_DIM), lambda i: (i // tps, 0, 0))
    in_specs = [row(d)] + [_const_spec(a.shape) for a in consts]
    args = [x] + consts
    if has_init:
        in_specs.append(conv_spec)
        args.append(conv_prev)
    return pl.pallas_call(
        functools.partial(_ssd_in_kernel, tiles_per_seq=tps, has_init=has_init),
        grid=(m // tm,),
        in_specs=in_specs,
        out_specs=[row(SSD_D_INNER), row(SSD_D_INNER), pl.BlockSpec((1, SSD_BC, tm), lambda i: (i, 0, 0)),
                   row(SSD_BC), row(LANES), conv_spec],
        out_shape=[jax.ShapeDtypeStruct((m, SSD_D_INNER), F32),
                   jax.ShapeDtypeStruct((m, SSD_D_INNER), F32),
                   jax.ShapeDtypeStruct((m // tm, SSD_BC, tm), F32),
                   jax.ShapeDtypeStruct((m, SSD_BC), BF16),
                   jax.ShapeDtypeStruct((m, LANES), F32),
                   jax.ShapeDtypeStruct((batch, tail, SSD_CONV_DIM), F32)],
        scratch_shapes=[pltpu.VMEM((tm + SUBLANES, SSD_CONV_DIM), F32)],
        compiler_params=_cparams("arbitrary"),
        name="ssd_in",
    )(*args)


def _dot_exact_lhs(a, x):
    x1 = x.astype(BF16)
    r1 = x - x1.astype(F32)
    x2 = r1.astype(BF16)
    x3 = (r1 - x2.astype(F32)).astype(BF16)
    return _dot(a, x1) + _dot(a, x2) + _dot(a, x3)


def _expand_heads(cols, g, lane_head):
    shape = (cols.shape[0], SSD_GROUP_W)
    h0 = SSD_HPG * g
    out = jnp.broadcast_to(cols[:, h0 + SSD_HPG - 1:h0 + SSD_HPG], shape)
    for r in range(SSD_HPG - 2, -1, -1):
        out = jnp.where(lane_head[:shape[0]] == r, jnp.broadcast_to(cols[:, h0 + r:h0 + r + 1], shape), out)
    return out


def _select_heads(parts, lane_head):
    out = parts[0]
    for r in range(1, len(parts)):
        out = jnp.where(lane_head == r, parts[r], out)
    return out


def _ssd_kernel(*refs, L, has_init):
    zs_ref, xs_ref, bt_ref, c_ref, dt_ref = refs[:5]
    h0_ref = refs[5] if has_init else None
    dtb_ref, alog_ref, dexp_ref, normw_ref = refs[5 + has_init:9 + has_init]
    yn_ref, ht_ref, st, cb_s = refs[-4:]
    c = pl.program_id(1)
    last = pl.num_programs(1) - 1

    @pl.when(c == 0)
    def _init():
        if has_init:
            for g in range(SSD_GROUPS):
                st[g] = h0_ref[0, g * SSD_GROUP_W:(g + 1) * SSD_GROUP_W, :].T
        else:
            st[...] = jnp.zeros(st.shape, F32)

    rows = dt_ref.shape[0]
    n_chunks = rows // L
    fused = L % LANES == 0
    group_n = lambda g: slice(g * SSD_STATE, (g + 1) * SSD_STATE)
    bt_rows = bt_ref.shape[2]

    def bt_chunk(g, ci):
        r0 = ci * L
        return bt_ref[r0 // bt_rows, group_n(g), r0 % bt_rows:r0 % bt_rows + L]

    for ci in range(n_chunks):
        rs = slice(ci * L, (ci + 1) * L)
        for g in range(SSD_GROUPS):
            cb_s[ci * SSD_GROUPS + g] = _dot(c_ref[rs, group_n(g)], bt_chunk(g, ci).astype(BF16))

    dt = jax.nn.softplus(dt_ref[...] + dtb_ref[...])
    dta = dt * (-jnp.exp(alog_ref[...]))
    row = lax.broadcasted_iota(jnp.int32, (rows, rows), 0)
    col = lax.broadcasted_iota(jnp.int32, (rows, rows), 1)
    same_chunk_tri = jnp.logical_and(row >= col, row // L == col // L)
    acs = _dot_exact_lhs(same_chunk_tri.astype(BF16), dta) * LOG2_E
    acs_t = acs.T
    dt_t = dt.T
    tri = lax.broadcasted_iota(jnp.int32, (L, L), 0) >= lax.broadcasted_iota(jnp.int32, (L, L), 1)
    lane_head = lax.broadcasted_iota(jnp.int32, (L, SSD_GROUP_W), 1) // SSD_HEAD_DIM
    lane_head_n = lax.broadcasted_iota(jnp.int32, (SSD_STATE, SSD_GROUP_W), 1) // SSD_HEAD_DIM

    for ci in range(n_chunks):
        rs = slice(ci * L, (ci + 1) * L)
        acs_c = acs[rs]
        acs_tc = acs_t[:, rs]
        dt_tc = dt_t[:, rs]
        cdec = jnp.exp2(acs_c[L - 1:L, :])
        wdt_t = jnp.exp2(acs_tc[:, L - 1:L] - acs_tc) * dt_tc
        for g in range(SSD_GROUPS):
            gs = slice(g * SSD_GROUP_W, (g + 1) * SSD_GROUP_W)
            xg = xs_ref[rs, gs]
            xgb = xg.astype(BF16)
            cg = c_ref[rs, group_n(g)].astype(F32)
            cb = cb_s[ci * SSD_GROUPS + g]
            sg = st[g]
            sgb = sg.astype(BF16)
            if fused:
                rhs = jnp.concatenate([xgb, sgb], axis=0)
            parts = []
            for r in range(SSD_HPG):
                h = SSD_HPG * g + r
                a_col = jnp.broadcast_to(acs_c[:, h:h + 1], (L, LANES))
                seg = a_col[:, :L] - acs_tc[h:h + 1, :]
                m = (cb * jnp.exp2(jnp.where(tri, seg, -jnp.inf)) * dt_tc[h:h + 1, :]).astype(BF16)
                ce = (cg * jnp.exp2(a_col)).astype(BF16)
                if fused:
                    parts.append(_dot(jnp.concatenate([m, ce], axis=1), rhs))
                else:
                    parts.append(_dot(m, xgb) + _dot(ce, sgb))
            y = _select_heads(parts, lane_head)
            y = y + dexp_ref[:, gs] * xg
            y = y * zs_ref[rs, gs]
            yn_ref[rs, gs] = _rms(y, normw_ref[:, gs]).astype(yn_ref.dtype)
            btg = bt_chunk(g, ci)
            upd = [_dot((btg * wdt_t[SSD_HPG * g + r:SSD_HPG * g + r + 1, :]).astype(BF16), xgb)
                   for r in range(SSD_HPG)]
            st[g] = sg * _expand_heads(cdec, g, lane_head) + _select_heads(upd, lane_head_n)

    @pl.when(c == last)
    def _state_out():
        for g in range(SSD_GROUPS):
            ht_ref[0, 0, g * SSD_GROUP_W:(g + 1) * SSD_GROUP_W, :] = st[g].T


def _ssd_scan(zs, xs, bt, c, dt, h0, w, batch, seq, L, state_stack):
    slab = bt.shape[2]
    rows = min(seq, max(slab, SSD_SCAN_ROWS))
    nc = seq // rows
    assert nc * rows == seq and rows % L == 0 and L % SUBLANES == 0 and rows % slab == 0 and slab % L == 0
    has_init = h0 is not None
    row_spec = lambda wd: pl.BlockSpec((rows, wd), lambda bi, ci: (bi * nc + ci, 0))
    state_spec = pl.BlockSpec((1, SSD_D_INNER, SSD_STATE), lambda bi, ci: (bi, 0, 0))
    bt_spec = pl.BlockSpec((rows // slab, SSD_BC, slab), lambda bi, ci: (bi * nc + ci, 0, 0))
    in_specs = [row_spec(SSD_D_INNER), row_spec(SSD_D_INNER), bt_spec, row_spec(SSD_BC), row_spec(LANES)]
    args = [zs, xs, bt, c, dt]
    if has_init:
        in_specs.append(state_spec)
        args.append(h0.reshape(batch, SSD_D_INNER, SSD_STATE))
    consts = [w["dt_bias"], w["a_log"], w["d_exp"], w["norm_w"]]
    prev, layer, count = state_stack
    extra_specs, extra_args, ht_spec, ht_shape = _stacked_out(
        prev, layer, count, (1, SSD_D_INNER, SSD_STATE), lambda bi, ci: (bi, 0, 0),
        (batch, SSD_D_INNER, SSD_STATE), F32)
    in_specs += [_const_spec(a.shape) for a in consts] + extra_specs
    args += consts + extra_args
    return pl.pallas_call(
        functools.partial(_ssd_kernel, L=L, has_init=has_init),
        grid=(batch, nc),
        in_specs=in_specs,
        out_specs=[row_spec(SSD_D_INNER), ht_spec],
        input_output_aliases={len(args) - 1: 1} if extra_args else {},
        out_shape=[jax.ShapeDtypeStruct((batch * seq, SSD_D_INNER), BF16), ht_shape],
        scratch_shapes=[pltpu.VMEM((SSD_GROUPS, SSD_STATE, SSD_GROUP_W), F32),
                        pltpu.VMEM((rows // L * SSD_GROUPS, L, L), F32)],
        compiler_params=_cparams("arbitrary", "arbitrary"),
        name="ssd_scan",
    )(*args)


def _out_ple_kernel(y_ref, h_ref, p_ref, wout_ref, plenorm_ref, wgate_ref, wup_ref, o_ref):
    h1 = h_ref[...] + _dot(y_ref[...], wout_ref[...])
    hn = _rms(h1, plenorm_ref[...]).astype(BF16)
    pb = p_ref[0].astype(BF16)
    for c in range(0, D_MODEL, COL_CHUNK):
        cs = slice(c, c + COL_CHUNK)
        gate = jax.nn.sigmoid(_dot(hn, wgate_ref[:, cs]))
        o_ref[:, cs] = h1[:, cs] + _dot(pb, wup_ref[:, cs]) * gate


def _out_ple(y, h, p, wout, plenorm, wgate, wup, tm):
    m, kd = y.shape
    assert m % tm == 0
    p_all, layer = p
    return pl.pallas_call(
        _out_ple_kernel,
        grid=(m // tm,),
        in_specs=[pl.BlockSpec((tm, kd), lambda i: (i, 0)),
                  pl.BlockSpec((tm, D_MODEL), lambda i: (i, 0)),
                  pl.BlockSpec((1, tm, PLE_DIM), lambda i: (layer, i, 0)),
                  _const_spec(wout.shape), _const_spec(plenorm.shape),
                  _const_spec(wgate.shape), _const_spec(wup.shape)],
        out_specs=pl.BlockSpec((tm, D_MODEL), lambda i: (i, 0)),
        out_shape=jax.ShapeDtypeStruct((m, D_MODEL), F32),
        compiler_params=_cparams("arbitrary"),
        name="out_ple",
    )(y, h, p_all, wout, plenorm, wgate, wup)


def _dot_hi_lo_rhs(x, a):
    hi = x.astype(BF16)
    lo = (x - hi.astype(F32)).astype(BF16)
    return _dot(hi, a) + _dot(lo, a)


def _head_part_matrix(width):
    def part(idx):
        within = idx % HEAD_PAD
        return (idx // HEAD_PAD) * 4 + jnp.where(within < MLA_NOPE, 0, jnp.where(within < MLA_QK, 1, 2))
    rows = part(lax.broadcasted_iota(jnp.int32, (width, width), 0))
    cols_i = lax.broadcasted_iota(jnp.int32, (width, width), 1)
    cols = part(cols_i)
    same = jnp.logical_and(rows == cols, cols_i % HEAD_PAD < MLA_QK)
    lane = lax.broadcasted_iota(jnp.int32, (1, width), 1) % HEAD_PAD
    inv_count = jnp.where(lane < MLA_NOPE, 1.0 / MLA_NOPE, jnp.where(lane < MLA_QK, 1.0 / MLA_ROPE, 0.0))
    return jnp.where(same, 1.0, 0.0).astype(BF16), inv_count


def _head_norm_rope(x, pmat, inv_count, gain_cos, gain_sin):
    inv = lax.rsqrt(_dot_hi_lo_rhs(x * x, pmat) * inv_count + EPS)
    shift = HEAD_PAD - MLA_ROPE
    x_rot = jnp.concatenate([pltpu.roll(x[:, s0:s0 + HEAD_PAD], shift, 1)
                             for s0 in range(0, x.shape[1], HEAD_PAD)], axis=1)
    return (x * inv) * gain_cos + (x_rot * inv) * gain_sin


def _mla_in_kernel(h_ref, lnw_ref, win_ref, qan_ref, wqb_ref, kvan_ref, gq_ref, gqrot_ref, gkr_ref, gkrrot_ref,
                   cos_ref, sin_ref, *rest, q_transposed):
    q_ref, lat_ref, kr_ref, gate_ref = rest[-4:]
    xn = _rms(h_ref[...], lnw_ref[...]).astype(BF16)
    c_q, c_kv, c_kr = 0, MLA_Q_LORA, MLA_Q_LORA + MLA_KV_LORA
    c_gate = c_kr + HEAD_PAD
    pair_w = 2 * HEAD_PAD
    _store_dot(gate_ref, xn, win_ref, c_gate)
    lat_ref[0] = _rms(_dot(xn, win_ref[:, c_kv:c_kr]), kvan_ref[...])
    pmat, inv_count = _head_part_matrix(pair_w)
    cos_t, sin_t = cos_ref[...], sin_ref[...]
    kr = _head_norm_rope(_dot(xn, win_ref[:, c_kr:c_gate]), pmat[:HEAD_PAD, :HEAD_PAD],
                         inv_count[:, :HEAD_PAD], gkr_ref[...] * cos_t[:, :HEAD_PAD],
                         gkrrot_ref[...] * sin_t[:, :HEAD_PAD])
    kr_ref[...] = kr[:, MLA_NOPE:MLA_QK]
    qa = _rms(_dot(xn, win_ref[:, c_q:c_kv]), qan_ref[...]).astype(BF16)
    gain_cos = gq_ref[...] * cos_t * SOFTMAX_LOG2_SCALE
    gain_sin = gqrot_ref[...] * sin_t * SOFTMAX_LOG2_SCALE
    for pair in range(MLA_HEADS // 2):
        qq = _dot(qa, wqb_ref[:, pair * pair_w:(pair + 1) * pair_w])
        qh = _head_norm_rope(qq, pmat, inv_count, gain_cos, gain_sin)
        if q_transposed:
            q_ref[0, pair * pair_w:(pair + 1) * pair_w, :] = qh.T.astype(q_ref.dtype)
        else:
            q_ref[:, pair * pair_w:(pair + 1) * pair_w] = qh.astype(q_ref.dtype)


def _stacked_out(prev, layer, count, block, index_tail, shape_tail, dtype):
    spec = pl.BlockSpec((1,) + block, lambda *g: (layer,) + index_tail(*g))
    shape = jax.ShapeDtypeStruct((count,) + shape_tail, dtype)
    if prev is None:
        return [], [], spec, shape
    return [pl.BlockSpec(memory_space=pl.ANY)], [prev], spec, shape


def _mla_in(h, lnw, w, rope, tm, q_transposed, lat_stack):
    m = h.shape[0]
    assert m % tm == 0
    cos_t, sin_t = rope
    period = cos_t.shape[0] // tm
    assert period * tm == cos_t.shape[0]
    rope_spec = pl.BlockSpec((tm, 2 * HEAD_PAD), lambda i: (i % period, 0))
    consts = [lnw, w["in_w"], w["q_a_norm"], w["q_b_w"], w["kv_a_norm"],
              w["gain_q"], w["gain_q_rot"], w["gain_kr"], w["gain_kr_rot"]]
    row = lambda wd: pl.BlockSpec((tm, wd), lambda i: (i, 0))
    qw = MLA_HEADS * HEAD_PAD
    if q_transposed:
        seq = period * tm
        assert m % seq == 0
        q_spec = pl.BlockSpec((1, qw, tm), lambda i: (i // period, 0, i % period))
        q_shape = jax.ShapeDtypeStruct((m // seq, qw, seq), BF16)
    else:
        q_spec, q_shape = row(qw), jax.ShapeDtypeStruct((m, qw), BF16)
    prev, layer, count = lat_stack
    extra_specs, extra_args, lat_spec, lat_shape = _stacked_out(
        prev, layer, count, (tm, MLA_KV_LORA), lambda i: (i, 0), (m, MLA_KV_LORA), F32)
    n_in = 1 + len(consts) + 2
    return pl.pallas_call(
        functools.partial(_mla_in_kernel, q_transposed=q_transposed),
        grid=(m // tm,),
        in_specs=[row(D_MODEL)] + [_const_spec(a.shape) for a in consts] + [rope_spec] * 2 + extra_specs,
        out_specs=[q_spec, lat_spec, row(MLA_ROPE), row(MLA_WIDTH)],
        input_output_aliases={n_in: 1} if extra_args else {},
        out_shape=[q_shape,
                   lat_shape,
                   jax.ShapeDtypeStruct((m, MLA_ROPE), F32),
                   jax.ShapeDtypeStruct((m, MLA_WIDTH), F32)],
        compiler_params=_cparams("arbitrary"),
        name="mla_in",
    )(h, *consts, cos_t, sin_t, *extra_args)


def _mla_kv_kernel(lat_ref, kr_ref, wk_ref, wv_ref, gkn_ref, k_ref, v_ref, *, v_transposed):
    latb = lat_ref[0].astype(BF16)
    if v_transposed:
        ones = jnp.ones((V_ONES_ROWS, latb.shape[0]), v_ref.dtype)
        for c in range(0, MLA_WIDTH, COL_CHUNK):
            vt = _dot(latb, wv_ref[:, c:c + COL_CHUNK]).T.astype(v_ref.dtype)
            for hh in range(COL_CHUNK // MLA_V):
                r0 = (c // MLA_V + hh) * V_SLOT_ROWS
                v_ref[0, r0:r0 + MLA_V, :] = vt[hh * MLA_V:(hh + 1) * MLA_V]
                v_ref[0, r0 + MLA_V:r0 + V_SLOT_ROWS, :] = ones
    else:
        _store_dot(v_ref, latb, wv_ref)
    rows = latb.shape[0]
    kr = jnp.concatenate([jnp.zeros((rows, MLA_NOPE), F32), kr_ref[...],
                          jnp.zeros((rows, HEAD_PAD - MLA_QK), F32)], axis=1)
    for h in range(MLA_HEADS):
        hs = slice(h * HEAD_PAD, (h + 1) * HEAD_PAD)
        kh = _dot(latb, wk_ref[:, hs])
        ms = jnp.sum(kh * kh, axis=-1, keepdims=True) * (1.0 / MLA_NOPE)
        k_ref[:, hs] = (kh * lax.rsqrt(ms + EPS) * gkn_ref[...] + kr).astype(k_ref.dtype)


def _mla_kv(lat, layer, kr, w, tm, v_transposed):
    m = lat.shape[1]
    assert m % tm == 0
    consts = [w["kv_k_w"], w["kv_v_w"], w["gain_kn"]]
    row = lambda wd: pl.BlockSpec((tm, wd), lambda i: (i, 0))
    if v_transposed:
        v_spec = pl.BlockSpec((1, MLA_HEADS * V_SLOT_ROWS, tm), lambda i: (i, 0, 0))
        v_shape = jax.ShapeDtypeStruct((m // tm, MLA_HEADS * V_SLOT_ROWS, tm), BF16)
    else:
        v_spec, v_shape = row(MLA_WIDTH), jax.ShapeDtypeStruct((m, MLA_WIDTH), BF16)
    return pl.pallas_call(
        functools.partial(_mla_kv_kernel, v_transposed=v_transposed),
        grid=(m // tm,),
        in_specs=[pl.BlockSpec((1, tm, MLA_KV_LORA), lambda i: (layer, i, 0)), row(MLA_ROPE)]
                 + [_const_spec(a.shape) for a in consts],
        out_specs=[row(MLA_HEADS * HEAD_PAD), v_spec],
        out_shape=[jax.ShapeDtypeStruct((m, MLA_HEADS * HEAD_PAD), BF16), v_shape],
        compiler_params=_cparams("arbitrary"),
        name="mla_kv",
    )(lat, kr, *consts)


def _gated_pair_out(o_ref, gate_ref, outs, pair):
    lane = lax.broadcasted_iota(jnp.int32, outs[0].shape, 1)
    ps = slice(pair * LANES, (pair + 1) * LANES)
    g = gate_ref[:, ps]
    o_ref[:, ps] = (jnp.where(lane < MLA_V, outs[0], outs[1]) * (g * jax.nn.sigmoid(g))).astype(o_ref.dtype)


def _attn_kernel(q_ref, k_ref, v_ref, gate_ref, o_ref, m_s, acc_s, s_s, *, tq):
    i = pl.program_id(1)
    key_c = lax.broadcasted_iota(jnp.int32, (tq, tq), 0) // CHUNK
    qry_c = lax.broadcasted_iota(jnp.int32, (tq, tq), 1) // CHUNK
    diag_mask = key_c <= qry_c
    m_s[...] = jnp.full(m_s.shape, NEG_BIG, F32)
    acc_s[...] = jnp.zeros(acc_s.shape, F32)

    def scores(j, slot):
        rows = pl.ds(pl.multiple_of(j * tq, tq), tq)
        for h in range(MLA_HEADS):
            hs = slice(h * HEAD_PAD, (h + 1) * HEAD_PAD)
            s_s[slot, h] = _dot(k_ref[rows, hs], q_ref[0, hs, :])

    def softmax_pv(j, mask, slot):
        for h in range(MLA_HEADS):
            s = s_s[slot, h]
            if mask is not None:
                s = jnp.where(mask, s, -jnp.inf)
            m_prev = m_s[h]
            m_new = jnp.maximum(m_prev, jnp.max(s, axis=0, keepdims=True))
            alpha = jnp.exp2(m_prev[0:1, :] - m_new[0:1, :])
            p = jnp.exp2(s - m_new[0:1, :]).astype(BF16)
            pv = _dot(v_ref[j, h * V_SLOT_ROWS:(h + 1) * V_SLOT_ROWS, :], p)
            acc_s[h] = alpha * acc_s[h] + pv
            m_s[h] = m_new

    def body(jj, carry):
        scores(2 * jj, 0)
        scores(2 * jj + 1, 1)
        softmax_pv(2 * jj, None, 0)
        softmax_pv(2 * jj + 1, None, 1)
        return carry

    lax.fori_loop(0, i // 2, body, 0)

    @pl.when(i % 2 == 1)
    def _odd_tile():
        scores(i - 1, 0)
        softmax_pv(i - 1, None, 0)

    scores(i, 1)
    softmax_pv(i, diag_mask, 1)
    for pair in range(MLA_HEADS // 2):
        o_t = jnp.concatenate([acc_s[2 * pair + sub, 0:MLA_V, :] / acc_s[2 * pair + sub, MLA_V:MLA_V + 1, :]
                               for sub in range(2)], axis=0)
        ps = slice(pair * LANES, (pair + 1) * LANES)
        g = gate_ref[:, ps]
        o_ref[:, ps] = (o_t.T * (g * jax.nn.sigmoid(g))).astype(o_ref.dtype)


def _attn_prompt(q_t, k, v_t, gate, batch, seq, tq):
    nq = seq // tq
    assert nq * tq == seq and tq % CHUNK == 0
    return pl.pallas_call(
        functools.partial(_attn_kernel, tq=tq),
        grid=(batch, nq),
        in_specs=[pl.BlockSpec((1, MLA_HEADS * HEAD_PAD, tq), lambda b, i: (b, 0, i)),
                  pl.BlockSpec((seq, MLA_HEADS * HEAD_PAD), lambda b, i: (b, 0)),
                  pl.BlockSpec((nq, MLA_HEADS * V_SLOT_ROWS, tq), lambda b, i: (b, 0, 0)),
                  pl.BlockSpec((tq, MLA_WIDTH), lambda b, i: (b * nq + i, 0))],
        out_specs=pl.BlockSpec((tq, MLA_WIDTH), lambda b, i: (b * nq + i, 0)),
        out_shape=jax.ShapeDtypeStruct((batch * seq, MLA_WIDTH), BF16),
        scratch_shapes=[pltpu.VMEM((MLA_HEADS, SUBLANES, tq), F32),
                        pltpu.VMEM((MLA_HEADS, V_SLOT_ROWS, tq), F32), pltpu.VMEM((2, MLA_HEADS, tq, tq), F32)],
        compiler_params=_cparams("arbitrary", "arbitrary"),
        name="attn_prompt",
    )(q_t, k, v_t, gate)


def _attn_step_kernel(q_ref, k_ref, v_ref, gate_ref, o_ref, s_s, *, q_pos0, n_keys):
    tq = q_ref.shape[0]
    tk = k_ref.shape[1]
    k_pos = lax.broadcasted_iota(jnp.int32, (tq, tk), 1)
    q_pos = lax.broadcasted_iota(jnp.int32, (tq, tk), 0) + q_pos0
    mask = jnp.logical_and(k_pos < n_keys, k_pos // CHUNK <= q_pos // CHUNK)
    for h in range(MLA_HEADS):
        hs = slice(h * HEAD_PAD, (h + 1) * HEAD_PAD)
        s_s[h] = _dot_nt(q_ref[:, hs], k_ref[0, :, hs])
    for pair in range(MLA_HEADS // 2):
        vs = slice(pair * LANES, (pair + 1) * LANES)
        outs = []
        for sub in range(2):
            s = jnp.where(mask, s_s[2 * pair + sub], -jnp.inf)
            p = jnp.exp2(s - jnp.max(s, axis=-1, keepdims=True))
            outs.append(_dot(p.astype(BF16), v_ref[0, :, vs]) / jnp.sum(p, axis=-1, keepdims=True))
        _gated_pair_out(o_ref, gate_ref, outs, pair)


def _attn_step(q, k, v, gate, batch, tq, n_keys):
    tk = k.shape[1]
    return pl.pallas_call(
        functools.partial(_attn_step_kernel, q_pos0=n_keys - tq, n_keys=n_keys),
        grid=(batch,),
        in_specs=[pl.BlockSpec((tq, MLA_HEADS * HEAD_PAD), lambda b: (b, 0)),
                  pl.BlockSpec((1, tk, MLA_HEADS * HEAD_PAD), lambda b: (b, 0, 0)),
                  pl.BlockSpec((1, tk, MLA_WIDTH), lambda b: (b, 0, 0)),
                  pl.BlockSpec((tq, MLA_WIDTH), lambda b: (b, 0))],
        out_specs=pl.BlockSpec((tq, MLA_WIDTH), lambda b: (b, 0)),
        out_shape=jax.ShapeDtypeStruct((batch * tq, MLA_WIDTH), BF16),
        scratch_shapes=[pltpu.VMEM((MLA_HEADS, tq, tk), F32)],
        compiler_params=_cparams("arbitrary"),
        name="attn_step",
    )(q, k, v, gate)


def _pad_cols(a, width):
    return jnp.pad(a, ((0, 0), (0, width - a.shape[1])))


def _ssd_weights(j, in_w, conv_w, conv_b, dt_bias, a_log, d, norm_w, out_w):
    zx = SSD_D_INNER + SSD_CONV_DIM
    return {
        "in_w": jnp.concatenate([0.5 * in_w[j][:, :SSD_D_INNER], in_w[j][:, SSD_D_INNER:zx],
                                 _pad_cols(in_w[j][:, zx:], LANES)], axis=1).astype(BF16),
        "conv_w": 0.5 * conv_w[j],
        "conv_b": 0.5 * conv_b[j][None, :],
        "dt_bias": _pad_cols(dt_bias[j][None, :], LANES),
        "a_log": _pad_cols(a_log[j][None, :], LANES),
        "d_exp": jnp.repeat(d[j], SSD_HEAD_DIM)[None, :],
        "norm_w": norm_w[j][None, :],
        "out_w": out_w[j].astype(BF16),
    }


def _head_pad_cols(w, per_head, take, offset=0):
    k = w.shape[0]
    w = w.reshape(k, MLA_HEADS, per_head)[:, :, offset:offset + take]
    return jnp.pad(w, ((0, 0), (0, 0), (0, HEAD_PAD - take))).reshape(k, MLA_HEADS * HEAD_PAD)


def _rotate_half(a):
    half = MLA_ROPE // 2
    return jnp.concatenate([-a[..., half:], a[..., :half]], axis=-1)


def _swap_halves(a):
    half = MLA_ROPE // 2
    return jnp.concatenate([a[..., half:], a[..., :half]], axis=-1)


def _rope_slot(a):
    pad = [(0, 0)] * (a.ndim - 1) + [(MLA_NOPE, HEAD_PAD - MLA_QK)]
    return jnp.pad(a, pad)


def _q_b_cols(w):
    k = w.shape[0]
    w = w.reshape(k, MLA_HEADS, MLA_QK)
    return jnp.concatenate([w, _rotate_half(w[:, :, MLA_NOPE:])], axis=-1).reshape(k, MLA_HEADS * HEAD_PAD)


def _mla_weights(j, in_w, q_a_norm, q_b_w, kv_a_norm, kv_b_w, qn, qr, kn, kr, out_w):
    c1, c2, c3 = MLA_Q_LORA, MLA_Q_LORA + MLA_KV_LORA, MLA_Q_LORA + MLA_KV_LORA + MLA_ROPE
    w = in_w[j]
    kr_cols = jnp.concatenate([jnp.zeros((w.shape[0], MLA_NOPE), F32), w[:, c2:c3],
                               _rotate_half(w[:, c2:c3])], axis=1)
    zeros = lambda n: jnp.zeros((n,), F32)
    gain_q = jnp.concatenate([qn[j], qr[j], zeros(HEAD_PAD - MLA_QK)])
    return {
        "in_w": jnp.concatenate([w[:, :c2], kr_cols, w[:, c3:]], axis=1).astype(BF16),
        "q_a_norm": q_a_norm[j][None, :],
        "q_b_w": _q_b_cols(q_b_w[j]).astype(BF16),
        "gain_q": jnp.tile(gain_q, 2)[None, :],
        "gain_q_rot": jnp.tile(_rope_slot(_swap_halves(qr[j])), 2)[None, :],
        "gain_kr_rot": _rope_slot(_swap_halves(kr[j]))[None, :],
        "kv_a_norm": kv_a_norm[j][None, :],
        "kv_k_w": _head_pad_cols(kv_b_w[j], MLA_NOPE + MLA_V, MLA_NOPE).astype(BF16),
        "kv_v_w": kv_b_w[j].reshape(MLA_KV_LORA, MLA_HEADS, MLA_NOPE + MLA_V)[:, :, MLA_NOPE:]
                  .reshape(MLA_KV_LORA, MLA_WIDTH).astype(BF16),
        "gain_kr": _rope_slot(kr[j])[None, :],
        "gain_kn": jnp.concatenate([kn[j], zeros(HEAD_PAD - MLA_NOPE)])[None, :],
        "out_w": out_w[j].astype(BF16),
    }


def _rope_tables(pos0, seq, rows):
    inv = 1.0 / (ROPE_BASE ** (jnp.arange(0, MLA_ROPE, 2, dtype=F32) / MLA_ROPE))
    ang = (pos0 + jnp.arange(seq)).astype(F32)[:, None] * inv[None, :]
    cos, sin = jnp.cos(ang), jnp.sin(ang)
    z = lambda n: jnp.zeros((seq, n), F32)
    cos_t = jnp.concatenate([jnp.ones((seq, MLA_NOPE), F32), cos, cos, z(HEAD_PAD - MLA_QK)], axis=1)
    sin_t = jnp.concatenate([z(MLA_NOPE), sin, sin, z(HEAD_PAD - MLA_QK)], axis=1)
    reps = max(1, rows // seq)
    return tuple(jnp.tile(t, (reps, 2)) for t in (cos_t, sin_t))


def _row_tile(m, want):
    tm = min(m, want)
    assert m % tm == 0
    return tm


def _ssd_layer(h, p, lnw, ple, w, batch, seq, chunk, conv_prev, h0, state_stack):
    tm = _row_tile(h.shape[0], ROW_TILE)
    zs, xs, bt, c, dt, conv_new = _ssd_in(h, lnw, w, conv_prev, batch, seq, min(seq, SSD_BLOCK_ROWS))
    yn, states = _ssd_scan(zs, xs, bt, c, dt, h0, w, batch, seq, chunk, state_stack)
    h = _out_ple(yn, h, p, w["out_w"], *ple, tm)
    return h, conv_new, states


def _mla_layer(h, p, lnw, ple, w, batch, seq, lat_past, kr_past, lat_stack):
    tm = _row_tile(h.shape[0], ROW_TILE)
    prompt = lat_past is None
    tm_in = _row_tile(h.shape[0], min(seq, MLA_IN_TILE) if prompt else MLA_IN_TILE)
    pos0 = 0 if prompt else lat_past.shape[1]
    rope = _rope_tables(pos0, seq, tm_in)
    q, lats, kr, gate = _mla_in(h, lnw, w, rope, tm_in, q_transposed=prompt, lat_stack=lat_stack)
    layer = lat_stack[1]
    if prompt:
        tile = min(seq, ATTN_TILE)
        k, v_t = _mla_kv(lats, layer, kr, w, tile, v_transposed=True)
        og = _attn_prompt(q, k, v_t, gate, batch, seq, tile)
    else:
        n_keys = pos0 + seq
        tk = -(-n_keys // LANES) * LANES
        pad = ((0, 0), (0, tk - n_keys), (0, 0))
        lat_all = jnp.pad(jnp.concatenate([lat_past, lats[layer].reshape(batch, seq, -1)], axis=1), pad)
        kr_all = jnp.pad(jnp.concatenate([kr_past, kr.reshape(batch, seq, -1)], axis=1), pad)
        k, v = _mla_kv(lat_all.reshape(1, batch * tk, -1), 0, kr_all.reshape(batch * tk, -1), w,
                       math.gcd(batch * tk, 512), v_transposed=False)
        og = _attn_step(q, k.reshape(batch, tk, -1), v.reshape(batch, tk, -1), gate, batch, seq, n_keys)
    h = _out_ple(og, h, p, w["out_w"], *ple, tm)
    return h, lats, kr.reshape(batch, seq, -1)


def kernel(x_prompt, x_sample, cache_conv, state_ssm, cache_kv_latent, cache_k_rope, p_prompt, p_sample, ln_w, ssd_in_w, ssd_conv_w, ssd_conv_b, ssd_dt_bias, ssd_A_log, ssd_D, ssd_norm_w, ssd_out_w, mla_in_w, mla_q_a_norm, mla_q_b_w, mla_kv_a_norm, mla_kv_b_w, mla_q_nope_norm, mla_q_rope_norm, mla_k_nope_norm, mla_k_rope_norm, mla_out_w, ple_up_w, ple_norm_w, ple_gate_w):
    bp, tp, d = x_prompt.shape
    bs, ts, _ = x_sample.shape
    hp = x_prompt.reshape(bp * tp, d)
    hs = x_sample.reshape(bs * ts, d)
    conv_p, kr_p, conv_s, kr_s = [], [], [], []
    lat_p = lat_s = ssm_p = ssm_s = None
    for i in range(DEPTH):
        j = i // 2
        lnw = ln_w[i][None, :]
        ple = (ple_norm_w[i][None, :], ple_gate_w[i].astype(BF16), ple_up_w[i].astype(BF16))
        pp = (p_prompt.reshape(DEPTH, bp * tp, PLE_DIM), i)
        ps = (p_sample.reshape(DEPTH, bs * ts, PLE_DIM), i)
        if i % 2 == 0:
            w = _ssd_weights(j, ssd_in_w, ssd_conv_w, ssd_conv_b, ssd_dt_bias, ssd_A_log, ssd_D,
                             ssd_norm_w, ssd_out_w)
            n_ssd = (DEPTH + 1) // 2
            hp, cp, ssm_p = _ssd_layer(hp, pp, lnw, ple, w, bp, tp, min(tp, SSD_SCAN_CHUNK), None, None,
                                       (ssm_p, j, n_ssd))
            hs, cs, ssm_s = _ssd_layer(hs, ps, lnw, ple, w, bs, ts, ts, cache_conv[j], state_ssm[j],
                                       (ssm_s, j, n_ssd))
            conv_p.append(cp); conv_s.append(cs)
        else:
            w = _mla_weights(j, mla_in_w, mla_q_a_norm, mla_q_b_w, mla_kv_a_norm, mla_kv_b_w,
                             mla_q_nope_norm, mla_q_rope_norm, mla_k_nope_norm, mla_k_rope_norm, mla_out_w)
            n_mla = DEPTH // 2
            hp, lat_p, rp = _mla_layer(hp, pp, lnw, ple, w, bp, tp, None, None, (lat_p, j, n_mla))
            hs, lat_s, rs = _mla_layer(hs, ps, lnw, ple, w, bs, ts, cache_kv_latent[j], cache_k_rope[j],
                                       (lat_s, j, n_mla))
            kr_p.append(rp); kr_s.append(rs)
    return (hp.reshape(bp, tp, d), hs.reshape(bs, ts, d),
            jnp.stack(conv_p), ssm_p.reshape(-1, bp, SSD_HEADS, SSD_HEAD_DIM, SSD_STATE),
            lat_p.reshape(-1, bp, tp, MLA_KV_LORA), jnp.stack(kr_p),
            jnp.stack(conv_s), ssm_s.reshape(-1, bs, SSD_HEADS, SSD_HEAD_DIM, SSD_STATE),
            lat_s.reshape(-1, bs, ts, MLA_KV_LORA), jnp.stack(kr_s))
```

```python
import functools
import math

import jax
import jax.numpy as jnp
from jax import lax
from jax.experimental import pallas as pl
from jax.experimental.pallas import tpu as pltpu

F32 = jnp.float32
BF16 = jnp.bfloat16

D_MODEL = 1024
DEPTH = 4
CHUNK = 64
PLE_DIM = 256
EPS = 1e-6

SSD_D_INNER = 2 * D_MODEL
SSD_HEAD_DIM = 64
SSD_HEADS = SSD_D_INNER // SSD_HEAD_DIM
SSD_GROUPS = 8
SSD_HPG = SSD_HEADS // SSD_GROUPS
SSD_STATE = 128
SSD_CONV_W = 4
SSD_BC = SSD_GROUPS * SSD_STATE
SSD_CONV_DIM = SSD_D_INNER + 2 * SSD_BC
SSD_GROUP_W = SSD_HPG * SSD_HEAD_DIM

MLA_HEADS = 16
MLA_NOPE = 64
MLA_ROPE = 32
MLA_V = 64
MLA_Q_LORA = 384
MLA_KV_LORA = 256
MLA_QK = MLA_NOPE + MLA_ROPE
MLA_WIDTH = MLA_HEADS * MLA_V
ROPE_BASE = 10000.0

LANES = 128
SUBLANES = 8
HEAD_PAD = LANES
VMEM_V7X_BYTES = 64 * 1024 * 1024
VMEM_LIMIT_BYTES = VMEM_V7X_BYTES * 13 // 16
NEG_BIG = -1e30
LOG2_E = math.log2(math.e)
SOFTMAX_LOG2_SCALE = (MLA_QK ** -0.5) * LOG2_E
COL_CHUNK = 512
CONV_ROWS = 128
V_ONES_ROWS = 16
V_SLOT_ROWS = MLA_V + V_ONES_ROWS
ROW_TILE = 512
MLA_IN_TILE = 1024
MLA_KV_TILE = 1024
ATTN_TILE = 256
SSD_BLOCK_ROWS = 256
SSD_SCAN_ROWS = 512
SSD_SCAN_CHUNK = 128


def _cparams(*sem):
    return pltpu.CompilerParams(dimension_semantics=sem, vmem_limit_bytes=VMEM_LIMIT_BYTES)


def _const_spec(shape):
    nd = len(shape)
    return pl.BlockSpec(shape, lambda *_: (0,) * nd, pipeline_mode=pl.Buffered(1))


def _rms(x, w):
    return x * lax.rsqrt(jnp.mean(x * x, axis=-1, keepdims=True) + EPS) * w


def _dot(a, b):
    return jnp.dot(a, b, preferred_element_type=F32)


def _dot_nt(a, b):
    return lax.dot_general(a, b, (((1,), (1,)), ((), ())), preferred_element_type=F32)


def _store_dot(o_ref, a, w_ref, col0=0):
    width = o_ref.shape[1]
    for c in range(0, width, COL_CHUNK):
        cw = min(COL_CHUNK, width - c)
        o_ref[:, c:c + cw] = _dot(a, w_ref[:, col0 + c:col0 + c + cw]).astype(o_ref.dtype)


def _ssd_in_kernel(*refs, tiles_per_seq, has_init):
    if has_init:
        (x_ref, lnw_ref, w_ref, convw_ref, convb_ref, cprev_ref,
         z_ref, xs_ref, b_ref, c_ref, dt_ref, convnew_ref, ext) = refs
    else:
        (x_ref, lnw_ref, w_ref, convw_ref, convb_ref,
         z_ref, xs_ref, b_ref, c_ref, dt_ref, convnew_ref, ext) = refs
    tm = x_ref.shape[0]
    tail = SSD_CONV_W - 1
    t = pl.program_id(0) % tiles_per_seq
    xn = _rms(x_ref[...], lnw_ref[...]).astype(BF16)

    @pl.when(t == 0)
    def _init():
        ext[0:SUBLANES, :] = jnp.zeros((SUBLANES, SSD_CONV_DIM), F32)
        if has_init:
            ext[SUBLANES - tail:SUBLANES, :] = cprev_ref[0]

    dt_ref[...] = _dot(xn, w_ref[:, SSD_D_INNER + SSD_CONV_DIM:])
    def project(c0):
        cs = slice(c0, c0 + COL_CHUNK)
        ext[SUBLANES:SUBLANES + tm, cs] = _dot(xn, w_ref[:, SSD_D_INNER + c0:SSD_D_INNER + c0 + COL_CHUNK])
        convnew_ref[0, :, cs] = ext[pl.ds(SUBLANES + tm - tail, tail), cs]

    project(0)
    for c0 in range(0, SSD_CONV_DIM, COL_CHUNK):
        cs = slice(c0, c0 + COL_CHUNK)
        if c0 + COL_CHUNK < SSD_CONV_DIM:
            project(c0 + COL_CHUNK)
        if c0 % (2 * COL_CHUNK) == 0:
            zc = slice(c0 // 2, c0 // 2 + COL_CHUNK)
            hz = _dot(xn, w_ref[:, zc])
            z_ref[:, zc] = hz * jnp.tanh(hz) + hz
        if c0 < SSD_D_INNER:
            o_ref, o0 = xs_ref, c0
        elif c0 < SSD_D_INNER + SSD_BC:
            o_ref, o0 = b_ref, c0 - SSD_D_INNER
        else:
            o_ref, o0 = c_ref, c0 - SSD_D_INNER - SSD_BC
        for r0 in range(0, tm, CONV_ROWS):
            rb = min(CONV_ROWS, tm - r0)
            for l0 in range(0, COL_CHUNK, LANES):
                ls = slice(c0 + l0, c0 + l0 + LANES)
                e = ext[r0:r0 + rb + SUBLANES, ls]
                e1 = pltpu.roll(e, 1, 0)
                far = e * convw_ref[1:2, ls] + e1 * convw_ref[0:1, ls]
                near = e * convw_ref[3:4, ls] + e1 * convw_ref[2:3, ls]
                hv = (convb_ref[:, ls] + near[SUBLANES:SUBLANES + rb]) + pltpu.roll(far, 2, 0)[SUBLANES:SUBLANES + rb]
                y = hv * jnp.tanh(hv) + hv
                if o_ref is b_ref:
                    b_ref[0, o0 + l0:o0 + l0 + LANES, r0:r0 + rb] = y.T
                else:
                    o_ref[r0:r0 + rb, o0 + l0:o0 + l0 + LANES] = y.astype(o_ref.dtype)
    ext[0:SUBLANES, :] = ext[tm:tm + SUBLANES, :]


def _ssd_in(x, lnw, w, conv_prev, batch, seq, tm):
    m, d = x.shape
    assert seq % tm == 0 and tm % SUBLANES == 0 and m == batch * seq
    tps = seq // tm
    has_init = conv_prev is not None
    tail = SSD_CONV_W - 1
    consts = [lnw, w["in_w"], w["conv_w"], w["conv_b"]]
    row = lambda wd: pl.BlockSpec((tm, wd), lambda i: (i, 0))
    conv_spec = pl.BlockSpec((1, tail, SSD_CONV_DIM), lambda i: (i // tps, 0, 0))
    in_specs = [row(d)] + [_const_spec(a.shape) for a in consts]
    args = [x] + consts
    if has_init:
        in_specs.append(conv_spec)
        args.append(conv_prev)
    return pl.pallas_call(
        functools.partial(_ssd_in_kernel, tiles_per_seq=tps, has_init=has_init),
        grid=(m // tm,),
        in_specs=in_specs,
        out_specs=[row(SSD_D_INNER), row(SSD_D_INNER), pl.BlockSpec((1, SSD_BC, tm), lambda i: (i, 0, 0)),
                   row(SSD_BC), row(LANES), conv_spec],
        out_shape=[jax.ShapeDtypeStruct((m, SSD_D_INNER), F32),
                   jax.ShapeDtypeStruct((m, SSD_D_INNER), F32),
                   jax.ShapeDtypeStruct((m // tm, SSD_BC, tm), F32),
                   jax.ShapeDtypeStruct((m, SSD_BC), BF16),
                   jax.ShapeDtypeStruct((m, LANES), F32),
                   jax.ShapeDtypeStruct((batch, tail, SSD_CONV_DIM), F32)],
        scratch_shapes=[pltpu.VMEM((tm + SUBLANES, SSD_CONV_DIM), F32)],
        compiler_params=_cparams("arbitrary"),
        name="ssd_in",
    )(*args)


def _dot_exact_lhs(a, x):
    x1 = x.astype(BF16)
    r1 = x - x1.astype(F32)
    x2 = r1.astype(BF16)
    x3 = (r1 - x2.astype(F32)).astype(BF16)
    return _dot(a, x1) + _dot(a, x2) + _dot(a, x3)


def _expand_heads(cols, g, lane_head):
    shape = (cols.shape[0], SSD_GROUP_W)
    h0 = SSD_HPG * g
    out = jnp.broadcast_to(cols[:, h0 + SSD_HPG - 1:h0 + SSD_HPG], shape)
    for r in range(SSD_HPG - 2, -1, -1):
        out = jnp.where(lane_head[:shape[0]] == r, jnp.broadcast_to(cols[:, h0 + r:h0 + r + 1], shape), out)
    return out


def _select_heads(parts, lane_head):
    out = parts[0]
    for r in range(1, len(parts)):
        out = jnp.where(lane_head == r, parts[r], out)
    return out


def _ssd_kernel(*refs, L, has_init):
    zs_ref, xs_ref, bt_ref, c_ref, dt_ref = refs[:5]
    h0_ref = refs[5] if has_init else None
    dtb_ref, alog_ref, dexp_ref, normw_ref = refs[5 + has_init:9 + has_init]
    yn_ref, ht_ref, st, cb_s = refs[-4:]
    c = pl.program_id(1)
    last = pl.num_programs(1) - 1

    @pl.when(c == 0)
    def _init():
        if has_init:
            for g in range(SSD_GROUPS):
                st[g] = h0_ref[0, g * SSD_GROUP_W:(g + 1) * SSD_GROUP_W, :].T
        else:
            st[...] = jnp.zeros(st.shape, F32)

    rows = dt_ref.shape[0]
    n_chunks = rows // L
    fused = L % LANES == 0
    group_n = lambda g: slice(g * SSD_STATE, (g + 1) * SSD_STATE)
    bt_rows = bt_ref.shape[2]

    def bt_chunk(g, ci):
        r0 = ci * L
        return bt_ref[r0 // bt_rows, group_n(g), r0 % bt_rows:r0 % bt_rows + L]

    for ci in range(n_chunks):
        rs = slice(ci * L, (ci + 1) * L)
        for g in range(SSD_GROUPS):
            cb_s[ci * SSD_GROUPS + g] = _dot(c_ref[rs, group_n(g)], bt_chunk(g, ci).astype(BF16))

    dt = jax.nn.softplus(dt_ref[...] + dtb_ref[...])
    dta = dt * (-jnp.exp(alog_ref[...]))
    row = lax.broadcasted_iota(jnp.int32, (rows, rows), 0)
    col = lax.broadcasted_iota(jnp.int32, (rows, rows), 1)
    same_chunk_tri = jnp.logical_and(row >= col, row // L == col // L)
    acs = _dot_exact_lhs(same_chunk_tri.astype(BF16), dta) * LOG2_E
    acs_t = acs.T
    dt_t = dt.T
    tri = lax.broadcasted_iota(jnp.int32, (L, L), 0) >= lax.broadcasted_iota(jnp.int32, (L, L), 1)
    lane_head = lax.broadcasted_iota(jnp.int32, (L, SSD_GROUP_W), 1) // SSD_HEAD_DIM
    lane_head_n = lax.broadcasted_iota(jnp.int32, (SSD_STATE, SSD_GROUP_W), 1) // SSD_HEAD_DIM

    for ci in range(n_chunks):
        rs = slice(ci * L, (ci + 1) * L)
        acs_c = acs[rs]
        acs_tc = acs_t[:, rs]
        dt_tc = dt_t[:, rs]
        cdec = jnp.exp2(acs_c[L - 1:L, :])
        wdt_t = jnp.exp2(acs_tc[:, L - 1:L] - acs_tc) * dt_tc
        for g in range(SSD_GROUPS):
            gs = slice(g * SSD_GROUP_W, (g + 1) * SSD_GROUP_W)
            xg = xs_ref[rs, gs]
            xgb = xg.astype(BF16)
            cg = c_ref[rs, group_n(g)].astype(F32)
            cb = cb_s[ci * SSD_GROUPS + g]
            sg = st[g]
            sgb = sg.astype(BF16)
            if fused:
                rhs = jnp.concatenate([xgb, sgb], axis=0)
            parts = []
            for r in range(SSD_HPG):
                h = SSD_HPG * g + r
                a_col = jnp.broadcast_to(acs_c[:, h:h + 1], (L, LANES))
                seg = a_col[:, :L] - acs_tc[h:h + 1, :]
                m = (cb * jnp.exp2(jnp.where(tri, seg, -jnp.inf)) * dt_tc[h:h + 1, :]).astype(BF16)
                ce = (cg * jnp.exp2(a_col)).astype(BF16)
                if fused:
                    parts.append(_dot(jnp.concatenate([m, ce], axis=1), rhs))
                else:
                    parts.append(_dot(m, xgb) + _dot(ce, sgb))
            y = _select_heads(parts, lane_head)
            y = y + dexp_ref[:, gs] * xg
            y = y * zs_ref[rs, gs]
            yn_ref[rs, gs] = _rms(y, normw_ref[:, gs]).astype(yn_ref.dtype)
            btg = bt_chunk(g, ci)
            upd = [_dot((btg * wdt_t[SSD_HPG * g + r:SSD_HPG * g + r + 1, :]).astype(BF16), xgb)
                   for r in range(SSD_HPG)]
            st[g] = sg * _expand_heads(cdec, g, lane_head) + _select_heads(upd, lane_head_n)

    @pl.when(c == last)
    def _state_out():
        for g in range(SSD_GROUPS):
            ht_ref[0, 0, g * SSD_GROUP_W:(g + 1) * SSD_GROUP_W, :] = st[g].T


def _ssd_scan(zs, xs, bt, c, dt, h0, w, batch, seq, L, state_stack):
    slab = bt.shape[2]
    rows = min(seq, max(slab, SSD_SCAN_ROWS))
    nc = seq // rows
    assert nc * rows == seq and rows % L == 0 and L % SUBLANES == 0 and rows % slab == 0 and slab % L == 0
    has_init = h0 is not None
    row_spec = lambda wd: pl.BlockSpec((rows, wd), lambda bi, ci: (bi * nc + ci, 0))
    state_spec = pl.BlockSpec((1, SSD_D_INNER, SSD_STATE), lambda bi, ci: (bi, 0, 0))
    bt_spec = pl.BlockSpec((rows // slab, SSD_BC, slab), lambda bi, ci: (bi * nc + ci, 0, 0))
    in_specs = [row_spec(SSD_D_INNER), row_spec(SSD_D_INNER), bt_spec, row_spec(SSD_BC), row_spec(LANES)]
    args = [zs, xs, bt, c, dt]
    if has_init:
        in_specs.append(state_spec)
        args.append(h0.reshape(batch, SSD_D_INNER, SSD_STATE))
    consts = [w["dt_bias"], w["a_log"], w["d_exp"], w["norm_w"]]
    prev, layer, count = state_stack
    extra_specs, extra_args, ht_spec, ht_shape = _stacked_out(
        prev, layer, count, (1, SSD_D_INNER, SSD_STATE), lambda bi, ci: (bi, 0, 0),
        (batch, SSD_D_INNER, SSD_STATE), F32)
    in_specs += [_const_spec(a.shape) for a in consts] + extra_specs
    args += consts + extra_args
    return pl.pallas_call(
        functools.partial(_ssd_kernel, L=L, has_init=has_init),
        grid=(batch, nc),
        in_specs=in_specs,
        out_specs=[row_spec(SSD_D_INNER), ht_spec],
        input_output_aliases={len(args) - 1: 1} if extra_args else {},
        out_shape=[jax.ShapeDtypeStruct((batch * seq, SSD_D_INNER), BF16), ht_shape],
        scratch_shapes=[pltpu.VMEM((SSD_GROUPS, SSD_STATE, SSD_GROUP_W), F32),
                        pltpu.VMEM((rows // L * SSD_GROUPS, L, L), F32)],
        compiler_params=_cparams("arbitrary", "arbitrary"),
        name="ssd_scan",
    )(*args)


def _out_ple_kernel(y_ref, h_ref, p_ref, wout_ref, plenorm_ref, wgate_ref, wup_ref, o_ref):
    h1 = h_ref[...] + _dot(y_ref[...], wout_ref[...])
    hn = _rms(h1, plenorm_ref[...]).astype(BF16)
    pb = p_ref[0].astype(BF16)
    for c in range(0, D_MODEL, COL_CHUNK):
        cs = slice(c, c + COL_CHUNK)
        gate = jax.nn.sigmoid(_dot(hn, wgate_ref[:, cs]))
        o_ref[:, cs] = h1[:, cs] + _dot(pb, wup_ref[:, cs]) * gate


def _out_ple(y, h, p, wout, plenorm, wgate, wup, tm):
    m, kd = y.shape
    assert m % tm == 0
    p_all, layer = p
    return pl.pallas_call(
        _out_ple_kernel,
        grid=(m // tm,),
        in_specs=[pl.BlockSpec((tm, kd), lambda i: (i, 0)),
                  pl.BlockSpec((tm, D_MODEL), lambda i: (i, 0)),
                  pl.BlockSpec((1, tm, PLE_DIM), lambda i: (layer, i, 0)),
                  _const_spec(wout.shape), _const_spec(plenorm.shape),
                  _const_spec(wgate.shape), _const_spec(wup.shape)],
        out_specs=pl.BlockSpec((tm, D_MODEL), lambda i: (i, 0)),
        out_shape=jax.ShapeDtypeStruct((m, D_MODEL), F32),
        compiler_params=_cparams("arbitrary"),
        name="out_ple",
    )(y, h, p_all, wout, plenorm, wgate, wup)


def _dot_hi_lo_rhs(x, a):
    hi = x.astype(BF16)
    lo = (x - hi.astype(F32)).astype(BF16)
    return _dot(hi, a) + _dot(lo, a)


def _head_part_matrix(width):
    def part(idx):
        within = idx % HEAD_PAD
        return (idx // HEAD_PAD) * 4 + jnp.where(within < MLA_NOPE, 0, jnp.where(within < MLA_QK, 1, 2))
    rows = part(lax.broadcasted_iota(jnp.int32, (width, width), 0))
    cols_i = lax.broadcasted_iota(jnp.int32, (width, width), 1)
    cols = part(cols_i)
    same = jnp.logical_and(rows == cols, cols_i % HEAD_PAD < MLA_QK)
    lane = lax.broadcasted_iota(jnp.int32, (1, width), 1) % HEAD_PAD
    inv_count = jnp.where(lane < MLA_NOPE, 1.0 / MLA_NOPE, jnp.where(lane < MLA_QK, 1.0 / MLA_ROPE, 0.0))
    return jnp.where(same, 1.0, 0.0).astype(BF16), inv_count


def _head_norm_rope(x, pmat, inv_count, gain_cos, gain_sin):
    inv = lax.rsqrt(_dot_hi_lo_rhs(x * x, pmat) * inv_count + EPS)
    shift = HEAD_PAD - MLA_ROPE
    x_rot = jnp.concatenate([pltpu.roll(x[:, s0:s0 + HEAD_PAD], shift, 1)
                             for s0 in range(0, x.shape[1], HEAD_PAD)], axis=1)
    return (x * inv) * gain_cos + (x_rot * inv) * gain_sin


def _mla_in_kernel(h_ref, lnw_ref, win_ref, qan_ref, wqb_ref, kvan_ref, gq_ref, gqrot_ref, gkr_ref, gkrrot_ref,
                   cos_ref, sin_ref, *rest, q_transposed):
    q_ref, lat_ref, kr_ref, gate_ref = rest[-4:]
    xn = _rms(h_ref[...], lnw_ref[...]).astype(BF16)
    c_q, c_kv, c_kr = 0, MLA_Q_LORA, MLA_Q_LORA + MLA_KV_LORA
    c_gate = c_kr + HEAD_PAD
    pair_w = 2 * HEAD_PAD
    _store_dot(gate_ref, xn, win_ref, c_gate)
    lat_ref[0] = _rms(_dot(xn, win_ref[:, c_kv:c_kr]), kvan_ref[...])
    pmat, inv_count = _head_part_matrix(pair_w)
    cos_t, sin_t = cos_ref[...], sin_ref[...]
    kr = _head_norm_rope(_dot(xn, win_ref[:, c_kr:c_gate]), pmat[:HEAD_PAD, :HEAD_PAD],
                         inv_count[:, :HEAD_PAD], gkr_ref[...] * cos_t[:, :HEAD_PAD],
                         gkrrot_ref[...] * sin_t[:, :HEAD_PAD])
    kr_ref[...] = kr[:, MLA_NOPE:MLA_QK]
    qa = _rms(_dot(xn, win_ref[:, c_q:c_kv]), qan_ref[...]).astype(BF16)
    gain_cos = gq_ref[...] * cos_t * SOFTMAX_LOG2_SCALE
    gain_sin = gqrot_ref[...] * sin_t * SOFTMAX_LOG2_SCALE
    for pair in range(MLA_HEADS // 2):
        qq = _dot(qa, wqb_ref[:, pair * pair_w:(pair + 1) * pair_w])
        qh = _head_norm_rope(qq, pmat, inv_count, gain_cos, gain_sin)
        if q_transposed:
            q_ref[0, pair * pair_w:(pair + 1) * pair_w, :] = qh.T.astype(q_ref.dtype)
        else:
            q_ref[:, pair * pair_w:(pair + 1) * pair_w] = qh.astype(q_ref.dtype)


def _stacked_out(prev, layer, count, block, index_tail, shape_tail, dtype):
    spec = pl.BlockSpec((1,) + block, lambda *g: (layer,) + index_tail(*g))
    shape = jax.ShapeDtypeStruct((count,) + shape_tail, dtype)
    if prev is None:
        return [], [], spec, shape
    return [pl.BlockSpec(memory_space=pl.ANY)], [prev], spec, shape


def _mla_in(h, lnw, w, rope, tm, q_transposed, lat_stack):
    m = h.shape[0]
    assert m % tm == 0
    cos_t, sin_t = rope
    period = cos_t.shape[0] // tm
    assert period * tm == cos_t.shape[0]
    rope_spec = pl.BlockSpec((tm, 2 * HEAD_PAD), lambda i: (i % period, 0))
    consts = [lnw, w["in_w"], w["q_a_norm"], w["q_b_w"], w["kv_a_norm"],
              w["gain_q"], w["gain_q_rot"], w["gain_kr"], w["gain_kr_rot"]]
    row = lambda wd: pl.BlockSpec((tm, wd), lambda i: (i, 0))
    qw = MLA_HEADS * HEAD_PAD
    if q_transposed:
        seq = period * tm
        assert m % seq == 0
        q_spec = pl.BlockSpec((1, qw, tm), lambda i: (i // period, 0, i % period))
        q_shape = jax.ShapeDtypeStruct((m // seq, qw, seq), BF16)
    else:
        q_spec, q_shape = row(qw), jax.ShapeDtypeStruct((m, qw), BF16)
    prev, layer, count = lat_stack
    extra_specs, extra_args, lat_spec, lat_shape = _stacked_out(
        prev, layer, count, (tm, MLA_KV_LORA), lambda i: (i, 0), (m, MLA_KV_LORA), F32)
    n_in = 1 + len(consts) + 2
    return pl.pallas_call(
        functools.partial(_mla_in_kernel, q_transposed=q_transposed),
        grid=(m // tm,),
        in_specs=[row(D_MODEL)] + [_const_spec(a.shape) for a in consts] + [rope_spec] * 2 + extra_specs,
        out_specs=[q_spec, lat_spec, row(MLA_ROPE), row(MLA_WIDTH)],
        input_output_aliases={n_in: 1} if extra_args else {},
        out_shape=[q_shape,
                   lat_shape,
                   jax.ShapeDtypeStruct((m, MLA_ROPE), F32),
                   jax.ShapeDtypeStruct((m, MLA_WIDTH), F32)],
        compiler_params=_cparams("arbitrary"),
        name="mla_in",
    )(h, *consts, cos_t, sin_t, *extra_args)


def _mla_kv_kernel(lat_ref, kr_ref, wk_ref, wv_ref, gkn_ref, k_ref, v_ref, *, v_transposed):
    latb = lat_ref[0].astype(BF16)
    if v_transposed:
        slab = v_ref.shape[2]
        ones = jnp.ones((V_ONES_ROWS, slab), v_ref.dtype)
        for c in range(0, MLA_WIDTH, COL_CHUNK):
            vt = _dot(latb, wv_ref[:, c:c + COL_CHUNK]).T.astype(v_ref.dtype)
            for t in range(v_ref.shape[0]):
                for hh in range(COL_CHUNK // MLA_V):
                    r0 = (c // MLA_V + hh) * V_SLOT_ROWS
                    v_ref[t, r0:r0 + MLA_V, :] = vt[hh * MLA_V:(hh + 1) * MLA_V, t * slab:(t + 1) * slab]
                    v_ref[t, r0 + MLA_V:r0 + V_SLOT_ROWS, :] = ones
    else:
        _store_dot(v_ref, latb, wv_ref)
    rows = latb.shape[0]
    kr = jnp.concatenate([jnp.zeros((rows, MLA_NOPE), F32), kr_ref[...],
                          jnp.zeros((rows, HEAD_PAD - MLA_QK), F32)], axis=1)
    for h in range(MLA_HEADS):
        hs = slice(h * HEAD_PAD, (h + 1) * HEAD_PAD)
        kh = _dot(latb, wk_ref[:, hs])
        ms = jnp.sum(kh * kh, axis=-1, keepdims=True) * (1.0 / MLA_NOPE)
        k_ref[:, hs] = (kh * lax.rsqrt(ms + EPS) * gkn_ref[...] + kr).astype(k_ref.dtype)


def _mla_kv(lat, layer, kr, w, tm, v_transposed, v_slab=None):
    m = lat.shape[1]
    assert m % tm == 0
    consts = [w["kv_k_w"], w["kv_v_w"], w["gain_kn"]]
    row = lambda wd: pl.BlockSpec((tm, wd), lambda i: (i, 0))
    if v_transposed:
        assert tm % v_slab == 0
        v_spec = pl.BlockSpec((tm // v_slab, MLA_HEADS * V_SLOT_ROWS, v_slab), lambda i: (i, 0, 0))
        v_shape = jax.ShapeDtypeStruct((m // v_slab, MLA_HEADS * V_SLOT_ROWS, v_slab), BF16)
    else:
        v_spec, v_shape = row(MLA_WIDTH), jax.ShapeDtypeStruct((m, MLA_WIDTH), BF16)
    return pl.pallas_call(
        functools.partial(_mla_kv_kernel, v_transposed=v_transposed),
        grid=(m // tm,),
        in_specs=[pl.BlockSpec((1, tm, MLA_KV_LORA), lambda i: (layer, i, 0)), row(MLA_ROPE)]
                 + [_const_spec(a.shape) for a in consts],
        out_specs=[row(MLA_HEADS * HEAD_PAD), v_spec],
        out_shape=[jax.ShapeDtypeStruct((m, MLA_HEADS * HEAD_PAD), BF16), v_shape],
        compiler_params=_cparams("arbitrary"),
        name="mla_kv",
    )(lat, kr, *consts)


def _gated_pair_out(o_ref, gate_ref, outs, pair):
    lane = lax.broadcasted_iota(jnp.int32, outs[0].shape, 1)
    ps = slice(pair * LANES, (pair + 1) * LANES)
    g = gate_ref[:, ps]
    o_ref[:, ps] = (jnp.where(lane < MLA_V, outs[0], outs[1]) * (g * jax.nn.sigmoid(g))).astype(o_ref.dtype)


def _attn_kernel(q_ref, k_ref, v_ref, gate_ref, o_ref, m_s, acc_s, s_s, *, tq):
    i = pl.program_id(1)
    key_c = lax.broadcasted_iota(jnp.int32, (tq, tq), 0) // CHUNK
    qry_c = lax.broadcasted_iota(jnp.int32, (tq, tq), 1) // CHUNK
    diag_mask = key_c <= qry_c
    m_s[...] = jnp.full(m_s.shape, NEG_BIG, F32)
    acc_s[...] = jnp.zeros(acc_s.shape, F32)

    def scores(j, slot):
        rows = pl.ds(pl.multiple_of(j * tq, tq), tq)
        for h in range(MLA_HEADS):
            hs = slice(h * HEAD_PAD, (h + 1) * HEAD_PAD)
            s_s[slot, h] = _dot(k_ref[rows, hs], q_ref[0, hs, :])

    def softmax_pv(j, mask, slot):
        for h in range(MLA_HEADS):
            s = s_s[slot, h]
            if mask is not None:
                s = jnp.where(mask, s, -jnp.inf)
            m_prev = m_s[h]
            m_new = jnp.maximum(m_prev, jnp.max(s, axis=0, keepdims=True))
            alpha = jnp.exp2(m_prev[0:1, :] - m_new[0:1, :])
            p = jnp.exp2(s - m_new[0:1, :]).astype(BF16)
            pv = _dot(v_ref[j, h * V_SLOT_ROWS:(h + 1) * V_SLOT_ROWS, :], p)
            acc_s[h] = alpha * acc_s[h] + pv
            m_s[h] = m_new

    def body(jj, carry):
        scores(2 * jj, 0)
        scores(2 * jj + 1, 1)
        softmax_pv(2 * jj, None, 0)
        softmax_pv(2 * jj + 1, None, 1)
        return carry

    lax.fori_loop(0, i // 2, body, 0)

    @pl.when(i % 2 == 1)
    def _odd_tile():
        scores(i - 1, 0)
        softmax_pv(i - 1, None, 0)

    scores(i, 1)
    softmax_pv(i, diag_mask, 1)
    for pair in range(MLA_HEADS // 2):
        o_t = jnp.concatenate([acc_s[2 * pair + sub, 0:MLA_V, :] / acc_s[2 * pair + sub, MLA_V:MLA_V + 1, :]
                               for sub in range(2)], axis=0)
        ps = slice(pair * LANES, (pair + 1) * LANES)
        g = gate_ref[:, ps]
        o_ref[:, ps] = (o_t.T * (g * jax.nn.sigmoid(g))).astype(o_ref.dtype)


def _attn_prompt(q_t, k, v_t, gate, batch, seq, tq):
    nq = seq // tq
    assert nq * tq == seq and tq % CHUNK == 0
    return pl.pallas_call(
        functools.partial(_attn_kernel, tq=tq),
        grid=(batch, nq),
        in_specs=[pl.BlockSpec((1, MLA_HEADS * HEAD_PAD, tq), lambda b, i: (b, 0, i)),
                  pl.BlockSpec((seq, MLA_HEADS * HEAD_PAD), lambda b, i: (b, 0)),
                  pl.BlockSpec((nq, MLA_HEADS * V_SLOT_ROWS, tq), lambda b, i: (b, 0, 0)),
                  pl.BlockSpec((tq, MLA_WIDTH), lambda b, i: (b * nq + i, 0))],
        out_specs=pl.BlockSpec((tq, MLA_WIDTH), lambda b, i: (b * nq + i, 0)),
        out_shape=jax.ShapeDtypeStruct((batch * seq, MLA_WIDTH), BF16),
        scratch_shapes=[pltpu.VMEM((MLA_HEADS, SUBLANES, tq), F32),
                        pltpu.VMEM((MLA_HEADS, V_SLOT_ROWS, tq), F32), pltpu.VMEM((2, MLA_HEADS, tq, tq), F32)],
        compiler_params=_cparams("arbitrary", "arbitrary"),
        name="attn_prompt",
    )(q_t, k, v_t, gate)


def _attn_step_kernel(q_ref, k_ref, v_ref, gate_ref, o_ref, s_s, *, q_pos0, n_keys):
    tq = q_ref.shape[0]
    tk = k_ref.shape[1]
    k_pos = lax.broadcasted_iota(jnp.int32, (tq, tk), 1)
    q_pos = lax.broadcasted_iota(jnp.int32, (tq, tk), 0) + q_pos0
    mask = jnp.logical_and(k_pos < n_keys, k_pos // CHUNK <= q_pos // CHUNK)
    for h in range(MLA_HEADS):
        hs = slice(h * HEAD_PAD, (h + 1) * HEAD_PAD)
        s_s[h] = _dot_nt(q_ref[:, hs], k_ref[0, :, hs])
    for pair in range(MLA_HEADS // 2):
        vs = slice(pair * LANES, (pair + 1) * LANES)
        outs = []
        for sub in range(2):
            s = jnp.where(mask, s_s[2 * pair + sub], -jnp.inf)
            p = jnp.exp2(s - jnp.max(s, axis=-1, keepdims=True))
            outs.append(_dot(p.astype(BF16), v_ref[0, :, vs]) / jnp.sum(p, axis=-1, keepdims=True))
        _gated_pair_out(o_ref, gate_ref, outs, pair)


def _attn_step(q, k, v, gate, batch, tq, n_keys):
    tk = k.shape[1]
    return pl.pallas_call(
        functools.partial(_attn_step_kernel, q_pos0=n_keys - tq, n_keys=n_keys),
        grid=(batch,),
        in_specs=[pl.BlockSpec((tq, MLA_HEADS * HEAD_PAD), lambda b: (b, 0)),
                  pl.BlockSpec((1, tk, MLA_HEADS * HEAD_PAD), lambda b: (b, 0, 0)),
                  pl.BlockSpec((1, tk, MLA_WIDTH), lambda b: (b, 0, 0)),
                  pl.BlockSpec((tq, MLA_WIDTH), lambda b: (b, 0))],
        out_specs=pl.BlockSpec((tq, MLA_WIDTH), lambda b: (b, 0)),
        out_shape=jax.ShapeDtypeStruct((batch * tq, MLA_WIDTH), BF16),
        scratch_shapes=[pltpu.VMEM((MLA_HEADS, tq, tk), F32)],
        compiler_params=_cparams("arbitrary"),
        name="attn_step",
    )(q, k, v, gate)


def _pad_cols(a, width):
    return jnp.pad(a, ((0, 0), (0, width - a.shape[1])))


def _ssd_weights(j, in_w, conv_w, conv_b, dt_bias, a_log, d, norm_w, out_w):
    zx = SSD_D_INNER + SSD_CONV_DIM
    return {
        "in_w": jnp.concatenate([0.5 * in_w[j][:, :SSD_D_INNER], in_w[j][:, SSD_D_INNER:zx],
                                 _pad_cols(in_w[j][:, zx:], LANES)], axis=1).astype(BF16),
        "conv_w": 0.5 * conv_w[j],
        "conv_b": 0.5 * conv_b[j][None, :],
        "dt_bias": _pad_cols(dt_bias[j][None, :], LANES),
        "a_log": _pad_cols(a_log[j][None, :], LANES),
        "d_exp": jnp.repeat(d[j], SSD_HEAD_DIM)[None, :],
        "norm_w": norm_w[j][None, :],
        "out_w": out_w[j].astype(BF16),
    }


def _head_pad_cols(w, per_head, take, offset=0):
    k = w.shape[0]
    w = w.reshape(k, MLA_HEADS, per_head)[:, :, offset:offset + take]
    return jnp.pad(w, ((0, 0), (0, 0), (0, HEAD_PAD - take))).reshape(k, MLA_HEADS * HEAD_PAD)


def _rotate_half(a):
    half = MLA_ROPE // 2
    return jnp.concatenate([-a[..., half:], a[..., :half]], axis=-1)


def _swap_halves(a):
    half = MLA_ROPE // 2
    return jnp.concatenate([a[..., half:], a[..., :half]], axis=-1)


def _rope_slot(a):
    pad = [(0, 0)] * (a.ndim - 1) + [(MLA_NOPE, HEAD_PAD - MLA_QK)]
    return jnp.pad(a, pad)


def _q_b_cols(w):
    k = w.shape[0]
    w = w.reshape(k, MLA_HEADS, MLA_QK)
    return jnp.concatenate([w, _rotate_half(w[:, :, MLA_NOPE:])], axis=-1).reshape(k, MLA_HEADS * HEAD_PAD)


def _mla_weights(j, in_w, q_a_norm, q_b_w, kv_a_norm, kv_b_w, qn, qr, kn, kr, out_w):
    c1, c2, c3 = MLA_Q_LORA, MLA_Q_LORA + MLA_KV_LORA, MLA_Q_LORA + MLA_KV_LORA + MLA_ROPE
    w = in_w[j]
    kr_cols = jnp.concatenate([jnp.zeros((w.shape[0], MLA_NOPE), F32), w[:, c2:c3],
                               _rotate_half(w[:, c2:c3])], axis=1)
    zeros = lambda n: jnp.zeros((n,), F32)
    gain_q = jnp.concatenate([qn[j], qr[j], zeros(HEAD_PAD - MLA_QK)])
    return {
        "in_w": jnp.concatenate([w[:, :c2], kr_cols, w[:, c3:]], axis=1).astype(BF16),
        "q_a_norm": q_a_norm[j][None, :],
        "q_b_w": _q_b_cols(q_b_w[j]).astype(BF16),
        "gain_q": jnp.tile(gain_q, 2)[None, :],
        "gain_q_rot": jnp.tile(_rope_slot(_swap_halves(qr[j])), 2)[None, :],
        "gain_kr_rot": _rope_slot(_swap_halves(kr[j]))[None, :],
        "kv_a_norm": kv_a_norm[j][None, :],
        "kv_k_w": _head_pad_cols(kv_b_w[j], MLA_NOPE + MLA_V, MLA_NOPE).astype(BF16),
        "kv_v_w": kv_b_w[j].reshape(MLA_KV_LORA, MLA_HEADS, MLA_NOPE + MLA_V)[:, :, MLA_NOPE:]
                  .reshape(MLA_KV_LORA, MLA_WIDTH).astype(BF16),
        "gain_kr": _rope_slot(kr[j])[None, :],
        "gain_kn": jnp.concatenate([kn[j], zeros(HEAD_PAD - MLA_NOPE)])[None, :],
        "out_w": out_w[j].astype(BF16),
    }


def _rope_tables(pos0, seq, rows):
    inv = 1.0 / (ROPE_BASE ** (jnp.arange(0, MLA_ROPE, 2, dtype=F32) / MLA_ROPE))
    ang = (pos0 + jnp.arange(seq)).astype(F32)[:, None] * inv[None, :]
    cos, sin = jnp.cos(ang), jnp.sin(ang)
    z = lambda n: jnp.zeros((seq, n), F32)
    cos_t = jnp.concatenate([jnp.ones((seq, MLA_NOPE), F32), cos, cos, z(HEAD_PAD - MLA_QK)], axis=1)
    sin_t = jnp.concatenate([z(MLA_NOPE), sin, sin, z(HEAD_PAD - MLA_QK)], axis=1)
    reps = max(1, rows // seq)
    return tuple(jnp.tile(t, (reps, 2)) for t in (cos_t, sin_t))


def _row_tile(m, want):
    tm = min(m, want)
    assert m % tm == 0
    return tm


def _ssd_layer(h, p, lnw, ple, w, batch, seq, chunk, conv_prev, h0, state_stack):
    tm = _row_tile(h.shape[0], ROW_TILE)
    zs, xs, bt, c, dt, conv_new = _ssd_in(h, lnw, w, conv_prev, batch, seq, min(seq, SSD_BLOCK_ROWS))
    yn, states = _ssd_scan(zs, xs, bt, c, dt, h0, w, batch, seq, chunk, state_stack)
    h = _out_ple(yn, h, p, w["out_w"], *ple, tm)
    return h, conv_new, states


def _mla_layer(h, p, lnw, ple, w, batch, seq, lat_past, kr_past, lat_stack):
    tm = _row_tile(h.shape[0], ROW_TILE)
    prompt = lat_past is None
    tm_in = _row_tile(h.shape[0], min(seq, MLA_IN_TILE) if prompt else MLA_IN_TILE)
    pos0 = 0 if prompt else lat_past.shape[1]
    rope = _rope_tables(pos0, seq, tm_in)
    q, lats, kr, gate = _mla_in(h, lnw, w, rope, tm_in, q_transposed=prompt, lat_stack=lat_stack)
    layer = lat_stack[1]
    if prompt:
        tile = min(seq, ATTN_TILE)
        k, v_t = _mla_kv(lats, layer, kr, w, _row_tile(h.shape[0], max(tile, MLA_KV_TILE)),
                         v_transposed=True, v_slab=tile)
        og = _attn_prompt(q, k, v_t, gate, batch, seq, tile)
    else:
        n_keys = pos0 + seq
        tk = -(-n_keys // LANES) * LANES
        pad = ((0, 0), (0, tk - n_keys), (0, 0))
        lat_all = jnp.pad(jnp.concatenate([lat_past, lats[layer].reshape(batch, seq, -1)], axis=1), pad)
        kr_all = jnp.pad(jnp.concatenate([kr_past, kr.reshape(batch, seq, -1)], axis=1), pad)
        k, v = _mla_kv(lat_all.reshape(1, batch * tk, -1), 0, kr_all.reshape(batch * tk, -1), w,
                       math.gcd(batch * tk, 512), v_transposed=False)
        og = _attn_step(q, k.reshape(batch, tk, -1), v.reshape(batch, tk, -1), gate, batch, seq, n_keys)
    h = _out_ple(og, h, p, w["out_w"], *ple, tm)
    return h, lats, kr.reshape(batch, seq, -1)


def kernel(x_prompt, x_sample, cache_conv, state_ssm, cache_kv_latent, cache_k_rope, p_prompt, p_sample, ln_w, ssd_in_w, ssd_conv_w, ssd_conv_b, ssd_dt_bias, ssd_A_log, ssd_D, ssd_norm_w, ssd_out_w, mla_in_w, mla_q_a_norm, mla_q_b_w, mla_kv_a_norm, mla_kv_b_w, mla_q_nope_norm, mla_q_rope_norm, mla_k_nope_norm, mla_k_rope_norm, mla_out_w, ple_up_w, ple_norm_w, ple_gate_w):
    bp, tp, d = x_prompt.shape
    bs, ts, _ = x_sample.shape
    hp = x_prompt.reshape(bp * tp, d)
    hs = x_sample.reshape(bs * ts, d)
    conv_p, kr_p, conv_s, kr_s = [], [], [], []
    lat_p = lat_s = ssm_p = ssm_s = None
    for i in range(DEPTH):
        j = i // 2
        lnw = ln_w[i][None, :]
        ple = (ple_norm_w[i][None, :], ple_gate_w[i].astype(BF16), ple_up_w[i].astype(BF16))
        pp = (p_prompt.reshape(DEPTH, bp * tp, PLE_DIM), i)
        ps = (p_sample.reshape(DEPTH, bs * ts, PLE_DIM), i)
        if i % 2 == 0:
            w = _ssd_weights(j, ssd_in_w, ssd_conv_w, ssd_conv_b, ssd_dt_bias, ssd_A_log, ssd_D,
                             ssd_norm_w, ssd_out_w)
            n_ssd = (DEPTH + 1) // 2
            hp, cp, ssm_p = _ssd_layer(hp, pp, lnw, ple, w, bp, tp, min(tp, SSD_SCAN_CHUNK), None, None,
                                       (ssm_p, j, n_ssd))
            hs, cs, ssm_s = _ssd_layer(hs, ps, lnw, ple, w, bs, ts, ts, cache_conv[j], state_ssm[j],
                                       (ssm_s, j, n_ssd))
            conv_p.append(cp); conv_s.append(cs)
        else:
            w = _mla_weights(j, mla_in_w, mla_q_a_norm, mla_q_b_w, mla_kv_a_norm, mla_kv_b_w,
                             mla_q_nope_norm, mla_q_rope_norm, mla_k_nope_norm, mla_k_rope_norm, mla_out_w)
            n_mla = DEPTH // 2
            hp, lat_p, rp = _mla_layer(hp, pp, lnw, ple, w, bp, tp, None, None, (lat_p, j, n_mla))
            hs, lat_s, rs = _mla_layer(hs, ps, lnw, ple, w, bs, ts, cache_kv_latent[j], cache_k_rope[j],
                                       (lat_s, j, n_mla))
            kr_p.append(rp); kr_s.append(rs)
    return (hp.reshape(bp, tp, d), hs.reshape(bs, ts, d),
            jnp.stack(conv_p), ssm_p.reshape(-1, bp, SSD_HEADS, SSD_HEAD_DIM, SSD_STATE),
            lat_p.reshape(-1, bp, tp, MLA_KV_LORA), jnp.stack(kr_p),
            jnp.stack(conv_s), ssm_s.reshape(-1, bs, SSD_HEADS, SSD_HEAD_DIM, SSD_STATE),
            lat_s.reshape(-1, bs, ts, MLA_KV_LORA), jnp.stack(kr_s))
```

```python
import functools
import math

import jax
import jax.numpy as jnp
from jax import lax
from jax.experimental import pallas as pl
from jax.experimental.pallas import tpu as pltpu

F32 = jnp.float32
BF16 = jnp.bfloat16

D_MODEL = 1024
DEPTH = 4
CHUNK = 64
PLE_DIM = 256
EPS = 1e-6

SSD_D_INNER = 2 * D_MODEL
SSD_HEAD_DIM = 64
SSD_HEADS = SSD_D_INNER // SSD_HEAD_DIM
SSD_GROUPS = 8
SSD_HPG = SSD_HEADS // SSD_GROUPS
SSD_STATE = 128
SSD_CONV_W = 4
SSD_BC = SSD_GROUPS * SSD_STATE
SSD_CONV_DIM = SSD_D_INNER + 2 * SSD_BC
SSD_GROUP_W = SSD_HPG * SSD_HEAD_DIM

MLA_HEADS = 16
MLA_NOPE = 64
MLA_ROPE = 32
MLA_V = 64
MLA_Q_LORA = 384
MLA_KV_LORA = 256
MLA_QK = MLA_NOPE + MLA_ROPE
MLA_WIDTH = MLA_HEADS * MLA_V
ROPE_BASE = 10000.0

LANES = 128
SUBLANES = 8
HEAD_PAD = LANES
VMEM_V7X_BYTES = 64 * 1024 * 1024
VMEM_LIMIT_BYTES = VMEM_V7X_BYTES * 13 // 16
NEG_BIG = -1e30
LOG2_E = math.log2(math.e)
SOFTMAX_LOG2_SCALE = (MLA_QK ** -0.5) * LOG2_E
COL_CHUNK = 512
CONV_ROWS = 128
V_ONES_ROWS = 16
V_SLOT_ROWS = MLA_V + V_ONES_ROWS
ROW_TILE = 512
MLA_IN_TILE = 1024
MLA_KV_TILE = 1024
ATTN_TILE = 256
SSD_BLOCK_ROWS = 256
SSD_SCAN_ROWS = 512
SSD_SCAN_CHUNK = 128


def _cparams(*sem):
    return pltpu.CompilerParams(dimension_semantics=sem, vmem_limit_bytes=VMEM_LIMIT_BYTES)


def _const_spec(shape):
    nd = len(shape)
    return pl.BlockSpec(shape, lambda *_: (0,) * nd, pipeline_mode=pl.Buffered(1))


def _rms(x, w):
    return x * lax.rsqrt(jnp.mean(x * x, axis=-1, keepdims=True) + EPS) * w


def _dot(a, b):
    return jnp.dot(a, b, preferred_element_type=F32)


def _dot_nt(a, b):
    return lax.dot_general(a, b, (((1,), (1,)), ((), ())), preferred_element_type=F32)


def _store_dot(o_ref, a, w_ref, col0=0):
    width = o_ref.shape[1]
    for c in range(0, width, COL_CHUNK):
        cw = min(COL_CHUNK, width - c)
        o_ref[:, c:c + cw] = _dot(a, w_ref[:, col0 + c:col0 + c + cw]).astype(o_ref.dtype)


def _ssd_in_kernel(*refs, tiles_per_seq, has_init):
    if has_init:
        (x_ref, lnw_ref, w_ref, convw_ref, convb_ref, cprev_ref,
         z_ref, xs_ref, b_ref, c_ref, dt_ref, convnew_ref, ext) = refs
    else:
        (x_ref, lnw_ref, w_ref, convw_ref, convb_ref,
         z_ref, xs_ref, b_ref, c_ref, dt_ref, convnew_ref, ext) = refs
    tm = x_ref.shape[0]
    tail = SSD_CONV_W - 1
    t = pl.program_id(0) % tiles_per_seq
    xn = _rms(x_ref[...], lnw_ref[...]).astype(BF16)

    @pl.when(t == 0)
    def _init():
        ext[0:SUBLANES, :] = jnp.zeros((SUBLANES, SSD_CONV_DIM), F32)
        if has_init:
            ext[SUBLANES - tail:SUBLANES, :] = cprev_ref[0]

    dt_ref[...] = _dot(xn, w_ref[:, SSD_D_INNER + SSD_CONV_DIM:])
    def project(c0):
        cs = slice(c0, c0 + COL_CHUNK)
        ext[SUBLANES:SUBLANES + tm, cs] = _dot(xn, w_ref[:, SSD_D_INNER + c0:SSD_D_INNER + c0 + COL_CHUNK])
        convnew_ref[0, :, cs] = ext[pl.ds(SUBLANES + tm - tail, tail), cs]

    project(0)
    for c0 in range(0, SSD_CONV_DIM, COL_CHUNK):
        cs = slice(c0, c0 + COL_CHUNK)
        if c0 + COL_CHUNK < SSD_CONV_DIM:
            project(c0 + COL_CHUNK)
        if c0 % (2 * COL_CHUNK) == 0:
            zc = slice(c0 // 2, c0 // 2 + COL_CHUNK)
            hz = _dot(xn, w_ref[:, zc])
            z_ref[:, zc] = hz * jnp.tanh(hz) + hz
        if c0 < SSD_D_INNER:
            o_ref, o0 = xs_ref, c0
        elif c0 < SSD_D_INNER + SSD_BC:
            o_ref, o0 = b_ref, c0 - SSD_D_INNER
        else:
            o_ref, o0 = c_ref, c0 - SSD_D_INNER - SSD_BC
        for r0 in range(0, tm, CONV_ROWS):
            rb = min(CONV_ROWS, tm - r0)
            for l0 in range(0, COL_CHUNK, LANES):
                ls = slice(c0 + l0, c0 + l0 + LANES)
                e = ext[r0:r0 + rb + SUBLANES, ls]
                e1 = pltpu.roll(e, 1, 0)
                far = e * convw_ref[1:2, ls] + e1 * convw_ref[0:1, ls]
                near = e * convw_ref[3:4, ls] + e1 * convw_ref[2:3, ls]
                hv = (convb_ref[:, ls] + near[SUBLANES:SUBLANES + rb]) + pltpu.roll(far, 2, 0)[SUBLANES:SUBLANES + rb]
                y = hv * jnp.tanh(hv) + hv
                if o_ref is b_ref:
                    b_ref[0, o0 + l0:o0 + l0 + LANES, r0:r0 + rb] = y.T
                else:
                    o_ref[r0:r0 + rb, o0 + l0:o0 + l0 + LANES] = y.astype(o_ref.dtype)
    ext[0:SUBLANES, :] = ext[tm:tm + SUBLANES, :]


def _ssd_in(x, lnw, w, conv_prev, batch, seq, tm):
    m, d = x.shape
    assert seq % tm == 0 and tm % SUBLANES == 0 and m == batch * seq
    tps = seq // tm
    has_init = conv_prev is not None
    tail = SSD_CONV_W - 1
    consts = [lnw, w["in_w"], w["conv_w"], w["conv_b"]]
    row = lambda wd: pl.BlockSpec((tm, wd), lambda i: (i, 0))
    conv_spec = pl.BlockSpec((1, tail, SSD_CONV_DIM), lambda i: (i // tps, 0, 0))
    in_specs = [row(d)] + [_const_spec(a.shape) for a in consts]
    args = [x] + consts
    if has_init:
        in_specs.append(conv_spec)
        args.append(conv_prev)
    return pl.pallas_call(
        functools.partial(_ssd_in_kernel, tiles_per_seq=tps, has_init=has_init),
        grid=(m // tm,),
        in_specs=in_specs,
        out_specs=[row(SSD_D_INNER), row(SSD_D_INNER), pl.BlockSpec((1, SSD_BC, tm), lambda i: (i, 0, 0)),
                   row(SSD_BC), row(LANES), conv_spec],
        out_shape=[jax.ShapeDtypeStruct((m, SSD_D_INNER), F32),
                   jax.ShapeDtypeStruct((m, SSD_D_INNER), F32),
                   jax.ShapeDtypeStruct((m // tm, SSD_BC, tm), F32),
                   jax.ShapeDtypeStruct((m, SSD_BC), BF16),
                   jax.ShapeDtypeStruct((m, LANES), F32),
                   jax.ShapeDtypeStruct((batch, tail, SSD_CONV_DIM), F32)],
        scratch_shapes=[pltpu.VMEM((tm + SUBLANES, SSD_CONV_DIM), F32)],
        compiler_params=_cparams("arbitrary"),
        name="ssd_in",
    )(*args)


def _dot_exact_lhs(a, x):
    x1 = x.astype(BF16)
    r1 = x - x1.astype(F32)
    x2 = r1.astype(BF16)
    x3 = (r1 - x2.astype(F32)).astype(BF16)
    return _dot(a, x1) + _dot(a, x2) + _dot(a, x3)


def _expand_heads(cols, g, lane_head):
    shape = (cols.shape[0], SSD_GROUP_W)
    h0 = SSD_HPG * g
    out = jnp.broadcast_to(cols[:, h0 + SSD_HPG - 1:h0 + SSD_HPG], shape)
    for r in range(SSD_HPG - 2, -1, -1):
        out = jnp.where(lane_head[:shape[0]] == r, jnp.broadcast_to(cols[:, h0 + r:h0 + r + 1], shape), out)
    return out


def _select_heads(parts, lane_head):
    out = parts[0]
    for r in range(1, len(parts)):
        out = jnp.where(lane_head == r, parts[r], out)
    return out


def _ssd_kernel(*refs, L, has_init):
    zs_ref, xs_ref, bt_ref, c_ref, dt_ref = refs[:5]
    h0_ref = refs[5] if has_init else None
    dtb_ref, alog_ref, dexp_ref, normw_ref = refs[5 + has_init:9 + has_init]
    yn_ref, ht_ref, st, cb_s = refs[-4:]
    c = pl.program_id(1)
    last = pl.num_programs(1) - 1

    @pl.when(c == 0)
    def _init():
        if has_init:
            for g in range(SSD_GROUPS):
                st[g] = h0_ref[0, g * SSD_GROUP_W:(g + 1) * SSD_GROUP_W, :].T
        else:
            st[...] = jnp.zeros(st.shape, F32)

    rows = dt_ref.shape[0]
    n_chunks = rows // L
    fused = L % LANES == 0
    group_n = lambda g: slice(g * SSD_STATE, (g + 1) * SSD_STATE)
    bt_rows = bt_ref.shape[2]

    def bt_chunk(g, ci):
        r0 = ci * L
        return bt_ref[r0 // bt_rows, group_n(g), r0 % bt_rows:r0 % bt_rows + L]

    for ci in range(n_chunks):
        rs = slice(ci * L, (ci + 1) * L)
        for g in range(SSD_GROUPS):
            cb_s[ci * SSD_GROUPS + g] = _dot(c_ref[rs, group_n(g)], bt_chunk(g, ci).astype(BF16))

    dt = jax.nn.softplus(dt_ref[...] + dtb_ref[...])
    dta = dt * (-jnp.exp(alog_ref[...]))
    row = lax.broadcasted_iota(jnp.int32, (rows, rows), 0)
    col = lax.broadcasted_iota(jnp.int32, (rows, rows), 1)
    same_chunk_tri = jnp.logical_and(row >= col, row // L == col // L)
    acs = _dot_exact_lhs(same_chunk_tri.astype(BF16), dta) * LOG2_E
    acs_t = acs.T
    dt_t = dt.T
    tri = lax.broadcasted_iota(jnp.int32, (L, L), 0) >= lax.broadcasted_iota(jnp.int32, (L, L), 1)
    lane_head = lax.broadcasted_iota(jnp.int32, (L, SSD_GROUP_W), 1) // SSD_HEAD_DIM
    lane_head_n = lax.broadcasted_iota(jnp.int32, (SSD_STATE, SSD_GROUP_W), 1) // SSD_HEAD_DIM

    for ci in range(n_chunks):
        rs = slice(ci * L, (ci + 1) * L)
        acs_c = acs[rs]
        acs_tc = acs_t[:, rs]
        dt_tc = dt_t[:, rs]
        cdec = jnp.exp2(acs_c[L - 1:L, :])
        wdt_t = jnp.exp2(acs_tc[:, L - 1:L] - acs_tc) * dt_tc
        for g in range(SSD_GROUPS):
            gs = slice(g * SSD_GROUP_W, (g + 1) * SSD_GROUP_W)
            xg = xs_ref[rs, gs]
            xgb = xg.astype(BF16)
            cg = c_ref[rs, group_n(g)].astype(F32)
            cb = cb_s[ci * SSD_GROUPS + g]
            sg = st[g]
            sgb = sg.astype(BF16)
            if fused:
                rhs = jnp.concatenate([xgb, sgb], axis=0)
            parts = []
            for r in range(SSD_HPG):
                h = SSD_HPG * g + r
                a_col = jnp.broadcast_to(acs_c[:, h:h + 1], (L, LANES))
                seg = a_col[:, :L] - acs_tc[h:h + 1, :]
                m = (cb * jnp.exp2(jnp.where(tri, seg, -jnp.inf)) * dt_tc[h:h + 1, :]).astype(BF16)
                ce = (cg * jnp.exp2(a_col)).astype(BF16)
                if fused:
                    parts.append(_dot(jnp.concatenate([m, ce], axis=1), rhs))
                else:
                    parts.append(_dot(m, xgb) + _dot(ce, sgb))
            y = _select_heads(parts, lane_head)
            y = y + dexp_ref[:, gs] * xg
            y = y * zs_ref[rs, gs]
            yn_ref[rs, gs] = _rms(y, normw_ref[:, gs]).astype(yn_ref.dtype)
            btg = bt_chunk(g, ci)
            upd = [_dot((btg * wdt_t[SSD_HPG * g + r:SSD_HPG * g + r + 1, :]).astype(BF16), xgb)
                   for r in range(SSD_HPG)]
            st[g] = sg * _expand_heads(cdec, g, lane_head) + _select_heads(upd, lane_head_n)

    @pl.when(c == last)
    def _state_out():
        for g in range(SSD_GROUPS):
            ht_ref[0, 0, g * SSD_GROUP_W:(g + 1) * SSD_GROUP_W, :] = st[g].T


def _ssd_scan(zs, xs, bt, c, dt, h0, w, batch, seq, L, state_stack):
    slab = bt.shape[2]
    rows = min(seq, max(slab, SSD_SCAN_ROWS))
    nc = seq // rows
    assert nc * rows == seq and rows % L == 0 and L % SUBLANES == 0 and rows % slab == 0 and slab % L == 0
    has_init = h0 is not None
    row_spec = lambda wd: pl.BlockSpec((rows, wd), lambda bi, ci: (bi * nc + ci, 0))
    state_spec = pl.BlockSpec((1, SSD_D_INNER, SSD_STATE), lambda bi, ci: (bi, 0, 0))
    bt_spec = pl.BlockSpec((rows // slab, SSD_BC, slab), lambda bi, ci: (bi * nc + ci, 0, 0))
    in_specs = [row_spec(SSD_D_INNER), row_spec(SSD_D_INNER), bt_spec, row_spec(SSD_BC), row_spec(LANES)]
    args = [zs, xs, bt, c, dt]
    if has_init:
        in_specs.append(state_spec)
        args.append(h0.reshape(batch, SSD_D_INNER, SSD_STATE))
    consts = [w["dt_bias"], w["a_log"], w["d_exp"], w["norm_w"]]
    prev, layer, count = state_stack
    extra_specs, extra_args, ht_spec, ht_shape = _stacked_out(
        prev, layer, count, (1, SSD_D_INNER, SSD_STATE), lambda bi, ci: (bi, 0, 0),
        (batch, SSD_D_INNER, SSD_STATE), F32)
    in_specs += [_const_spec(a.shape) for a in consts] + extra_specs
    args += consts + extra_args
    return pl.pallas_call(
        functools.partial(_ssd_kernel, L=L, has_init=has_init),
        grid=(batch, nc),
        in_specs=in_specs,
        out_specs=[row_spec(SSD_D_INNER), ht_spec],
        input_output_aliases={len(args) - 1: 1} if extra_args else {},
        out_shape=[jax.ShapeDtypeStruct((batch * seq, SSD_D_INNER), BF16), ht_shape],
        scratch_shapes=[pltpu.VMEM((SSD_GROUPS, SSD_STATE, SSD_GROUP_W), F32),
                        pltpu.VMEM((rows // L * SSD_GROUPS, L, L), F32)],
        compiler_params=_cparams("arbitrary", "arbitrary"),
        name="ssd_scan",
    )(*args)


def _out_ple_kernel(y_ref, h_ref, p_ref, wout_ref, plenorm_ref, wgate_ref, wup_ref, o_ref):
    h1 = h_ref[...] + _dot(y_ref[...], wout_ref[...])
    hn = _rms(h1, plenorm_ref[...]).astype(BF16)
    pb = p_ref[0].astype(BF16)
    for c in range(0, D_MODEL, COL_CHUNK):
        cs = slice(c, c + COL_CHUNK)
        gate = jax.nn.sigmoid(_dot(hn, wgate_ref[:, cs]))
        o_ref[:, cs] = h1[:, cs] + _dot(pb, wup_ref[:, cs]) * gate


def _out_ple(y, h, p, wout, plenorm, wgate, wup, tm):
    m, kd = y.shape
    assert m % tm == 0
    p_all, layer = p
    return pl.pallas_call(
        _out_ple_kernel,
        grid=(m // tm,),
        in_specs=[pl.BlockSpec((tm, kd), lambda i: (i, 0)),
                  pl.BlockSpec((tm, D_MODEL), lambda i: (i, 0)),
                  pl.BlockSpec((1, tm, PLE_DIM), lambda i: (layer, i, 0)),
                  _const_spec(wout.shape), _const_spec(plenorm.shape),
                  _const_spec(wgate.shape), _const_spec(wup.shape)],
        out_specs=pl.BlockSpec((tm, D_MODEL), lambda i: (i, 0)),
        out_shape=jax.ShapeDtypeStruct((m, D_MODEL), F32),
        compiler_params=_cparams("arbitrary"),
        name="out_ple",
    )(y, h, p_all, wout, plenorm, wgate, wup)


def _dot_hi_lo_rhs(x, a):
    hi = x.astype(BF16)
    lo = (x - hi.astype(F32)).astype(BF16)
    return _dot(hi, a) + _dot(lo, a)


def _head_part_matrix(width):
    def part(idx):
        within = idx % HEAD_PAD
        return (idx // HEAD_PAD) * 4 + jnp.where(within < MLA_NOPE, 0, jnp.where(within < MLA_QK, 1, 2))
    rows = part(lax.broadcasted_iota(jnp.int32, (width, width), 0))
    cols_i = lax.broadcasted_iota(jnp.int32, (width, width), 1)
    cols = part(cols_i)
    same = jnp.logical_and(rows == cols, cols_i % HEAD_PAD < MLA_QK)
    lane = lax.broadcasted_iota(jnp.int32, (1, width), 1) % HEAD_PAD
    inv_count = jnp.where(lane < MLA_NOPE, 1.0 / MLA_NOPE, jnp.where(lane < MLA_QK, 1.0 / MLA_ROPE, 0.0))
    return jnp.where(same, 1.0, 0.0).astype(BF16), inv_count


def _head_norm_rope(x, pmat, inv_count, gain_cos, gain_sin):
    inv = lax.rsqrt(_dot_hi_lo_rhs(x * x, pmat) * inv_count + EPS)
    shift = HEAD_PAD - MLA_ROPE
    x_rot = jnp.concatenate([pltpu.roll(x[:, s0:s0 + HEAD_PAD], shift, 1)
                             for s0 in range(0, x.shape[1], HEAD_PAD)], axis=1)
    return (x * inv) * gain_cos + (x_rot * inv) * gain_sin


def _mla_in_kernel(h_ref, lnw_ref, win_ref, qan_ref, wqb_ref, kvan_ref, gq_ref, gqrot_ref, gkr_ref, gkrrot_ref,
                   cos_ref, sin_ref, *rest, q_transposed):
    q_ref, lat_ref, kr_ref, gate_ref = rest[-4:]
    xn = _rms(h_ref[...], lnw_ref[...]).astype(BF16)
    c_q, c_kv, c_kr = 0, MLA_Q_LORA, MLA_Q_LORA + MLA_KV_LORA
    c_gate = c_kr + HEAD_PAD
    pair_w = 2 * HEAD_PAD
    _store_dot(gate_ref, xn, win_ref, c_gate)
    lat_ref[0] = _rms(_dot(xn, win_ref[:, c_kv:c_kr]), kvan_ref[...])
    pmat, inv_count = _head_part_matrix(pair_w)
    cos_t, sin_t = cos_ref[...], sin_ref[...]
    kr = _head_norm_rope(_dot(xn, win_ref[:, c_kr:c_gate]), pmat[:HEAD_PAD, :HEAD_PAD],
                         inv_count[:, :HEAD_PAD], gkr_ref[...] * cos_t[:, :HEAD_PAD],
                         gkrrot_ref[...] * sin_t[:, :HEAD_PAD])
    kr_ref[...] = kr[:, MLA_NOPE:MLA_QK]
    qa = _rms(_dot(xn, win_ref[:, c_q:c_kv]), qan_ref[...]).astype(BF16)
    gain_cos = gq_ref[...] * cos_t * SOFTMAX_LOG2_SCALE
    gain_sin = gqrot_ref[...] * sin_t * SOFTMAX_LOG2_SCALE
    for pair in range(MLA_HEADS // 2):
        qq = _dot(qa, wqb_ref[:, pair * pair_w:(pair + 1) * pair_w])
        qh = _head_norm_rope(qq, pmat, inv_count, gain_cos, gain_sin)
        if q_transposed:
            q_ref[0, pair * pair_w:(pair + 1) * pair_w, :] = qh.T.astype(q_ref.dtype)
        else:
            q_ref[:, pair * pair_w:(pair + 1) * pair_w] = qh.astype(q_ref.dtype)


def _stacked_out(prev, layer, count, block, index_tail, shape_tail, dtype):
    spec = pl.BlockSpec((1,) + block, lambda *g: (layer,) + index_tail(*g))
    shape = jax.ShapeDtypeStruct((count,) + shape_tail, dtype)
    if prev is None:
        return [], [], spec, shape
    return [pl.BlockSpec(memory_space=pl.ANY)], [prev], spec, shape


def _mla_in(h, lnw, w, rope, tm, q_transposed, lat_stack):
    m = h.shape[0]
    assert m % tm == 0
    cos_t, sin_t = rope
    period = cos_t.shape[0] // tm
    assert period * tm == cos_t.shape[0]
    rope_spec = pl.BlockSpec((tm, 2 * HEAD_PAD), lambda i: (i % period, 0))
    consts = [lnw, w["in_w"], w["q_a_norm"], w["q_b_w"], w["kv_a_norm"],
              w["gain_q"], w["gain_q_rot"], w["gain_kr"], w["gain_kr_rot"]]
    row = lambda wd: pl.BlockSpec((tm, wd), lambda i: (i, 0))
    qw = MLA_HEADS * HEAD_PAD
    if q_transposed:
        seq = period * tm
        assert m % seq == 0
        q_spec = pl.BlockSpec((1, qw, tm), lambda i: (i // period, 0, i % period))
        q_shape = jax.ShapeDtypeStruct((m // seq, qw, seq), BF16)
    else:
        q_spec, q_shape = row(qw), jax.ShapeDtypeStruct((m, qw), BF16)
    prev, layer, count = lat_stack
    extra_specs, extra_args, lat_spec, lat_shape = _stacked_out(
        prev, layer, count, (tm, MLA_KV_LORA), lambda i: (i, 0), (m, MLA_KV_LORA), F32)
    n_in = 1 + len(consts) + 2
    return pl.pallas_call(
        functools.partial(_mla_in_kernel, q_transposed=q_transposed),
        grid=(m // tm,),
        in_specs=[row(D_MODEL)] + [_const_spec(a.shape) for a in consts] + [rope_spec] * 2 + extra_specs,
        out_specs=[q_spec, lat_spec, row(MLA_ROPE), row(MLA_WIDTH)],
        input_output_aliases={n_in: 1} if extra_args else {},
        out_shape=[q_shape,
                   lat_shape,
                   jax.ShapeDtypeStruct((m, MLA_ROPE), F32),
                   jax.ShapeDtypeStruct((m, MLA_WIDTH), F32)],
        compiler_params=_cparams("arbitrary"),
        name="mla_in",
    )(h, *consts, cos_t, sin_t, *extra_args)


def _mla_kv_kernel(lat_ref, kr_ref, wk_ref, wv_ref, gkn_ref, k_ref, v_ref, *, v_transposed):
    latb = lat_ref[0].astype(BF16)
    if v_transposed:
        slab = v_ref.shape[2]
        ones = jnp.ones((V_ONES_ROWS, slab), v_ref.dtype)
        for c in range(0, MLA_WIDTH, COL_CHUNK):
            vt = _dot(latb, wv_ref[:, c:c + COL_CHUNK]).T.astype(v_ref.dtype)
            for t in range(v_ref.shape[0]):
                for hh in range(COL_CHUNK // MLA_V):
                    r0 = (c // MLA_V + hh) * V_SLOT_ROWS
                    v_ref[t, r0:r0 + MLA_V, :] = vt[hh * MLA_V:(hh + 1) * MLA_V, t * slab:(t + 1) * slab]
                    v_ref[t, r0 + MLA_V:r0 + V_SLOT_ROWS, :] = ones
    else:
        _store_dot(v_ref, latb, wv_ref)
    rows = latb.shape[0]
    kr = jnp.concatenate([jnp.zeros((rows, MLA_NOPE), F32), kr_ref[...],
                          jnp.zeros((rows, HEAD_PAD - MLA_QK), F32)], axis=1)
    for h in range(MLA_HEADS):
        hs = slice(h * HEAD_PAD, (h + 1) * HEAD_PAD)
        kh = _dot(latb, wk_ref[:, hs])
        ms = jnp.sum(kh * kh, axis=-1, keepdims=True) * (1.0 / MLA_NOPE)
        k_ref[:, hs] = (kh * lax.rsqrt(ms + EPS) * gkn_ref[...] + kr).astype(k_ref.dtype)


def _mla_kv(lat, layer, kr, w, tm, v_transposed, v_slab=None):
    m = lat.shape[1]
    assert m % tm == 0
    consts = [w["kv_k_w"], w["kv_v_w"], w["gain_kn"]]
    row = lambda wd: pl.BlockSpec((tm, wd), lambda i: (i, 0))
    if v_transposed:
        assert tm % v_slab == 0
        v_spec = pl.BlockSpec((tm // v_slab, MLA_HEADS * V_SLOT_ROWS, v_slab), lambda i: (i, 0, 0))
        v_shape = jax.ShapeDtypeStruct((m // v_slab, MLA_HEADS * V_SLOT_ROWS, v_slab), BF16)
    else:
        v_spec, v_shape = row(MLA_WIDTH), jax.ShapeDtypeStruct((m, MLA_WIDTH), BF16)
    return pl.pallas_call(
        functools.partial(_mla_kv_kernel, v_transposed=v_transposed),
        grid=(m // tm,),
        in_specs=[pl.BlockSpec((1, tm, MLA_KV_LORA), lambda i: (layer, i, 0)), row(MLA_ROPE)]
                 + [_const_spec(a.shape) for a in consts],
        out_specs=[row(MLA_HEADS * HEAD_PAD), v_spec],
        out_shape=[jax.ShapeDtypeStruct((m, MLA_HEADS * HEAD_PAD), BF16), v_shape],
        compiler_params=_cparams("arbitrary"),
        name="mla_kv",
    )(lat, kr, *consts)


def _gated_pair_out(o_ref, gate_ref, outs, pair):
    lane = lax.broadcasted_iota(jnp.int32, outs[0].shape, 1)
    ps = slice(pair * LANES, (pair + 1) * LANES)
    g = gate_ref[:, ps]
    o_ref[:, ps] = (jnp.where(lane < MLA_V, outs[0], outs[1]) * (g * jax.nn.sigmoid(g))).astype(o_ref.dtype)


def _attn_kernel(q_ref, k_ref, v_ref, gate_ref, o_ref, m_s, acc_s, s_s, *, tq):
    i = pl.program_id(1)
    key_c = lax.broadcasted_iota(jnp.int32, (tq, tq), 0) // CHUNK
    qry_c = lax.broadcasted_iota(jnp.int32, (tq, tq), 1) // CHUNK
    diag_mask = key_c <= qry_c
    m_s[...] = jnp.full(m_s.shape, NEG_BIG, F32)
    acc_s[...] = jnp.zeros(acc_s.shape, F32)

    def scores(j, slot):
        rows = pl.ds(pl.multiple_of(j * tq, tq), tq)
        for h in range(MLA_HEADS):
            hs = slice(h * HEAD_PAD, (h + 1) * HEAD_PAD)
            s_s[slot, h] = _dot(k_ref[rows, hs], q_ref[0, hs, :])

    def softmax_pv(j, mask, slot):
        for h in range(MLA_HEADS):
            s = s_s[slot, h]
            if mask is not None:
                s = jnp.where(mask, s, -jnp.inf)
            m_prev = m_s[h]
            m_new = jnp.maximum(m_prev, jnp.max(s, axis=0, keepdims=True))
            alpha = jnp.exp2(m_prev[0:1, :] - m_new[0:1, :])
            p = jnp.exp2(s - m_new[0:1, :]).astype(BF16)
            pv = _dot(v_ref[j, h * V_SLOT_ROWS:(h + 1) * V_SLOT_ROWS, :], p)
            acc_s[h] = alpha * acc_s[h] + pv
            m_s[h] = m_new

    def body(jj, carry):
        scores(2 * jj, 0)
        scores(2 * jj + 1, 1)
        softmax_pv(2 * jj, None, 0)
        softmax_pv(2 * jj + 1, None, 1)
        return carry

    lax.fori_loop(0, i // 2, body, 0)

    @pl.when(i % 2 == 1)
    def _odd_tile():
        scores(i - 1, 0)
        softmax_pv(i - 1, None, 0)

    scores(i, 1)
    softmax_pv(i, diag_mask, 1)
    for pair in range(MLA_HEADS // 2):
        o_t = jnp.concatenate([acc_s[2 * pair + sub, 0:MLA_V, :] / acc_s[2 * pair + sub, MLA_V:MLA_V + 1, :]
                               for sub in range(2)], axis=0)
        ps = slice(pair * LANES, (pair + 1) * LANES)
        g = gate_ref[:, ps]
        o_ref[:, ps] = (o_t.T * (g * jax.nn.sigmoid(g))).astype(o_ref.dtype)


def _attn_prompt(q_t, k, v_t, gate, batch, seq, tq):
    nq = seq // tq
    assert nq * tq == seq and tq % CHUNK == 0
    return pl.pallas_call(
        functools.partial(_attn_kernel, tq=tq),
        grid=(batch, nq),
        in_specs=[pl.BlockSpec((1, MLA_HEADS * HEAD_PAD, tq), lambda b, i: (b, 0, i)),
                  pl.BlockSpec((seq, MLA_HEADS * HEAD_PAD), lambda b, i: (b, 0)),
                  pl.BlockSpec((nq, MLA_HEADS * V_SLOT_ROWS, tq), lambda b, i: (b, 0, 0)),
                  pl.BlockSpec((tq, MLA_WIDTH), lambda b, i: (b * nq + i, 0))],
        out_specs=pl.BlockSpec((tq, MLA_WIDTH), lambda b, i: (b * nq + i, 0)),
        out_shape=jax.ShapeDtypeStruct((batch * seq, MLA_WIDTH), BF16),
        scratch_shapes=[pltpu.VMEM((MLA_HEADS, SUBLANES, tq), F32),
                        pltpu.VMEM((MLA_HEADS, V_SLOT_ROWS, tq), F32), pltpu.VMEM((2, MLA_HEADS, tq, tq), F32)],
        compiler_params=_cparams("arbitrary", "arbitrary"),
        name="attn_prompt",
    )(q_t, k, v_t, gate)


def _attn_step_kernel(q_ref, k_ref, v_ref, gate_ref, o_ref, s_s, *, q_pos0, n_keys):
    tq = q_ref.shape[0]
    tk = k_ref.shape[1]
    k_pos = lax.broadcasted_iota(jnp.int32, (tq, tk), 1)
    q_pos = lax.broadcasted_iota(jnp.int32, (tq, tk), 0) + q_pos0
    mask = jnp.logical_and(k_pos < n_keys, k_pos // CHUNK <= q_pos // CHUNK)
    for h in range(MLA_HEADS):
        hs = slice(h * HEAD_PAD, (h + 1) * HEAD_PAD)
        s_s[h] = _dot_nt(q_ref[:, hs], k_ref[0, :, hs])
    for pair in range(MLA_HEADS // 2):
        vs = slice(pair * LANES, (pair + 1) * LANES)
        outs = []
        for sub in range(2):
            s = jnp.where(mask, s_s[2 * pair + sub], -jnp.inf)
            p = jnp.exp2(s - jnp.max(s, axis=-1, keepdims=True))
            outs.append(_dot(p.astype(BF16), v_ref[0, :, vs]) / jnp.sum(p, axis=-1, keepdims=True))
        _gated_pair_out(o_ref, gate_ref, outs, pair)


def _attn_step(q, k, v, gate, batch, tq, n_keys):
    tk = k.shape[1]
    return pl.pallas_call(
        functools.partial(_attn_step_kernel, q_pos0=n_keys - tq, n_keys=n_keys),
        grid=(batch,),
        in_specs=[pl.BlockSpec((tq, MLA_HEADS * HEAD_PAD), lambda b: (b, 0)),
                  pl.BlockSpec((1, tk, MLA_HEADS * HEAD_PAD), lambda b: (b, 0, 0)),
                  pl.BlockSpec((1, tk, MLA_WIDTH), lambda b: (b, 0, 0)),
                  pl.BlockSpec((tq, MLA_WIDTH), lambda b: (b, 0))],
        out_specs=pl.BlockSpec((tq, MLA_WIDTH), lambda b: (b, 0)),
        out_shape=jax.ShapeDtypeStruct((batch * tq, MLA_WIDTH), BF16),
        scratch_shapes=[pltpu.VMEM((MLA_HEADS, tq, tk), F32)],
        compiler_params=_cparams("arbitrary"),
        name="attn_step",
    )(q, k, v, gate)


def _pad_cols(a, width):
    return jnp.pad(a, ((0, 0), (0, width - a.shape[1])))


def _ssd_weights(j, in_w, conv_w, conv_b, dt_bias, a_log, d, norm_w, out_w):
    zx = SSD_D_INNER + SSD_CONV_DIM
    return {
        "in_w": jnp.concatenate([0.5 * in_w[j][:, :SSD_D_INNER], in_w[j][:, SSD_D_INNER:zx],
                                 _pad_cols(in_w[j][:, zx:], LANES)], axis=1).astype(BF16),
        "conv_w": 0.5 * conv_w[j],
        "conv_b": 0.5 * conv_b[j][None, :],
        "dt_bias": _pad_cols(dt_bias[j][None, :], LANES),
        "a_log": _pad_cols(a_log[j][None, :], LANES),
        "d_exp": jnp.repeat(d[j], SSD_HEAD_DIM)[None, :],
        "norm_w": norm_w[j][None, :],
        "out_w": out_w[j].astype(BF16),
    }


def _head_pad_cols(w, per_head, take, offset=0):
    k = w.shape[0]
    w = w.reshape(k, MLA_HEADS, per_head)[:, :, offset:offset + take]
    return jnp.pad(w, ((0, 0), (0, 0), (0, HEAD_PAD - take))).reshape(k, MLA_HEADS * HEAD_PAD)


def _rotate_half(a):
    half = MLA_ROPE // 2
    return jnp.concatenate([-a[..., half:], a[..., :half]], axis=-1)


def _swap_halves(a):
    half = MLA_ROPE // 2
    return jnp.concatenate([a[..., half:], a[..., :half]], axis=-1)


def _rope_slot(a):
    pad = [(0, 0)] * (a.ndim - 1) + [(MLA_NOPE, HEAD_PAD - MLA_QK)]
    return jnp.pad(a, pad)


def _q_b_cols(w):
    k = w.shape[0]
    w = w.reshape(k, MLA_HEADS, MLA_QK)
    return jnp.concatenate([w, _rotate_half(w[:, :, MLA_NOPE:])], axis=-1).reshape(k, MLA_HEADS * HEAD_PAD)


def _mla_weights(j, in_w, q_a_norm, q_b_w, kv_a_norm, kv_b_w, qn, qr, kn, kr, out_w):
    c1, c2, c3 = MLA_Q_LORA, MLA_Q_LORA + MLA_KV_LORA, MLA_Q_LORA + MLA_KV_LORA + MLA_ROPE
    w = in_w[j]
    kr_cols = jnp.concatenate([jnp.zeros((w.shape[0], MLA_NOPE), F32), w[:, c2:c3],
                               _rotate_half(w[:, c2:c3])], axis=1)
    zeros = lambda n: jnp.zeros((n,), F32)
    gain_q = jnp.concatenate([qn[j], qr[j], zeros(HEAD_PAD - MLA_QK)])
    return {
        "in_w": jnp.concatenate([w[:, :c2], kr_cols, w[:, c3:]], axis=1).astype(BF16),
        "q_a_norm": q_a_norm[j][None, :],
        "q_b_w": _q_b_cols(q_b_w[j]).astype(BF16),
        "gain_q": jnp.tile(gain_q, 2)[None, :],
        "gain_q_rot": jnp.tile(_rope_slot(_swap_halves(qr[j])), 2)[None, :],
        "gain_kr_rot": _rope_slot(_swap_halves(kr[j]))[None, :],
        "kv_a_norm": kv_a_norm[j][None, :],
        "kv_k_w": _head_pad_cols(kv_b_w[j], MLA_NOPE + MLA_V, MLA_NOPE).astype(BF16),
        "kv_v_w": kv_b_w[j].reshape(MLA_KV_LORA, MLA_HEADS, MLA_NOPE + MLA_V)[:, :, MLA_NOPE:]
                  .reshape(MLA_KV_LORA, MLA_WIDTH).astype(BF16),
        "gain_kr": _rope_slot(kr[j])[None, :],
        "gain_kn": jnp.concatenate([kn[j], zeros(HEAD_PAD - MLA_NOPE)])[None, :],
        "out_w": out_w[j].astype(BF16),
    }


def _rope_tables(pos0, seq, rows):
    inv = 1.0 / (ROPE_BASE ** (jnp.arange(0, MLA_ROPE, 2, dtype=F32) / MLA_ROPE))
    ang = (pos0 + jnp.arange(seq)).astype(F32)[:, None] * inv[None, :]
    cos, sin = jnp.cos(ang), jnp.sin(ang)
    z = lambda n: jnp.zeros((seq, n), F32)
    cos_t = jnp.concatenate([jnp.ones((seq, MLA_NOPE), F32), cos, cos, z(HEAD_PAD - MLA_QK)], axis=1)
    sin_t = jnp.concatenate([z(MLA_NOPE), sin, sin, z(HEAD_PAD - MLA_QK)], axis=1)
    reps = max(1, rows // seq)
    return tuple(jnp.tile(t, (reps, 2)) for t in (cos_t, sin_t))


def _row_tile(m, want):
    tm = min(m, want)
    assert m % tm == 0
    return tm


def _ssd_layer(h, p, lnw, ple, w, batch, seq, chunk, conv_prev, h0, state_stack):
    tm = _row_tile(h.shape[0], ROW_TILE)
    zs, xs, bt, c, dt, conv_new = _ssd_in(h, lnw, w, conv_prev, batch, seq, min(seq, SSD_BLOCK_ROWS))
    yn, states = _ssd_scan(zs, xs, bt, c, dt, h0, w, batch, seq, chunk, state_stack)
    h = _out_ple(yn, h, p, w["out_w"], *ple, tm)
    return h, conv_new, states


def _mla_layer(h, p, lnw, ple, w, batch, seq, lat_past, kr_past, lat_stack):
    tm = _row_tile(h.shape[0], ROW_TILE)
    prompt = lat_past is None
    tm_in = _row_tile(h.shape[0], min(seq, MLA_IN_TILE) if prompt else MLA_IN_TILE)
    pos0 = 0 if prompt else lat_past.shape[1]
    rope = _rope_tables(pos0, seq, tm_in)
    q, lats, kr, gate = _mla_in(h, lnw, w, rope, tm_in, q_transposed=prompt, lat_stack=lat_stack)
    layer = lat_stack[1]
    if prompt:
        tile = min(seq, ATTN_TILE)
        k, v_t = _mla_kv(lats, layer, kr, w, _row_tile(h.shape[0], max(tile, MLA_KV_TILE)),
                         v_transposed=True, v_slab=tile)
        og = _attn_prompt(q, k, v_t, gate, batch, seq, tile)
    else:
        n_keys = pos0 + seq
        tk = -(-n_keys // LANES) * LANES
        pad = ((0, 0), (0, tk - n_keys), (0, 0))
        lat_all = jnp.pad(jnp.concatenate([lat_past, lats[layer].reshape(batch, seq, -1)], axis=1), pad)
        kr_all = jnp.pad(jnp.concatenate([kr_past, kr.reshape(batch, seq, -1)], axis=1), pad)
        k, v = _mla_kv(lat_all.reshape(1, batch * tk, -1), 0, kr_all.reshape(batch * tk, -1), w,
                       math.gcd(batch * tk, MLA_KV_TILE), v_transposed=False)
        og = _attn_step(q, k.reshape(batch, tk, -1), v.reshape(batch, tk, -1), gate, batch, seq, n_keys)
    h = _out_ple(og, h, p, w["out_w"], *ple, tm)
    return h, lats, kr.reshape(batch, seq, -1)


def kernel(x_prompt, x_sample, cache_conv, state_ssm, cache_kv_latent, cache_k_rope, p_prompt, p_sample, ln_w, ssd_in_w, ssd_conv_w, ssd_conv_b, ssd_dt_bias, ssd_A_log, ssd_D, ssd_norm_w, ssd_out_w, mla_in_w, mla_q_a_norm, mla_q_b_w, mla_kv_a_norm, mla_kv_b_w, mla_q_nope_norm, mla_q_rope_norm, mla_k_nope_norm, mla_k_rope_norm, mla_out_w, ple_up_w, ple_norm_w, ple_gate_w):
    bp, tp, d = x_prompt.shape
    bs, ts, _ = x_sample.shape
    hp = x_prompt.reshape(bp * tp, d)
    hs = x_sample.reshape(bs * ts, d)
    conv_p, kr_p, conv_s, kr_s = [], [], [], []
    lat_p = lat_s = ssm_p = ssm_s = None
    for i in range(DEPTH):
        j = i // 2
        lnw = ln_w[i][None, :]
        ple = (ple_norm_w[i][None, :], ple_gate_w[i].astype(BF16), ple_up_w[i].astype(BF16))
        pp = (p_prompt.reshape(DEPTH, bp * tp, PLE_DIM), i)
        ps = (p_sample.reshape(DEPTH, bs * ts, PLE_DIM), i)
        if i % 2 == 0:
            w = _ssd_weights(j, ssd_in_w, ssd_conv_w, ssd_conv_b, ssd_dt_bias, ssd_A_log, ssd_D,
                             ssd_norm_w, ssd_out_w)
            n_ssd = (DEPTH + 1) // 2
            hp, cp, ssm_p = _ssd_layer(hp, pp, lnw, ple, w, bp, tp, min(tp, SSD_SCAN_CHUNK), None, None,
                                       (ssm_p, j, n_ssd))
            hs, cs, ssm_s = _ssd_layer(hs, ps, lnw, ple, w, bs, ts, ts, cache_conv[j], state_ssm[j],
                                       (ssm_s, j, n_ssd))
            conv_p.append(cp); conv_s.append(cs)
        else:
            w = _mla_weights(j, mla_in_w, mla_q_a_norm, mla_q_b_w, mla_kv_a_norm, mla_kv_b_w,
                             mla_q_nope_norm, mla_q_rope_norm, mla_k_nope_norm, mla_k_rope_norm, mla_out_w)
            n_mla = DEPTH // 2
            hp, lat_p, rp = _mla_layer(hp, pp, lnw, ple, w, bp, tp, None, None, (lat_p, j, n_mla))
            hs, lat_s, rs = _mla_layer(hs, ps, lnw, ple, w, bs, ts, cache_kv_latent[j], cache_k_rope[j],
                                       (lat_s, j, n_mla))
            kr_p.append(rp); kr_s.append(rs)
    return (hp.reshape(bp, tp, d), hs.reshape(bs, ts, d),
            jnp.stack(conv_p), ssm_p.reshape(-1, bp, SSD_HEADS, SSD_HEAD_DIM, SSD_STATE),
            lat_p.reshape(-1, bp, tp, MLA_KV_LORA), jnp.stack(kr_p),
            jnp.stack(conv_s), ssm_s.reshape(-1, bs, SSD_HEADS, SSD_HEAD_DIM, SSD_STATE),
            lat_s.reshape(-1, bs, ts, MLA_KV_LORA), jnp.stack(kr_s))
```
